```python
import math
import jax, jax.numpy as jnp
from jax import lax
import numpy as np

D_MODEL = 2048
BATCH = 8
SEQ = 8192
DEPTH = 4

CHUNK = 64
Q_BLOCK = 128
NORM_EPS = 1e-6
N_BRANCH = 3

RET_HEAD_DIM = 128
RET_WIDTH = D_MODEL // 2
RET_HEADS = RET_WIDTH // RET_HEAD_DIM
ROPE_BASE = 10000.0

SB_HEAD_DIM = 128
SB_WIDTH = D_MODEL // 2
SB_HEADS = SB_WIDTH // SB_HEAD_DIM

SSD_HEAD_DIM = 64
SSD_WIDTH = D_MODEL // 2
SSD_HEADS = SSD_WIDTH // SSD_HEAD_DIM
SSD_GROUPS = 4
SSD_HEADS_PER_GROUP = SSD_HEADS // SSD_GROUPS
SSD_STATE = 128
SSD_CONV = 4
SSD_CONV_DIM = SSD_WIDTH + 2 * SSD_GROUPS * SSD_STATE

FFN_HIDDEN = ((8 * D_MODEL // 3 + 255) // 256) * 256

IN_SIZES = (RET_WIDTH, RET_WIDTH, RET_WIDTH, RET_WIDTH,
            SB_WIDTH, SB_WIDTH, SB_WIDTH,
            SSD_WIDTH, SSD_CONV_DIM, SSD_HEADS,
            N_BRANCH * D_MODEL)
IN_COLS = sum(IN_SIZES)

kernel_name = 'hybrid_retention_stickbreaking_ssd_trunk'


def rms_norm(x, w):
    xf = x.astype(jnp.float32)
    y = xf * lax.rsqrt(jnp.mean(xf * xf, axis=-1, keepdims=True) + NORM_EPS)
    return (y * w.astype(jnp.float32)).astype(x.dtype)


def apply_rotary(t, positions):
    half = t.shape[-1] // 2
    inv_freq = ROPE_BASE ** (-2.0 * jnp.arange(half, dtype=jnp.float32) / t.shape[-1])
    ang = positions.astype(jnp.float32)[:, :, None] * inv_freq
    cos = jnp.cos(ang)[:, :, None, :]
    sin = jnp.sin(ang)[:, :, None, :]
    t1, t2 = t[..., :half], t[..., half:]
    return jnp.concatenate([t1 * cos - t2 * sin, t1 * sin + t2 * cos], axis=-1)


def retention_mixer(q, k, v, g, positions, gn_w):
    out_dtype = q.dtype
    b, s, _ = q.shape
    nc = s // CHUNK
    H, Dh = RET_HEADS, RET_HEAD_DIM
    q = apply_rotary(q.astype(jnp.float32).reshape(b, s, H, Dh), positions)
    k = apply_rotary(k.astype(jnp.float32).reshape(b, s, H, Dh), positions) * (Dh ** -0.5)
    v = v.astype(jnp.float32).reshape(b, s, H, Dh)
    log_gamma = jnp.log1p(-jnp.exp2(-5.0 - jnp.arange(H, dtype=jnp.float32)))
    idx = jnp.arange(CHUNK, dtype=jnp.float32)
    intra_decay = jnp.exp(log_gamma[:, None, None] * jnp.abs(idx[:, None] - idx[None, :]))
    q = q.reshape(b, nc, CHUNK, H, Dh)
    k = k.reshape(b, nc, CHUNK, H, Dh)
    v = v.reshape(b, nc, CHUNK, H, Dh)
    scores = jnp.einsum('bcihd,bcjhd->bchij', q, k) * intra_decay
    o_intra = jnp.einsum('bchij,bcjhd->bcihd', scores, v)
    q_decay = jnp.exp(log_gamma[None, :] * (idx[:, None] + 1.0))
    k_decay = jnp.exp(log_gamma[None, :] * (CHUNK - 1.0 - idx[:, None]))
    chunk_decay = jnp.exp(log_gamma * CHUNK)

    def step(state, inp):
        qc, kc, vc = inp
        cross = jnp.einsum('bihd,bhde->bihe', qc * q_decay[None, :, :, None], state)
        state = state * chunk_decay[None, :, None, None] + jnp.einsum(
            'bjhd,bjhe->bhde', kc * k_decay[None, :, :, None], vc)
        return state, cross

    state0 = jnp.zeros((b, H, Dh, Dh), jnp.float32)
    _, o_cross = lax.scan(step, state0, (jnp.moveaxis(q, 1, 0), jnp.moveaxis(k, 1, 0), jnp.moveaxis(v, 1, 0)))
    o = (o_intra + jnp.moveaxis(o_cross, 0, 1)).reshape(b, s, H, Dh)
    mu = jnp.mean(o, axis=-1, keepdims=True)
    var = jnp.mean(jnp.square(o - mu), axis=-1, keepdims=True)
    o = (o - mu) * lax.rsqrt(var + NORM_EPS) * gn_w.astype(jnp.float32).reshape(H, Dh)
    o = o.reshape(b, s, RET_WIDTH) * jax.nn.silu(g.astype(jnp.float32))
    return o.astype(out_dtype)


def stick_breaking_mixer(q, k, v):
    b, s, _ = q.shape
    H, Dh = SB_HEADS, SB_HEAD_DIM
    q = q.reshape(b, s, H, Dh).transpose(0, 2, 1, 3)
    k = k.reshape(b, s, H, Dh).transpose(0, 2, 1, 3)
    v = v.reshape(b, s, H, Dh).transpose(0, 2, 1, 3)
    key_pos = jnp.arange(s)
    scale = Dh ** -0.5

    def block(i):
        qb = lax.dynamic_slice_in_dim(q, i * Q_BLOCK, Q_BLOCK, axis=2)
        z = jnp.einsum('bhqd,bhkd->bhqk', qb, k).astype(jnp.float32) * scale
        t = i * Q_BLOCK + jnp.arange(Q_BLOCK)
        mask = key_pos[None, :] < t[:, None]
        log_keep = jnp.where(mask, jax.nn.log_sigmoid(-z), 0.0)
        later = lax.cumsum(log_keep, axis=3, reverse=True) - log_keep
        w = jnp.where(mask, jnp.exp(jax.nn.log_sigmoid(z) + later), 0.0)
        return jnp.einsum('bhqk,bhkd->bhqd', w.astype(v.dtype), v)

    out = lax.map(block, jnp.arange(s // Q_BLOCK))
    return out.transpose(1, 0, 3, 2, 4).reshape(b, s, SB_WIDTH)


def causal_depthwise_conv(u, w, bias):
    out = lax.conv_general_dilated(
        u, w[:, None, :], window_strides=(1,), padding=[(SSD_CONV - 1, 0)],
        dimension_numbers=('NWC', 'WIO', 'NWC'), feature_group_count=u.shape[-1])
    return out + bias


def ssd_mixer(z, xbc, dt_raw, conv_w, conv_b, dt_bias, a_log, d_skip, norm_w):
    out_dtype = z.dtype
    b, s, _ = z.shape
    nc = s // CHUNK
    G, E, P, N = SSD_GROUPS, SSD_HEADS_PER_GROUP, SSD_HEAD_DIM, SSD_STATE
    f32 = jnp.float32
    xbc = jax.nn.silu(causal_depthwise_conv(xbc.astype(f32), conv_w.astype(f32), conv_b.astype(f32)))
    x = xbc[..., :SSD_WIDTH].reshape(b, nc, CHUNK, G, E, P)
    Bm = xbc[..., SSD_WIDTH:SSD_WIDTH + G * N].reshape(b, nc, CHUNK, G, N)
    Cm = xbc[..., SSD_WIDTH + G * N:].reshape(b, nc, CHUNK, G, N)
    dt = jax.nn.softplus(dt_raw.astype(f32) + dt_bias.astype(f32)).reshape(b, nc, CHUNK, G, E)
    A = -jnp.exp(a_log.astype(f32)).reshape(G, E)
    a = dt * A
    acum = jnp.cumsum(a, axis=2)
    xdt = x * dt[..., None]
    acum_t = jnp.moveaxis(acum, 2, -1)
    seg = acum_t[..., :, None] - acum_t[..., None, :]
    causal = jnp.tril(jnp.ones((CHUNK, CHUNK), dtype=bool))
    decay = jnp.exp(jnp.where(causal, seg, -jnp.inf))
    cb = jnp.einsum('bclgn,bcsgn->bcgls', Cm, Bm)
    y_diag = jnp.einsum('bcgls,bcgels,bcsgep->bclgep', cb, decay, xdt)
    decay_states = jnp.exp(acum[:, :, -1:] - acum)
    states = jnp.einsum('bclgn,bclge,bclgep->bcgepn', Bm, decay_states, xdt)
    chunk_decay = jnp.exp(acum[:, :, -1])

    def step(state, inp):
        c_c, acum_c, st_c, dec_c = inp
        y_off = jnp.einsum('blgn,bgepn,blge->blgep', c_c, state, jnp.exp(acum_c))
        state = state * dec_c[..., None, None] + st_c
        return state, y_off

    state0 = jnp.zeros((b, G, E, P, N), f32)
    _, y_off = lax.scan(step, state0, (jnp.moveaxis(Cm, 1, 0), jnp.moveaxis(acum, 1, 0),
                                       jnp.moveaxis(states, 1, 0), jnp.moveaxis(chunk_decay, 1, 0)))
    y = y_diag + jnp.moveaxis(y_off, 0, 1) + x * d_skip.astype(f32).reshape(G, E)[..., None]
    y = y.reshape(b, s, SSD_WIDTH)
    y = rms_norm(y * jax.nn.silu(z.astype(f32)), norm_w)
    return y.astype(out_dtype)


def hybrid_layer(x, positions, n_mix_pre, n_mix_post, n_ffn_pre, n_ffn_post, w_in, b_gate, ret_gn_w,
                 conv_w, conv_b, dt_bias, a_log, d_skip, ssd_norm_w, w_br_ret, w_br_sb, w_br_ssd,
                 w_out, w_gate, w_up, w_down):
    b, s, d = x.shape
    h = rms_norm(x, n_mix_pre)
    proj = h @ w_in
    split_points = np.cumsum(IN_SIZES)[:-1].tolist()
    rq, rk, rv, rg, sq, sk, sv, sz, sxbc, sdt, gate_logits = jnp.split(proj, split_points, axis=-1)
    y_ret = retention_mixer(rq, rk, rv, rg, positions, ret_gn_w)
    y_sb = stick_breaking_mixer(sq, sk, sv)
    y_ssd = ssd_mixer(sz, sxbc, sdt, conv_w, conv_b, dt_bias, a_log, d_skip, ssd_norm_w)
    gates = jax.nn.sigmoid(gate_logits + b_gate).reshape(b, s, N_BRANCH, d)
    merged = (gates[:, :, 0] * (y_ret @ w_br_ret)
              + gates[:, :, 1] * (y_sb @ w_br_sb)
              + gates[:, :, 2] * (y_ssd @ w_br_ssd))
    x = x + rms_norm(merged @ w_out, n_mix_post)
    h = rms_norm(x, n_ffn_pre)
    f = (jax.nn.silu(h @ w_gate) * (h @ w_up)) @ w_down
    return x + rms_norm(f, n_ffn_post)


def _fwd_setup_inputs(seed: int = 0) -> dict:
    key = jax.random.key(seed)
    ks = jax.random.split(key, 24)
    f32 = jnp.float32

    def normal(k, shape, scale):
        return jax.random.normal(k, shape, f32) * scale

    def gain(k, shape):
        return 1.0 + 0.02 * jax.random.normal(k, shape, f32)

    dt_init = jnp.exp(jax.random.uniform(ks[10], (DEPTH, SSD_HEADS), f32)
                      * (math.log(0.1) - math.log(0.001)) + math.log(0.001))
    return {
        'x': jax.random.normal(ks[0], (BATCH, SEQ, D_MODEL), f32),
        'positions': jnp.broadcast_to(jnp.arange(SEQ, dtype=jnp.int32), (BATCH, SEQ)),
        'norm_mix_pre': gain(ks[1], (DEPTH, D_MODEL)),
        'norm_mix_post': gain(ks[2], (DEPTH, D_MODEL)),
        'norm_ffn_pre': gain(ks[3], (DEPTH, D_MODEL)),
        'norm_ffn_post': gain(ks[4], (DEPTH, D_MODEL)),
        'w_in': normal(ks[5], (DEPTH, D_MODEL, IN_COLS), D_MODEL ** -0.5),
        'b_gate': normal(ks[6], (DEPTH, N_BRANCH * D_MODEL), 0.01),
        'ret_gn_w': gain(ks[7], (DEPTH, RET_WIDTH)),
        'ssd_conv_w': normal(ks[8], (DEPTH, SSD_CONV, SSD_CONV_DIM), SSD_CONV ** -0.5),
        'ssd_conv_b': normal(ks[9], (DEPTH, SSD_CONV_DIM), 0.01),
        'ssd_dt_bias': dt_init + jnp.log(-jnp.expm1(-dt_init)),
        'ssd_a_log': jnp.log(jax.random.uniform(ks[11], (DEPTH, SSD_HEADS), f32, 1.0, 16.0)),
        'ssd_d': gain(ks[12], (DEPTH, SSD_HEADS)),
        'ssd_norm_w': gain(ks[13], (DEPTH, SSD_WIDTH)),
        'w_branch_ret': normal(ks[14], (DEPTH, RET_WIDTH, D_MODEL), RET_WIDTH ** -0.5),
        'w_branch_sb': normal(ks[15], (DEPTH, SB_WIDTH, D_MODEL), SB_WIDTH ** -0.5),
        'w_branch_ssd': normal(ks[16], (DEPTH, SSD_WIDTH, D_MODEL), SSD_WIDTH ** -0.5),
        'w_out': normal(ks[17], (DEPTH, D_MODEL, D_MODEL), D_MODEL ** -0.5),
        'ffn_w_gate': normal(ks[18], (DEPTH, D_MODEL, FFN_HIDDEN), D_MODEL ** -0.5),
        'ffn_w_up': normal(ks[19], (DEPTH, D_MODEL, FFN_HIDDEN), D_MODEL ** -0.5),
        'ffn_w_down': normal(ks[20], (DEPTH, FFN_HIDDEN, D_MODEL), FFN_HIDDEN ** -0.5),
    }


def _fwd_reference(x, positions, norm_mix_pre, norm_mix_post, norm_ffn_pre, norm_ffn_post, w_in, b_gate,
              ret_gn_w, ssd_conv_w, ssd_conv_b, ssd_dt_bias, ssd_a_log, ssd_d, ssd_norm_w,
              w_branch_ret, w_branch_sb, w_branch_ssd, w_out, ffn_w_gate, ffn_w_up, ffn_w_down):
    for l in range(DEPTH):
        x = hybrid_layer(x, positions, norm_mix_pre[l], norm_mix_post[l], norm_ffn_pre[l], norm_ffn_post[l],
                         w_in[l], b_gate[l], ret_gn_w[l], ssd_conv_w[l], ssd_conv_b[l], ssd_dt_bias[l],
                         ssd_a_log[l], ssd_d[l], ssd_norm_w[l], w_branch_ret[l], w_branch_sb[l],
                         w_branch_ssd[l], w_out[l], ffn_w_gate[l], ffn_w_up[l], ffn_w_down[l])
    return x


import jax as _jax
import jax.numpy as _jnp

TWIN_FORMAT = 'train_step'
FWD_PARAMS = ['x', 'positions', 'norm_mix_pre', 'norm_mix_post', 'norm_ffn_pre', 'norm_ffn_post', 'w_in', 'b_gate', 'ret_gn_w', 'ssd_conv_w', 'ssd_conv_b', 'ssd_dt_bias', 'ssd_a_log', 'ssd_d', 'ssd_norm_w', 'w_branch_ret', 'w_branch_sb', 'w_branch_ssd', 'w_out', 'ffn_w_gate', 'ffn_w_up', 'ffn_w_down']
TWIN_WEIGHTS = ['norm_mix_pre', 'norm_mix_post', 'norm_ffn_pre', 'norm_ffn_post', 'w_in', 'b_gate', 'ret_gn_w', 'ssd_conv_w', 'ssd_conv_b', 'ssd_dt_bias', 'ssd_a_log', 'ssd_d', 'ssd_norm_w', 'w_branch_ret', 'w_branch_sb', 'w_branch_ssd', 'w_out', 'ffn_w_gate', 'ffn_w_up', 'ffn_w_down']
TWIN_DIFF_INPUT = 'x'
TWIN_INPUTS = ['x', 'positions', 'norm_mix_pre', 'norm_mix_post', 'norm_ffn_pre', 'norm_ffn_post', 'w_in', 'b_gate', 'ret_gn_w', 'ssd_conv_w', 'ssd_conv_b', 'ssd_dt_bias', 'ssd_a_log', 'ssd_d', 'ssd_norm_w', 'w_branch_ret', 'w_branch_sb', 'w_branch_ssd', 'w_out', 'ffn_w_gate', 'ffn_w_up', 'ffn_w_down', 'loss_target', 'm_norm_mix_pre', 'm_norm_mix_post', 'm_norm_ffn_pre', 'm_norm_ffn_post', 'm_w_in', 'm_b_gate', 'm_ret_gn_w', 'm_ssd_conv_w', 'm_ssd_conv_b', 'm_ssd_dt_bias', 'm_ssd_a_log', 'm_ssd_d', 'm_ssd_norm_w', 'm_w_branch_ret', 'm_w_branch_sb', 'm_w_branch_ssd', 'm_w_out', 'm_ffn_w_gate', 'm_ffn_w_up', 'm_ffn_w_down', 'v_norm_mix_pre', 'v_norm_mix_post', 'v_norm_ffn_pre', 'v_norm_ffn_post', 'v_w_in', 'v_b_gate', 'v_ret_gn_w', 'v_ssd_conv_w', 'v_ssd_conv_b', 'v_ssd_dt_bias', 'v_ssd_a_log', 'v_ssd_d', 'v_ssd_norm_w', 'v_w_branch_ret', 'v_w_branch_sb', 'v_w_branch_ssd', 'v_w_out', 'v_ffn_w_gate', 'v_ffn_w_up', 'v_ffn_w_down']
TWIN_OUTPUTS = ['loss', 'grad_x', 'grad_norm_mix_pre', 'grad_norm_mix_post', 'grad_norm_ffn_pre', 'grad_norm_ffn_post', 'grad_w_in', 'grad_b_gate', 'grad_ret_gn_w', 'grad_ssd_conv_w', 'grad_ssd_conv_b', 'grad_ssd_dt_bias', 'grad_ssd_a_log', 'grad_ssd_d', 'grad_ssd_norm_w', 'grad_w_branch_ret', 'grad_w_branch_sb', 'grad_w_branch_ssd', 'grad_w_out', 'grad_ffn_w_gate', 'grad_ffn_w_up', 'grad_ffn_w_down', 'delta_norm_mix_pre', 'delta_norm_mix_post', 'delta_norm_ffn_pre', 'delta_norm_ffn_post', 'delta_w_in', 'delta_b_gate', 'delta_ret_gn_w', 'delta_ssd_conv_w', 'delta_ssd_conv_b', 'delta_ssd_dt_bias', 'delta_ssd_a_log', 'delta_ssd_d', 'delta_ssd_norm_w', 'delta_w_branch_ret', 'delta_w_branch_sb', 'delta_w_branch_ssd', 'delta_w_out', 'delta_ffn_w_gate', 'delta_ffn_w_up', 'delta_ffn_w_down', 'new_m_norm_mix_pre', 'new_m_norm_mix_post', 'new_m_norm_ffn_pre', 'new_m_norm_ffn_post', 'new_m_w_in', 'new_m_b_gate', 'new_m_ret_gn_w', 'new_m_ssd_conv_w', 'new_m_ssd_conv_b', 'new_m_ssd_dt_bias', 'new_m_ssd_a_log', 'new_m_ssd_d', 'new_m_ssd_norm_w', 'new_m_w_branch_ret', 'new_m_w_branch_sb', 'new_m_w_branch_ssd', 'new_m_w_out', 'new_m_ffn_w_gate', 'new_m_ffn_w_up', 'new_m_ffn_w_down', 'new_v_norm_mix_pre', 'new_v_norm_mix_post', 'new_v_norm_ffn_pre', 'new_v_norm_ffn_post', 'new_v_w_in', 'new_v_b_gate', 'new_v_ret_gn_w', 'new_v_ssd_conv_w', 'new_v_ssd_conv_b', 'new_v_ssd_dt_bias', 'new_v_ssd_a_log', 'new_v_ssd_d', 'new_v_ssd_norm_w', 'new_v_w_branch_ret', 'new_v_w_branch_sb', 'new_v_w_branch_ssd', 'new_v_w_out', 'new_v_ffn_w_gate', 'new_v_ffn_w_up', 'new_v_ffn_w_down']
TWIN_LEAF_KINDS = {'loss': 'loss', 'grad_x': 'grad_x', 'grad_norm_mix_pre': 'grad_w', 'grad_norm_mix_post': 'grad_w', 'grad_norm_ffn_pre': 'grad_w', 'grad_norm_ffn_post': 'grad_w', 'grad_w_in': 'grad_w', 'grad_b_gate': 'grad_w', 'grad_ret_gn_w': 'grad_w', 'grad_ssd_conv_w': 'grad_w', 'grad_ssd_conv_b': 'grad_w', 'grad_ssd_dt_bias': 'grad_w', 'grad_ssd_a_log': 'grad_w', 'grad_ssd_d': 'grad_w', 'grad_ssd_norm_w': 'grad_w', 'grad_w_branch_ret': 'grad_w', 'grad_w_branch_sb': 'grad_w', 'grad_w_branch_ssd': 'grad_w', 'grad_w_out': 'grad_w', 'grad_ffn_w_gate': 'grad_w', 'grad_ffn_w_up': 'grad_w', 'grad_ffn_w_down': 'grad_w', 'delta_norm_mix_pre': 'delta_w', 'delta_norm_mix_post': 'delta_w', 'delta_norm_ffn_pre': 'delta_w', 'delta_norm_ffn_post': 'delta_w', 'delta_w_in': 'delta_w', 'delta_b_gate': 'delta_w', 'delta_ret_gn_w': 'delta_w', 'delta_ssd_conv_w': 'delta_w', 'delta_ssd_conv_b': 'delta_w', 'delta_ssd_dt_bias': 'delta_w', 'delta_ssd_a_log': 'delta_w', 'delta_ssd_d': 'delta_w', 'delta_ssd_norm_w': 'delta_w', 'delta_w_branch_ret': 'delta_w', 'delta_w_branch_sb': 'delta_w', 'delta_w_branch_ssd': 'delta_w', 'delta_w_out': 'delta_w', 'delta_ffn_w_gate': 'delta_w', 'delta_ffn_w_up': 'delta_w', 'delta_ffn_w_down': 'delta_w', 'new_m_norm_mix_pre': 'new_m', 'new_m_norm_mix_post': 'new_m', 'new_m_norm_ffn_pre': 'new_m', 'new_m_norm_ffn_post': 'new_m', 'new_m_w_in': 'new_m', 'new_m_b_gate': 'new_m', 'new_m_ret_gn_w': 'new_m', 'new_m_ssd_conv_w': 'new_m', 'new_m_ssd_conv_b': 'new_m', 'new_m_ssd_dt_bias': 'new_m', 'new_m_ssd_a_log': 'new_m', 'new_m_ssd_d': 'new_m', 'new_m_ssd_norm_w': 'new_m', 'new_m_w_branch_ret': 'new_m', 'new_m_w_branch_sb': 'new_m', 'new_m_w_branch_ssd': 'new_m', 'new_m_w_out': 'new_m', 'new_m_ffn_w_gate': 'new_m', 'new_m_ffn_w_up': 'new_m', 'new_m_ffn_w_down': 'new_m', 'new_v_norm_mix_pre': 'new_v', 'new_v_norm_mix_post': 'new_v', 'new_v_norm_ffn_pre': 'new_v', 'new_v_norm_ffn_post': 'new_v', 'new_v_w_in': 'new_v', 'new_v_b_gate': 'new_v', 'new_v_ret_gn_w': 'new_v', 'new_v_ssd_conv_w': 'new_v', 'new_v_ssd_conv_b': 'new_v', 'new_v_ssd_dt_bias': 'new_v', 'new_v_ssd_a_log': 'new_v', 'new_v_ssd_d': 'new_v', 'new_v_ssd_norm_w': 'new_v', 'new_v_w_branch_ret': 'new_v', 'new_v_w_branch_sb': 'new_v', 'new_v_w_branch_ssd': 'new_v', 'new_v_w_out': 'new_v', 'new_v_ffn_w_gate': 'new_v', 'new_v_ffn_w_up': 'new_v', 'new_v_ffn_w_down': 'new_v'}


def _forward(args):
    return _fwd_reference(*[args[k] for k in FWD_PARAMS])


def _output_shape():
    def fwd():
        inp = _fwd_setup_inputs(0)
        return _fwd_reference(*[inp[k] for k in FWD_PARAMS])
    out = _jax.eval_shape(fwd)
    return out.shape, out.dtype

N_MICROBATCH = 1
ADAM_LR = 0.001
ADAM_B1 = 0.9
ADAM_B2 = 0.999
ADAM_EPS = 1e-08
ADAM_WD = 0.01
ADAM_STEP = 10
PER_EXAMPLE_BATCH_AXIS = {'x': 0, 'positions': 0, 'loss_target': 0}
SHARED_INPUTS = []
_WEIGHT_DTYPES = {'norm_mix_pre': _jnp.float32, 'norm_mix_post': _jnp.float32, 'norm_ffn_pre': _jnp.float32, 'norm_ffn_post': _jnp.float32, 'w_in': _jnp.float32, 'b_gate': _jnp.float32, 'ret_gn_w': _jnp.float32, 'ssd_conv_w': _jnp.float32, 'ssd_conv_b': _jnp.float32, 'ssd_dt_bias': _jnp.float32, 'ssd_a_log': _jnp.float32, 'ssd_d': _jnp.float32, 'ssd_norm_w': _jnp.float32, 'w_branch_ret': _jnp.float32, 'w_branch_sb': _jnp.float32, 'w_branch_ssd': _jnp.float32, 'w_out': _jnp.float32, 'ffn_w_gate': _jnp.float32, 'ffn_w_up': _jnp.float32, 'ffn_w_down': _jnp.float32}
MOMENT_SCALE = {'norm_mix_pre': 1.753873e+00, 'norm_mix_post': 3.139860e+01, 'norm_ffn_pre': 1.172922e+00, 'norm_ffn_post': 3.182597e+01, 'w_in': 5.873613e-01, 'b_gate': 3.119469e-01, 'ret_gn_w': 6.666362e-01, 'ssd_conv_w': 9.917614e-01, 'ssd_conv_b': 2.427844e+00, 'ssd_dt_bias': 3.781408e+00, 'ssd_a_log': 1.003773e+01, 'ssd_d': 7.707528e+00, 'ssd_norm_w': 1.541804e+00, 'w_branch_ret': 4.519525e-01, 'w_branch_sb': 5.674130e-01, 'w_branch_ssd': 1.139402e+00, 'w_out': 1.348524e+00, 'ffn_w_gate': 4.741980e-01, 'ffn_w_up': 5.225198e-01, 'ffn_w_down': 8.698188e-01}


def _to_microbatches(a, axis):
    t = _jnp.moveaxis(a, axis, 0)
    t = t.reshape((N_MICROBATCH, t.shape[0] // N_MICROBATCH) + t.shape[1:])
    return _jnp.moveaxis(t, 1, axis + 1)


def setup_inputs(seed: int = 0) -> dict:
    inp = _fwd_setup_inputs(seed)
    key = _jax.random.fold_in(_jax.random.key(seed), 7919)
    shape, _ = _output_shape()
    out = dict(inp)
    out["loss_target"] = _jax.random.normal(_jax.random.fold_in(key, 0), shape, _jnp.float32)
    for i, name in enumerate(TWIN_WEIGHTS):
        w = inp[name].astype(_jnp.float32)
        if MOMENT_SCALE is None:
            s = _jnp.sqrt(_jnp.mean(_jnp.square(w)) + 1e-30)
        else:
            s = MOMENT_SCALE[name]
        km, kv = _jax.random.split(_jax.random.fold_in(key, i + 1))
        out[name] = w
        out["m_" + name] = s * _jax.random.normal(km, w.shape, _jnp.float32)
        out["v_" + name] = (s * s) * _jax.random.uniform(kv, w.shape, _jnp.float32, 0.5, 1.5)
    if N_MICROBATCH > 1:
        for name, axis in PER_EXAMPLE_BATCH_AXIS.items():
            out[name] = _to_microbatches(out[name], axis)
    return {'x': out['x'], 'positions': out['positions'], 'norm_mix_pre': out['norm_mix_pre'], 'norm_mix_post': out['norm_mix_post'], 'norm_ffn_pre': out['norm_ffn_pre'], 'norm_ffn_post': out['norm_ffn_post'], 'w_in': out['w_in'], 'b_gate': out['b_gate'], 'ret_gn_w': out['ret_gn_w'], 'ssd_conv_w': out['ssd_conv_w'], 'ssd_conv_b': out['ssd_conv_b'], 'ssd_dt_bias': out['ssd_dt_bias'], 'ssd_a_log': out['ssd_a_log'], 'ssd_d': out['ssd_d'], 'ssd_norm_w': out['ssd_norm_w'], 'w_branch_ret': out['w_branch_ret'], 'w_branch_sb': out['w_branch_sb'], 'w_branch_ssd': out['w_branch_ssd'], 'w_out': out['w_out'], 'ffn_w_gate': out['ffn_w_gate'], 'ffn_w_up': out['ffn_w_up'], 'ffn_w_down': out['ffn_w_down'], 'loss_target': out['loss_target'], 'm_norm_mix_pre': out['m_norm_mix_pre'], 'm_norm_mix_post': out['m_norm_mix_post'], 'm_norm_ffn_pre': out['m_norm_ffn_pre'], 'm_norm_ffn_post': out['m_norm_ffn_post'], 'm_w_in': out['m_w_in'], 'm_b_gate': out['m_b_gate'], 'm_ret_gn_w': out['m_ret_gn_w'], 'm_ssd_conv_w': out['m_ssd_conv_w'], 'm_ssd_conv_b': out['m_ssd_conv_b'], 'm_ssd_dt_bias': out['m_ssd_dt_bias'], 'm_ssd_a_log': out['m_ssd_a_log'], 'm_ssd_d': out['m_ssd_d'], 'm_ssd_norm_w': out['m_ssd_norm_w'], 'm_w_branch_ret': out['m_w_branch_ret'], 'm_w_branch_sb': out['m_w_branch_sb'], 'm_w_branch_ssd': out['m_w_branch_ssd'], 'm_w_out': out['m_w_out'], 'm_ffn_w_gate': out['m_ffn_w_gate'], 'm_ffn_w_up': out['m_ffn_w_up'], 'm_ffn_w_down': out['m_ffn_w_down'], 'v_norm_mix_pre': out['v_norm_mix_pre'], 'v_norm_mix_post': out['v_norm_mix_post'], 'v_norm_ffn_pre': out['v_norm_ffn_pre'], 'v_norm_ffn_post': out['v_norm_ffn_post'], 'v_w_in': out['v_w_in'], 'v_b_gate': out['v_b_gate'], 'v_ret_gn_w': out['v_ret_gn_w'], 'v_ssd_conv_w': out['v_ssd_conv_w'], 'v_ssd_conv_b': out['v_ssd_conv_b'], 'v_ssd_dt_bias': out['v_ssd_dt_bias'], 'v_ssd_a_log': out['v_ssd_a_log'], 'v_ssd_d': out['v_ssd_d'], 'v_ssd_norm_w': out['v_ssd_norm_w'], 'v_w_branch_ret': out['v_w_branch_ret'], 'v_w_branch_sb': out['v_w_branch_sb'], 'v_w_branch_ssd': out['v_w_branch_ssd'], 'v_w_out': out['v_w_out'], 'v_ffn_w_gate': out['v_ffn_w_gate'], 'v_ffn_w_up': out['v_ffn_w_up'], 'v_ffn_w_down': out['v_ffn_w_down']}


def _loss(weights, diff, rest, loss_target):
    with _jax.named_scope("forward"):
        args = {**rest, TWIN_DIFF_INPUT: diff, **{k: w.astype(_WEIGHT_DTYPES[k]) for k, w in weights.items()}}
        y = _forward(args)
    with _jax.named_scope("loss_head"):
        err = _jnp.square(y.astype(_jnp.float32) - loss_target)
        return 0.5 * _jnp.sum(_jnp.mean(err, axis=-1)) if err.ndim else 0.5 * err


def _adamw(w, g, m, v):
    m = ADAM_B1 * m + (1.0 - ADAM_B1) * g
    v = ADAM_B2 * v + (1.0 - ADAM_B2) * _jnp.square(g)
    m_hat = m / (1.0 - ADAM_B1 ** ADAM_STEP)
    v_hat = v / (1.0 - ADAM_B2 ** ADAM_STEP)
    delta = -ADAM_LR * (m_hat / (_jnp.sqrt(v_hat) + ADAM_EPS) + ADAM_WD * w)
    return delta, m, v


def reference(x, positions, norm_mix_pre, norm_mix_post, norm_ffn_pre, norm_ffn_post, w_in, b_gate, ret_gn_w, ssd_conv_w, ssd_conv_b, ssd_dt_bias, ssd_a_log, ssd_d, ssd_norm_w, w_branch_ret, w_branch_sb, w_branch_ssd, w_out, ffn_w_gate, ffn_w_up, ffn_w_down, loss_target, m_norm_mix_pre, m_norm_mix_post, m_norm_ffn_pre, m_norm_ffn_post, m_w_in, m_b_gate, m_ret_gn_w, m_ssd_conv_w, m_ssd_conv_b, m_ssd_dt_bias, m_ssd_a_log, m_ssd_d, m_ssd_norm_w, m_w_branch_ret, m_w_branch_sb, m_w_branch_ssd, m_w_out, m_ffn_w_gate, m_ffn_w_up, m_ffn_w_down, v_norm_mix_pre, v_norm_mix_post, v_norm_ffn_pre, v_norm_ffn_post, v_w_in, v_b_gate, v_ret_gn_w, v_ssd_conv_w, v_ssd_conv_b, v_ssd_dt_bias, v_ssd_a_log, v_ssd_d, v_ssd_norm_w, v_w_branch_ret, v_w_branch_sb, v_w_branch_ssd, v_w_out, v_ffn_w_gate, v_ffn_w_up, v_ffn_w_down):
    given = dict(x=x, positions=positions, norm_mix_pre=norm_mix_pre, norm_mix_post=norm_mix_post, norm_ffn_pre=norm_ffn_pre, norm_ffn_post=norm_ffn_post, w_in=w_in, b_gate=b_gate, ret_gn_w=ret_gn_w, ssd_conv_w=ssd_conv_w, ssd_conv_b=ssd_conv_b, ssd_dt_bias=ssd_dt_bias, ssd_a_log=ssd_a_log, ssd_d=ssd_d, ssd_norm_w=ssd_norm_w, w_branch_ret=w_branch_ret, w_branch_sb=w_branch_sb, w_branch_ssd=w_branch_ssd, w_out=w_out, ffn_w_gate=ffn_w_gate, ffn_w_up=ffn_w_up, ffn_w_down=ffn_w_down, loss_target=loss_target, m_norm_mix_pre=m_norm_mix_pre, m_norm_mix_post=m_norm_mix_post, m_norm_ffn_pre=m_norm_ffn_pre, m_norm_ffn_post=m_norm_ffn_post, m_w_in=m_w_in, m_b_gate=m_b_gate, m_ret_gn_w=m_ret_gn_w, m_ssd_conv_w=m_ssd_conv_w, m_ssd_conv_b=m_ssd_conv_b, m_ssd_dt_bias=m_ssd_dt_bias, m_ssd_a_log=m_ssd_a_log, m_ssd_d=m_ssd_d, m_ssd_norm_w=m_ssd_norm_w, m_w_branch_ret=m_w_branch_ret, m_w_branch_sb=m_w_branch_sb, m_w_branch_ssd=m_w_branch_ssd, m_w_out=m_w_out, m_ffn_w_gate=m_ffn_w_gate, m_ffn_w_up=m_ffn_w_up, m_ffn_w_down=m_ffn_w_down, v_norm_mix_pre=v_norm_mix_pre, v_norm_mix_post=v_norm_mix_post, v_norm_ffn_pre=v_norm_ffn_pre, v_norm_ffn_post=v_norm_ffn_post, v_w_in=v_w_in, v_b_gate=v_b_gate, v_ret_gn_w=v_ret_gn_w, v_ssd_conv_w=v_ssd_conv_w, v_ssd_conv_b=v_ssd_conv_b, v_ssd_dt_bias=v_ssd_dt_bias, v_ssd_a_log=v_ssd_a_log, v_ssd_d=v_ssd_d, v_ssd_norm_w=v_ssd_norm_w, v_w_branch_ret=v_w_branch_ret, v_w_branch_sb=v_w_branch_sb, v_w_branch_ssd=v_w_branch_ssd, v_w_out=v_w_out, v_ffn_w_gate=v_ffn_w_gate, v_ffn_w_up=v_ffn_w_up, v_ffn_w_down=v_ffn_w_down)
    weights = {n: given[n] for n in TWIN_WEIGHTS}
    shared = {n: given[n] for n in SHARED_INPUTS}
    per_example = {n: given[n] for n in ['x', 'positions']}
    grad_fn = _jax.value_and_grad(_loss, argnums=(0, 1))

    def one_microbatch(ex, loss_target):
        ex = dict(ex)
        diff = ex.pop(TWIN_DIFF_INPUT)
        return grad_fn(weights, diff, {**shared, **ex}, loss_target)

    if N_MICROBATCH == 1:
        loss, (grad_w, grad_x) = one_microbatch(per_example, given["loss_target"])
    else:
        def body(carry, xs):
            loss_sum, grad_sum = carry
            l_k, (gw_k, gx_k) = one_microbatch(xs[0], xs[1])
            with _jax.named_scope("update"):
                return (loss_sum + l_k, _jax.tree.map(_jnp.add, grad_sum, gw_k)), gx_k

        init = (_jnp.zeros((), _jnp.float32), _jax.tree.map(_jnp.zeros_like, weights))
        (loss, grad_w), grad_x = _jax.lax.scan(body, init, (per_example, given["loss_target"]))
    with _jax.named_scope("update"):
        delta_w, new_m, new_v = {}, {}, {}
        for n in TWIN_WEIGHTS:
            delta_w[n], new_m[n], new_v[n] = _adamw(weights[n], grad_w[n], given["m_" + n], given["v_" + n])
    return (loss, grad_x, *[grad_w[n] for n in TWIN_WEIGHTS], *[delta_w[n] for n in TWIN_WEIGHTS],
            *[new_m[n] for n in TWIN_WEIGHTS], *[new_v[n] for n in TWIN_WEIGHTS])
```

```python
import functools
import math

import jax
import jax.numpy as jnp
import numpy as np
from jax import lax
from jax.experimental import pallas as pl
from jax.experimental.pallas import tpu as pltpu

F32 = jnp.float32
BF16 = jnp.bfloat16

N_DEV = 8
HEAD = 128
SSD_P = 64
SSD_G = 4
SSD_N = 128
SSD_K = 4
CHUNK = 64
NORM_EPS = 1e-6
ROPE_BASE = 10000.0
LANE = 128
VMEM_LIMIT = 56 * 1024 * 1024

ADAM_LR, ADAM_B1, ADAM_B2, ADAM_EPS, ADAM_WD, ADAM_STEP = 0.001, 0.9, 0.999, 1e-08, 0.01, 10


def _cparams(sem):
    return pltpu.CompilerParams(dimension_semantics=sem, vmem_limit_bytes=VMEM_LIMIT)


def _pick(n, pref):
    if n <= pref:
        return n
    t = pref
    while t >= LANE:
        if n % t == 0:
            return t
        t -= LANE
    return n


def matmul(a, b, *, ta=False, tb=False, lead_a=None, lead_b=None, out_dtype=F32,
           tm=1024, tn=1024, tk=512, name="mm"):
    la, lb = lead_a is not None, lead_b is not None
    a2, b2 = a.shape[1:] if la else a.shape, b.shape[1:] if lb else b.shape
    (kd_a, m) = a2 if ta else a2[::-1]
    (kd_b, n) = b2[::-1] if tb else b2
    assert kd_a == kd_b, (a.shape, b.shape)
    nlead = a.shape[0] if la else (b.shape[0] if lb else 1)
    batch = "batch" in (lead_a, lead_b)
    kblocks = nlead if "k" in (lead_a, lead_b) else 1
    if la and lb:
        assert lead_a == lead_b and a.shape[0] == b.shape[0]
    tm, tn, tk = _pick(m, tm), _pick(n, tn), _pick(kd_a, tk)
    kt = kd_a // tk
    nk = kt * kblocks
    grid = ((nlead if batch else 1), m // tm, n // tn, nk)

    def lead_idx(g, k):
        return g if batch else k // kt

    def a_map(g, i, j, k):
        idx = (k % kt, i) if ta else (i, k % kt)
        return ((lead_idx(g, k),) + idx) if la else idx

    def b_map(g, i, j, k):
        idx = (j, k % kt) if tb else (k % kt, j)
        return ((lead_idx(g, k),) + idx) if lb else idx

    a_blk = (tk, tm) if ta else (tm, tk)
    b_blk = (tn, tk) if tb else (tk, tn)
    a_spec = pl.BlockSpec(((None,) + a_blk) if la else a_blk, a_map)
    b_spec = pl.BlockSpec(((None,) + b_blk) if lb else b_blk, b_map)
    if batch:
        o_spec = pl.BlockSpec((None, tm, tn), lambda g, i, j, k: (g, i, j))
        o_shape = jax.ShapeDtypeStruct((nlead, m, n), out_dtype)
    else:
        o_spec = pl.BlockSpec((tm, tn), lambda g, i, j, k: (i, j))
        o_shape = jax.ShapeDtypeStruct((m, n), out_dtype)
    dims = (((0 if ta else 1,), (1 if tb else 0,)), ((), ()))

    def body(a_ref, b_ref, o_ref, acc_ref):
        k = pl.program_id(3)

        @pl.when(k == 0)
        def _():
            acc_ref[...] = jnp.zeros_like(acc_ref)

        acc_ref[...] += lax.dot_general(a_ref[...].astype(BF16), b_ref[...].astype(BF16), dims,
                                        preferred_element_type=F32)

        @pl.when(k == nk - 1)
        def _():
            o_ref[...] = acc_ref[...].astype(o_ref.dtype)

    return pl.pallas_call(
        body, out_shape=o_shape, grid=grid, in_specs=[a_spec, b_spec], out_specs=o_spec,
        scratch_shapes=[pltpu.VMEM((tm, tn), F32)], name=name,
        compiler_params=_cparams(("parallel", "parallel", "parallel", "arbitrary")),
    )(a, b)


def _row_spec(r, tr):
    if not isinstance(r, tuple):
        return r, pl.BlockSpec((tr, r.shape[-1]), lambda i: (i, 0))
    if len(r) == 3:
        arr, w, cb = r
        return arr, pl.BlockSpec((tr, w), lambda i: (i, cb))
    arr, w, cb, ld = r
    return arr, pl.BlockSpec((None, tr, w), lambda i: (ld, i, cb))


def _full_spec(c):
    nd = c.ndim
    return pl.BlockSpec(c.shape, lambda i: (0,) * nd)


def rowwise(fn, rows, consts, outs, *, tr, name):
    arrs, specs = zip(*[_row_spec(r, tr) for r in rows])
    n_rows = arrs[0].shape[-2]
    nr, nc = len(rows), len(consts)

    def body(*refs):
        vals = [r[...] for r in refs[:nr + nc]]
        res = fn(*vals)
        for o_ref, r in zip(refs[nr + nc:], res):
            o_ref[...] = r.astype(o_ref.dtype)

    return pl.pallas_call(
        body, grid=(n_rows // tr,),
        in_specs=list(specs) + [_full_spec(c) for c in consts],
        out_specs=[pl.BlockSpec((tr, w), lambda i: (i, 0)) for w, _ in outs],
        out_shape=[jax.ShapeDtypeStruct((n_rows, w), dt) for w, dt in outs],
        name=name, compiler_params=_cparams(("parallel",)),
    )(*arrs, *consts)


def rowwise_vjp(fn, rows, consts, cts, row_grads, const_grads, *, tr, name):
    arrs, specs = zip(*[_row_spec(r, tr) for r in rows])
    n_rows = arrs[0].shape[-2]
    nr, nc = len(rows), len(consts)
    ct_present = [c for c in cts if c is not None]
    ct_arrs, ct_specs = zip(*[_row_spec(c, tr) for c in ct_present])
    add_present = [g[2] for g in row_grads if g[2] is not None]
    add_arrs, add_specs = zip(*[_row_spec(c, tr) for c in add_present]) if add_present else ((), ())
    n_ct, n_add = len(ct_present), len(add_present)
    widths = [s.block_shape[-1] for s in specs]

    def body(*refs):
        ins = refs[:nr + nc]
        ct_refs = refs[nr + nc:nr + nc + n_ct]
        add_refs = refs[nr + nc + n_ct:nr + nc + n_ct + n_add]
        out_refs = refs[nr + nc + n_ct + n_add:]
        vals = [r[...] for r in ins]
        res, f_vjp = jax.vjp(fn, *vals)
        it = iter(ct_refs)
        ct_vals = tuple(next(it)[...].astype(r.dtype) if c is not None else jnp.zeros_like(r)
                        for c, r in zip(cts, res))
        grads = f_vjp(ct_vals)
        ita = iter(add_refs)
        for o_ref, (idx, _, add) in zip(out_refs, row_grads):
            g = grads[idx].astype(F32)
            if add is not None:
                g = g + next(ita)[...].astype(F32)
            o_ref[...] = g.astype(o_ref.dtype)
        first = pl.program_id(0) == 0
        for o_ref, idx in zip(out_refs[len(row_grads):], const_grads):
            g = grads[nr + idx].astype(F32)

            @pl.when(first)
            def _():
                o_ref[...] = g

            @pl.when(jnp.logical_not(first))
            def _():
                o_ref[...] += g

    out_specs = [pl.BlockSpec((tr, widths[idx]), lambda i: (i, 0)) for idx, _, _ in row_grads]
    out_shape = [jax.ShapeDtypeStruct((n_rows, widths[idx]), dt) for idx, dt, _ in row_grads]
    out_specs += [_full_spec(consts[idx]) for idx in const_grads]
    out_shape += [jax.ShapeDtypeStruct(consts[idx].shape, F32) for idx in const_grads]
    return pl.pallas_call(
        body, grid=(n_rows // tr,),
        in_specs=list(specs) + [_full_spec(c) for c in consts] + list(ct_specs) + list(add_specs),
        out_specs=out_specs, out_shape=out_shape,
        name=name, compiler_params=_cparams(("arbitrary",)),
    )(*arrs, *consts, *ct_arrs, *add_arrs)


def f_rms(x, w):
    xf = x.astype(F32)
    return xf * lax.rsqrt(jnp.mean(xf * xf, axis=-1, keepdims=True) + NORM_EPS) * w


def f_rms_pre(x, w):
    return (f_rms(x, w),)


def f_rms_post(x, o, w):
    return (x + f_rms(o, w),)


def f_merge(u0, u1, u2, g0, g1, g2, b0, b1, b2):
    return (jax.nn.sigmoid(g0 + b0) * u0 + jax.nn.sigmoid(g1 + b1) * u1 + jax.nn.sigmoid(g2 + b2) * u2,)


def f_ssd_gate(y, z, w):
    return (f_rms(y * jax.nn.silu(z), w),)


@jax.custom_vjp
def _swap_halves(x):
    return pltpu.roll(x, HEAD // 2, 1)


_swap_halves.defvjp(lambda x: (_swap_halves(x), None), lambda _, g: (_swap_halves(g),))


def rope_tables(positions, *, tr):
    s = positions.shape[0]
    half = HEAD // 2
    inv = ROPE_BASE ** (-2.0 * jnp.arange(half, dtype=F32) / HEAD)
    inv = jnp.concatenate([inv, inv]).reshape(1, HEAD)
    sign = jnp.concatenate([-jnp.ones((half,), F32), jnp.ones((half,), F32)]).reshape(1, HEAD)

    def fn(pos, inv, sign):
        ang = pos.astype(F32) * inv
        return jnp.cos(ang), jnp.sin(ang) * sign

    return rowwise(fn, [positions.reshape(s, 1)], [inv, sign], [(HEAD, F32), (HEAD, F32)], tr=tr, name="rope_tables")


def _ret_consts(n_heads, blk):
    lg = np.log1p(-np.exp2(-5.0 - np.arange(n_heads)))[:, None, None]
    i = np.arange(blk)
    dist = np.abs(i[:, None] - i[None, :])[None]
    allowed = ((i[None, :] // CHUNK) <= (i[:, None] // CHUNK))[None]
    dm = np.where(allowed, np.exp(lg * dist), 0.0)
    qd = np.broadcast_to(np.exp(lg * (i[None, :, None] + 1.0)), (n_heads, blk, HEAD))
    kd = np.broadcast_to(np.exp(lg * (blk - 1.0 - i[None, :, None])), (n_heads, blk, HEAD))
    cd = np.broadcast_to(np.exp(lg * blk), (n_heads, 1, HEAD))
    return [jnp.asarray(a, F32) for a in (dm, qd, kd, cd)]


def _ret_block(q, k, v, g, gnw, state, cosf, sinf, dm, qd, kd, cd):
    qr = q * cosf + _swap_halves(q) * sinf
    kr = (k * cosf + _swap_halves(k) * sinf) * (HEAD ** -0.5)
    vb = v.astype(BF16)
    scores = _dot(qr.astype(BF16), kr.astype(BF16), _NT) * dm
    o = _dot(scores.astype(BF16), vb) + _dot((qr * qd).astype(BF16), state.astype(BF16))
    new_state = state * cd + _dot((kr * kd).astype(BF16), vb, _TN)
    mu = jnp.mean(o, axis=-1, keepdims=True)
    var = jnp.mean(jnp.square(o - mu), axis=-1, keepdims=True)
    y = (o - mu) * lax.rsqrt(var + NORM_EPS) * gnw * jax.nn.silu(g)
    return y, new_state


def _ret_specs(n_heads, blk, nb, cols, reverse):
    jm = (lambda j: nb - 1 - j) if reverse else (lambda j: j)
    col = lambda c0: pl.BlockSpec((blk, HEAD), lambda h, j: (jm(j), c0 + h))
    tab = pl.BlockSpec((blk, HEAD), lambda h, j: (jm(j), 0))
    specs = [col(c) for c in cols]
    specs += [pl.BlockSpec((1, HEAD), lambda h, j: (0, h)), tab, tab]
    specs += [pl.BlockSpec((None, blk, blk), lambda h, j: (h, 0, 0)),
              pl.BlockSpec((None, blk, HEAD), lambda h, j: (h, 0, 0)),
              pl.BlockSpec((None, blk, HEAD), lambda h, j: (h, 0, 0)),
              pl.BlockSpec((None, 1, HEAD), lambda h, j: (h, 0, 0))]
    state = pl.BlockSpec((None, None, HEAD, HEAD), lambda h, j: (h, jm(j), 0, 0))
    out_col = pl.BlockSpec((blk, HEAD), lambda h, j: (jm(j), h))
    return specs, state, out_col


def retention_fwd(proj, cols, gn_w, cosf, sinf, n_heads, *, blk, name):
    s = proj.shape[0]
    nb = s // blk
    consts = _ret_consts(n_heads, blk)
    specs, state_spec, out_col = _ret_specs(n_heads, blk, nb, cols, False)

    def body(q_ref, k_ref, v_ref, g_ref, gn_ref, cos_ref, sin_ref, dm_ref, qd_ref, kd_ref, cd_ref,
             y_ref, st_ref, state):
        @pl.when(pl.program_id(1) == 0)
        def _():
            state[...] = jnp.zeros_like(state)

        st = state[...]
        st_ref[...] = st
        y, new_state = _ret_block(q_ref[...], k_ref[...], v_ref[...], g_ref[...], gn_ref[...], st,
                                  cos_ref[...], sin_ref[...], dm_ref[...], qd_ref[...], kd_ref[...], cd_ref[...])
        y_ref[...] = y.astype(y_ref.dtype)
        state[...] = new_state

    return pl.pallas_call(
        body, grid=(n_heads, nb), in_specs=specs, out_specs=[out_col, state_spec],
        out_shape=[jax.ShapeDtypeStruct((s, n_heads * HEAD), BF16),
                   jax.ShapeDtypeStruct((n_heads, nb, HEAD, HEAD), F32)],
        scratch_shapes=[pltpu.VMEM((HEAD, HEAD), F32)],
        name=name, compiler_params=_cparams(("parallel", "arbitrary")),
    )(proj, proj, proj, proj, gn_w, cosf, sinf, *consts)


def retention_bwd(proj, cols, gn_w, cosf, sinf, states, dy, n_heads, *, blk, name):
    s = proj.shape[0]
    nb = s // blk
    consts = _ret_consts(n_heads, blk)
    specs, state_spec, out_col = _ret_specs(n_heads, blk, nb, cols, True)

    def body(q_ref, k_ref, v_ref, g_ref, gn_ref, cos_ref, sin_ref, dm_ref, qd_ref, kd_ref, cd_ref,
             st_ref, dy_ref, dq_ref, dk_ref, dv_ref, dg_ref, dgn_ref, dstate):
        first = pl.program_id(1) == 0

        @pl.when(first)
        def _():
            dstate[...] = jnp.zeros_like(dstate)
            dgn_ref[...] = jnp.zeros_like(dgn_ref)

        tabs = (cos_ref[...], sin_ref[...], dm_ref[...], qd_ref[...], kd_ref[...], cd_ref[...])
        fn = lambda q, k, v, g, gnw, st: _ret_block(q, k, v, g, gnw, st, *tabs)
        _, f_vjp = jax.vjp(fn, q_ref[...], k_ref[...], v_ref[...], g_ref[...], gn_ref[...], st_ref[...])
        dq, dk, dv, dg, dgn, dst = f_vjp((dy_ref[...].astype(F32), dstate[...]))
        dq_ref[...] = dq.astype(dq_ref.dtype)
        dk_ref[...] = dk.astype(dk_ref.dtype)
        dv_ref[...] = dv.astype(dv_ref.dtype)
        dg_ref[...] = dg.astype(dg_ref.dtype)
        dgn_ref[...] += dgn
        dstate[...] = dst

    o_shape = jax.ShapeDtypeStruct((s, n_heads * HEAD), BF16)
    return pl.pallas_call(
        body, grid=(n_heads, nb), in_specs=specs + [state_spec, out_col],
        out_specs=[out_col, out_col, out_col, out_col, pl.BlockSpec((1, HEAD), lambda h, j: (0, h))],
        out_shape=[o_shape, o_shape, o_shape, o_shape, jax.ShapeDtypeStruct((1, n_heads * HEAD), F32)],
        scratch_shapes=[pltpu.VMEM((HEAD, HEAD), F32)],
        name=name, compiler_params=_cparams(("parallel", "arbitrary")),
    )(proj, proj, proj, proj, gn_w, cosf, sinf, *consts, states, dy)


CONV_PAD = 8


def _conv_pre(u_ext, taps, bias, n_out):
    n = u_ext.shape[0]
    views = [pltpu.roll(u_ext, n - (k + CONV_PAD - SSD_K + 1), 0)[:n_out] for k in range(SSD_K)]
    pre = bias
    for k in range(SSD_K):
        pre = pre + taps[k] * views[k]
    return pre, views


def ssd_conv_fwd(u_pad, taps, bias, *, chunk, name):
    s, c = u_pad.shape[0] - 2 * CONV_PAD, u_pad.shape[1]

    def body(u_ref, t0, t1, t2, t3, b_ref, o_ref):
        taps_v = [t[...] for t in (t0, t1, t2, t3)]
        bias_v = b_ref[...]

        @pl.loop(0, s // chunk)
        def _(ci):
            r0 = pl.multiple_of(ci * chunk, chunk)
            pre, _ = _conv_pre(u_ref[pl.ds(r0, chunk + CONV_PAD), :], taps_v, bias_v, chunk)
            o_ref[pl.ds(r0, chunk), :] = pre * jax.nn.sigmoid(pre)

    row = pl.BlockSpec((1, LANE), lambda i: (0, i))
    return pl.pallas_call(
        body, grid=(c // LANE,),
        in_specs=[pl.BlockSpec((s + 2 * CONV_PAD, LANE), lambda i: (0, i))] + [row] * 5,
        out_specs=pl.BlockSpec((s, LANE), lambda i: (0, i)),
        out_shape=jax.ShapeDtypeStruct((s, c), F32),
        name=name, compiler_params=_cparams(("parallel",)),
    )(u_pad, *taps, bias)


def ssd_conv_bwd(u_pad, taps, bias, dxc_pad, *, chunk, name):
    s, c = u_pad.shape[0] - 2 * CONV_PAD, u_pad.shape[1]
    ext = chunk + CONV_PAD

    def body(u_ref, t0, t1, t2, t3, b_ref, d_ref, du_ref, dw_ref, db_ref):
        taps_v = [t[...] for t in (t0, t1, t2, t3)]
        bias_v = b_ref[...]
        dw_ref[...] = jnp.zeros_like(dw_ref)
        db_ref[...] = jnp.zeros_like(db_ref)

        @pl.loop(0, s // chunk)
        def _(ci):
            r0 = pl.multiple_of(ci * chunk, chunk)
            pre, views = _conv_pre(u_ref[pl.ds(r0, ext + CONV_PAD), :], taps_v, bias_v, ext)
            sig = jax.nn.sigmoid(pre)
            dpre = d_ref[pl.ds(r0, ext), :] * (sig * (1.0 + pre * (1.0 - sig)))
            du = taps_v[SSD_K - 1] * dpre[:chunk]
            for k in range(SSD_K - 1):
                du = du + taps_v[k] * pltpu.roll(dpre, ext - (SSD_K - 1 - k), 0)[:chunk]
            du_ref[pl.ds(r0, chunk), :] = du.astype(du_ref.dtype)
            own = dpre[:chunk]
            for k in range(SSD_K):
                dw_ref[k:k + 1, :] += jnp.sum(own * views[k][:chunk], axis=0, keepdims=True)
            db_ref[...] += jnp.sum(own, axis=0, keepdims=True)

    row = pl.BlockSpec((1, LANE), lambda i: (0, i))
    return pl.pallas_call(
        body, grid=(c // LANE,),
        in_specs=[pl.BlockSpec((s + 2 * CONV_PAD, LANE), lambda i: (0, i))] + [row] * 5
        + [pl.BlockSpec((s + CONV_PAD, LANE), lambda i: (0, i))],
        out_specs=[pl.BlockSpec((s, LANE), lambda i: (0, i)), pl.BlockSpec((SSD_K, LANE), lambda i: (0, i)), row],
        out_shape=[jax.ShapeDtypeStruct((s, c), BF16), jax.ShapeDtypeStruct((SSD_K, c), F32),
                   jax.ShapeDtypeStruct((1, c), F32)],
        name=name, compiler_params=_cparams(("parallel",)),
    )(u_pad, *taps, bias, dxc_pad)


def _tri_dot(tri, x, passes=3):
    out = None
    rem = x
    for _ in range(passes):
        piece = rem.astype(BF16)
        rem = rem - piece.astype(F32)
        d = _dot(tri, piece)
        out = d if out is None else out + d
    return out


def _tri(n, upper):
    rr = lax.broadcasted_iota(jnp.int32, (n, n), 0)
    cc = lax.broadcasted_iota(jnp.int32, (n, n), 1)
    return ((rr <= cc) if upper else (rr >= cc)).astype(BF16)


@jax.custom_vjp
def _cumsum_rows(a):
    return _tri_dot(_tri(a.shape[0], False), a)


_cumsum_rows.defvjp(lambda a: (_cumsum_rows(a), None), lambda _, g: (_tri_dot(_tri(g.shape[0], True), g),))


def _softplus(x):
    return jnp.maximum(x, 0.0) + jnp.log(1.0 + jnp.exp(-jnp.abs(x)))


def _ssd_block(x, bm, cm, dtraw, dtb, alog, dsk, state_t, group):
    blk, width = x.shape
    e_heads = width // SSD_P
    dt = _softplus(dtraw + dtb)
    acum = _cumsum_rows(dt * (-jnp.exp(alog)))
    acum_t = acum.T
    lane_h = lax.broadcasted_iota(jnp.int32, (1, LANE), 1)
    sub_h = lax.broadcasted_iota(jnp.int32, (LANE, 1), 0)
    lane_e = lax.broadcasted_iota(jnp.int32, (1, width), 1) // SSD_P
    causal = lax.broadcasted_iota(jnp.int32, (blk, blk), 0) >= lax.broadcasted_iota(jnp.int32, (blk, blk), 1)
    last_row = lax.broadcasted_iota(jnp.int32, (blk, 1), 0) == blk - 1
    cb = _dot(cm.astype(BF16), bm.astype(BF16), _NT)
    y = jnp.zeros((blk, width), F32)
    dt_l = jnp.zeros((blk, width), F32)
    ac_l = jnp.zeros((blk, width), F32)
    d_l = jnp.zeros((1, width), F32)
    for e in range(e_heads):
        head = group * e_heads + e
        pick = lane_h == head
        col = jnp.sum(jnp.where(pick, acum, 0.0), axis=1, keepdims=True)
        dt_e = jnp.sum(jnp.where(pick, dt, 0.0), axis=1, keepdims=True)
        d_e = jnp.sum(jnp.where(pick, dsk, 0.0), axis=1, keepdims=True)
        row = jnp.sum(jnp.where(sub_h == head, acum_t, 0.0), axis=0, keepdims=True)
        mine = lane_e == e
        decay = jnp.exp(jnp.where(causal, col - row, -jnp.inf))
        y = y + _dot((cb * decay).astype(BF16), jnp.where(mine, x * dt_e, 0.0).astype(BF16))
        dt_l = dt_l + jnp.where(mine, dt_e, 0.0)
        ac_l = ac_l + jnp.where(mine, col, 0.0)
        d_l = d_l + jnp.where(mine, d_e, 0.0)
    ac_last = jnp.sum(jnp.where(last_row, ac_l, 0.0), axis=0, keepdims=True)
    y = y + jnp.exp(ac_l) * _dot(cm.astype(BF16), state_t.astype(BF16)) + x * d_l
    inject = (x * dt_l * jnp.exp(ac_last - ac_l)).astype(BF16)
    new_state = state_t * jnp.exp(ac_last) + _dot(bm.astype(BF16), inject, _TN)
    return y, new_state


def ssd_scan_fwd(xc, proj, dt_col, dtb, alog, dsk, *, blk, name):
    s = xc.shape[0]
    nb = s // blk
    e_w = 4 * SSD_P
    b_off = SSD_G * e_w // SSD_N
    c_off = b_off + SSD_G
    row = pl.BlockSpec((1, LANE), lambda j, g: (0, 0))

    def body(x_ref, b_ref, c_ref, dt_ref, dtb_ref, al_ref, d_ref, y_ref, st_ref, state):
        j, g = pl.program_id(0), pl.program_id(1)

        @pl.when(j == 0)
        def _():
            state[g] = jnp.zeros((SSD_N, e_w), F32)

        st = state[g]
        st_ref[...] = st
        y, new_state = _ssd_block(x_ref[...], b_ref[...], c_ref[...], dt_ref[...], dtb_ref[...], al_ref[...],
                                  d_ref[...], st, g)
        y_ref[...] = y
        state[g] = new_state

    return pl.pallas_call(
        body, grid=(nb, SSD_G),
        in_specs=[pl.BlockSpec((blk, e_w), lambda j, g: (j, g)),
                  pl.BlockSpec((blk, SSD_N), lambda j, g: (j, b_off + g)),
                  pl.BlockSpec((blk, SSD_N), lambda j, g: (j, c_off + g)),
                  pl.BlockSpec((blk, LANE), lambda j, g: (j, dt_col)), row, row, row],
        out_specs=[pl.BlockSpec((blk, e_w), lambda j, g: (j, g)),
                   pl.BlockSpec((None, None, SSD_N, e_w), lambda j, g: (j, g, 0, 0))],
        out_shape=[jax.ShapeDtypeStruct((s, SSD_G * e_w), F32),
                   jax.ShapeDtypeStruct((nb, SSD_G, SSD_N, e_w), F32)],
        scratch_shapes=[pltpu.VMEM((SSD_G, SSD_N, e_w), F32)],
        name=name, compiler_params=_cparams(("arbitrary", "arbitrary")),
    )(xc, xc, xc, proj, dtb, alog, dsk)


def ssd_scan_bwd(xc, proj, dt_col, dtb, alog, dsk, states, dy, *, blk, name):
    s = xc.shape[0]
    nb = s // blk
    e_w = 4 * SSD_P
    b_off = SSD_G * e_w // SSD_N
    c_off = b_off + SSD_G
    row = pl.BlockSpec((1, LANE), lambda j, g: (0, 0))
    jm = lambda j: nb - 1 - j

    def body(x_ref, b_ref, c_ref, dt_ref, dtb_ref, al_ref, d_ref, st_ref, dy_ref,
             dx_ref, db_ref, dc_ref, ddt_ref, ddtb_ref, dal_ref, dd_ref, dstate):
        j, g = pl.program_id(0), pl.program_id(1)

        @pl.when(j == 0)
        def _():
            dstate[g] = jnp.zeros((SSD_N, e_w), F32)

        @pl.when(jnp.logical_and(j == 0, g == 0))
        def _():
            ddtb_ref[...] = jnp.zeros_like(ddtb_ref)
            dal_ref[...] = jnp.zeros_like(dal_ref)
            dd_ref[...] = jnp.zeros_like(dd_ref)

        @pl.when(g == 0)
        def _():
            ddt_ref[...] = jnp.zeros_like(ddt_ref)

        fn = functools.partial(_ssd_block, group=g)
        _, f_vjp = jax.vjp(fn, x_ref[...], b_ref[...], c_ref[...], dt_ref[...], dtb_ref[...], al_ref[...],
                           d_ref[...], st_ref[...])
        dx, db, dc, ddt, ddtb, dal, dd, dst = f_vjp((dy_ref[...], dstate[g]))
        dx_ref[...] = dx
        db_ref[...] = db
        dc_ref[...] = dc
        ddt_ref[...] += ddt
        ddtb_ref[...] += ddtb
        dal_ref[...] += dal
        dd_ref[...] += dd
        dstate[g] = dst

    return pl.pallas_call(
        body, grid=(nb, SSD_G),
        in_specs=[pl.BlockSpec((blk, e_w), lambda j, g: (jm(j), g)),
                  pl.BlockSpec((blk, SSD_N), lambda j, g: (jm(j), b_off + g)),
                  pl.BlockSpec((blk, SSD_N), lambda j, g: (jm(j), c_off + g)),
                  pl.BlockSpec((blk, LANE), lambda j, g: (jm(j), dt_col)), row, row, row,
                  pl.BlockSpec((None, None, SSD_N, e_w), lambda j, g: (jm(j), g, 0, 0)),
                  pl.BlockSpec((blk, e_w), lambda j, g: (jm(j), g))],
        out_specs=[pl.BlockSpec((blk, e_w), lambda j, g: (jm(j), g)),
                   pl.BlockSpec((blk, SSD_N), lambda j, g: (jm(j), g)),
                   pl.BlockSpec((blk, SSD_N), lambda j, g: (jm(j), g)),
                   pl.BlockSpec((blk, LANE), lambda j, g: (jm(j), 0)), row, row, row],
        out_shape=[jax.ShapeDtypeStruct((s, SSD_G * e_w), F32),
                   jax.ShapeDtypeStruct((s, SSD_G * SSD_N), F32),
                   jax.ShapeDtypeStruct((s, SSD_G * SSD_N), F32),
                   jax.ShapeDtypeStruct((s, LANE), F32)] + [jax.ShapeDtypeStruct((1, LANE), F32)] * 3,
        scratch_shapes=[pltpu.VMEM((SSD_G, SSD_N, e_w), F32)],
        name=name, compiler_params=_cparams(("arbitrary", "arbitrary")),
    )(xc, xc, xc, proj, dtb, alog, dsk, states, dy)


_NT = (((1,), (1,)), ((), ()))
_TN = (((0,), (0,)), ((), ()))


def _dot(a, b, dims=(((1,), (0,)), ((), ()))):
    return lax.dot_general(a, b, dims, preferred_element_type=F32)


def _split_dot(x, tri, passes):
    out = None
    rem = x
    for _ in range(passes):
        piece = rem.astype(BF16)
        rem = rem - piece.astype(F32)
        d = _dot(piece, tri)
        out = d if out is None else out + d
    return out


def _sb_scores(q, k_ref, j, blk, row, scale):
    kb = k_ref[pl.ds(pl.multiple_of(j * blk, blk), blk), :].astype(BF16)
    z = _dot(q, kb, _NT) * scale
    col = j * blk + lax.broadcasted_iota(jnp.int32, (blk, blk), 1)
    mask = col < row
    sp = jnp.maximum(z, 0.0) + jnp.log(1.0 + jnp.exp(-jnp.abs(z)))
    lk = jnp.where(mask, -sp, 0.0)
    return kb, mask, lk, z - sp


def sb_attention_fwd(proj, q_col, k_col, v_col, n_heads, *, blk, name):
    s = proj.shape[0]
    nq = s // blk
    scale = HEAD ** -0.5

    def body(q_ref, k_ref, v_ref, o_ref, r_ref):
        i = pl.program_id(1)
        q = q_ref[...].astype(BF16)
        row = i * blk + lax.broadcasted_iota(jnp.int32, (blk, blk), 0)
        rr = lax.broadcasted_iota(jnp.int32, (blk, blk), 0)
        cc = lax.broadcasted_iota(jnp.int32, (blk, blk), 1)
        tri_after = (rr > cc).astype(BF16)

        def step(jj, carry):
            acc, run = carry
            j = i - jj
            _, mask, lk, ls = _sb_scores(q, k_ref, j, blk, row, scale)
            later = _split_dot(lk, tri_after, 2) + run
            w = jnp.where(mask, jnp.exp(ls + later), 0.0)
            vb = v_ref[pl.ds(pl.multiple_of(j * blk, blk), blk), :].astype(BF16)
            return acc + _dot(w.astype(BF16), vb), run + jnp.sum(lk, axis=1, keepdims=True)

        acc, run = lax.fori_loop(0, i + 1, step, (jnp.zeros((blk, HEAD), F32), jnp.zeros((blk, 1), F32)))
        o_ref[...] = acc.astype(o_ref.dtype)
        r_ref[...] = jnp.broadcast_to(run, (blk, HEAD))

    blk_spec = lambda c0: pl.BlockSpec((blk, HEAD), lambda h, i: (i, c0 + h))
    full_spec = lambda c0: pl.BlockSpec((s, HEAD), lambda h, i: (0, c0 + h))
    out_spec = pl.BlockSpec((blk, HEAD), lambda h, i: (i, h))
    return pl.pallas_call(
        body, grid=(n_heads, nq),
        in_specs=[blk_spec(q_col), full_spec(k_col), full_spec(v_col)],
        out_specs=[out_spec, out_spec],
        out_shape=[jax.ShapeDtypeStruct((s, n_heads * HEAD), BF16),
                   jax.ShapeDtypeStruct((s, n_heads * HEAD), F32)],
        name=name, compiler_params=_cparams(("parallel", "arbitrary")),
    )(proj, proj, proj)


def sb_attention_bwd(proj, d_out, run_tot, q_col, k_col, v_col, n_heads, *, blk, name):
    s = proj.shape[0]
    nq = s // blk
    scale = HEAD ** -0.5

    def body(q_ref, k_ref, v_ref, do_ref, r_ref, dq_ref, dk_ref, dv_ref, dk_acc, dv_acc):
        i = pl.program_id(1)

        @pl.when(i == 0)
        def _():
            dk_acc[...] = jnp.zeros_like(dk_acc)
            dv_acc[...] = jnp.zeros_like(dv_acc)

        q = q_ref[...].astype(BF16)
        do = do_ref[...].astype(BF16)
        rtot = r_ref[:, :1]
        row = i * blk + lax.broadcasted_iota(jnp.int32, (blk, blk), 0)
        rr = lax.broadcasted_iota(jnp.int32, (blk, blk), 0)
        cc = lax.broadcasted_iota(jnp.int32, (blk, blk), 1)
        tri_upto = (rr <= cc).astype(BF16)
        tri_before = (rr < cc).astype(BF16)

        def step(j, carry):
            dq, pre, gpre = carry
            kb, mask, lk, ls = _sb_scores(q, k_ref, j, blk, row, scale)
            rows = pl.ds(pl.multiple_of(j * blk, blk), blk)
            vb = v_ref[rows, :].astype(BF16)
            later = rtot - (pre + _split_dot(lk, tri_upto, 3))
            w = jnp.where(mask, jnp.exp(ls + later), 0.0)
            g = w * _dot(do, vb, _NT)
            g_before = _split_dot(g, tri_before, 2) + gpre
            sig = jnp.exp(ls)
            dz = (jnp.where(mask, g * (1.0 - sig) - sig * g_before, 0.0) * scale).astype(BF16)
            dk_acc[rows, :] += _dot(dz, q, _TN)
            dv_acc[rows, :] += _dot(w.astype(BF16), do, _TN)
            return (dq + _dot(dz, kb), pre + jnp.sum(lk, axis=1, keepdims=True),
                    gpre + jnp.sum(g, axis=1, keepdims=True))

        zero = jnp.zeros((blk, 1), F32)
        dq, _, _ = lax.fori_loop(0, i + 1, step, (jnp.zeros((blk, HEAD), F32), zero, zero))
        dq_ref[...] = dq.astype(dq_ref.dtype)

        @pl.when(i == nq - 1)
        def _():
            dk_ref[...] = dk_acc[...].astype(dk_ref.dtype)
            dv_ref[...] = dv_acc[...].astype(dv_ref.dtype)

    blk_spec = lambda c0: pl.BlockSpec((blk, HEAD), lambda h, i: (i, c0 + h))
    full_spec = lambda c0: pl.BlockSpec((s, HEAD), lambda h, i: (0, c0 + h))
    o_shape = jax.ShapeDtypeStruct((s, n_heads * HEAD), BF16)
    return pl.pallas_call(
        body, grid=(n_heads, nq),
        in_specs=[blk_spec(q_col), full_spec(k_col), full_spec(v_col), blk_spec(0), blk_spec(0)],
        out_specs=[blk_spec(0), full_spec(0), full_spec(0)],
        out_shape=[o_shape, o_shape, o_shape],
        scratch_shapes=[pltpu.VMEM((s, HEAD), F32), pltpu.VMEM((s, HEAD), F32)],
        name=name, compiler_params=_cparams(("parallel", "arbitrary")),
    )(proj, proj, proj, d_out, run_tot)


def swiglu_fwd(gu, *, tr, name):
    nb, _, s, hb = gu.shape

    def body(gu_ref, a_ref):
        gate = gu_ref[0]
        a_ref[...] = (gate * jax.nn.sigmoid(gate) * gu_ref[1]).astype(a_ref.dtype)

    return pl.pallas_call(
        body, grid=(nb, s // tr),
        in_specs=[pl.BlockSpec((None, 2, tr, hb), lambda i, r: (i, 0, r, 0))],
        out_specs=pl.BlockSpec((None, tr, hb), lambda i, r: (i, r, 0)),
        out_shape=jax.ShapeDtypeStruct((nb, s, hb), BF16),
        name=name, compiler_params=_cparams(("parallel", "parallel")),
    )(gu)


def swiglu_bwd(gu, da, *, tr, name):
    nb, _, s, hb = gu.shape

    def body(gu_ref, da_ref, dgu_ref):
        gate, up, d = gu_ref[0], gu_ref[1], da_ref[...]
        sig = jax.nn.sigmoid(gate)
        dgu_ref[0] = (d * up * (sig * (1.0 + gate * (1.0 - sig)))).astype(dgu_ref.dtype)
        dgu_ref[1] = (d * gate * sig).astype(dgu_ref.dtype)

    return pl.pallas_call(
        body, grid=(nb, s // tr),
        in_specs=[pl.BlockSpec((None, 2, tr, hb), lambda i, r: (i, 0, r, 0)),
                  pl.BlockSpec((None, tr, hb), lambda i, r: (i, r, 0))],
        out_specs=pl.BlockSpec((None, 2, tr, hb), lambda i, r: (i, 0, r, 0)),
        out_shape=jax.ShapeDtypeStruct(gu.shape, BF16),
        name=name, compiler_params=_cparams(("parallel", "parallel")),
    )(gu, da)


def loss_head(y, target, *, tr, name):
    s, d = y.shape

    def body(y_ref, t_ref, l_ref, dy_ref):
        err = y_ref[...] - t_ref[...]
        dy_ref[...] = err * (1.0 / d)
        part = 0.5 * jnp.sum(jnp.mean(err * err, axis=-1, keepdims=True), axis=0, keepdims=True)

        @pl.when(pl.program_id(0) == 0)
        def _():
            l_ref[...] = jnp.zeros_like(l_ref)

        l_ref[...] += jnp.broadcast_to(part, l_ref.shape)

    row = pl.BlockSpec((tr, d), lambda i: (i, 0))
    return pl.pallas_call(
        body, grid=(s // tr,), in_specs=[row, row],
        out_specs=[pl.BlockSpec((8, LANE), lambda i: (0, 0)), row],
        out_shape=[jax.ShapeDtypeStruct((8, LANE), F32), jax.ShapeDtypeStruct((s, d), F32)],
        name=name, compiler_params=_cparams(("arbitrary",)),
    )(y, target)


def _adamw_math(w, g, m, v):
    m = ADAM_B1 * m + (1.0 - ADAM_B1) * g
    v = ADAM_B2 * v + (1.0 - ADAM_B2) * jnp.square(g)
    m_hat = m / (1.0 - ADAM_B1 ** ADAM_STEP)
    v_hat = v / (1.0 - ADAM_B2 ** ADAM_STEP)
    delta = -ADAM_LR * (m_hat / (jnp.sqrt(v_hat) + ADAM_EPS) + ADAM_WD * w)
    return delta, m, v


def adamw_sum(w, m, v, parts, *, tr, name):
    n, r, c = parts.shape
    tr = _pick_rows(r, tr)

    def body(w_ref, m_ref, v_ref, p_ref, g_ref, d_ref, nm_ref, nv_ref):
        g = p_ref[0]
        for i in range(1, n):
            g = g + p_ref[i]
        delta, nm, nv = _adamw_math(w_ref[...], g, m_ref[...], v_ref[...])
        g_ref[...] = g
        d_ref[...] = delta
        nm_ref[...] = nm
        nv_ref[...] = nv

    row = pl.BlockSpec((tr, c), lambda i: (i, 0))
    shape = jax.ShapeDtypeStruct((r, c), F32)
    return pl.pallas_call(
        body, grid=(r // tr,),
        in_specs=[row, row, row, pl.BlockSpec((n, tr, c), lambda i: (0, i, 0))],
        out_specs=[row] * 4, out_shape=[shape] * 4,
        name=name, compiler_params=_cparams(("parallel",)),
    )(w, m, v, parts)


def _pick_rows(r, pref):
    t = min(pref, r)
    while r % t or (t % 8 and t != r):
        t -= 1
    return t


_HBM = pl.BlockSpec(memory_space=pltpu.HBM)
_MESH = pl.DeviceIdType.MESH


def _flat_index(px, py, pc):
    return 4 * px + 2 * py + pc


def all_gather(x, *, name):
    def body(x_ref, out_ref, send_sems, recv_sems, local_sem):
        x, y, c = lax.axis_index("x"), lax.axis_index("y"), lax.axis_index("c")
        me, sibling = (x, y, c), (x, y, 1 - c)
        chips = [(1 - x, y), (x, 1 - y), (1 - x, 1 - y)]

        def slot(p):
            return out_ref.at[_flat_index(*p)]

        def copy(k, block, to, src=None):
            return pltpu.make_async_remote_copy(
                src_ref=slot(block) if src is None else src, dst_ref=slot(block),
                send_sem=send_sems.at[k], recv_sem=recv_sems.at[k], device_id=to, device_id_type=_MESH)

        mine = pltpu.make_async_copy(x_ref, slot(me), local_sem)
        mine.start()
        first = [copy(0, me, sibling, src=x_ref)]
        first += [copy(1 + j, me, (*chip, c), src=x_ref) for j, chip in enumerate(chips)]
        for cp in first:
            cp.start()
        passed = [copy(4 + j, (*chip, c), sibling) for j, chip in enumerate(chips)]
        for j, chip in enumerate(chips):
            copy(1 + j, (*chip, c), me).wait_recv()
            passed[j].start()
        copy(0, sibling, me).wait_recv()
        for j, chip in enumerate(chips):
            copy(4 + j, (*chip, 1 - c), me).wait_recv()
        for cp in first + passed:
            cp.wait_send()
        mine.wait()

    return pl.pallas_call(
        body, out_shape=jax.ShapeDtypeStruct((N_DEV,) + x.shape, x.dtype),
        in_specs=[_HBM], out_specs=_HBM,
        scratch_shapes=[pltpu.SemaphoreType.DMA((7,)), pltpu.SemaphoreType.DMA((7,)), pltpu.SemaphoreType.DMA],
        name=name,
    )(x)


def exchange(parts, *, name):
    def body(p_ref, out_ref, send_sems, recv_sems, local_sem):
        x, y, c = lax.axis_index("x"), lax.axis_index("y"), lax.axis_index("c")
        me = _flat_index(x, y, c)
        peers = [(1 - x if k & 4 else x, 1 - y if k & 2 else y, 1 - c if k & 1 else c) for k in range(1, N_DEV)]

        def copy(k, peer):
            idx = _flat_index(*peer)
            return pltpu.make_async_remote_copy(
                src_ref=p_ref.at[idx], dst_ref=out_ref.at[me],
                send_sem=send_sems.at[k], recv_sem=recv_sems.at[k], device_id=peer, device_id_type=_MESH)

        def landed(k, peer):
            idx = _flat_index(*peer)
            return pltpu.make_async_remote_copy(
                src_ref=p_ref.at[idx], dst_ref=out_ref.at[idx],
                send_sem=send_sems.at[k], recv_sem=recv_sems.at[k], device_id=peer, device_id_type=_MESH)

        mine = pltpu.make_async_copy(p_ref.at[me], out_ref.at[me], local_sem)
        mine.start()
        sends = [copy(k, peer) for k, peer in enumerate(peers)]
        for cp in sends:
            cp.start()
        for k, peer in enumerate(peers):
            landed(k, peer).wait_recv()
        for cp in sends:
            cp.wait_send()
        mine.wait()

    return pl.pallas_call(
        body, out_shape=jax.ShapeDtypeStruct(parts.shape, parts.dtype),
        in_specs=[_HBM], out_specs=_HBM,
        scratch_shapes=[pltpu.SemaphoreType.DMA((7,)), pltpu.SemaphoreType.DMA((7,)), pltpu.SemaphoreType.DMA],
        name=name,
    )(parts)


TR = 256
TR_WIDE = 128
BLK = 256
CONV_CHUNK = 512
DT_PAD = LANE

BIG = ("w_in", "w_branch", "w_out", "ffn_w_gu", "ffn_w_down")
SMALL = ("norm_mix_pre", "norm_mix_post", "norm_ffn_pre", "norm_ffn_post", "b_gate", "ret_gn_w", "ssd_conv_w",
         "ssd_conv_b", "ssd_dt_bias", "ssd_a_log", "ssd_d", "ssd_norm_w")


def _row(v):
    return v.reshape(1, -1)


def _pad_lanes(v):
    return jnp.pad(v.reshape(1, -1), ((0, 0), (0, LANE - v.shape[-1])))


def _assemble_w_in(g, d):
    full = jnp.transpose(g, (1, 0, 2)).reshape(d, -1)
    n_main, n_dt = 5 * d, full.shape[1] - 8 * d
    main, dt, gates = full[:, :n_main], full[:, n_main:n_main + n_dt], full[:, n_main + n_dt:]
    return jnp.concatenate([main, gates, dt, jnp.zeros((d, DT_PAD - n_dt), full.dtype)], axis=1)


def _split_dw_in(dw, d, n_dt):
    n_main = 5 * d
    full = jnp.concatenate([dw[:, :n_main], dw[:, 8 * d:8 * d + n_dt], dw[:, n_main:8 * d]], axis=1)
    return jnp.transpose(full.reshape(d, N_DEV, -1), (1, 0, 2))


def _layer_fwd(x, lw, cosf, sinf):
    s, d = x.shape
    heads = d // 2 // HEAD
    half_blocks = d // 2 // LANE
    (h,) = rowwise(f_rms_pre, [x], [lw["norm_mix_pre"]], [(d, BF16)], tr=TR, name="mix_pre_norm")
    proj = matmul(h, lw["w_cat"], name="in_proj")
    ret_cols = tuple(i * heads for i in range(4))
    y_ret, ret_states = retention_fwd(proj, ret_cols, lw["ret_gn_w"], cosf, sinf, heads, blk=BLK, name="retention_fwd")
    y_sb, sb_run = sb_attention_fwd(proj, 4 * heads, 5 * heads, 6 * heads, heads, blk=BLK, name="stickbreak_fwd")
    u_pad = jnp.pad(proj[:, 4 * d:5 * d], ((CONV_PAD, CONV_PAD), (0, 0)))
    xc = ssd_conv_fwd(u_pad, lw["conv_taps"], lw["ssd_conv_b"], chunk=CONV_CHUNK, name="ssd_conv_fwd")
    dt_col = 8 * d // LANE
    y_scan, ssd_states = ssd_scan_fwd(xc, proj, dt_col, lw["ssd_dt_bias"], lw["ssd_a_log"], lw["ssd_d"], blk=BLK,
                                      name="ssd_scan_fwd")
    z_spec = (proj, d // 2, 7)
    (y_ssd,) = rowwise(f_ssd_gate, [y_scan, z_spec], [lw["ssd_norm_w"]], [(d // 2, BF16)], tr=TR, name="ssd_gate_norm")
    y3 = jnp.stack([y_ret, y_sb, y_ssd])
    u3 = matmul(y3, lw["w_branch"], lead_a="batch", lead_b="batch", name="branch_proj")
    merge_rows = [(u3, d, 0, i) for i in range(3)] + [(proj, d, 5 + i) for i in range(3)]
    (merged,) = rowwise(f_merge, merge_rows, lw["b_gate"], [(d, BF16)], tr=TR_WIDE, name="gate_merge")
    o = matmul(merged, lw["w_out"], name="out_proj")
    (x1,) = rowwise(f_rms_post, [x, o], [lw["norm_mix_post"]], [(d, F32)], tr=TR, name="mix_post_norm")
    (h2,) = rowwise(f_rms_pre, [x1], [lw["norm_ffn_pre"]], [(d, BF16)], tr=TR, name="ffn_pre_norm")
    gu = matmul(h2, lw["ffn_w_gu"], lead_b="batch", name="ffn_up")
    gu = gu.reshape(N_DEV, 2, s, -1)
    act = swiglu_fwd(gu, tr=TR, name="swiglu_fwd")
    f = matmul(act, lw["ffn_w_down"], lead_a="k", lead_b="k", name="ffn_down")
    (x2,) = rowwise(f_rms_post, [x1, f], [lw["norm_ffn_post"]], [(d, F32)], tr=TR, name="ffn_post_norm")
    res = dict(x=x, h=h, proj=proj, ret_states=ret_states, sb_run=sb_run, xc=xc, y_scan=y_scan,
               ssd_states=ssd_states, y3=y3, u3=u3, merged=merged, o=o, x1=x1, h2=h2, gu=gu, act=act, f=f)
    return x2, res


def _layer_bwd(dx2, res, lw, cosf, sinf):
    x, proj = res["x"], res["proj"]
    s, d = x.shape
    heads = d // 2 // HEAD
    n_dt = d // 2 // SSD_P
    df, dn_ffn_post = rowwise_vjp(f_rms_post, [res["x1"], res["f"]], [lw["norm_ffn_post"]], [dx2],
                                  [(1, BF16, None)], [0], tr=TR, name="ffn_post_norm_bwd")
    d_act = matmul(df, lw["ffn_w_down"], tb=True, lead_b="batch", name="ffn_down_dx")
    dw_down = matmul(res["act"], df, ta=True, lead_a="batch", name="ffn_down_dw")
    dgu = swiglu_bwd(res["gu"], d_act, tr=TR, name="swiglu_bwd").reshape(2 * N_DEV, s, -1)
    dh2 = matmul(dgu, lw["ffn_w_gu"], tb=True, lead_a="k", lead_b="k", name="ffn_up_dx")
    dw_gu = matmul(res["h2"], dgu, ta=True, lead_b="batch", name="ffn_up_dw")
    dx1, dn_ffn_pre = rowwise_vjp(f_rms_pre, [res["x1"]], [lw["norm_ffn_pre"]], [dh2], [(0, F32, dx2)], [0],
                                  tr=TR, name="ffn_pre_norm_bwd")
    do, dn_mix_post = rowwise_vjp(f_rms_post, [x, res["o"]], [lw["norm_mix_post"]], [dx1], [(1, BF16, None)], [0],
                                  tr=TR, name="mix_post_norm_bwd")
    dmerged = matmul(do, lw["w_out"], tb=True, name="out_proj_dx")
    dw_out = matmul(res["merged"], do, ta=True, name="out_proj_dw")
    merge_rows = [(res["u3"], d, 0, i) for i in range(3)] + [(proj, d, 5 + i) for i in range(3)]
    mg = rowwise_vjp(f_merge, merge_rows, lw["b_gate"], [dmerged], [(i, BF16, None) for i in range(6)], [0, 1, 2],
                     tr=TR_WIDE, name="gate_merge_bwd")
    du3 = jnp.stack(mg[:3])
    d_gate_logits, db_gate = mg[3:6], mg[6:9]
    dy3 = matmul(du3, lw["w_branch"], tb=True, lead_a="batch", lead_b="batch", name="branch_proj_dx")
    dw_branch = matmul(res["y3"], du3, ta=True, lead_a="batch", lead_b="batch", name="branch_proj_dw")
    ret_cols = tuple(i * heads for i in range(4))
    dq, dk, dv, dg, d_gn = retention_bwd(proj, ret_cols, lw["ret_gn_w"], cosf, sinf, res["ret_states"], dy3[0], heads,
                                         blk=BLK, name="retention_bwd")
    dsq, dsk, dsv = sb_attention_bwd(proj, dy3[1], res["sb_run"], 4 * heads, 5 * heads, 6 * heads, heads, blk=BLK,
                                     name="stickbreak_bwd")
    z_spec = (proj, d // 2, 7)
    dy_scan, dz, d_ssd_norm = rowwise_vjp(f_ssd_gate, [res["y_scan"], z_spec], [lw["ssd_norm_w"]], [dy3[2]],
                                          [(0, F32, None), (1, BF16, None)], [0], tr=TR, name="ssd_gate_norm_bwd")
    dt_col = 8 * d // LANE
    dxs, dbm, dcm, ddt, d_dtb, d_alog, d_dskip = ssd_scan_bwd(
        res["xc"], proj, dt_col, lw["ssd_dt_bias"], lw["ssd_a_log"], lw["ssd_d"], res["ssd_states"], dy_scan, blk=BLK,
        name="ssd_scan_bwd")
    dxc_pad = jnp.pad(jnp.concatenate([dxs, dbm, dcm], axis=1), ((0, CONV_PAD), (0, 0)))
    u_pad = jnp.pad(proj[:, 4 * d:5 * d], ((CONV_PAD, CONV_PAD), (0, 0)))
    du, d_taps, d_conv_b = ssd_conv_bwd(u_pad, lw["conv_taps"], lw["ssd_conv_b"], dxc_pad, chunk=CONV_CHUNK,
                                        name="ssd_conv_bwd")
    dproj = jnp.concatenate([dq, dk, dv, dg, dsq, dsk, dsv, dz, du, *d_gate_logits, ddt.astype(BF16)], axis=1)
    dh = matmul(dproj, lw["w_cat"], tb=True, name="in_proj_dx")
    dw_cat = matmul(res["h"], dproj, ta=True, name="in_proj_dw")
    dx, dn_mix_pre = rowwise_vjp(f_rms_pre, [x], [lw["norm_mix_pre"]], [dh], [(0, F32, dx1)], [0], tr=TR,
                                 name="mix_pre_norm_bwd")
    big = dict(
        w_in=_split_dw_in(dw_cat, d, n_dt),
        w_branch=jnp.transpose(dw_branch.reshape(3, d // 2, N_DEV, -1), (2, 0, 1, 3)).reshape(N_DEV, 3 * d // 2, -1),
        w_out=dw_out.reshape(N_DEV, d // N_DEV, d),
        ffn_w_gu=dw_gu.reshape(N_DEV, 2 * d, -1),
        ffn_w_down=dw_down,
    )
    small = dict(
        norm_mix_pre=dn_mix_pre[0], norm_mix_post=dn_mix_post[0], norm_ffn_pre=dn_ffn_pre[0],
        norm_ffn_post=dn_ffn_post[0], b_gate=jnp.concatenate([b[0] for b in db_gate]), ret_gn_w=d_gn[0],
        ssd_conv_w=d_taps, ssd_conv_b=d_conv_b[0], ssd_dt_bias=d_dtb[0, :n_dt], ssd_a_log=d_alog[0, :n_dt],
        ssd_d=d_dskip[0, :n_dt], ssd_norm_w=d_ssd_norm[0],
    )
    return dx, big, small


def _adam_rows(cols):
    return max(8, (1 << 17) // cols // 8 * 8)


def kernel(x, positions, norm_mix_pre, norm_mix_post, norm_ffn_pre, norm_ffn_post, w_in, b_gate, ret_gn_w, ssd_conv_w, ssd_conv_b, ssd_dt_bias, ssd_a_log, ssd_d, ssd_norm_w, w_branch_ret, w_branch_sb, w_branch_ssd, w_out, ffn_w_gate, ffn_w_up, ffn_w_down, loss_target, m_norm_mix_pre, m_norm_mix_post, m_norm_ffn_pre, m_norm_ffn_post, m_w_in, m_b_gate, m_ret_gn_w, m_ssd_conv_w, m_ssd_conv_b, m_ssd_dt_bias, m_ssd_a_log, m_ssd_d, m_ssd_norm_w, m_w_branch_ret, m_w_branch_sb, m_w_branch_ssd, m_w_out, m_ffn_w_gate, m_ffn_w_up, m_ffn_w_down, v_norm_mix_pre, v_norm_mix_post, v_norm_ffn_pre, v_norm_ffn_post, v_w_in, v_b_gate, v_ret_gn_w, v_ssd_conv_w, v_ssd_conv_b, v_ssd_dt_bias, v_ssd_a_log, v_ssd_d, v_ssd_norm_w, v_w_branch_ret, v_w_branch_sb, v_w_branch_ssd, v_w_out, v_ffn_w_gate, v_ffn_w_up, v_ffn_w_down):
    depth = w_in.shape[0]
    s, d = x.shape[1], x.shape[2]
    axes = ("x", "y", "c")
    me = _flat_index(lax.axis_index("x"), lax.axis_index("y"), lax.axis_index("c"))

    def big_shards(w_in_, br_ret, br_sb, br_ssd, w_out_, gate, up, down):
        return dict(w_in=w_in_, w_branch=jnp.concatenate([br_ret, br_sb, br_ssd], axis=1), w_out=w_out_,
                    ffn_w_gu=jnp.concatenate([gate, up], axis=1), ffn_w_down=down)

    big_w = big_shards(w_in, w_branch_ret, w_branch_sb, w_branch_ssd, w_out, ffn_w_gate, ffn_w_up, ffn_w_down)
    big_m = big_shards(m_w_in, m_w_branch_ret, m_w_branch_sb, m_w_branch_ssd, m_w_out, m_ffn_w_gate, m_ffn_w_up, m_ffn_w_down)
    big_v = big_shards(v_w_in, v_w_branch_ret, v_w_branch_sb, v_w_branch_ssd, v_w_out, v_ffn_w_gate, v_ffn_w_up, v_ffn_w_down)

    taps_all = all_gather(ssd_conv_w.reshape(-1, LANE), name="gather_conv_w")
    taps_all = jnp.transpose(taps_all.reshape(N_DEV, depth, SSD_K, -1), (1, 2, 0, 3)).reshape(depth, SSD_K, -1)

    cosf, sinf = rope_tables(positions.reshape(s), tr=TR)
    small_w = dict(norm_mix_pre=norm_mix_pre, norm_mix_post=norm_mix_post, norm_ffn_pre=norm_ffn_pre,
                   norm_ffn_post=norm_ffn_post, b_gate=b_gate, ret_gn_w=ret_gn_w, ssd_conv_b=ssd_conv_b,
                   ssd_dt_bias=ssd_dt_bias, ssd_a_log=ssd_a_log, ssd_d=ssd_d, ssd_norm_w=ssd_norm_w, taps=taps_all)

    def layer_weights(sw, gathered):
        lw = dict(gathered)
        for n in ("norm_mix_pre", "norm_mix_post", "norm_ffn_pre", "norm_ffn_post", "ret_gn_w", "ssd_conv_b", "ssd_norm_w"):
            lw[n] = _row(sw[n])
        for n in ("ssd_dt_bias", "ssd_a_log", "ssd_d"):
            lw[n] = _pad_lanes(sw[n])
        lw["b_gate"] = [_row(sw["b_gate"][i * d:(i + 1) * d]) for i in range(3)]
        lw["conv_taps"] = [sw["taps"][k:k + 1] for k in range(SSD_K)]
        return lw

    def fwd_body(xc, layer):
        sw, bw = layer
        g = {n: all_gather(bw[n].astype(BF16), name="gather_" + n) for n in BIG}
        gathered = dict(
            w_cat=_assemble_w_in(g["w_in"], d),
            w_branch=jnp.transpose(g["w_branch"].reshape(N_DEV, 3, d // 2, -1), (1, 2, 0, 3)).reshape(3, d // 2, d),
            w_out=g["w_out"].reshape(d, d),
            ffn_w_gu=g["ffn_w_gu"].reshape(2 * N_DEV, d, -1),
            ffn_w_down=g["ffn_w_down"],
        )
        x2, res = _layer_fwd(xc, layer_weights(sw, gathered), cosf, sinf)
        return x2, (res, gathered)

    def shard2d(a):
        return a.reshape(-1, a.shape[-1])

    def bwd_body(dx2, layer):
        sw, bw, bm, bv, res, gathered = layer
        dx, big_g, small_g = _layer_bwd(dx2, res, layer_weights(sw, gathered), cosf, sinf)
        outs = {}
        for n in BIG:
            parts = exchange(big_g[n], name="exchange_" + n)
            w2, m2, v2 = shard2d(bw[n]), shard2d(bm[n]), shard2d(bv[n])
            parts = parts.reshape(N_DEV, *w2.shape)
            outs[n] = adamw_sum(w2, m2, v2, parts, tr=_adam_rows(w2.shape[1]), name="adamw_" + n)
        return dx, (outs, small_g)

    def per_layer_big(t):
        return dict(w_in=t["w_in"],
                    w_branch=t["w_branch"].reshape(depth, 3, d // 2, -1),
                    w_out=t["w_out"],
                    ffn_w_gu=t["ffn_w_gu"].reshape(depth, 2, d, -1),
                    ffn_w_down=t["ffn_w_down"])

    big_w, big_m, big_v = per_layer_big(big_w), per_layer_big(big_m), per_layer_big(big_v)
    y, (res_all, gathered_all) = lax.scan(fwd_body, x.reshape(s, d), (small_w, big_w))
    loss_tile, dy = loss_head(y, loss_target.reshape(s, d), tr=TR, name="loss_head")
    loss = lax.psum(loss_tile[0, 0], axes)
    dx, (big_out, small_g) = lax.scan(bwd_body, dy, (small_w, big_w, big_m, big_v, res_all, gathered_all), reverse=True)

    n_dt = ssd_dt_bias.shape[-1]
    small_in = dict(norm_mix_pre=(norm_mix_pre, m_norm_mix_pre, v_norm_mix_pre), norm_mix_post=(norm_mix_post, m_norm_mix_post, v_norm_mix_post),
                    norm_ffn_pre=(norm_ffn_pre, m_norm_ffn_pre, v_norm_ffn_pre), norm_ffn_post=(norm_ffn_post, m_norm_ffn_post, v_norm_ffn_post),
                    b_gate=(b_gate, m_b_gate, v_b_gate), ret_gn_w=(ret_gn_w, m_ret_gn_w, v_ret_gn_w),
                    ssd_conv_b=(ssd_conv_b, m_ssd_conv_b, v_ssd_conv_b), ssd_dt_bias=(ssd_dt_bias, m_ssd_dt_bias, v_ssd_dt_bias),
                    ssd_a_log=(ssd_a_log, m_ssd_a_log, v_ssd_a_log), ssd_d=(ssd_d, m_ssd_d, v_ssd_d),
                    ssd_norm_w=(ssd_norm_w, m_ssd_norm_w, v_ssd_norm_w))
    rep = [n for n in SMALL if n != "ssd_conv_w"]

    def pack(arrs):
        flat = jnp.concatenate([a.reshape(-1) for a in arrs])
        rows = -(-flat.shape[0] // (8 * LANE)) * 8
        return jnp.pad(flat, (0, rows * LANE - flat.shape[0])).reshape(rows, LANE)

    conv_g = small_g["ssd_conv_w"]
    g_pack = pack([small_g[n] for n in rep] + [conv_g])
    g_all = all_gather(g_pack, name="gather_small_grads")
    zeros_conv = jnp.zeros_like(conv_g)
    w_pack, m_pack, v_pack = (pack([small_in[n][i] for n in rep] + [zeros_conv]) for i in range(3))
    sm = adamw_sum(w_pack, m_pack, v_pack, g_all, tr=TR, name="adamw_small")

    def unpack(p):
        flat, out, off = p.reshape(-1), {}, 0
        for n in rep:
            shp = small_in[n][0].shape
            size = math.prod(shp)
            out[n] = flat[off:off + size].reshape(shp)
            off += size
        out["conv_sum"] = flat[off:off + conv_g.size].reshape(conv_g.shape)
        return out

    sm = [unpack(p) for p in sm]
    ch = ssd_conv_w.shape[-1]
    conv_mine = lax.dynamic_slice_in_dim(sm[0]["conv_sum"], me * ch, ch, axis=2)
    conv_out = adamw_sum(ssd_conv_w.reshape(-1, LANE), m_ssd_conv_w.reshape(-1, LANE), v_ssd_conv_w.reshape(-1, LANE),
                         conv_mine.reshape(1, -1, LANE), tr=TR, name="adamw_conv_w")
    for i in range(4):
        sm[i]["ssd_conv_w"] = conv_out[i].reshape(ssd_conv_w.shape)

    def big_named(i):
        o = {n: big_out[n][i] for n in BIG}
        br = o["w_branch"].reshape(depth, 3, d // 2, -1)
        gu = o["ffn_w_gu"].reshape(depth, 2, d, -1)
        return dict(w_in=o["w_in"].reshape(w_in.shape), w_branch_ret=br[:, 0], w_branch_sb=br[:, 1], w_branch_ssd=br[:, 2],
                    w_out=o["w_out"].reshape(w_out.shape), ffn_w_gate=gu[:, 0], ffn_w_up=gu[:, 1],
                    ffn_w_down=o["ffn_w_down"].reshape(ffn_w_down.shape))

    order = ["norm_mix_pre", "norm_mix_post", "norm_ffn_pre", "norm_ffn_post", "w_in", "b_gate", "ret_gn_w", "ssd_conv_w",
             "ssd_conv_b", "ssd_dt_bias", "ssd_a_log", "ssd_d", "ssd_norm_w", "w_branch_ret", "w_branch_sb", "w_branch_ssd",
             "w_out", "ffn_w_gate", "ffn_w_up", "ffn_w_down"]
    outs = [loss, dx.reshape(x.shape)]
    for i in range(4):
        named = {**sm[i], **big_named(i)}
        outs += [named[n] for n in order]
    return tuple(outs)
```

```python
import functools
import math

import jax
import jax.numpy as jnp
import numpy as np
from jax import lax
from jax.experimental import pallas as pl
from jax.experimental.pallas import tpu as pltpu

F32 = jnp.float32
BF16 = jnp.bfloat16

N_DEV = 8
HEAD = 128
SSD_P = 64
SSD_G = 4
SSD_N = 128
SSD_K = 4
CHUNK = 64
NORM_EPS = 1e-6
ROPE_BASE = 10000.0
LANE = 128
VMEM_LIMIT = 56 * 1024 * 1024

ADAM_LR, ADAM_B1, ADAM_B2, ADAM_EPS, ADAM_WD, ADAM_STEP = 0.001, 0.9, 0.999, 1e-08, 0.01, 10


def _cparams(sem):
    return pltpu.CompilerParams(dimension_semantics=sem, vmem_limit_bytes=VMEM_LIMIT)


def _pick(n, pref):
    if n <= pref:
        return n
    t = pref
    while t >= LANE:
        if n % t == 0:
            return t
        t -= LANE
    return n


def matmul(a, b, *, ta=False, tb=False, lead_a=None, lead_b=None, out_dtype=F32, add=None,
           tm=1024, tn=1024, tk=512, name="mm"):
    la, lb = lead_a is not None, lead_b is not None
    a2, b2 = a.shape[1:] if la else a.shape, b.shape[1:] if lb else b.shape
    (kd_a, m) = a2 if ta else a2[::-1]
    (kd_b, n) = b2[::-1] if tb else b2
    assert kd_a == kd_b, (a.shape, b.shape)
    nlead = a.shape[0] if la else (b.shape[0] if lb else 1)
    batch = "batch" in (lead_a, lead_b)
    kblocks = nlead if "k" in (lead_a, lead_b) else 1
    if la and lb:
        assert lead_a == lead_b and a.shape[0] == b.shape[0]
    tm, tn, tk = _pick(m, tm), _pick(n, tn), _pick(kd_a, tk)
    kt = kd_a // tk
    nk = kt * kblocks
    grid = ((nlead if batch else 1), m // tm, n // tn, nk)

    def lead_idx(g, k):
        return g if batch else k // kt

    def a_map(g, i, j, k):
        idx = (k % kt, i) if ta else (i, k % kt)
        return ((lead_idx(g, k),) + idx) if la else idx

    def b_map(g, i, j, k):
        idx = (j, k % kt) if tb else (k % kt, j)
        return ((lead_idx(g, k),) + idx) if lb else idx

    a_blk = (tk, tm) if ta else (tm, tk)
    b_blk = (tn, tk) if tb else (tk, tn)
    a_spec = pl.BlockSpec(((None,) + a_blk) if la else a_blk, a_map)
    b_spec = pl.BlockSpec(((None,) + b_blk) if lb else b_blk, b_map)
    if batch:
        o_spec = pl.BlockSpec((None, tm, tn), lambda g, i, j, k: (g, i, j))
        o_shape = jax.ShapeDtypeStruct((nlead, m, n), out_dtype)
    else:
        o_spec = pl.BlockSpec((tm, tn), lambda g, i, j, k: (i, j))
        o_shape = jax.ShapeDtypeStruct((m, n), out_dtype)
    dims = (((0 if ta else 1,), (1 if tb else 0,)), ((), ()))

    def body(a_ref, b_ref, *rest):
        c_ref = rest[0] if add is not None else None
        o_ref, acc_ref = rest[-2:]
        k = pl.program_id(3)

        @pl.when(k == 0)
        def _():
            acc_ref[...] = jnp.zeros_like(acc_ref) if c_ref is None else c_ref[...].astype(F32)

        acc_ref[...] += lax.dot_general(a_ref[...].astype(BF16), b_ref[...].astype(BF16), dims,
                                        preferred_element_type=F32)

        @pl.when(k == nk - 1)
        def _():
            o_ref[...] = acc_ref[...].astype(o_ref.dtype)

    extra = [] if add is None else [add]
    extra_specs = [] if add is None else [o_spec]
    return pl.pallas_call(
        body, out_shape=o_shape, grid=grid, in_specs=[a_spec, b_spec] + extra_specs, out_specs=o_spec,
        scratch_shapes=[pltpu.VMEM((tm, tn), F32)], name=name,
        compiler_params=_cparams(("parallel", "parallel", "parallel", "arbitrary")),
    )(a, b, *extra)


def _row_spec(r, tr):
    if not isinstance(r, tuple):
        return r, pl.BlockSpec((tr, r.shape[-1]), lambda i: (i, 0))
    if len(r) == 3:
        arr, w, cb = r
        return arr, pl.BlockSpec((tr, w), lambda i: (i, cb))
    arr, w, cb, ld = r
    return arr, pl.BlockSpec((None, tr, w), lambda i: (ld, i, cb))


def _full_spec(c):
    nd = c.ndim
    return pl.BlockSpec(c.shape, lambda i: (0,) * nd)


def rowwise(fn, rows, consts, outs, *, tr, name):
    arrs, specs = zip(*[_row_spec(r, tr) for r in rows])
    n_rows = arrs[0].shape[-2]
    nr, nc = len(rows), len(consts)

    def body(*refs):
        vals = [r[...] for r in refs[:nr + nc]]
        res = fn(*vals)
        for o_ref, r in zip(refs[nr + nc:], res):
            o_ref[...] = r.astype(o_ref.dtype)

    return pl.pallas_call(
        body, grid=(n_rows // tr,),
        in_specs=list(specs) + [_full_spec(c) for c in consts],
        out_specs=[pl.BlockSpec((tr, w), lambda i: (i, 0)) for w, _ in outs],
        out_shape=[jax.ShapeDtypeStruct((n_rows, w), dt) for w, dt in outs],
        name=name, compiler_params=_cparams(("parallel",)),
    )(*arrs, *consts)


def rowwise_vjp(fn, rows, consts, cts, row_grads, const_grads, *, tr, name):
    arrs, specs = zip(*[_row_spec(r, tr) for r in rows])
    n_rows = arrs[0].shape[-2]
    nr, nc = len(rows), len(consts)
    ct_present = [c for c in cts if c is not None]
    ct_arrs, ct_specs = zip(*[_row_spec(c, tr) for c in ct_present])
    add_present = [g[2] for g in row_grads if g[2] is not None]
    add_arrs, add_specs = zip(*[_row_spec(c, tr) for c in add_present]) if add_present else ((), ())
    n_ct, n_add = len(ct_present), len(add_present)
    widths = [s.block_shape[-1] for s in specs]

    def body(*refs):
        ins = refs[:nr + nc]
        ct_refs = refs[nr + nc:nr + nc + n_ct]
        add_refs = refs[nr + nc + n_ct:nr + nc + n_ct + n_add]
        out_refs = refs[nr + nc + n_ct + n_add:]
        vals = [r[...] for r in ins]
        res, f_vjp = jax.vjp(fn, *vals)
        it = iter(ct_refs)
        ct_vals = tuple(next(it)[...].astype(r.dtype) if c is not None else jnp.zeros_like(r)
                        for c, r in zip(cts, res))
        grads = f_vjp(ct_vals)
        ita = iter(add_refs)
        for o_ref, (idx, _, add) in zip(out_refs, row_grads):
            g = grads[idx].astype(F32)
            if add is not None:
                g = g + next(ita)[...].astype(F32)
            o_ref[...] = g.astype(o_ref.dtype)
        first = pl.program_id(0) == 0
        for o_ref, idx in zip(out_refs[len(row_grads):], const_grads):
            g = grads[nr + idx].astype(F32)

            @pl.when(first)
            def _():
                o_ref[...] = g

            @pl.when(jnp.logical_not(first))
            def _():
                o_ref[...] += g

    out_specs = [pl.BlockSpec((tr, widths[idx]), lambda i: (i, 0)) for idx, _, _ in row_grads]
    out_shape = [jax.ShapeDtypeStruct((n_rows, widths[idx]), dt) for idx, dt, _ in row_grads]
    out_specs += [_full_spec(consts[idx]) for idx in const_grads]
    out_shape += [jax.ShapeDtypeStruct(consts[idx].shape, F32) for idx in const_grads]
    return pl.pallas_call(
        body, grid=(n_rows // tr,),
        in_specs=list(specs) + [_full_spec(c) for c in consts] + list(ct_specs) + list(add_specs),
        out_specs=out_specs, out_shape=out_shape,
        name=name, compiler_params=_cparams(("arbitrary",)),
    )(*arrs, *consts, *ct_arrs, *add_arrs)


def f_rms(x, w):
    xf = x.astype(F32)
    return xf * lax.rsqrt(jnp.mean(xf * xf, axis=-1, keepdims=True) + NORM_EPS) * w


def f_rms_pre(x, w):
    return (f_rms(x, w),)


def f_rms_post(x, o, w):
    return (x + f_rms(o, w),)


def f_merge(u0, u1, u2, g0, g1, g2, b0, b1, b2):
    return (jax.nn.sigmoid(g0 + b0) * u0 + jax.nn.sigmoid(g1 + b1) * u1 + jax.nn.sigmoid(g2 + b2) * u2,)


def f_ssd_gate(y, z, w):
    return (f_rms(y * jax.nn.silu(z), w),)


@jax.custom_vjp
def _swap_halves(x):
    return pltpu.roll(x, HEAD // 2, 1)


_swap_halves.defvjp(lambda x: (_swap_halves(x), None), lambda _, g: (_swap_halves(g),))


def rope_tables(positions, *, tr):
    s = positions.shape[0]
    half = HEAD // 2
    inv = ROPE_BASE ** (-2.0 * jnp.arange(half, dtype=F32) / HEAD)
    inv = jnp.concatenate([inv, inv]).reshape(1, HEAD)
    sign = jnp.concatenate([-jnp.ones((half,), F32), jnp.ones((half,), F32)]).reshape(1, HEAD)

    def fn(pos, inv, sign):
        ang = pos.astype(F32) * inv
        return jnp.cos(ang), jnp.sin(ang) * sign

    return rowwise(fn, [positions.reshape(s, 1)], [inv, sign], [(HEAD, F32), (HEAD, F32)], tr=tr, name="rope_tables")


def _ret_consts(n_heads, blk):
    lg = np.log1p(-np.exp2(-5.0 - np.arange(n_heads)))[:, None, None]
    i = np.arange(blk)
    dist = np.abs(i[:, None] - i[None, :])[None]
    allowed = ((i[None, :] // CHUNK) <= (i[:, None] // CHUNK))[None]
    dm = np.where(allowed, np.exp(lg * dist), 0.0)
    qd = np.broadcast_to(np.exp(lg * (i[None, :, None] + 1.0)), (n_heads, blk, HEAD))
    kd = np.broadcast_to(np.exp(lg * (blk - 1.0 - i[None, :, None])), (n_heads, blk, HEAD))
    cd = np.broadcast_to(np.exp(lg * blk), (n_heads, 1, HEAD))
    return [jnp.asarray(a, F32) for a in (dm, qd, kd, cd)]


def _ret_block(q, k, v, g, gnw, state, cosf, sinf, dm, qd, kd, cd):
    qr = q * cosf + _swap_halves(q) * sinf
    kr = (k * cosf + _swap_halves(k) * sinf) * (HEAD ** -0.5)
    vb = v.astype(BF16)
    scores = _dot(qr.astype(BF16), kr.astype(BF16), _NT) * dm
    o = _dot(scores.astype(BF16), vb) + _dot((qr * qd).astype(BF16), state.astype(BF16))
    new_state = state * cd + _dot((kr * kd).astype(BF16), vb, _TN)
    mu = jnp.mean(o, axis=-1, keepdims=True)
    var = jnp.mean(jnp.square(o - mu), axis=-1, keepdims=True)
    y = (o - mu) * lax.rsqrt(var + NORM_EPS) * gnw * jax.nn.silu(g)
    return y, new_state


def _ret_specs(n_heads, blk, nb, cols, reverse):
    jm = (lambda j: nb - 1 - j) if reverse else (lambda j: j)
    col = lambda c0: pl.BlockSpec((blk, HEAD), lambda h, j: (jm(j), c0 + h))
    tab = pl.BlockSpec((blk, HEAD), lambda h, j: (jm(j), 0))
    specs = [col(c) for c in cols]
    specs += [pl.BlockSpec((1, HEAD), lambda h, j: (0, h)), tab, tab]
    specs += [pl.BlockSpec((None, blk, blk), lambda h, j: (h, 0, 0)),
              pl.BlockSpec((None, blk, HEAD), lambda h, j: (h, 0, 0)),
              pl.BlockSpec((None, blk, HEAD), lambda h, j: (h, 0, 0)),
              pl.BlockSpec((None, 1, HEAD), lambda h, j: (h, 0, 0))]
    state = pl.BlockSpec((None, None, HEAD, HEAD), lambda h, j: (h, jm(j), 0, 0))
    out_col = pl.BlockSpec((blk, HEAD), lambda h, j: (jm(j), h))
    return specs, state, out_col


def retention_fwd(proj, cols, gn_w, cosf, sinf, n_heads, *, blk, name):
    s = proj.shape[0]
    nb = s // blk
    consts = _ret_consts(n_heads, blk)
    specs, state_spec, out_col = _ret_specs(n_heads, blk, nb, cols, False)

    def body(q_ref, k_ref, v_ref, g_ref, gn_ref, cos_ref, sin_ref, dm_ref, qd_ref, kd_ref, cd_ref,
             y_ref, st_ref, state):
        @pl.when(pl.program_id(1) == 0)
        def _():
            state[...] = jnp.zeros_like(state)

        st = state[...]
        st_ref[...] = st
        y, new_state = _ret_block(q_ref[...], k_ref[...], v_ref[...], g_ref[...], gn_ref[...], st,
                                  cos_ref[...], sin_ref[...], dm_ref[...], qd_ref[...], kd_ref[...], cd_ref[...])
        y_ref[...] = y.astype(y_ref.dtype)
        state[...] = new_state

    return pl.pallas_call(
        body, grid=(n_heads, nb), in_specs=specs, out_specs=[out_col, state_spec],
        out_shape=[jax.ShapeDtypeStruct((s, n_heads * HEAD), BF16),
                   jax.ShapeDtypeStruct((n_heads, nb, HEAD, HEAD), F32)],
        scratch_shapes=[pltpu.VMEM((HEAD, HEAD), F32)],
        name=name, compiler_params=_cparams(("parallel", "arbitrary")),
    )(proj, proj, proj, proj, gn_w, cosf, sinf, *consts)


def retention_bwd(proj, cols, gn_w, cosf, sinf, states, dy, n_heads, *, blk, name):
    s = proj.shape[0]
    nb = s // blk
    consts = _ret_consts(n_heads, blk)
    specs, state_spec, out_col = _ret_specs(n_heads, blk, nb, cols, True)

    def body(q_ref, k_ref, v_ref, g_ref, gn_ref, cos_ref, sin_ref, dm_ref, qd_ref, kd_ref, cd_ref,
             st_ref, dy_ref, dq_ref, dk_ref, dv_ref, dg_ref, dgn_ref, dstate):
        first = pl.program_id(1) == 0

        @pl.when(first)
        def _():
            dstate[...] = jnp.zeros_like(dstate)
            dgn_ref[...] = jnp.zeros_like(dgn_ref)

        tabs = (cos_ref[...], sin_ref[...], dm_ref[...], qd_ref[...], kd_ref[...], cd_ref[...])
        fn = lambda q, k, v, g, gnw, st: _ret_block(q, k, v, g, gnw, st, *tabs)
        _, f_vjp = jax.vjp(fn, q_ref[...], k_ref[...], v_ref[...], g_ref[...], gn_ref[...], st_ref[...])
        dq, dk, dv, dg, dgn, dst = f_vjp((dy_ref[...].astype(F32), dstate[...]))
        dq_ref[...] = dq.astype(dq_ref.dtype)
        dk_ref[...] = dk.astype(dk_ref.dtype)
        dv_ref[...] = dv.astype(dv_ref.dtype)
        dg_ref[...] = dg.astype(dg_ref.dtype)
        dgn_ref[...] += dgn
        dstate[...] = dst

    o_shape = jax.ShapeDtypeStruct((s, n_heads * HEAD), BF16)
    return pl.pallas_call(
        body, grid=(n_heads, nb), in_specs=specs + [state_spec, out_col],
        out_specs=[out_col, out_col, out_col, out_col, pl.BlockSpec((1, HEAD), lambda h, j: (0, h))],
        out_shape=[o_shape, o_shape, o_shape, o_shape, jax.ShapeDtypeStruct((1, n_heads * HEAD), F32)],
        scratch_shapes=[pltpu.VMEM((HEAD, HEAD), F32)],
        name=name, compiler_params=_cparams(("parallel", "arbitrary")),
    )(proj, proj, proj, proj, gn_w, cosf, sinf, *consts, states, dy)


CONV_PAD = 8


def _conv_pre(u_ext, taps, bias, n_out):
    n = u_ext.shape[0]
    views = [pltpu.roll(u_ext, n - (k + CONV_PAD - SSD_K + 1), 0)[:n_out] for k in range(SSD_K)]
    pre = bias
    for k in range(SSD_K):
        pre = pre + taps[k] * views[k]
    return pre, views


def ssd_conv_fwd(u_pad, taps, bias, *, chunk, name):
    s, c = u_pad.shape[0] - 2 * CONV_PAD, u_pad.shape[1]

    def body(u_ref, t0, t1, t2, t3, b_ref, o_ref):
        taps_v = [t[...] for t in (t0, t1, t2, t3)]
        bias_v = b_ref[...]

        @pl.loop(0, s // chunk)
        def _(ci):
            r0 = pl.multiple_of(ci * chunk, chunk)
            pre, _ = _conv_pre(u_ref[pl.ds(r0, chunk + CONV_PAD), :], taps_v, bias_v, chunk)
            o_ref[pl.ds(r0, chunk), :] = pre * jax.nn.sigmoid(pre)

    row = pl.BlockSpec((1, LANE), lambda i: (0, i))
    return pl.pallas_call(
        body, grid=(c // LANE,),
        in_specs=[pl.BlockSpec((s + 2 * CONV_PAD, LANE), lambda i: (0, i))] + [row] * 5,
        out_specs=pl.BlockSpec((s, LANE), lambda i: (0, i)),
        out_shape=jax.ShapeDtypeStruct((s, c), F32),
        name=name, compiler_params=_cparams(("parallel",)),
    )(u_pad, *taps, bias)


def ssd_conv_bwd(u_pad, taps, bias, dxc_pad, *, chunk, name):
    s, c = u_pad.shape[0] - 2 * CONV_PAD, u_pad.shape[1]
    ext = chunk + CONV_PAD

    def body(u_ref, t0, t1, t2, t3, b_ref, d_ref, du_ref, dw_ref, db_ref):
        taps_v = [t[...] for t in (t0, t1, t2, t3)]
        bias_v = b_ref[...]
        dw_ref[...] = jnp.zeros_like(dw_ref)
        db_ref[...] = jnp.zeros_like(db_ref)

        @pl.loop(0, s // chunk)
        def _(ci):
            r0 = pl.multiple_of(ci * chunk, chunk)
            pre, views = _conv_pre(u_ref[pl.ds(r0, ext + CONV_PAD), :], taps_v, bias_v, ext)
            sig = jax.nn.sigmoid(pre)
            dpre = d_ref[pl.ds(r0, ext), :] * (sig * (1.0 + pre * (1.0 - sig)))
            du = taps_v[SSD_K - 1] * dpre[:chunk]
            for k in range(SSD_K - 1):
                du = du + taps_v[k] * pltpu.roll(dpre, ext - (SSD_K - 1 - k), 0)[:chunk]
            du_ref[pl.ds(r0, chunk), :] = du.astype(du_ref.dtype)
            own = dpre[:chunk]
            for k in range(SSD_K):
                dw_ref[k:k + 1, :] += jnp.sum(own * views[k][:chunk], axis=0, keepdims=True)
            db_ref[...] += jnp.sum(own, axis=0, keepdims=True)

    row = pl.BlockSpec((1, LANE), lambda i: (0, i))
    return pl.pallas_call(
        body, grid=(c // LANE,),
        in_specs=[pl.BlockSpec((s + 2 * CONV_PAD, LANE), lambda i: (0, i))] + [row] * 5
        + [pl.BlockSpec((s + CONV_PAD, LANE), lambda i: (0, i))],
        out_specs=[pl.BlockSpec((s, LANE), lambda i: (0, i)), pl.BlockSpec((SSD_K, LANE), lambda i: (0, i)), row],
        out_shape=[jax.ShapeDtypeStruct((s, c), BF16), jax.ShapeDtypeStruct((SSD_K, c), F32),
                   jax.ShapeDtypeStruct((1, c), F32)],
        name=name, compiler_params=_cparams(("parallel",)),
    )(u_pad, *taps, bias, dxc_pad)


def _tri_dot(tri, x, passes=3):
    out = None
    rem = x
    for _ in range(passes):
        piece = rem.astype(BF16)
        rem = rem - piece.astype(F32)
        d = _dot(tri, piece)
        out = d if out is None else out + d
    return out


def _tri(n, upper):
    rr = lax.broadcasted_iota(jnp.int32, (n, n), 0)
    cc = lax.broadcasted_iota(jnp.int32, (n, n), 1)
    return ((rr <= cc) if upper else (rr >= cc)).astype(BF16)


@jax.custom_vjp
def _cumsum_rows(a):
    return _tri_dot(_tri(a.shape[0], False), a)


_cumsum_rows.defvjp(lambda a: (_cumsum_rows(a), None), lambda _, g: (_tri_dot(_tri(g.shape[0], True), g),))


def _softplus(x):
    return jnp.maximum(x, 0.0) + jnp.log(1.0 + jnp.exp(-jnp.abs(x)))


def _ssd_block(x, bm, cm, dtraw, dtb, alog, dsk, state_t, group):
    blk, width = x.shape
    e_heads = width // SSD_P
    dt = _softplus(dtraw + dtb)
    acum = _cumsum_rows(dt * (-jnp.exp(alog)))
    acum_t = acum.T
    lane_h = lax.broadcasted_iota(jnp.int32, (1, LANE), 1)
    sub_h = lax.broadcasted_iota(jnp.int32, (LANE, 1), 0)
    lane_e = lax.broadcasted_iota(jnp.int32, (1, width), 1) // SSD_P
    causal = lax.broadcasted_iota(jnp.int32, (blk, blk), 0) >= lax.broadcasted_iota(jnp.int32, (blk, blk), 1)
    last_row = lax.broadcasted_iota(jnp.int32, (blk, 1), 0) == blk - 1
    cb = _dot(cm.astype(BF16), bm.astype(BF16), _NT)
    y = jnp.zeros((blk, width), F32)
    dt_l = jnp.zeros((blk, width), F32)
    ac_l = jnp.zeros((blk, width), F32)
    d_l = jnp.zeros((1, width), F32)
    for e in range(e_heads):
        head = group * e_heads + e
        pick = lane_h == head
        col = jnp.sum(jnp.where(pick, acum, 0.0), axis=1, keepdims=True)
        dt_e = jnp.sum(jnp.where(pick, dt, 0.0), axis=1, keepdims=True)
        d_e = jnp.sum(jnp.where(pick, dsk, 0.0), axis=1, keepdims=True)
        row = jnp.sum(jnp.where(sub_h == head, acum_t, 0.0), axis=0, keepdims=True)
        mine = lane_e == e
        decay = jnp.exp(jnp.where(causal, col - row, -jnp.inf))
        y = y + _dot((cb * decay).astype(BF16), jnp.where(mine, x * dt_e, 0.0).astype(BF16))
        dt_l = dt_l + jnp.where(mine, dt_e, 0.0)
        ac_l = ac_l + jnp.where(mine, col, 0.0)
        d_l = d_l + jnp.where(mine, d_e, 0.0)
    ac_last = jnp.sum(jnp.where(last_row, ac_l, 0.0), axis=0, keepdims=True)
    y = y + jnp.exp(ac_l) * _dot(cm.astype(BF16), state_t.astype(BF16)) + x * d_l
    inject = (x * dt_l * jnp.exp(ac_last - ac_l)).astype(BF16)
    new_state = state_t * jnp.exp(ac_last) + _dot(bm.astype(BF16), inject, _TN)
    return y, new_state


def ssd_scan_fwd(xc, proj, dt_col, dtb, alog, dsk, *, blk, name):
    s = xc.shape[0]
    nb = s // blk
    e_w = 4 * SSD_P
    b_off = SSD_G * e_w // SSD_N
    c_off = b_off + SSD_G
    row = pl.BlockSpec((1, LANE), lambda j, g: (0, 0))

    def body(x_ref, b_ref, c_ref, dt_ref, dtb_ref, al_ref, d_ref, y_ref, st_ref, state):
        j, g = pl.program_id(0), pl.program_id(1)

        @pl.when(j == 0)
        def _():
            state[g] = jnp.zeros((SSD_N, e_w), F32)

        st = state[g]
        st_ref[...] = st
        y, new_state = _ssd_block(x_ref[...], b_ref[...], c_ref[...], dt_ref[...], dtb_ref[...], al_ref[...],
                                  d_ref[...], st, g)
        y_ref[...] = y
        state[g] = new_state

    return pl.pallas_call(
        body, grid=(nb, SSD_G),
        in_specs=[pl.BlockSpec((blk, e_w), lambda j, g: (j, g)),
                  pl.BlockSpec((blk, SSD_N), lambda j, g: (j, b_off + g)),
                  pl.BlockSpec((blk, SSD_N), lambda j, g: (j, c_off + g)),
                  pl.BlockSpec((blk, LANE), lambda j, g: (j, dt_col)), row, row, row],
        out_specs=[pl.BlockSpec((blk, e_w), lambda j, g: (j, g)),
                   pl.BlockSpec((None, None, SSD_N, e_w), lambda j, g: (j, g, 0, 0))],
        out_shape=[jax.ShapeDtypeStruct((s, SSD_G * e_w), F32),
                   jax.ShapeDtypeStruct((nb, SSD_G, SSD_N, e_w), F32)],
        scratch_shapes=[pltpu.VMEM((SSD_G, SSD_N, e_w), F32)],
        name=name, compiler_params=_cparams(("arbitrary", "arbitrary")),
    )(xc, xc, xc, proj, dtb, alog, dsk)


def ssd_scan_bwd(xc, proj, dt_col, dtb, alog, dsk, states, dy, *, blk, name):
    s = xc.shape[0]
    nb = s // blk
    e_w = 4 * SSD_P
    b_off = SSD_G * e_w // SSD_N
    c_off = b_off + SSD_G
    row = pl.BlockSpec((1, LANE), lambda j, g: (0, 0))
    jm = lambda j: nb - 1 - j

    def body(x_ref, b_ref, c_ref, dt_ref, dtb_ref, al_ref, d_ref, st_ref, dy_ref,
             dx_ref, db_ref, dc_ref, ddt_ref, ddtb_ref, dal_ref, dd_ref, dstate):
        j, g = pl.program_id(0), pl.program_id(1)

        @pl.when(j == 0)
        def _():
            dstate[g] = jnp.zeros((SSD_N, e_w), F32)

        @pl.when(jnp.logical_and(j == 0, g == 0))
        def _():
            ddtb_ref[...] = jnp.zeros_like(ddtb_ref)
            dal_ref[...] = jnp.zeros_like(dal_ref)
            dd_ref[...] = jnp.zeros_like(dd_ref)

        @pl.when(g == 0)
        def _():
            ddt_ref[...] = jnp.zeros_like(ddt_ref)

        fn = functools.partial(_ssd_block, group=g)
        _, f_vjp = jax.vjp(fn, x_ref[...], b_ref[...], c_ref[...], dt_ref[...], dtb_ref[...], al_ref[...],
                           d_ref[...], st_ref[...])
        dx, db, dc, ddt, ddtb, dal, dd, dst = f_vjp((dy_ref[...], dstate[g]))
        dx_ref[...] = dx
        db_ref[...] = db
        dc_ref[...] = dc
        ddt_ref[...] += ddt
        ddtb_ref[...] += ddtb
        dal_ref[...] += dal
        dd_ref[...] += dd
        dstate[g] = dst

    return pl.pallas_call(
        body, grid=(nb, SSD_G),
        in_specs=[pl.BlockSpec((blk, e_w), lambda j, g: (jm(j), g)),
                  pl.BlockSpec((blk, SSD_N), lambda j, g: (jm(j), b_off + g)),
                  pl.BlockSpec((blk, SSD_N), lambda j, g: (jm(j), c_off + g)),
                  pl.BlockSpec((blk, LANE), lambda j, g: (jm(j), dt_col)), row, row, row,
                  pl.BlockSpec((None, None, SSD_N, e_w), lambda j, g: (jm(j), g, 0, 0)),
                  pl.BlockSpec((blk, e_w), lambda j, g: (jm(j), g))],
        out_specs=[pl.BlockSpec((blk, e_w), lambda j, g: (jm(j), g)),
                   pl.BlockSpec((blk, SSD_N), lambda j, g: (jm(j), g)),
                   pl.BlockSpec((blk, SSD_N), lambda j, g: (jm(j), g)),
                   pl.BlockSpec((blk, LANE), lambda j, g: (jm(j), 0)), row, row, row],
        out_shape=[jax.ShapeDtypeStruct((s, SSD_G * e_w), F32),
                   jax.ShapeDtypeStruct((s, SSD_G * SSD_N), F32),
                   jax.ShapeDtypeStruct((s, SSD_G * SSD_N), F32),
                   jax.ShapeDtypeStruct((s, LANE), F32)] + [jax.ShapeDtypeStruct((1, LANE), F32)] * 3,
        scratch_shapes=[pltpu.VMEM((SSD_G, SSD_N, e_w), F32)],
        name=name, compiler_params=_cparams(("arbitrary", "arbitrary")),
    )(xc, xc, xc, proj, dtb, alog, dsk, states, dy)


SB_DEAD = -104.0

_NT = (((1,), (1,)), ((), ()))
_TN = (((0,), (0,)), ((), ()))


def _dot(a, b, dims=(((1,), (0,)), ((), ()))):
    return lax.dot_general(a, b, dims, preferred_element_type=F32)


def _split_dot(x, tri, passes):
    out = None
    rem = x
    for _ in range(passes):
        piece = rem.astype(BF16)
        rem = rem - piece.astype(F32)
        d = _dot(piece, tri)
        out = d if out is None else out + d
    return out


def _sb_scores(q, k_ref, j, blk, row, scale):
    kb = k_ref[pl.ds(pl.multiple_of(j * blk, blk), blk), :].astype(BF16)
    z = _dot(q, kb, _NT) * scale
    col = j * blk + lax.broadcasted_iota(jnp.int32, (blk, blk), 1)
    mask = col < row
    sp = jnp.maximum(z, 0.0) + jnp.log(1.0 + jnp.exp(-jnp.abs(z)))
    lk = jnp.where(mask, -sp, 0.0)
    return kb, mask, lk, z - sp


def sb_attention_fwd(proj, q_col, k_col, v_col, n_heads, *, blk, name):
    s = proj.shape[0]
    nq = s // blk
    scale = HEAD ** -0.5

    def body(q_ref, k_ref, v_ref, o_ref, r_ref, n_ref):
        i = pl.program_id(1)
        q = q_ref[...].astype(BF16)
        row = i * blk + lax.broadcasted_iota(jnp.int32, (blk, blk), 0)
        rr = lax.broadcasted_iota(jnp.int32, (blk, blk), 0)
        cc = lax.broadcasted_iota(jnp.int32, (blk, blk), 1)
        tri_after = (rr > cc).astype(BF16)

        def alive(carry):
            jj, _, run = carry
            return jnp.logical_and(jj <= i, jnp.max(run) > SB_DEAD)

        def step(carry):
            jj, acc, run = carry
            j = i - jj
            _, mask, lk, ls = _sb_scores(q, k_ref, j, blk, row, scale)
            later = _split_dot(lk, tri_after, 2) + run
            w = jnp.where(mask, jnp.exp(ls + later), 0.0)
            vb = v_ref[pl.ds(pl.multiple_of(j * blk, blk), blk), :].astype(BF16)
            return jj + 1, acc + _dot(w.astype(BF16), vb), run + jnp.sum(lk, axis=1, keepdims=True)

        n, acc, run = lax.while_loop(
            alive, step, (jnp.int32(0), jnp.zeros((blk, HEAD), F32), jnp.zeros((blk, 1), F32)))
        o_ref[...] = acc.astype(o_ref.dtype)
        r_ref[...] = jnp.broadcast_to(run, (blk, HEAD))
        n_ref[pl.program_id(0), i] = n

    blk_spec = lambda c0: pl.BlockSpec((blk, HEAD), lambda h, i: (i, c0 + h))
    full_spec = lambda c0: pl.BlockSpec((s, HEAD), lambda h, i: (0, c0 + h))
    out_spec = pl.BlockSpec((blk, HEAD), lambda h, i: (i, h))
    return pl.pallas_call(
        body, grid=(n_heads, nq),
        in_specs=[blk_spec(q_col), full_spec(k_col), full_spec(v_col)],
        out_specs=[out_spec, out_spec, pl.BlockSpec(memory_space=pltpu.SMEM)],
        out_shape=[jax.ShapeDtypeStruct((s, n_heads * HEAD), BF16),
                   jax.ShapeDtypeStruct((s, n_heads * HEAD), F32),
                   jax.ShapeDtypeStruct((n_heads, nq), jnp.int32)],
        name=name, compiler_params=_cparams(("arbitrary", "arbitrary")),
    )(proj, proj, proj)


def sb_attention_bwd(proj, d_out, run_tot, visited, q_col, k_col, v_col, n_heads, *, blk, name):
    s = proj.shape[0]
    nq = s // blk
    scale = HEAD ** -0.5

    def body(n_ref, q_ref, k_ref, v_ref, do_ref, r_ref, dq_ref, dk_ref, dv_ref, dk_acc, dv_acc):
        i = pl.program_id(1)
        first = i + 1 - jnp.clip(n_ref[pl.program_id(0), i], 1, i + 1)

        @pl.when(i == 0)
        def _():
            dk_acc[...] = jnp.zeros_like(dk_acc)
            dv_acc[...] = jnp.zeros_like(dv_acc)

        q = q_ref[...].astype(BF16)
        do = do_ref[...].astype(BF16)
        rtot = r_ref[:, :1]
        row = i * blk + lax.broadcasted_iota(jnp.int32, (blk, blk), 0)
        rr = lax.broadcasted_iota(jnp.int32, (blk, blk), 0)
        cc = lax.broadcasted_iota(jnp.int32, (blk, blk), 1)
        tri_upto = (rr <= cc).astype(BF16)
        tri_before = (rr < cc).astype(BF16)

        def step(j, carry):
            dq, pre, gpre = carry
            kb, mask, lk, ls = _sb_scores(q, k_ref, j, blk, row, scale)
            rows = pl.ds(pl.multiple_of(j * blk, blk), blk)
            vb = v_ref[rows, :].astype(BF16)
            later = rtot - (pre + _split_dot(lk, tri_upto, 3))
            w = jnp.where(mask, jnp.exp(ls + later), 0.0)
            g = w * _dot(do, vb, _NT)
            g_before = _split_dot(g, tri_before, 2) + gpre
            sig = jnp.exp(ls)
            dz = (jnp.where(mask, g * (1.0 - sig) - sig * g_before, 0.0) * scale).astype(BF16)
            dk_acc[rows, :] += _dot(dz, q, _TN)
            dv_acc[rows, :] += _dot(w.astype(BF16), do, _TN)
            return (dq + _dot(dz, kb), pre + jnp.sum(lk, axis=1, keepdims=True),
                    gpre + jnp.sum(g, axis=1, keepdims=True))

        zero = jnp.zeros((blk, 1), F32)
        dq, _, _ = lax.fori_loop(first, i + 1, step, (jnp.zeros((blk, HEAD), F32), zero, zero))
        dq_ref[...] = dq.astype(dq_ref.dtype)

        @pl.when(i == nq - 1)
        def _():
            dk_ref[...] = dk_acc[...].astype(dk_ref.dtype)
            dv_ref[...] = dv_acc[...].astype(dv_ref.dtype)

    blk_spec = lambda c0: pl.BlockSpec((blk, HEAD), lambda h, i: (i, c0 + h))
    full_spec = lambda c0: pl.BlockSpec((s, HEAD), lambda h, i: (0, c0 + h))
    o_shape = jax.ShapeDtypeStruct((s, n_heads * HEAD), BF16)
    return pl.pallas_call(
        body, grid=(n_heads, nq),
        in_specs=[pl.BlockSpec(memory_space=pltpu.SMEM), blk_spec(q_col), full_spec(k_col), full_spec(v_col),
                  blk_spec(0), blk_spec(0)],
        out_specs=[blk_spec(0), full_spec(0), full_spec(0)],
        out_shape=[o_shape, o_shape, o_shape],
        scratch_shapes=[pltpu.VMEM((s, HEAD), F32), pltpu.VMEM((s, HEAD), F32)],
        name=name, compiler_params=_cparams(("parallel", "arbitrary")),
    )(visited, proj, proj, proj, d_out, run_tot)


def swiglu_fwd(gu, *, tr, name):
    nb, _, s, hb = gu.shape

    def body(gu_ref, a_ref):
        gate = gu_ref[0]
        a_ref[...] = (gate * jax.nn.sigmoid(gate) * gu_ref[1]).astype(a_ref.dtype)

    return pl.pallas_call(
        body, grid=(nb, s // tr),
        in_specs=[pl.BlockSpec((None, 2, tr, hb), lambda i, r: (i, 0, r, 0))],
        out_specs=pl.BlockSpec((None, tr, hb), lambda i, r: (i, r, 0)),
        out_shape=jax.ShapeDtypeStruct((nb, s, hb), BF16),
        name=name, compiler_params=_cparams(("parallel", "parallel")),
    )(gu)


def swiglu_bwd(gu, da, *, tr, name):
    nb, _, s, hb = gu.shape

    def body(gu_ref, da_ref, dgu_ref):
        gate, up, d = gu_ref[0], gu_ref[1], da_ref[...]
        sig = jax.nn.sigmoid(gate)
        dgu_ref[0] = (d * up * (sig * (1.0 + gate * (1.0 - sig)))).astype(dgu_ref.dtype)
        dgu_ref[1] = (d * gate * sig).astype(dgu_ref.dtype)

    return pl.pallas_call(
        body, grid=(nb, s // tr),
        in_specs=[pl.BlockSpec((None, 2, tr, hb), lambda i, r: (i, 0, r, 0)),
                  pl.BlockSpec((None, tr, hb), lambda i, r: (i, r, 0))],
        out_specs=pl.BlockSpec((None, 2, tr, hb), lambda i, r: (i, 0, r, 0)),
        out_shape=jax.ShapeDtypeStruct(gu.shape, BF16),
        name=name, compiler_params=_cparams(("parallel", "parallel")),
    )(gu, da)


def loss_head(y, target, *, tr, name):
    s, d = y.shape

    def body(y_ref, t_ref, l_ref, dy_ref):
        err = y_ref[...] - t_ref[...]
        dy_ref[...] = err * (1.0 / d)
        part = 0.5 * jnp.sum(jnp.mean(err * err, axis=-1, keepdims=True), axis=0, keepdims=True)

        @pl.when(pl.program_id(0) == 0)
        def _():
            l_ref[...] = jnp.zeros_like(l_ref)

        l_ref[...] += jnp.broadcast_to(part, l_ref.shape)

    row = pl.BlockSpec((tr, d), lambda i: (i, 0))
    return pl.pallas_call(
        body, grid=(s // tr,), in_specs=[row, row],
        out_specs=[pl.BlockSpec((8, LANE), lambda i: (0, 0)), row],
        out_shape=[jax.ShapeDtypeStruct((8, LANE), F32), jax.ShapeDtypeStruct((s, d), F32)],
        name=name, compiler_params=_cparams(("arbitrary",)),
    )(y, target)


def _adamw_math(w, g, m, v):
    m = ADAM_B1 * m + (1.0 - ADAM_B1) * g
    v = ADAM_B2 * v + (1.0 - ADAM_B2) * jnp.square(g)
    m_hat = m / (1.0 - ADAM_B1 ** ADAM_STEP)
    v_hat = v / (1.0 - ADAM_B2 ** ADAM_STEP)
    delta = -ADAM_LR * (m_hat / (jnp.sqrt(v_hat) + ADAM_EPS) + ADAM_WD * w)
    return delta, m, v


def adamw_sum(w, m, v, parts, *, tr, name):
    n, r, c = parts.shape
    tr = _pick_rows(r, tr)

    def body(w_ref, m_ref, v_ref, p_ref, g_ref, d_ref, nm_ref, nv_ref):
        g = p_ref[0].astype(F32)
        for i in range(1, n):
            g = g + p_ref[i].astype(F32)
        delta, nm, nv = _adamw_math(w_ref[...], g, m_ref[...], v_ref[...])
        g_ref[...] = g
        d_ref[...] = delta
        nm_ref[...] = nm
        nv_ref[...] = nv

    row = pl.BlockSpec((tr, c), lambda i: (i, 0))
    shape = jax.ShapeDtypeStruct((r, c), F32)
    return pl.pallas_call(
        body, grid=(r // tr,),
        in_specs=[row, row, row, pl.BlockSpec((n, tr, c), lambda i: (0, i, 0))],
        out_specs=[row] * 4, out_shape=[shape] * 4,
        name=name, compiler_params=_cparams(("parallel",)),
    )(w, m, v, parts)


def _pick_rows(r, pref):
    t = min(pref, r)
    while r % t or (t % 16 and t != r):
        t -= 1
    return t


_HBM = pl.BlockSpec(memory_space=pltpu.HBM)
_MESH = pl.DeviceIdType.MESH


def _flat_index(px, py, pc):
    return 4 * px + 2 * py + pc


def all_gather(x, *, name):
    def body(x_ref, out_ref, send_sems, recv_sems, local_sem):
        x, y, c = lax.axis_index("x"), lax.axis_index("y"), lax.axis_index("c")
        me, sibling = (x, y, c), (x, y, 1 - c)
        chips = [(1 - x, y), (x, 1 - y), (1 - x, 1 - y)]

        def slot(p):
            return out_ref.at[_flat_index(*p)]

        def copy(k, block, to, src=None):
            return pltpu.make_async_remote_copy(
                src_ref=slot(block) if src is None else src, dst_ref=slot(block),
                send_sem=send_sems.at[k], recv_sem=recv_sems.at[k], device_id=to, device_id_type=_MESH)

        mine = pltpu.make_async_copy(x_ref, slot(me), local_sem)
        mine.start()
        first = [copy(0, me, sibling, src=x_ref)]
        first += [copy(1 + j, me, (*chip, c), src=x_ref) for j, chip in enumerate(chips)]
        for cp in first:
            cp.start()
        passed = [copy(4 + j, (*chip, c), sibling) for j, chip in enumerate(chips)]
        for j, chip in enumerate(chips):
            copy(1 + j, (*chip, c), me).wait_recv()
            passed[j].start()
        copy(0, sibling, me).wait_recv()
        for j, chip in enumerate(chips):
            copy(4 + j, (*chip, 1 - c), me).wait_recv()
        for cp in first + passed:
            cp.wait_send()
        mine.wait()

    return pl.pallas_call(
        body, out_shape=jax.ShapeDtypeStruct((N_DEV,) + x.shape, x.dtype),
        in_specs=[_HBM], out_specs=_HBM,
        scratch_shapes=[pltpu.SemaphoreType.DMA((7,)), pltpu.SemaphoreType.DMA((7,)), pltpu.SemaphoreType.DMA],
        name=name,
    )(x)


def exchange(parts, *, name):
    def body(p_ref, out_ref, send_sems, recv_sems, local_sem):
        x, y, c = lax.axis_index("x"), lax.axis_index("y"), lax.axis_index("c")
        me = _flat_index(x, y, c)
        peers = [(1 - x if k & 4 else x, 1 - y if k & 2 else y, 1 - c if k & 1 else c) for k in range(1, N_DEV)]

        def copy(k, peer):
            idx = _flat_index(*peer)
            return pltpu.make_async_remote_copy(
                src_ref=p_ref.at[idx], dst_ref=out_ref.at[me],
                send_sem=send_sems.at[k], recv_sem=recv_sems.at[k], device_id=peer, device_id_type=_MESH)

        def landed(k, peer):
            idx = _flat_index(*peer)
            return pltpu.make_async_remote_copy(
                src_ref=p_ref.at[idx], dst_ref=out_ref.at[idx],
                send_sem=send_sems.at[k], recv_sem=recv_sems.at[k], device_id=peer, device_id_type=_MESH)

        mine = pltpu.make_async_copy(p_ref.at[me], out_ref.at[me], local_sem)
        mine.start()
        sends = [copy(k, peer) for k, peer in enumerate(peers)]
        for cp in sends:
            cp.start()
        for k, peer in enumerate(peers):
            landed(k, peer).wait_recv()
        for cp in sends:
            cp.wait_send()
        mine.wait()

    return pl.pallas_call(
        body, out_shape=jax.ShapeDtypeStruct(parts.shape, parts.dtype),
        in_specs=[_HBM], out_specs=_HBM,
        scratch_shapes=[pltpu.SemaphoreType.DMA((7,)), pltpu.SemaphoreType.DMA((7,)), pltpu.SemaphoreType.DMA],
        name=name,
    )(parts)


TR = 256
TR_WIDE = 128
BLK = 256
CONV_CHUNK = 512
DT_PAD = LANE

BIG = ("w_in", "w_branch", "w_out", "ffn_w_gu", "ffn_w_down")
SMALL = ("norm_mix_pre", "norm_mix_post", "norm_ffn_pre", "norm_ffn_post", "b_gate", "ret_gn_w", "ssd_conv_w",
         "ssd_conv_b", "ssd_dt_bias", "ssd_a_log", "ssd_d", "ssd_norm_w")


def _row(v):
    return v.reshape(1, -1)


def _pad_lanes(v):
    return jnp.pad(v.reshape(1, -1), ((0, 0), (0, LANE - v.shape[-1])))


def _assemble_w_in(g, d):
    full = jnp.transpose(g, (1, 0, 2)).reshape(d, -1)
    n_main, n_dt = 5 * d, full.shape[1] - 8 * d
    main, dt, gates = full[:, :n_main], full[:, n_main:n_main + n_dt], full[:, n_main + n_dt:]
    return jnp.concatenate([main, gates], axis=1), jnp.pad(dt, ((0, 0), (0, DT_PAD - n_dt)))


def _split_dw_in(dw, dw_dt, d, n_dt):
    n_main = 5 * d
    full = jnp.concatenate([dw[:, :n_main], dw_dt[:, :n_dt], dw[:, n_main:]], axis=1)
    return jnp.transpose(full.reshape(d, N_DEV, -1), (1, 0, 2))


def _layer_fwd(x, lw, cosf, sinf):
    s, d = x.shape
    heads = d // 2 // HEAD
    (h,) = rowwise(f_rms_pre, [x], [lw["norm_mix_pre"]], [(d, BF16)], tr=TR, name="mix_pre_norm")
    proj = matmul(h, lw["w_cat"], tn=2048, name="in_proj")
    dt_raw = matmul(h, lw["w_dt"], name="in_proj_dt")
    ret_cols = tuple(i * heads for i in range(4))
    y_ret, ret_states = retention_fwd(proj, ret_cols, lw["ret_gn_w"], cosf, sinf, heads, blk=BLK, name="retention_fwd")
    y_sb, sb_run, sb_visited = sb_attention_fwd(proj, 4 * heads, 5 * heads, 6 * heads, heads, blk=BLK,
                                                name="stickbreak_fwd")
    u_pad = jnp.pad(proj[:, 4 * d:5 * d], ((CONV_PAD, CONV_PAD), (0, 0)))
    xc = ssd_conv_fwd(u_pad, lw["conv_taps"], lw["ssd_conv_b"], chunk=CONV_CHUNK, name="ssd_conv_fwd")
    y_scan, ssd_states = ssd_scan_fwd(xc, dt_raw, 0, lw["ssd_dt_bias"], lw["ssd_a_log"], lw["ssd_d"], blk=BLK,
                                      name="ssd_scan_fwd")
    z_spec = (proj, d // 2, 7)
    (y_ssd,) = rowwise(f_ssd_gate, [y_scan, z_spec], [lw["ssd_norm_w"]], [(d // 2, BF16)], tr=TR, name="ssd_gate_norm")
    y3 = jnp.stack([y_ret, y_sb, y_ssd])
    u3 = matmul(y3, lw["w_branch"], lead_a="batch", lead_b="batch", name="branch_proj")
    merge_rows = [(u3, d, 0, i) for i in range(3)] + [(proj, d, 5 + i) for i in range(3)]
    (merged,) = rowwise(f_merge, merge_rows, lw["b_gate"], [(d, BF16)], tr=TR_WIDE, name="gate_merge")
    o = matmul(merged, lw["w_out"], name="out_proj")
    (x1,) = rowwise(f_rms_post, [x, o], [lw["norm_mix_post"]], [(d, F32)], tr=TR, name="mix_post_norm")
    (h2,) = rowwise(f_rms_pre, [x1], [lw["norm_ffn_pre"]], [(d, BF16)], tr=TR, name="ffn_pre_norm")
    gu = matmul(h2, lw["ffn_w_gu"], lead_b="batch", name="ffn_up")
    gu = gu.reshape(N_DEV, 2, s, -1)
    act = swiglu_fwd(gu, tr=TR, name="swiglu_fwd")
    f = matmul(act, lw["ffn_w_down"], lead_a="k", lead_b="k", name="ffn_down")
    (x2,) = rowwise(f_rms_post, [x1, f], [lw["norm_ffn_post"]], [(d, F32)], tr=TR, name="ffn_post_norm")
    res = dict(x=x, h=h, proj=proj, dt_raw=dt_raw, ret_states=ret_states, sb_run=sb_run, sb_visited=sb_visited, xc=xc,
               y_scan=y_scan, ssd_states=ssd_states, y3=y3, u3=u3, merged=merged, o=o, x1=x1, h2=h2, gu=gu, act=act, f=f)
    return x2, res


def _layer_bwd(dx2, res, lw, cosf, sinf):
    x, proj = res["x"], res["proj"]
    s, d = x.shape
    heads = d // 2 // HEAD
    n_dt = d // 2 // SSD_P
    df, dn_ffn_post = rowwise_vjp(f_rms_post, [res["x1"], res["f"]], [lw["norm_ffn_post"]], [dx2],
                                  [(1, BF16, None)], [0], tr=TR, name="ffn_post_norm_bwd")
    d_act = matmul(df, lw["ffn_w_down"], tb=True, lead_b="batch", name="ffn_down_dx")
    dw_down = matmul(res["act"], df, ta=True, lead_a="batch", name="ffn_down_dw")
    dgu = swiglu_bwd(res["gu"], d_act, tr=TR, name="swiglu_bwd").reshape(2 * N_DEV, s, -1)
    dh2 = matmul(dgu, lw["ffn_w_gu"], tb=True, lead_a="k", lead_b="k", name="ffn_up_dx")
    dw_gu = matmul(res["h2"], dgu, ta=True, lead_b="batch", name="ffn_up_dw")
    dx1, dn_ffn_pre = rowwise_vjp(f_rms_pre, [res["x1"]], [lw["norm_ffn_pre"]], [dh2], [(0, F32, dx2)], [0],
                                  tr=TR, name="ffn_pre_norm_bwd")
    do, dn_mix_post = rowwise_vjp(f_rms_post, [x, res["o"]], [lw["norm_mix_post"]], [dx1], [(1, BF16, None)], [0],
                                  tr=TR, name="mix_post_norm_bwd")
    dmerged = matmul(do, lw["w_out"], tb=True, name="out_proj_dx")
    dw_out = matmul(res["merged"], do, ta=True, name="out_proj_dw")
    merge_rows = [(res["u3"], d, 0, i) for i in range(3)] + [(proj, d, 5 + i) for i in range(3)]
    mg = rowwise_vjp(f_merge, merge_rows, lw["b_gate"], [dmerged], [(i, BF16, None) for i in range(6)], [0, 1, 2],
                     tr=TR_WIDE, name="gate_merge_bwd")
    du3 = jnp.stack(mg[:3])
    d_gate_logits, db_gate = mg[3:6], mg[6:9]
    dy3 = matmul(du3, lw["w_branch"], tb=True, lead_a="batch", lead_b="batch", name="branch_proj_dx")
    dw_branch = matmul(res["y3"], du3, ta=True, lead_a="batch", lead_b="batch", name="branch_proj_dw")
    ret_cols = tuple(i * heads for i in range(4))
    dq, dk, dv, dg, d_gn = retention_bwd(proj, ret_cols, lw["ret_gn_w"], cosf, sinf, res["ret_states"], dy3[0], heads,
                                         blk=BLK, name="retention_bwd")
    dsq, dsk, dsv = sb_attention_bwd(proj, dy3[1], res["sb_run"], res["sb_visited"], 4 * heads, 5 * heads, 6 * heads,
                                     heads, blk=BLK, name="stickbreak_bwd")
    z_spec = (proj, d // 2, 7)
    dy_scan, dz, d_ssd_norm = rowwise_vjp(f_ssd_gate, [res["y_scan"], z_spec], [lw["ssd_norm_w"]], [dy3[2]],
                                          [(0, F32, None), (1, BF16, None)], [0], tr=TR, name="ssd_gate_norm_bwd")
    dxs, dbm, dcm, ddt, d_dtb, d_alog, d_dskip = ssd_scan_bwd(
        res["xc"], res["dt_raw"], 0, lw["ssd_dt_bias"], lw["ssd_a_log"], lw["ssd_d"], res["ssd_states"], dy_scan,
        blk=BLK, name="ssd_scan_bwd")
    dxc_pad = jnp.pad(jnp.concatenate([dxs, dbm, dcm], axis=1), ((0, CONV_PAD), (0, 0)))
    u_pad = jnp.pad(proj[:, 4 * d:5 * d], ((CONV_PAD, CONV_PAD), (0, 0)))
    du, d_taps, d_conv_b = ssd_conv_bwd(u_pad, lw["conv_taps"], lw["ssd_conv_b"], dxc_pad, chunk=CONV_CHUNK,
                                        name="ssd_conv_bwd")
    dproj = jnp.concatenate([dq, dk, dv, dg, dsq, dsk, dsv, dz, du, *d_gate_logits], axis=1)
    dh_dt = matmul(ddt, lw["w_dt"], tb=True, name="in_proj_dt_dx")
    dh = matmul(dproj, lw["w_cat"], tb=True, add=dh_dt, name="in_proj_dx")
    dw_cat = matmul(res["h"], dproj, ta=True, tn=2048, name="in_proj_dw")
    dw_dt = matmul(res["h"], ddt, ta=True, name="in_proj_dt_dw")
    dx, dn_mix_pre = rowwise_vjp(f_rms_pre, [x], [lw["norm_mix_pre"]], [dh], [(0, F32, dx1)], [0], tr=TR,
                                 name="mix_pre_norm_bwd")
    big = dict(
        w_in=_split_dw_in(dw_cat, dw_dt, d, n_dt),
        w_branch=jnp.transpose(dw_branch.reshape(3, d // 2, N_DEV, -1), (2, 0, 1, 3)).reshape(N_DEV, 3 * d // 2, -1),
        w_out=dw_out.reshape(N_DEV, d // N_DEV, d),
        ffn_w_gu=dw_gu.reshape(N_DEV, 2 * d, -1),
        ffn_w_down=dw_down,
    )
    small = dict(
        norm_mix_pre=dn_mix_pre[0], norm_mix_post=dn_mix_post[0], norm_ffn_pre=dn_ffn_pre[0],
        norm_ffn_post=dn_ffn_post[0], b_gate=jnp.concatenate([b[0] for b in db_gate]), ret_gn_w=d_gn[0],
        ssd_conv_w=d_taps, ssd_conv_b=d_conv_b[0], ssd_dt_bias=d_dtb[0, :n_dt], ssd_a_log=d_alog[0, :n_dt],
        ssd_d=d_dskip[0, :n_dt], ssd_norm_w=d_ssd_norm[0],
    )
    return dx, big, small


def _adam_rows(cols):
    return max(8, (1 << 17) // cols // 8 * 8)


def kernel(x, positions, norm_mix_pre, norm_mix_post, norm_ffn_pre, norm_ffn_post, w_in, b_gate, ret_gn_w, ssd_conv_w, ssd_conv_b, ssd_dt_bias, ssd_a_log, ssd_d, ssd_norm_w, w_branch_ret, w_branch_sb, w_branch_ssd, w_out, ffn_w_gate, ffn_w_up, ffn_w_down, loss_target, m_norm_mix_pre, m_norm_mix_post, m_norm_ffn_pre, m_norm_ffn_post, m_w_in, m_b_gate, m_ret_gn_w, m_ssd_conv_w, m_ssd_conv_b, m_ssd_dt_bias, m_ssd_a_log, m_ssd_d, m_ssd_norm_w, m_w_branch_ret, m_w_branch_sb, m_w_branch_ssd, m_w_out, m_ffn_w_gate, m_ffn_w_up, m_ffn_w_down, v_norm_mix_pre, v_norm_mix_post, v_norm_ffn_pre, v_norm_ffn_post, v_w_in, v_b_gate, v_ret_gn_w, v_ssd_conv_w, v_ssd_conv_b, v_ssd_dt_bias, v_ssd_a_log, v_ssd_d, v_ssd_norm_w, v_w_branch_ret, v_w_branch_sb, v_w_branch_ssd, v_w_out, v_ffn_w_gate, v_ffn_w_up, v_ffn_w_down):
    depth = w_in.shape[0]
    s, d = x.shape[1], x.shape[2]
    axes = ("x", "y", "c")
    me = _flat_index(lax.axis_index("x"), lax.axis_index("y"), lax.axis_index("c"))

    def big_shards(w_in_, br_ret, br_sb, br_ssd, w_out_, gate, up, down):
        return dict(w_in=w_in_, w_branch=jnp.concatenate([br_ret, br_sb, br_ssd], axis=1), w_out=w_out_,
                    ffn_w_gu=jnp.concatenate([gate, up], axis=1), ffn_w_down=down)

    big_w = big_shards(w_in, w_branch_ret, w_branch_sb, w_branch_ssd, w_out, ffn_w_gate, ffn_w_up, ffn_w_down)
    big_m = big_shards(m_w_in, m_w_branch_ret, m_w_branch_sb, m_w_branch_ssd, m_w_out, m_ffn_w_gate, m_ffn_w_up, m_ffn_w_down)
    big_v = big_shards(v_w_in, v_w_branch_ret, v_w_branch_sb, v_w_branch_ssd, v_w_out, v_ffn_w_gate, v_ffn_w_up, v_ffn_w_down)

    taps_all = all_gather(ssd_conv_w.reshape(-1, LANE), name="gather_conv_w")
    taps_all = jnp.transpose(taps_all.reshape(N_DEV, depth, SSD_K, -1), (1, 2, 0, 3)).reshape(depth, SSD_K, -1)

    cosf, sinf = rope_tables(positions.reshape(s), tr=TR)
    small_w = dict(norm_mix_pre=norm_mix_pre, norm_mix_post=norm_mix_post, norm_ffn_pre=norm_ffn_pre,
                   norm_ffn_post=norm_ffn_post, b_gate=b_gate, ret_gn_w=ret_gn_w, ssd_conv_b=ssd_conv_b,
                   ssd_dt_bias=ssd_dt_bias, ssd_a_log=ssd_a_log, ssd_d=ssd_d, ssd_norm_w=ssd_norm_w, taps=taps_all)

    def layer_weights(sw, gathered):
        lw = dict(gathered)
        for n in ("norm_mix_pre", "norm_mix_post", "norm_ffn_pre", "norm_ffn_post", "ret_gn_w", "ssd_conv_b", "ssd_norm_w"):
            lw[n] = _row(sw[n])
        for n in ("ssd_dt_bias", "ssd_a_log", "ssd_d"):
            lw[n] = _pad_lanes(sw[n])
        lw["b_gate"] = [_row(sw["b_gate"][i * d:(i + 1) * d]) for i in range(3)]
        lw["conv_taps"] = [sw["taps"][k:k + 1] for k in range(SSD_K)]
        return lw

    def fwd_layer(xc, sw, bw):
        g = {n: all_gather(bw[n].astype(BF16), name="gather_" + n) for n in BIG}
        w_cat, w_dt = _assemble_w_in(g["w_in"], d)
        gathered = dict(
            w_cat=w_cat, w_dt=w_dt,
            w_branch=jnp.transpose(g["w_branch"].reshape(N_DEV, 3, d // 2, -1), (1, 2, 0, 3)).reshape(3, d // 2, d),
            w_out=g["w_out"].reshape(d, d),
            ffn_w_gu=g["ffn_w_gu"].reshape(2 * N_DEV, d, -1),
            ffn_w_down=g["ffn_w_down"],
        )
        lw = layer_weights(sw, gathered)
        x2, res = _layer_fwd(xc, lw, cosf, sinf)
        return x2, res, lw

    def bwd_layer(dx2, res, lw, bw, bm, bv):
        dx, big_g, small_g = _layer_bwd(dx2, res, lw, cosf, sinf)
        outs = {}
        for n in BIG:
            parts = exchange(big_g[n].astype(BF16), name="exchange_" + n)
            outs[n] = adamw_sum(bw[n], bm[n], bv[n], parts.reshape(N_DEV, *bw[n].shape),
                                tr=_adam_rows(bw[n].shape[1]), name="adamw_" + n)
        return dx, outs, small_g

    def layer_slice(t, l):
        return {n: a[l] for n, a in t.items()}

    xs, saved = x.reshape(s, d), []
    for l in range(depth):
        xs, res, lw = fwd_layer(xs, layer_slice(small_w, l), layer_slice(big_w, l))
        saved.append((res, lw))
    loss_tile, dy = loss_head(xs, loss_target.reshape(s, d), tr=TR, name="loss_head")
    loss = lax.psum(loss_tile[0, 0], axes)
    dx, big_layers, small_layers = dy, [None] * depth, [None] * depth
    for l in reversed(range(depth)):
        res, lw = saved[l]
        dx, big_layers[l], small_layers[l] = bwd_layer(dx, res, lw, layer_slice(big_w, l), layer_slice(big_m, l),
                                                       layer_slice(big_v, l))
    big_out = {n: [jnp.stack([big_layers[l][n][i] for l in range(depth)]) for i in range(4)] for n in BIG}
    small_g = {n: jnp.stack([small_layers[l][n] for l in range(depth)]) for n in SMALL}

    n_dt = ssd_dt_bias.shape[-1]
    small_in = dict(norm_mix_pre=(norm_mix_pre, m_norm_mix_pre, v_norm_mix_pre), norm_mix_post=(norm_mix_post, m_norm_mix_post, v_norm_mix_post),
                    norm_ffn_pre=(norm_ffn_pre, m_norm_ffn_pre, v_norm_ffn_pre), norm_ffn_post=(norm_ffn_post, m_norm_ffn_post, v_norm_ffn_post),
                    b_gate=(b_gate, m_b_gate, v_b_gate), ret_gn_w=(ret_gn_w, m_ret_gn_w, v_ret_gn_w),
                    ssd_conv_b=(ssd_conv_b, m_ssd_conv_b, v_ssd_conv_b), ssd_dt_bias=(ssd_dt_bias, m_ssd_dt_bias, v_ssd_dt_bias),
                    ssd_a_log=(ssd_a_log, m_ssd_a_log, v_ssd_a_log), ssd_d=(ssd_d, m_ssd_d, v_ssd_d),
                    ssd_norm_w=(ssd_norm_w, m_ssd_norm_w, v_ssd_norm_w))
    rep = [n for n in SMALL if n != "ssd_conv_w"]

    def pack(arrs):
        flat = jnp.concatenate([a.reshape(-1) for a in arrs])
        rows = -(-flat.shape[0] // (16 * LANE)) * 16
        return jnp.pad(flat, (0, rows * LANE - flat.shape[0])).reshape(rows, LANE)

    conv_g = small_g["ssd_conv_w"]
    g_pack = pack([small_g[n] for n in rep] + [conv_g])
    g_all = all_gather(g_pack, name="gather_small_grads")
    zeros_conv = jnp.zeros_like(conv_g)
    w_pack, m_pack, v_pack = (pack([small_in[n][i] for n in rep] + [zeros_conv]) for i in range(3))
    sm = adamw_sum(w_pack, m_pack, v_pack, g_all, tr=TR, name="adamw_small")

    def unpack(p):
        flat, out, off = p.reshape(-1), {}, 0
        for n in rep:
            shp = small_in[n][0].shape
            size = math.prod(shp)
            out[n] = flat[off:off + size].reshape(shp)
            off += size
        out["conv_sum"] = flat[off:off + conv_g.size].reshape(conv_g.shape)
        return out

    sm = [unpack(p) for p in sm]
    ch = ssd_conv_w.shape[-1]
    conv_mine = lax.dynamic_slice_in_dim(sm[0]["conv_sum"], me * ch, ch, axis=2)
    conv_out = adamw_sum(ssd_conv_w.reshape(-1, LANE), m_ssd_conv_w.reshape(-1, LANE), v_ssd_conv_w.reshape(-1, LANE),
                         conv_mine.reshape(1, -1, LANE), tr=TR, name="adamw_conv_w")
    for i in range(4):
        sm[i]["ssd_conv_w"] = conv_out[i].reshape(ssd_conv_w.shape)

    def big_named(i):
        o = {n: big_out[n][i] for n in BIG}
        br = o["w_branch"].reshape(depth, 3, d // 2, -1)
        gu = o["ffn_w_gu"].reshape(depth, 2, d, -1)
        return dict(w_in=o["w_in"].reshape(w_in.shape), w_branch_ret=br[:, 0], w_branch_sb=br[:, 1], w_branch_ssd=br[:, 2],
                    w_out=o["w_out"].reshape(w_out.shape), ffn_w_gate=gu[:, 0], ffn_w_up=gu[:, 1],
                    ffn_w_down=o["ffn_w_down"].reshape(ffn_w_down.shape))

    order = ["norm_mix_pre", "norm_mix_post", "norm_ffn_pre", "norm_ffn_post", "w_in", "b_gate", "ret_gn_w", "ssd_conv_w",
             "ssd_conv_b", "ssd_dt_bias", "ssd_a_log", "ssd_d", "ssd_norm_w", "w_branch_ret", "w_branch_sb", "w_branch_ssd",
             "w_out", "ffn_w_gate", "ffn_w_up", "ffn_w_down"]
    outs = [loss, dx.reshape(x.shape)]
    for i in range(4):
        named = {**sm[i], **big_named(i)}
        outs += [named[n] for n in order]
    return tuple(outs)
```

```python
import functools
import math

import jax
import jax.numpy as jnp
import numpy as np
from jax import lax
from jax.experimental import pallas as pl
from jax.experimental.pallas import tpu as pltpu

F32 = jnp.float32
BF16 = jnp.bfloat16

N_DEV = 8
HEAD = 128
SSD_P = 64
SSD_G = 4
SSD_N = 128
SSD_K = 4
CHUNK = 64
NORM_EPS = 1e-6
ROPE_BASE = 10000.0
LANE = 128
VMEM_LIMIT = 56 * 1024 * 1024

ADAM_LR, ADAM_B1, ADAM_B2, ADAM_EPS, ADAM_WD, ADAM_STEP = 0.001, 0.9, 0.999, 1e-08, 0.01, 10


def _cparams(sem):
    return pltpu.CompilerParams(dimension_semantics=sem, vmem_limit_bytes=VMEM_LIMIT)


def _pick(n, pref):
    if n <= pref:
        return n
    t = pref
    while t >= LANE:
        if n % t == 0:
            return t
        t -= LANE
    return n


def matmul(a, b, *, ta=False, tb=False, lead_a=None, lead_b=None, out_dtype=F32, add=None, hosted=(),
           tm=1024, tn=1024, tk=512, name="mm"):
    la, lb = lead_a is not None, lead_b is not None
    a2, b2 = a.shape[1:] if la else a.shape, b.shape[1:] if lb else b.shape
    (kd_a, m) = a2 if ta else a2[::-1]
    (kd_b, n) = b2[::-1] if tb else b2
    assert kd_a == kd_b, (a.shape, b.shape)
    nlead = a.shape[0] if la else (b.shape[0] if lb else 1)
    batch = "batch" in (lead_a, lead_b)
    kblocks = nlead if "k" in (lead_a, lead_b) else 1
    if la and lb:
        assert lead_a == lead_b and a.shape[0] == b.shape[0]
    tm, tn, tk = _pick(m, tm), _pick(n, tn), _pick(kd_a, tk)
    kt = kd_a // tk
    nk = kt * kblocks
    grid = ((nlead if batch else 1), m // tm, n // tn, nk)

    def lead_idx(g, k):
        return g if batch else k // kt

    def a_map(g, i, j, k):
        idx = (k % kt, i) if ta else (i, k % kt)
        return ((lead_idx(g, k),) + idx) if la else idx

    def b_map(g, i, j, k):
        idx = (j, k % kt) if tb else (k % kt, j)
        return ((lead_idx(g, k),) + idx) if lb else idx

    a_blk = (tk, tm) if ta else (tm, tk)
    b_blk = (tn, tk) if tb else (tk, tn)
    a_spec = pl.BlockSpec(((None,) + a_blk) if la else a_blk, a_map)
    b_spec = pl.BlockSpec(((None,) + b_blk) if lb else b_blk, b_map)
    if batch:
        o_spec = pl.BlockSpec((None, tm, tn), lambda g, i, j, k: (g, i, j))
        o_shape = jax.ShapeDtypeStruct((nlead, m, n), out_dtype)
    else:
        o_spec = pl.BlockSpec((tm, tn), lambda g, i, j, k: (i, j))
        o_shape = jax.ShapeDtypeStruct((m, n), out_dtype)
    dims = (((0 if ta else 1,), (1 if tb else 0,)), ((), ()))

    n_add, n_host = int(add is not None), len(hosted)

    def body(a_ref, b_ref, *rest):
        c_ref = rest[0] if n_add else None
        src_refs = rest[n_add:n_add + n_host]
        o_ref = rest[n_add + n_host]
        dst_refs = rest[n_add + n_host + 1:n_add + 2 * n_host + 1]
        acc_ref = rest[n_add + 2 * n_host + 1]
        sems = rest[n_add + 2 * n_host + 2:]
        k = pl.program_id(3)
        ids = [pl.program_id(ax) for ax in range(4)]
        comms = [_COMM[kind](src_refs[c], dst_refs[c], *sems[3 * c:3 * c + 3]) for c, (kind, _) in enumerate(hosted)]

        if hosted:
            @pl.when(functools.reduce(jnp.logical_and, [i == 0 for i in ids]))
            def _():
                for start, _ in comms:
                    start()

        @pl.when(k == 0)
        def _():
            acc_ref[...] = jnp.zeros_like(acc_ref) if c_ref is None else c_ref[...].astype(F32)

        acc_ref[...] += lax.dot_general(a_ref[...].astype(BF16), b_ref[...].astype(BF16), dims,
                                        preferred_element_type=F32)

        @pl.when(k == nk - 1)
        def _():
            o_ref[...] = acc_ref[...].astype(o_ref.dtype)

        if hosted:
            @pl.when(functools.reduce(jnp.logical_and, [i == g - 1 for i, g in zip(ids, grid)]))
            def _():
                for _, finish in comms:
                    finish()

    extra = ([] if add is None else [add]) + [arr for _, arr in hosted]
    extra_specs = ([] if add is None else [o_spec]) + [_HBM] * n_host
    out_shapes = [o_shape] + [_comm_out_shape(kind, arr) for kind, arr in hosted]
    scratch = [pltpu.VMEM((tm, tn), F32)] + _COMM_SEMS * n_host
    sem = ("arbitrary",) * 4 if hosted else ("parallel", "parallel", "parallel", "arbitrary")
    res = pl.pallas_call(
        body, out_shape=out_shapes, grid=grid, in_specs=[a_spec, b_spec] + extra_specs,
        out_specs=[o_spec] + [_HBM] * n_host, scratch_shapes=scratch, name=name, compiler_params=_cparams(sem),
    )(a, b, *extra)
    return (res[0], res[1:]) if hosted else res[0]


def _row_spec(r, tr):
    if not isinstance(r, tuple):
        return r, pl.BlockSpec((tr, r.shape[-1]), lambda i: (i, 0))
    if len(r) == 3:
        arr, w, cb = r
        return arr, pl.BlockSpec((tr, w), lambda i: (i, cb))
    arr, w, cb, ld = r
    return arr, pl.BlockSpec((None, tr, w), lambda i: (ld, i, cb))


def _full_spec(c):
    nd = c.ndim
    return pl.BlockSpec(c.shape, lambda i: (0,) * nd)


def rowwise(fn, rows, consts, outs, *, tr, name):
    arrs, specs = zip(*[_row_spec(r, tr) for r in rows])
    n_rows = arrs[0].shape[-2]
    nr, nc = len(rows), len(consts)

    def body(*refs):
        vals = [r[...] for r in refs[:nr + nc]]
        res = fn(*vals)
        for o_ref, r in zip(refs[nr + nc:], res):
            o_ref[...] = r.astype(o_ref.dtype)

    return pl.pallas_call(
        body, grid=(n_rows // tr,),
        in_specs=list(specs) + [_full_spec(c) for c in consts],
        out_specs=[pl.BlockSpec((tr, w), lambda i: (i, 0)) for w, _ in outs],
        out_shape=[jax.ShapeDtypeStruct((n_rows, w), dt) for w, dt in outs],
        name=name, compiler_params=_cparams(("parallel",)),
    )(*arrs, *consts)


def rowwise_vjp(fn, rows, consts, cts, row_grads, const_grads, *, tr, name):
    arrs, specs = zip(*[_row_spec(r, tr) for r in rows])
    n_rows = arrs[0].shape[-2]
    nr, nc = len(rows), len(consts)
    ct_present = [c for c in cts if c is not None]
    ct_arrs, ct_specs = zip(*[_row_spec(c, tr) for c in ct_present])
    add_present = [g[2] for g in row_grads if g[2] is not None]
    add_arrs, add_specs = zip(*[_row_spec(c, tr) for c in add_present]) if add_present else ((), ())
    n_ct, n_add = len(ct_present), len(add_present)
    widths = [s.block_shape[-1] for s in specs]

    def body(*refs):
        ins = refs[:nr + nc]
        ct_refs = refs[nr + nc:nr + nc + n_ct]
        add_refs = refs[nr + nc + n_ct:nr + nc + n_ct + n_add]
        out_refs = refs[nr + nc + n_ct + n_add:]
        vals = [r[...] for r in ins]
        res, f_vjp = jax.vjp(fn, *vals)
        it = iter(ct_refs)
        ct_vals = tuple(next(it)[...].astype(r.dtype) if c is not None else jnp.zeros_like(r)
                        for c, r in zip(cts, res))
        grads = f_vjp(ct_vals)
        ita = iter(add_refs)
        for o_ref, (idx, _, add) in zip(out_refs, row_grads):
            g = grads[idx].astype(F32)
            if add is not None:
                g = g + next(ita)[...].astype(F32)
            o_ref[...] = g.astype(o_ref.dtype)
        first = pl.program_id(0) == 0
        for o_ref, idx in zip(out_refs[len(row_grads):], const_grads):
            g = grads[nr + idx].astype(F32)

            @pl.when(first)
            def _():
                o_ref[...] = g

            @pl.when(jnp.logical_not(first))
            def _():
                o_ref[...] += g

    out_specs = [pl.BlockSpec((tr, widths[idx]), lambda i: (i, 0)) for idx, _, _ in row_grads]
    out_shape = [jax.ShapeDtypeStruct((n_rows, widths[idx]), dt) for idx, dt, _ in row_grads]
    out_specs += [_full_spec(consts[idx]) for idx in const_grads]
    out_shape += [jax.ShapeDtypeStruct(consts[idx].shape, F32) for idx in const_grads]
    return pl.pallas_call(
        body, grid=(n_rows // tr,),
        in_specs=list(specs) + [_full_spec(c) for c in consts] + list(ct_specs) + list(add_specs),
        out_specs=out_specs, out_shape=out_shape,
        name=name, compiler_params=_cparams(("arbitrary",)),
    )(*arrs, *consts, *ct_arrs, *add_arrs)


def f_rms(x, w):
    xf = x.astype(F32)
    return xf * lax.rsqrt(jnp.mean(xf * xf, axis=-1, keepdims=True) + NORM_EPS) * w


def f_rms_pre(x, w):
    return (f_rms(x, w),)


def f_rms_post(x, o, w):
    return (x + f_rms(o, w),)


def f_merge(u0, u1, u2, g0, g1, g2, b0, b1, b2):
    return (jax.nn.sigmoid(g0 + b0) * u0 + jax.nn.sigmoid(g1 + b1) * u1 + jax.nn.sigmoid(g2 + b2) * u2,)


def f_ssd_gate(y, z, w):
    return (f_rms(y * jax.nn.silu(z), w),)


@jax.custom_vjp
def _swap_halves(x):
    return pltpu.roll(x, HEAD // 2, 1)


_swap_halves.defvjp(lambda x: (_swap_halves(x), None), lambda _, g: (_swap_halves(g),))


def rope_tables(positions, *, tr):
    s = positions.shape[0]
    half = HEAD // 2
    inv = ROPE_BASE ** (-2.0 * jnp.arange(half, dtype=F32) / HEAD)
    inv = jnp.concatenate([inv, inv]).reshape(1, HEAD)
    sign = jnp.concatenate([-jnp.ones((half,), F32), jnp.ones((half,), F32)]).reshape(1, HEAD)

    def fn(pos, inv, sign):
        ang = pos.astype(F32) * inv
        return jnp.cos(ang), jnp.sin(ang) * sign

    return rowwise(fn, [positions.reshape(s, 1)], [inv, sign], [(HEAD, F32), (HEAD, F32)], tr=tr, name="rope_tables")


def _ret_consts(n_heads, blk):
    lg = np.log1p(-np.exp2(-5.0 - np.arange(n_heads)))[:, None, None]
    i = np.arange(blk)
    dist = np.abs(i[:, None] - i[None, :])[None]
    allowed = ((i[None, :] // CHUNK) <= (i[:, None] // CHUNK))[None]
    dm = np.where(allowed, np.exp(lg * dist), 0.0)
    qd = np.broadcast_to(np.exp(lg * (i[None, :, None] + 1.0)), (n_heads, blk, HEAD))
    kd = np.broadcast_to(np.exp(lg * (blk - 1.0 - i[None, :, None])), (n_heads, blk, HEAD))
    cd = np.broadcast_to(np.exp(lg * blk), (n_heads, 1, HEAD))
    return [jnp.asarray(a, F32) for a in (dm, qd, kd, cd)]


def _ret_block(q, k, v, g, gnw, state, cosf, sinf, dm, qd, kd, cd):
    qr = q * cosf + _swap_halves(q) * sinf
    kr = (k * cosf + _swap_halves(k) * sinf) * (HEAD ** -0.5)
    vb = v.astype(BF16)
    scores = _dot(qr.astype(BF16), kr.astype(BF16), _NT) * dm
    o = _dot(scores.astype(BF16), vb) + _dot((qr * qd).astype(BF16), state.astype(BF16))
    new_state = state * cd + _dot((kr * kd).astype(BF16), vb, _TN)
    mu = jnp.mean(o, axis=-1, keepdims=True)
    var = jnp.mean(jnp.square(o - mu), axis=-1, keepdims=True)
    y = (o - mu) * lax.rsqrt(var + NORM_EPS) * gnw * jax.nn.silu(g)
    return y, new_state


def _ret_specs(n_heads, blk, nb, cols, reverse):
    jm = (lambda j: nb - 1 - j) if reverse else (lambda j: j)
    col = lambda c0: pl.BlockSpec((blk, HEAD), lambda h, j: (jm(j), c0 + h))
    tab = pl.BlockSpec((blk, HEAD), lambda h, j: (jm(j), 0))
    specs = [col(c) for c in cols]
    specs += [pl.BlockSpec((1, HEAD), lambda h, j: (0, h)), tab, tab]
    specs += [pl.BlockSpec((None, blk, blk), lambda h, j: (h, 0, 0)),
              pl.BlockSpec((None, blk, HEAD), lambda h, j: (h, 0, 0)),
              pl.BlockSpec((None, blk, HEAD), lambda h, j: (h, 0, 0)),
              pl.BlockSpec((None, 1, HEAD), lambda h, j: (h, 0, 0))]
    state = pl.BlockSpec((None, None, HEAD, HEAD), lambda h, j: (h, jm(j), 0, 0))
    out_col = pl.BlockSpec((blk, HEAD), lambda h, j: (jm(j), h))
    return specs, state, out_col


def retention_fwd(proj, cols, gn_w, cosf, sinf, n_heads, *, blk, name):
    s = proj.shape[0]
    nb = s // blk
    consts = _ret_consts(n_heads, blk)
    specs, state_spec, out_col = _ret_specs(n_heads, blk, nb, cols, False)

    def body(q_ref, k_ref, v_ref, g_ref, gn_ref, cos_ref, sin_ref, dm_ref, qd_ref, kd_ref, cd_ref,
             y_ref, st_ref, state):
        @pl.when(pl.program_id(1) == 0)
        def _():
            state[...] = jnp.zeros_like(state)

        st = state[...]
        st_ref[...] = st
        y, new_state = _ret_block(q_ref[...], k_ref[...], v_ref[...], g_ref[...], gn_ref[...], st,
                                  cos_ref[...], sin_ref[...], dm_ref[...], qd_ref[...], kd_ref[...], cd_ref[...])
        y_ref[...] = y.astype(y_ref.dtype)
        state[...] = new_state

    return pl.pallas_call(
        body, grid=(n_heads, nb), in_specs=specs, out_specs=[out_col, state_spec],
        out_shape=[jax.ShapeDtypeStruct((s, n_heads * HEAD), BF16),
                   jax.ShapeDtypeStruct((n_heads, nb, HEAD, HEAD), F32)],
        scratch_shapes=[pltpu.VMEM((HEAD, HEAD), F32)],
        name=name, compiler_params=_cparams(("parallel", "arbitrary")),
    )(proj, proj, proj, proj, gn_w, cosf, sinf, *consts)


def retention_bwd(proj, cols, gn_w, cosf, sinf, states, dy, n_heads, *, blk, name):
    s = proj.shape[0]
    nb = s // blk
    consts = _ret_consts(n_heads, blk)
    specs, state_spec, out_col = _ret_specs(n_heads, blk, nb, cols, True)

    def body(q_ref, k_ref, v_ref, g_ref, gn_ref, cos_ref, sin_ref, dm_ref, qd_ref, kd_ref, cd_ref,
             st_ref, dy_ref, dq_ref, dk_ref, dv_ref, dg_ref, dgn_ref, dstate):
        first = pl.program_id(1) == 0

        @pl.when(first)
        def _():
            dstate[...] = jnp.zeros_like(dstate)
            dgn_ref[...] = jnp.zeros_like(dgn_ref)

        tabs = (cos_ref[...], sin_ref[...], dm_ref[...], qd_ref[...], kd_ref[...], cd_ref[...])
        fn = lambda q, k, v, g, gnw, st: _ret_block(q, k, v, g, gnw, st, *tabs)
        _, f_vjp = jax.vjp(fn, q_ref[...], k_ref[...], v_ref[...], g_ref[...], gn_ref[...], st_ref[...])
        dq, dk, dv, dg, dgn, dst = f_vjp((dy_ref[...].astype(F32), dstate[...]))
        dq_ref[...] = dq.astype(dq_ref.dtype)
        dk_ref[...] = dk.astype(dk_ref.dtype)
        dv_ref[...] = dv.astype(dv_ref.dtype)
        dg_ref[...] = dg.astype(dg_ref.dtype)
        dgn_ref[...] += dgn
        dstate[...] = dst

    o_shape = jax.ShapeDtypeStruct((s, n_heads * HEAD), BF16)
    return pl.pallas_call(
        body, grid=(n_heads, nb), in_specs=specs + [state_spec, out_col],
        out_specs=[out_col, out_col, out_col, out_col, pl.BlockSpec((1, HEAD), lambda h, j: (0, h))],
        out_shape=[o_shape, o_shape, o_shape, o_shape, jax.ShapeDtypeStruct((1, n_heads * HEAD), F32)],
        scratch_shapes=[pltpu.VMEM((HEAD, HEAD), F32)],
        name=name, compiler_params=_cparams(("parallel", "arbitrary")),
    )(proj, proj, proj, proj, gn_w, cosf, sinf, *consts, states, dy)


CONV_PAD = 8


def _conv_pre(u_ext, taps, bias, n_out):
    n = u_ext.shape[0]
    views = [pltpu.roll(u_ext, n - (k + CONV_PAD - SSD_K + 1), 0)[:n_out] for k in range(SSD_K)]
    pre = bias
    for k in range(SSD_K):
        pre = pre + taps[k] * views[k]
    return pre, views


def ssd_conv_fwd(u_pad, taps, bias, *, chunk, name):
    s, c = u_pad.shape[0] - 2 * CONV_PAD, u_pad.shape[1]

    def body(u_ref, t0, t1, t2, t3, b_ref, o_ref):
        taps_v = [t[...] for t in (t0, t1, t2, t3)]
        bias_v = b_ref[...]

        @pl.loop(0, s // chunk)
        def _(ci):
            r0 = pl.multiple_of(ci * chunk, chunk)
            pre, _ = _conv_pre(u_ref[pl.ds(r0, chunk + CONV_PAD), :], taps_v, bias_v, chunk)
            o_ref[pl.ds(r0, chunk), :] = pre * jax.nn.sigmoid(pre)

    row = pl.BlockSpec((1, LANE), lambda i: (0, i))
    return pl.pallas_call(
        body, grid=(c // LANE,),
        in_specs=[pl.BlockSpec((s + 2 * CONV_PAD, LANE), lambda i: (0, i))] + [row] * 5,
        out_specs=pl.BlockSpec((s, LANE), lambda i: (0, i)),
        out_shape=jax.ShapeDtypeStruct((s, c), F32),
        name=name, compiler_params=_cparams(("parallel",)),
    )(u_pad, *taps, bias)


def ssd_conv_bwd(u_pad, taps, bias, dxc_pad, *, chunk, name):
    s, c = u_pad.shape[0] - 2 * CONV_PAD, u_pad.shape[1]
    ext = chunk + CONV_PAD

    def body(u_ref, t0, t1, t2, t3, b_ref, d_ref, du_ref, dw_ref, db_ref):
        taps_v = [t[...] for t in (t0, t1, t2, t3)]
        bias_v = b_ref[...]
        dw_ref[...] = jnp.zeros_like(dw_ref)
        db_ref[...] = jnp.zeros_like(db_ref)

        @pl.loop(0, s // chunk)
        def _(ci):
            r0 = pl.multiple_of(ci * chunk, chunk)
            pre, views = _conv_pre(u_ref[pl.ds(r0, ext + CONV_PAD), :], taps_v, bias_v, ext)
            sig = jax.nn.sigmoid(pre)
            dpre = d_ref[pl.ds(r0, ext), :] * (sig * (1.0 + pre * (1.0 - sig)))
            du = taps_v[SSD_K - 1] * dpre[:chunk]
            for k in range(SSD_K - 1):
                du = du + taps_v[k] * pltpu.roll(dpre, ext - (SSD_K - 1 - k), 0)[:chunk]
            du_ref[pl.ds(r0, chunk), :] = du.astype(du_ref.dtype)
            own = dpre[:chunk]
            for k in range(SSD_K):
                dw_ref[k:k + 1, :] += jnp.sum(own * views[k][:chunk], axis=0, keepdims=True)
            db_ref[...] += jnp.sum(own, axis=0, keepdims=True)

    row = pl.BlockSpec((1, LANE), lambda i: (0, i))
    return pl.pallas_call(
        body, grid=(c // LANE,),
        in_specs=[pl.BlockSpec((s + 2 * CONV_PAD, LANE), lambda i: (0, i))] + [row] * 5
        + [pl.BlockSpec((s + CONV_PAD, LANE), lambda i: (0, i))],
        out_specs=[pl.BlockSpec((s, LANE), lambda i: (0, i)), pl.BlockSpec((SSD_K, LANE), lambda i: (0, i)), row],
        out_shape=[jax.ShapeDtypeStruct((s, c), BF16), jax.ShapeDtypeStruct((SSD_K, c), F32),
                   jax.ShapeDtypeStruct((1, c), F32)],
        name=name, compiler_params=_cparams(("parallel",)),
    )(u_pad, *taps, bias, dxc_pad)


def _tri_dot(tri, x, passes=3):
    out = None
    rem = x
    for _ in range(passes):
        piece = rem.astype(BF16)
        rem = rem - piece.astype(F32)
        d = _dot(tri, piece)
        out = d if out is None else out + d
    return out


def _tri(n, upper):
    rr = lax.broadcasted_iota(jnp.int32, (n, n), 0)
    cc = lax.broadcasted_iota(jnp.int32, (n, n), 1)
    return ((rr <= cc) if upper else (rr >= cc)).astype(BF16)


@jax.custom_vjp
def _cumsum_rows(a):
    return _tri_dot(_tri(a.shape[0], False), a)


_cumsum_rows.defvjp(lambda a: (_cumsum_rows(a), None), lambda _, g: (_tri_dot(_tri(g.shape[0], True), g),))


def _softplus(x):
    return jnp.maximum(x, 0.0) + jnp.log(1.0 + jnp.exp(-jnp.abs(x)))


def _ssd_block(x, bm, cm, dtraw, dtb, alog, dsk, state_t, group):
    blk, width = x.shape
    e_heads = width // SSD_P
    dt = _softplus(dtraw + dtb)
    acum = _cumsum_rows(dt * (-jnp.exp(alog)))
    acum_t = acum.T
    lane_h = lax.broadcasted_iota(jnp.int32, (1, LANE), 1)
    sub_h = lax.broadcasted_iota(jnp.int32, (LANE, 1), 0)
    lane_e = lax.broadcasted_iota(jnp.int32, (1, width), 1) // SSD_P
    causal = lax.broadcasted_iota(jnp.int32, (blk, blk), 0) >= lax.broadcasted_iota(jnp.int32, (blk, blk), 1)
    last_row = lax.broadcasted_iota(jnp.int32, (blk, 1), 0) == blk - 1
    cb = _dot(cm.astype(BF16), bm.astype(BF16), _NT)
    y = jnp.zeros((blk, width), F32)
    dt_l = jnp.zeros((blk, width), F32)
    ac_l = jnp.zeros((blk, width), F32)
    d_l = jnp.zeros((1, width), F32)
    for e in range(e_heads):
        head = group * e_heads + e
        pick = lane_h == head
        col = jnp.sum(jnp.where(pick, acum, 0.0), axis=1, keepdims=True)
        dt_e = jnp.sum(jnp.where(pick, dt, 0.0), axis=1, keepdims=True)
        d_e = jnp.sum(jnp.where(pick, dsk, 0.0), axis=1, keepdims=True)
        row = jnp.sum(jnp.where(sub_h == head, acum_t, 0.0), axis=0, keepdims=True)
        mine = lane_e == e
        decay = jnp.exp(jnp.where(causal, col - row, -jnp.inf))
        y = y + _dot((cb * decay).astype(BF16), jnp.where(mine, x * dt_e, 0.0).astype(BF16))
        dt_l = dt_l + jnp.where(mine, dt_e, 0.0)
        ac_l = ac_l + jnp.where(mine, col, 0.0)
        d_l = d_l + jnp.where(mine, d_e, 0.0)
    ac_last = jnp.sum(jnp.where(last_row, ac_l, 0.0), axis=0, keepdims=True)
    y = y + jnp.exp(ac_l) * _dot(cm.astype(BF16), state_t.astype(BF16)) + x * d_l
    inject = (x * dt_l * jnp.exp(ac_last - ac_l)).astype(BF16)
    new_state = state_t * jnp.exp(ac_last) + _dot(bm.astype(BF16), inject, _TN)
    return y, new_state


def ssd_scan_fwd(xc, proj, dt_col, dtb, alog, dsk, *, blk, name):
    s = xc.shape[0]
    nb = s // blk
    e_w = 4 * SSD_P
    b_off = SSD_G * e_w // SSD_N
    c_off = b_off + SSD_G
    row = pl.BlockSpec((1, LANE), lambda j, g: (0, 0))

    def body(x_ref, b_ref, c_ref, dt_ref, dtb_ref, al_ref, d_ref, y_ref, st_ref, state):
        j, g = pl.program_id(0), pl.program_id(1)

        @pl.when(j == 0)
        def _():
            state[g] = jnp.zeros((SSD_N, e_w), F32)

        st = state[g]
        st_ref[...] = st
        y, new_state = _ssd_block(x_ref[...], b_ref[...], c_ref[...], dt_ref[...], dtb_ref[...], al_ref[...],
                                  d_ref[...], st, g)
        y_ref[...] = y
        state[g] = new_state

    return pl.pallas_call(
        body, grid=(nb, SSD_G),
        in_specs=[pl.BlockSpec((blk, e_w), lambda j, g: (j, g)),
                  pl.BlockSpec((blk, SSD_N), lambda j, g: (j, b_off + g)),
                  pl.BlockSpec((blk, SSD_N), lambda j, g: (j, c_off + g)),
                  pl.BlockSpec((blk, LANE), lambda j, g: (j, dt_col)), row, row, row],
        out_specs=[pl.BlockSpec((blk, e_w), lambda j, g: (j, g)),
                   pl.BlockSpec((None, None, SSD_N, e_w), lambda j, g: (j, g, 0, 0))],
        out_shape=[jax.ShapeDtypeStruct((s, SSD_G * e_w), F32),
                   jax.ShapeDtypeStruct((nb, SSD_G, SSD_N, e_w), F32)],
        scratch_shapes=[pltpu.VMEM((SSD_G, SSD_N, e_w), F32)],
        name=name, compiler_params=_cparams(("arbitrary", "arbitrary")),
    )(xc, xc, xc, proj, dtb, alog, dsk)


def ssd_scan_bwd(xc, proj, dt_col, dtb, alog, dsk, states, dy, *, blk, name):
    s = xc.shape[0]
    nb = s // blk
    e_w = 4 * SSD_P
    b_off = SSD_G * e_w // SSD_N
    c_off = b_off + SSD_G
    row = pl.BlockSpec((1, LANE), lambda j, g: (0, 0))
    jm = lambda j: nb - 1 - j

    def body(x_ref, b_ref, c_ref, dt_ref, dtb_ref, al_ref, d_ref, st_ref, dy_ref,
             dx_ref, db_ref, dc_ref, ddt_ref, ddtb_ref, dal_ref, dd_ref, dstate):
        j, g = pl.program_id(0), pl.program_id(1)

        @pl.when(j == 0)
        def _():
            dstate[g] = jnp.zeros((SSD_N, e_w), F32)

        @pl.when(jnp.logical_and(j == 0, g == 0))
        def _():
            ddtb_ref[...] = jnp.zeros_like(ddtb_ref)
            dal_ref[...] = jnp.zeros_like(dal_ref)
            dd_ref[...] = jnp.zeros_like(dd_ref)

        @pl.when(g == 0)
        def _():
            ddt_ref[...] = jnp.zeros_like(ddt_ref)

        fn = functools.partial(_ssd_block, group=g)
        _, f_vjp = jax.vjp(fn, x_ref[...], b_ref[...], c_ref[...], dt_ref[...], dtb_ref[...], al_ref[...],
                           d_ref[...], st_ref[...])
        dx, db, dc, ddt, ddtb, dal, dd, dst = f_vjp((dy_ref[...], dstate[g]))
        dx_ref[...] = dx
        db_ref[...] = db
        dc_ref[...] = dc
        ddt_ref[...] += ddt
        ddtb_ref[...] += ddtb
        dal_ref[...] += dal
        dd_ref[...] += dd
        dstate[g] = dst

    return pl.pallas_call(
        body, grid=(nb, SSD_G),
        in_specs=[pl.BlockSpec((blk, e_w), lambda j, g: (jm(j), g)),
                  pl.BlockSpec((blk, SSD_N), lambda j, g: (jm(j), b_off + g)),
                  pl.BlockSpec((blk, SSD_N), lambda j, g: (jm(j), c_off + g)),
                  pl.BlockSpec((blk, LANE), lambda j, g: (jm(j), dt_col)), row, row, row,
                  pl.BlockSpec((None, None, SSD_N, e_w), lambda j, g: (jm(j), g, 0, 0)),
                  pl.BlockSpec((blk, e_w), lambda j, g: (jm(j), g))],
        out_specs=[pl.BlockSpec((blk, e_w), lambda j, g: (jm(j), g)),
                   pl.BlockSpec((blk, SSD_N), lambda j, g: (jm(j), g)),
                   pl.BlockSpec((blk, SSD_N), lambda j, g: (jm(j), g)),
                   pl.BlockSpec((blk, LANE), lambda j, g: (jm(j), 0)), row, row, row],
        out_shape=[jax.ShapeDtypeStruct((s, SSD_G * e_w), F32),
                   jax.ShapeDtypeStruct((s, SSD_G * SSD_N), F32),
                   jax.ShapeDtypeStruct((s, SSD_G * SSD_N), F32),
                   jax.ShapeDtypeStruct((s, LANE), F32)] + [jax.ShapeDtypeStruct((1, LANE), F32)] * 3,
        scratch_shapes=[pltpu.VMEM((SSD_G, SSD_N, e_w), F32)],
        name=name, compiler_params=_cparams(("arbitrary", "arbitrary")),
    )(xc, xc, xc, proj, dtb, alog, dsk, states, dy)


SB_DEAD = -104.0

_NT = (((1,), (1,)), ((), ()))
_TN = (((0,), (0,)), ((), ()))


def _dot(a, b, dims=(((1,), (0,)), ((), ()))):
    return lax.dot_general(a, b, dims, preferred_element_type=F32)


def _split_dot(x, tri, passes):
    out = None
    rem = x
    for _ in range(passes):
        piece = rem.astype(BF16)
        rem = rem - piece.astype(F32)
        d = _dot(piece, tri)
        out = d if out is None else out + d
    return out


def _sb_scores(q, k_ref, j, blk, row, scale):
    kb = k_ref[pl.ds(pl.multiple_of(j * blk, blk), blk), :].astype(BF16)
    z = _dot(q, kb, _NT) * scale
    col = j * blk + lax.broadcasted_iota(jnp.int32, (blk, blk), 1)
    mask = col < row
    sp = jnp.maximum(z, 0.0) + jnp.log(1.0 + jnp.exp(-jnp.abs(z)))
    lk = jnp.where(mask, -sp, 0.0)
    return kb, mask, lk, z - sp


def sb_attention_fwd(proj, q_col, k_col, v_col, n_heads, *, blk, name):
    s = proj.shape[0]
    nq = s // blk
    scale = HEAD ** -0.5

    def body(q_ref, k_ref, v_ref, o_ref, r_ref, n_ref):
        i = pl.program_id(1)
        q = q_ref[...].astype(BF16)
        row = i * blk + lax.broadcasted_iota(jnp.int32, (blk, blk), 0)
        rr = lax.broadcasted_iota(jnp.int32, (blk, blk), 0)
        cc = lax.broadcasted_iota(jnp.int32, (blk, blk), 1)
        tri_after = (rr > cc).astype(BF16)

        def alive(carry):
            jj, _, run = carry
            return jnp.logical_and(jj <= i, jnp.max(run) > SB_DEAD)

        def step(carry):
            jj, acc, run = carry
            j = i - jj
            _, mask, lk, ls = _sb_scores(q, k_ref, j, blk, row, scale)
            later = _split_dot(lk, tri_after, 2) + run
            w = jnp.where(mask, jnp.exp(ls + later), 0.0)
            vb = v_ref[pl.ds(pl.multiple_of(j * blk, blk), blk), :].astype(BF16)
            return jj + 1, acc + _dot(w.astype(BF16), vb), run + jnp.sum(lk, axis=1, keepdims=True)

        n, acc, run = lax.while_loop(
            alive, step, (jnp.int32(0), jnp.zeros((blk, HEAD), F32), jnp.zeros((blk, 1), F32)))
        o_ref[...] = acc.astype(o_ref.dtype)
        r_ref[...] = jnp.broadcast_to(run, (blk, HEAD))
        n_ref[pl.program_id(0), i] = n

    blk_spec = lambda c0: pl.BlockSpec((blk, HEAD), lambda h, i: (i, c0 + h))
    full_spec = lambda c0: pl.BlockSpec((s, HEAD), lambda h, i: (0, c0 + h))
    out_spec = pl.BlockSpec((blk, HEAD), lambda h, i: (i, h))
    return pl.pallas_call(
        body, grid=(n_heads, nq),
        in_specs=[blk_spec(q_col), full_spec(k_col), full_spec(v_col)],
        out_specs=[out_spec, out_spec, pl.BlockSpec(memory_space=pltpu.SMEM)],
        out_shape=[jax.ShapeDtypeStruct((s, n_heads * HEAD), BF16),
                   jax.ShapeDtypeStruct((s, n_heads * HEAD), F32),
                   jax.ShapeDtypeStruct((n_heads, nq), jnp.int32)],
        name=name, compiler_params=_cparams(("arbitrary", "arbitrary")),
    )(proj, proj, proj)


def sb_attention_bwd(proj, d_out, run_tot, visited, q_col, k_col, v_col, n_heads, *, blk, name):
    s = proj.shape[0]
    nq = s // blk
    scale = HEAD ** -0.5

    def body(n_ref, q_ref, k_ref, v_ref, do_ref, r_ref, dq_ref, dk_ref, dv_ref, dk_acc, dv_acc):
        i = pl.program_id(1)
        first = i + 1 - jnp.clip(n_ref[pl.program_id(0), i], 1, i + 1)

        @pl.when(i == 0)
        def _():
            dk_acc[...] = jnp.zeros_like(dk_acc)
            dv_acc[...] = jnp.zeros_like(dv_acc)

        q = q_ref[...].astype(BF16)
        do = do_ref[...].astype(BF16)
        rtot = r_ref[:, :1]
        row = i * blk + lax.broadcasted_iota(jnp.int32, (blk, blk), 0)
        rr = lax.broadcasted_iota(jnp.int32, (blk, blk), 0)
        cc = lax.broadcasted_iota(jnp.int32, (blk, blk), 1)
        tri_upto = (rr <= cc).astype(BF16)
        tri_before = (rr < cc).astype(BF16)

        def step(j, carry):
            dq, pre, gpre = carry
            kb, mask, lk, ls = _sb_scores(q, k_ref, j, blk, row, scale)
            rows = pl.ds(pl.multiple_of(j * blk, blk), blk)
            vb = v_ref[rows, :].astype(BF16)
            later = rtot - (pre + _split_dot(lk, tri_upto, 3))
            w = jnp.where(mask, jnp.exp(ls + later), 0.0)
            g = w * _dot(do, vb, _NT)
            g_before = _split_dot(g, tri_before, 2) + gpre
            sig = jnp.exp(ls)
            dz = (jnp.where(mask, g * (1.0 - sig) - sig * g_before, 0.0) * scale).astype(BF16)
            dk_acc[rows, :] += _dot(dz, q, _TN)
            dv_acc[rows, :] += _dot(w.astype(BF16), do, _TN)
            return (dq + _dot(dz, kb), pre + jnp.sum(lk, axis=1, keepdims=True),
                    gpre + jnp.sum(g, axis=1, keepdims=True))

        zero = jnp.zeros((blk, 1), F32)
        dq, _, _ = lax.fori_loop(first, i + 1, step, (jnp.zeros((blk, HEAD), F32), zero, zero))
        dq_ref[...] = dq.astype(dq_ref.dtype)

        @pl.when(i == nq - 1)
        def _():
            dk_ref[...] = dk_acc[...].astype(dk_ref.dtype)
            dv_ref[...] = dv_acc[...].astype(dv_ref.dtype)

    blk_spec = lambda c0: pl.BlockSpec((blk, HEAD), lambda h, i: (i, c0 + h))
    full_spec = lambda c0: pl.BlockSpec((s, HEAD), lambda h, i: (0, c0 + h))
    o_shape = jax.ShapeDtypeStruct((s, n_heads * HEAD), BF16)
    return pl.pallas_call(
        body, grid=(n_heads, nq),
        in_specs=[pl.BlockSpec(memory_space=pltpu.SMEM), blk_spec(q_col), full_spec(k_col), full_spec(v_col),
                  blk_spec(0), blk_spec(0)],
        out_specs=[blk_spec(0), full_spec(0), full_spec(0)],
        out_shape=[o_shape, o_shape, o_shape],
        scratch_shapes=[pltpu.VMEM((s, HEAD), F32), pltpu.VMEM((s, HEAD), F32)],
        name=name, compiler_params=_cparams(("parallel", "arbitrary")),
    )(visited, proj, proj, proj, d_out, run_tot)


def swiglu_fwd(gu, *, tr, name):
    nb, _, s, hb = gu.shape

    def body(gu_ref, a_ref):
        gate = gu_ref[0]
        a_ref[...] = (gate * jax.nn.sigmoid(gate) * gu_ref[1]).astype(a_ref.dtype)

    return pl.pallas_call(
        body, grid=(nb, s // tr),
        in_specs=[pl.BlockSpec((None, 2, tr, hb), lambda i, r: (i, 0, r, 0))],
        out_specs=pl.BlockSpec((None, tr, hb), lambda i, r: (i, r, 0)),
        out_shape=jax.ShapeDtypeStruct((nb, s, hb), BF16),
        name=name, compiler_params=_cparams(("parallel", "parallel")),
    )(gu)


def swiglu_bwd(gu, da, *, tr, name):
    nb, _, s, hb = gu.shape

    def body(gu_ref, da_ref, dgu_ref):
        gate, up, d = gu_ref[0], gu_ref[1], da_ref[...]
        sig = jax.nn.sigmoid(gate)
        dgu_ref[0] = (d * up * (sig * (1.0 + gate * (1.0 - sig)))).astype(dgu_ref.dtype)
        dgu_ref[1] = (d * gate * sig).astype(dgu_ref.dtype)

    return pl.pallas_call(
        body, grid=(nb, s // tr),
        in_specs=[pl.BlockSpec((None, 2, tr, hb), lambda i, r: (i, 0, r, 0)),
                  pl.BlockSpec((None, tr, hb), lambda i, r: (i, r, 0))],
        out_specs=pl.BlockSpec((None, 2, tr, hb), lambda i, r: (i, 0, r, 0)),
        out_shape=jax.ShapeDtypeStruct(gu.shape, BF16),
        name=name, compiler_params=_cparams(("parallel", "parallel")),
    )(gu, da)


def loss_head(y, target, *, tr, name):
    s, d = y.shape

    def body(y_ref, t_ref, l_ref, dy_ref):
        err = y_ref[...] - t_ref[...]
        dy_ref[...] = err * (1.0 / d)
        part = 0.5 * jnp.sum(jnp.mean(err * err, axis=-1, keepdims=True), axis=0, keepdims=True)

        @pl.when(pl.program_id(0) == 0)
        def _():
            l_ref[...] = jnp.zeros_like(l_ref)

        l_ref[...] += jnp.broadcast_to(part, l_ref.shape)

    row = pl.BlockSpec((tr, d), lambda i: (i, 0))
    return pl.pallas_call(
        body, grid=(s // tr,), in_specs=[row, row],
        out_specs=[pl.BlockSpec((8, LANE), lambda i: (0, 0)), row],
        out_shape=[jax.ShapeDtypeStruct((8, LANE), F32), jax.ShapeDtypeStruct((s, d), F32)],
        name=name, compiler_params=_cparams(("arbitrary",)),
    )(y, target)


def _adamw_math(w, g, m, v):
    m = ADAM_B1 * m + (1.0 - ADAM_B1) * g
    v = ADAM_B2 * v + (1.0 - ADAM_B2) * jnp.square(g)
    m_hat = m / (1.0 - ADAM_B1 ** ADAM_STEP)
    v_hat = v / (1.0 - ADAM_B2 ** ADAM_STEP)
    delta = -ADAM_LR * (m_hat / (jnp.sqrt(v_hat) + ADAM_EPS) + ADAM_WD * w)
    return delta, m, v


def adamw_sum(w, m, v, parts, *, tr, name):
    n, r, c = parts.shape
    tr = _pick_rows(r, tr)

    def body(w_ref, m_ref, v_ref, p_ref, g_ref, d_ref, nm_ref, nv_ref):
        g = p_ref[0].astype(F32)
        for i in range(1, n):
            g = g + p_ref[i].astype(F32)
        delta, nm, nv = _adamw_math(w_ref[...], g, m_ref[...], v_ref[...])
        g_ref[...] = g
        d_ref[...] = delta
        nm_ref[...] = nm
        nv_ref[...] = nv

    row = pl.BlockSpec((tr, c), lambda i: (i, 0))
    shape = jax.ShapeDtypeStruct((r, c), F32)
    return pl.pallas_call(
        body, grid=(r // tr,),
        in_specs=[row, row, row, pl.BlockSpec((n, tr, c), lambda i: (0, i, 0))],
        out_specs=[row] * 4, out_shape=[shape] * 4,
        name=name, compiler_params=_cparams(("parallel",)),
    )(w, m, v, parts)


def _pick_rows(r, pref):
    t = min(pref, r)
    while r % t or (t % 16 and t != r):
        t -= 1
    return t


_HBM = pl.BlockSpec(memory_space=pltpu.HBM)
_MESH = pl.DeviceIdType.MESH


def _flat_index(px, py, pc):
    return 4 * px + 2 * py + pc


def _gather_ops(x_ref, out_ref, send_sems, recv_sems, local_sem):
    x, y, c = lax.axis_index("x"), lax.axis_index("y"), lax.axis_index("c")
    me, sibling = (x, y, c), (x, y, 1 - c)
    chips = [(1 - x, y), (x, 1 - y), (1 - x, 1 - y)]

    def slot(p):
        return out_ref.at[_flat_index(*p)]

    def copy(k, block, to, src=None):
        return pltpu.make_async_remote_copy(
            src_ref=slot(block) if src is None else src, dst_ref=slot(block),
            send_sem=send_sems.at[k], recv_sem=recv_sems.at[k], device_id=to, device_id_type=_MESH)

    mine = pltpu.make_async_copy(x_ref, slot(me), local_sem)
    first = [copy(0, me, sibling, src=x_ref)]
    first += [copy(1 + j, me, (*chip, c), src=x_ref) for j, chip in enumerate(chips)]
    passed = [copy(4 + j, (*chip, c), sibling) for j, chip in enumerate(chips)]

    def start():
        mine.start()
        for cp in first:
            cp.start()

    def finish():
        for j, chip in enumerate(chips):
            copy(1 + j, (*chip, c), me).wait_recv()
            passed[j].start()
        copy(0, sibling, me).wait_recv()
        for j, chip in enumerate(chips):
            copy(4 + j, (*chip, 1 - c), me).wait_recv()
        for cp in first + passed:
            cp.wait_send()
        mine.wait()

    return start, finish


def _exchange_ops(p_ref, out_ref, send_sems, recv_sems, local_sem):
    x, y, c = lax.axis_index("x"), lax.axis_index("y"), lax.axis_index("c")
    me = _flat_index(x, y, c)
    peers = [(1 - x if k & 4 else x, 1 - y if k & 2 else y, 1 - c if k & 1 else c) for k in range(1, N_DEV)]

    def copy(k, peer, dst_slot):
        return pltpu.make_async_remote_copy(
            src_ref=p_ref.at[_flat_index(*peer)], dst_ref=out_ref.at[dst_slot],
            send_sem=send_sems.at[k], recv_sem=recv_sems.at[k], device_id=peer, device_id_type=_MESH)

    mine = pltpu.make_async_copy(p_ref.at[me], out_ref.at[me], local_sem)
    sends = [copy(k, peer, me) for k, peer in enumerate(peers)]

    def start():
        mine.start()
        for cp in sends:
            cp.start()

    def finish():
        for k, peer in enumerate(peers):
            copy(k, peer, _flat_index(*peer)).wait_recv()
        for cp in sends:
            cp.wait_send()
        mine.wait()

    return start, finish


_COMM = {"gather": _gather_ops, "exchange": _exchange_ops}
_COMM_SEMS = [pltpu.SemaphoreType.DMA((N_DEV - 1,)), pltpu.SemaphoreType.DMA((N_DEV - 1,)), pltpu.SemaphoreType.DMA]


def _comm_out_shape(kind, arr):
    return jax.ShapeDtypeStruct(((N_DEV,) + arr.shape) if kind == "gather" else arr.shape, arr.dtype)


def _comm_call(kind, arr, name):
    def body(src_ref, dst_ref, send_sems, recv_sems, local_sem):
        start, finish = _COMM[kind](src_ref, dst_ref, send_sems, recv_sems, local_sem)
        start()
        finish()

    return pl.pallas_call(body, out_shape=_comm_out_shape(kind, arr), in_specs=[_HBM], out_specs=_HBM,
                          scratch_shapes=_COMM_SEMS, name=name)(arr)


def all_gather(x, *, name):
    return _comm_call("gather", x, name)


def exchange(parts, *, name):
    return _comm_call("exchange", parts, name)


TR = 256
TR_WIDE = 128
BLK = 256
CONV_CHUNK = 512
DT_PAD = LANE

BIG = ("w_in", "w_branch", "w_out", "ffn_w_gu", "ffn_w_down")
SMALL = ("norm_mix_pre", "norm_mix_post", "norm_ffn_pre", "norm_ffn_post", "b_gate", "ret_gn_w", "ssd_conv_w",
         "ssd_conv_b", "ssd_dt_bias", "ssd_a_log", "ssd_d", "ssd_norm_w")


def _row(v):
    return v.reshape(1, -1)


def _pad_lanes(v):
    return jnp.pad(v.reshape(1, -1), ((0, 0), (0, LANE - v.shape[-1])))


def _assemble_w_in(g, d):
    full = jnp.transpose(g, (1, 0, 2)).reshape(d, -1)
    n_main, n_dt = 5 * d, full.shape[1] - 8 * d
    main, dt, gates = full[:, :n_main], full[:, n_main:n_main + n_dt], full[:, n_main + n_dt:]
    return jnp.concatenate([main, gates], axis=1), jnp.pad(dt, ((0, 0), (0, DT_PAD - n_dt)))


def _split_dw_in(dw, dw_dt, d, n_dt):
    n_main = 5 * d
    full = jnp.concatenate([dw[:, :n_main], dw_dt[:, :n_dt], dw[:, n_main:]], axis=1)
    return jnp.transpose(full.reshape(d, N_DEV, -1), (1, 0, 2))


def _layer_fwd(x, lw, cosf, sinf, next_shards):
    s, d = x.shape
    heads = d // 2 // HEAD
    nxt = {}
    (h,) = rowwise(f_rms_pre, [x], [lw["norm_mix_pre"]], [(d, BF16)], tr=TR, name="mix_pre_norm")
    if next_shards is None:
        proj = matmul(h, lw["w_cat"], tn=2048, name="in_proj")
    else:
        proj, (nxt["w_in"],) = matmul(h, lw["w_cat"], tn=2048, hosted=[("gather", next_shards["w_in"])],
                                      name="in_proj_gather")
    dt_raw = matmul(h, lw["w_dt"], name="in_proj_dt")
    ret_cols = tuple(i * heads for i in range(4))
    y_ret, ret_states = retention_fwd(proj, ret_cols, lw["ret_gn_w"], cosf, sinf, heads, blk=BLK, name="retention_fwd")
    y_sb, sb_run, sb_visited = sb_attention_fwd(proj, 4 * heads, 5 * heads, 6 * heads, heads, blk=BLK,
                                                name="stickbreak_fwd")
    u_pad = jnp.pad(proj[:, 4 * d:5 * d], ((CONV_PAD, CONV_PAD), (0, 0)))
    xc = ssd_conv_fwd(u_pad, lw["conv_taps"], lw["ssd_conv_b"], chunk=CONV_CHUNK, name="ssd_conv_fwd")
    y_scan, ssd_states = ssd_scan_fwd(xc, dt_raw, 0, lw["ssd_dt_bias"], lw["ssd_a_log"], lw["ssd_d"], blk=BLK,
                                      name="ssd_scan_fwd")
    z_spec = (proj, d // 2, 7)
    (y_ssd,) = rowwise(f_ssd_gate, [y_scan, z_spec], [lw["ssd_norm_w"]], [(d // 2, BF16)], tr=TR, name="ssd_gate_norm")
    y3 = jnp.stack([y_ret, y_sb, y_ssd])
    u3 = matmul(y3, lw["w_branch"], lead_a="batch", lead_b="batch", name="branch_proj")
    merge_rows = [(u3, d, 0, i) for i in range(3)] + [(proj, d, 5 + i) for i in range(3)]
    (merged,) = rowwise(f_merge, merge_rows, lw["b_gate"], [(d, BF16)], tr=TR_WIDE, name="gate_merge")
    o = matmul(merged, lw["w_out"], name="out_proj")
    (x1,) = rowwise(f_rms_post, [x, o], [lw["norm_mix_post"]], [(d, F32)], tr=TR, name="mix_post_norm")
    (h2,) = rowwise(f_rms_pre, [x1], [lw["norm_ffn_pre"]], [(d, BF16)], tr=TR, name="ffn_pre_norm")
    if next_shards is None:
        gu = matmul(h2, lw["ffn_w_gu"], lead_b="batch", name="ffn_up")
    else:
        rest = [n for n in BIG if n != "w_in"]
        gu, got = matmul(h2, lw["ffn_w_gu"], lead_b="batch", hosted=[("gather", next_shards[n]) for n in rest],
                         name="ffn_up_gather")
        nxt.update(zip(rest, got))
    gu = gu.reshape(N_DEV, 2, s, -1)
    act = swiglu_fwd(gu, tr=TR, name="swiglu_fwd")
    f = matmul(act, lw["ffn_w_down"], lead_a="k", lead_b="k", name="ffn_down")
    (x2,) = rowwise(f_rms_post, [x1, f], [lw["norm_ffn_post"]], [(d, F32)], tr=TR, name="ffn_post_norm")
    res = dict(x=x, h=h, proj=proj, dt_raw=dt_raw, ret_states=ret_states, sb_run=sb_run, sb_visited=sb_visited, xc=xc,
               y_scan=y_scan, ssd_states=ssd_states, y3=y3, u3=u3, merged=merged, o=o, x1=x1, h2=h2, gu=gu, act=act, f=f)
    return x2, res, (nxt if next_shards is not None else None)


def _layer_bwd(dx2, res, lw, cosf, sinf, pending):
    x, proj = res["x"], res["proj"]
    s, d = x.shape
    heads = d // 2 // HEAD
    n_dt = d // 2 // SSD_P
    got = {}
    df, dn_ffn_post = rowwise_vjp(f_rms_post, [res["x1"], res["f"]], [lw["norm_ffn_post"]], [dx2],
                                  [(1, BF16, None)], [0], tr=TR, name="ffn_post_norm_bwd")
    d_act = matmul(df, lw["ffn_w_down"], tb=True, lead_b="batch", name="ffn_down_dx")
    dw_down = matmul(res["act"], df, ta=True, lead_a="batch", out_dtype=BF16, name="ffn_down_dw")
    dgu = swiglu_bwd(res["gu"], d_act, tr=TR, name="swiglu_bwd").reshape(2 * N_DEV, s, -1)
    if pending is None:
        dh2 = matmul(dgu, lw["ffn_w_gu"], tb=True, lead_a="k", lead_b="k", name="ffn_up_dx")
        dw_gu = matmul(res["h2"], dgu, ta=True, lead_b="batch", out_dtype=BF16, name="ffn_up_dw")
    else:
        dh2, (got["prev_w_branch"], got["prev_w_out"]) = matmul(
            dgu, lw["ffn_w_gu"], tb=True, lead_a="k", lead_b="k",
            hosted=[("exchange", pending["w_branch"]), ("exchange", pending["w_out"])], name="ffn_up_dx_exchange")
        dw_gu, (got["prev_w_in"],) = matmul(res["h2"], dgu, ta=True, lead_b="batch", out_dtype=BF16,
                                            hosted=[("exchange", pending["w_in"])], name="ffn_up_dw_exchange")
    dw_gu = dw_gu.reshape(N_DEV, 2 * d, -1)
    dx1, dn_ffn_pre = rowwise_vjp(f_rms_pre, [res["x1"]], [lw["norm_ffn_pre"]], [dh2], [(0, F32, dx2)], [0],
                                  tr=TR, name="ffn_pre_norm_bwd")
    do, dn_mix_post = rowwise_vjp(f_rms_post, [x, res["o"]], [lw["norm_mix_post"]], [dx1], [(1, BF16, None)], [0],
                                  tr=TR, name="mix_post_norm_bwd")
    dmerged = matmul(do, lw["w_out"], tb=True, name="out_proj_dx")
    dw_out = matmul(res["merged"], do, ta=True, out_dtype=BF16, name="out_proj_dw")
    merge_rows = [(res["u3"], d, 0, i) for i in range(3)] + [(proj, d, 5 + i) for i in range(3)]
    mg = rowwise_vjp(f_merge, merge_rows, lw["b_gate"], [dmerged], [(i, BF16, None) for i in range(6)], [0, 1, 2],
                     tr=TR_WIDE, name="gate_merge_bwd")
    du3 = jnp.stack(mg[:3])
    d_gate_logits, db_gate = mg[3:6], mg[6:9]
    dy3 = matmul(du3, lw["w_branch"], tb=True, lead_a="batch", lead_b="batch", name="branch_proj_dx")
    dw_branch = matmul(res["y3"], du3, ta=True, lead_a="batch", lead_b="batch", out_dtype=BF16, name="branch_proj_dw")
    ret_cols = tuple(i * heads for i in range(4))
    dq, dk, dv, dg, d_gn = retention_bwd(proj, ret_cols, lw["ret_gn_w"], cosf, sinf, res["ret_states"], dy3[0], heads,
                                         blk=BLK, name="retention_bwd")
    dsq, dsk, dsv = sb_attention_bwd(proj, dy3[1], res["sb_run"], res["sb_visited"], 4 * heads, 5 * heads, 6 * heads,
                                     heads, blk=BLK, name="stickbreak_bwd")
    z_spec = (proj, d // 2, 7)
    dy_scan, dz, d_ssd_norm = rowwise_vjp(f_ssd_gate, [res["y_scan"], z_spec], [lw["ssd_norm_w"]], [dy3[2]],
                                          [(0, F32, None), (1, BF16, None)], [0], tr=TR, name="ssd_gate_norm_bwd")
    dxs, dbm, dcm, ddt, d_dtb, d_alog, d_dskip = ssd_scan_bwd(
        res["xc"], res["dt_raw"], 0, lw["ssd_dt_bias"], lw["ssd_a_log"], lw["ssd_d"], res["ssd_states"], dy_scan,
        blk=BLK, name="ssd_scan_bwd")
    dxc_pad = jnp.pad(jnp.concatenate([dxs, dbm, dcm], axis=1), ((0, CONV_PAD), (0, 0)))
    u_pad = jnp.pad(proj[:, 4 * d:5 * d], ((CONV_PAD, CONV_PAD), (0, 0)))
    du, d_taps, d_conv_b = ssd_conv_bwd(u_pad, lw["conv_taps"], lw["ssd_conv_b"], dxc_pad, chunk=CONV_CHUNK,
                                        name="ssd_conv_bwd")
    dproj = jnp.concatenate([dq, dk, dv, dg, dsq, dsk, dsv, dz, du, *d_gate_logits], axis=1)
    dh_dt = matmul(ddt, lw["w_dt"], tb=True, name="in_proj_dt_dx")
    dh, (got["ffn_w_down"], got["ffn_w_gu"]) = matmul(
        dproj, lw["w_cat"], tb=True, add=dh_dt, hosted=[("exchange", dw_down), ("exchange", dw_gu)],
        name="in_proj_dx_exchange")
    dw_cat = matmul(res["h"], dproj, ta=True, tn=2048, out_dtype=BF16, name="in_proj_dw")
    dw_dt = matmul(res["h"], ddt, ta=True, out_dtype=BF16, name="in_proj_dt_dw")
    dx, dn_mix_pre = rowwise_vjp(f_rms_pre, [x], [lw["norm_mix_pre"]], [dh], [(0, F32, dx1)], [0], tr=TR,
                                 name="mix_pre_norm_bwd")
    mine = dict(
        w_in=_split_dw_in(dw_cat, dw_dt, d, n_dt),
        w_branch=jnp.transpose(dw_branch.reshape(3, d // 2, N_DEV, -1), (2, 0, 1, 3)).reshape(N_DEV, 3 * d // 2, -1),
        w_out=dw_out.reshape(N_DEV, d // N_DEV, d),
    )
    small = dict(
        norm_mix_pre=dn_mix_pre[0], norm_mix_post=dn_mix_post[0], norm_ffn_pre=dn_ffn_pre[0],
        norm_ffn_post=dn_ffn_post[0], b_gate=jnp.concatenate([b[0] for b in db_gate]), ret_gn_w=d_gn[0],
        ssd_conv_w=d_taps, ssd_conv_b=d_conv_b[0], ssd_dt_bias=d_dtb[0, :n_dt], ssd_a_log=d_alog[0, :n_dt],
        ssd_d=d_dskip[0, :n_dt], ssd_norm_w=d_ssd_norm[0],
    )
    return dx, got, mine, small


def _adam_rows(cols):
    return max(8, (1 << 17) // cols // 8 * 8)


def kernel(x, positions, norm_mix_pre, norm_mix_post, norm_ffn_pre, norm_ffn_post, w_in, b_gate, ret_gn_w, ssd_conv_w, ssd_conv_b, ssd_dt_bias, ssd_a_log, ssd_d, ssd_norm_w, w_branch_ret, w_branch_sb, w_branch_ssd, w_out, ffn_w_gate, ffn_w_up, ffn_w_down, loss_target, m_norm_mix_pre, m_norm_mix_post, m_norm_ffn_pre, m_norm_ffn_post, m_w_in, m_b_gate, m_ret_gn_w, m_ssd_conv_w, m_ssd_conv_b, m_ssd_dt_bias, m_ssd_a_log, m_ssd_d, m_ssd_norm_w, m_w_branch_ret, m_w_branch_sb, m_w_branch_ssd, m_w_out, m_ffn_w_gate, m_ffn_w_up, m_ffn_w_down, v_norm_mix_pre, v_norm_mix_post, v_norm_ffn_pre, v_norm_ffn_post, v_w_in, v_b_gate, v_ret_gn_w, v_ssd_conv_w, v_ssd_conv_b, v_ssd_dt_bias, v_ssd_a_log, v_ssd_d, v_ssd_norm_w, v_w_branch_ret, v_w_branch_sb, v_w_branch_ssd, v_w_out, v_ffn_w_gate, v_ffn_w_up, v_ffn_w_down):
    depth = w_in.shape[0]
    s, d = x.shape[1], x.shape[2]
    axes = ("x", "y", "c")
    me = _flat_index(lax.axis_index("x"), lax.axis_index("y"), lax.axis_index("c"))

    def big_shards(w_in_, br_ret, br_sb, br_ssd, w_out_, gate, up, down):
        return dict(w_in=w_in_, w_branch=jnp.concatenate([br_ret, br_sb, br_ssd], axis=1), w_out=w_out_,
                    ffn_w_gu=jnp.concatenate([gate, up], axis=1), ffn_w_down=down)

    big_w = big_shards(w_in, w_branch_ret, w_branch_sb, w_branch_ssd, w_out, ffn_w_gate, ffn_w_up, ffn_w_down)
    big_m = big_shards(m_w_in, m_w_branch_ret, m_w_branch_sb, m_w_branch_ssd, m_w_out, m_ffn_w_gate, m_ffn_w_up, m_ffn_w_down)
    big_v = big_shards(v_w_in, v_w_branch_ret, v_w_branch_sb, v_w_branch_ssd, v_w_out, v_ffn_w_gate, v_ffn_w_up, v_ffn_w_down)

    taps_all = all_gather(ssd_conv_w.reshape(-1, LANE), name="gather_conv_w")
    taps_all = jnp.transpose(taps_all.reshape(N_DEV, depth, SSD_K, -1), (1, 2, 0, 3)).reshape(depth, SSD_K, -1)

    cosf, sinf = rope_tables(positions.reshape(s), tr=TR)
    small_w = dict(norm_mix_pre=norm_mix_pre, norm_mix_post=norm_mix_post, norm_ffn_pre=norm_ffn_pre,
                   norm_ffn_post=norm_ffn_post, b_gate=b_gate, ret_gn_w=ret_gn_w, ssd_conv_b=ssd_conv_b,
                   ssd_dt_bias=ssd_dt_bias, ssd_a_log=ssd_a_log, ssd_d=ssd_d, ssd_norm_w=ssd_norm_w, taps=taps_all)

    def layer_weights(sw, gathered):
        lw = dict(gathered)
        for n in ("norm_mix_pre", "norm_mix_post", "norm_ffn_pre", "norm_ffn_post", "ret_gn_w", "ssd_conv_b", "ssd_norm_w"):
            lw[n] = _row(sw[n])
        for n in ("ssd_dt_bias", "ssd_a_log", "ssd_d"):
            lw[n] = _pad_lanes(sw[n])
        lw["b_gate"] = [_row(sw["b_gate"][i * d:(i + 1) * d]) for i in range(3)]
        lw["conv_taps"] = [sw["taps"][k:k + 1] for k in range(SSD_K)]
        return lw

    def layer_slice(t, l):
        return {n: a[l] for n, a in t.items()}

    def bf16_shards(l):
        return {n: big_w[n][l].astype(BF16) for n in BIG}

    def arrange(g):
        w_cat, w_dt = _assemble_w_in(g["w_in"], d)
        return dict(
            w_cat=w_cat, w_dt=w_dt,
            w_branch=jnp.transpose(g["w_branch"].reshape(N_DEV, 3, d // 2, -1), (1, 2, 0, 3)).reshape(3, d // 2, d),
            w_out=g["w_out"].reshape(d, d),
            ffn_w_gu=g["ffn_w_gu"].reshape(2 * N_DEV, d, -1),
            ffn_w_down=g["ffn_w_down"],
        )

    def adam(l, n, parts):
        w2 = big_w[n][l]
        return adamw_sum(w2, big_m[n][l], big_v[n][l], parts.reshape(N_DEV, *w2.shape), tr=_adam_rows(w2.shape[1]),
                         name="adamw_" + n)

    xs, saved = x.reshape(s, d), []
    gathered = {n: all_gather(a, name="gather_" + n) for n, a in bf16_shards(0).items()}
    for l in range(depth):
        lw = layer_weights(layer_slice(small_w, l), arrange(gathered))
        xs, res, gathered = _layer_fwd(xs, lw, cosf, sinf, bf16_shards(l + 1) if l + 1 < depth else None)
        saved.append((res, lw))
    loss_tile, dy = loss_head(xs, loss_target.reshape(s, d), tr=TR, name="loss_head")
    loss = lax.psum(loss_tile[0, 0], axes)

    dx, pending, small_layers = dy, None, [None] * depth
    big_layers = [dict() for _ in range(depth)]
    for l in reversed(range(depth)):
        res, lw = saved[l]
        dx, got, pending, small_layers[l] = _layer_bwd(dx, res, lw, cosf, sinf, pending)
        for n in ("ffn_w_gu", "ffn_w_down"):
            big_layers[l][n] = adam(l, n, got[n])
        if l + 1 < depth:
            for n in ("w_in", "w_branch", "w_out"):
                big_layers[l + 1][n] = adam(l + 1, n, got["prev_" + n])
    for n in ("w_in", "w_branch", "w_out"):
        big_layers[0][n] = adam(0, n, exchange(pending[n], name="exchange_" + n))
    big_out = {n: [jnp.stack([big_layers[l][n][i] for l in range(depth)]) for i in range(4)] for n in BIG}
    small_g = {n: jnp.stack([small_layers[l][n] for l in range(depth)]) for n in SMALL}

    n_dt = ssd_dt_bias.shape[-1]
    small_in = dict(norm_mix_pre=(norm_mix_pre, m_norm_mix_pre, v_norm_mix_pre), norm_mix_post=(norm_mix_post, m_norm_mix_post, v_norm_mix_post),
                    norm_ffn_pre=(norm_ffn_pre, m_norm_ffn_pre, v_norm_ffn_pre), norm_ffn_post=(norm_ffn_post, m_norm_ffn_post, v_norm_ffn_post),
                    b_gate=(b_gate, m_b_gate, v_b_gate), ret_gn_w=(ret_gn_w, m_ret_gn_w, v_ret_gn_w),
                    ssd_conv_b=(ssd_conv_b, m_ssd_conv_b, v_ssd_conv_b), ssd_dt_bias=(ssd_dt_bias, m_ssd_dt_bias, v_ssd_dt_bias),
                    ssd_a_log=(ssd_a_log, m_ssd_a_log, v_ssd_a_log), ssd_d=(ssd_d, m_ssd_d, v_ssd_d),
                    ssd_norm_w=(ssd_norm_w, m_ssd_norm_w, v_ssd_norm_w))
    rep = [n for n in SMALL if n != "ssd_conv_w"]

    def pack(arrs):
        flat = jnp.concatenate([a.reshape(-1) for a in arrs])
        rows = -(-flat.shape[0] // (16 * LANE)) * 16
        return jnp.pad(flat, (0, rows * LANE - flat.shape[0])).reshape(rows, LANE)

    conv_g = small_g["ssd_conv_w"]
    g_pack = pack([small_g[n] for n in rep] + [conv_g])
    g_all = all_gather(g_pack, name="gather_small_grads")
    zeros_conv = jnp.zeros_like(conv_g)
    w_pack, m_pack, v_pack = (pack([small_in[n][i] for n in rep] + [zeros_conv]) for i in range(3))
    sm = adamw_sum(w_pack, m_pack, v_pack, g_all, tr=TR, name="adamw_small")

    def unpack(p):
        flat, out, off = p.reshape(-1), {}, 0
        for n in rep:
            shp = small_in[n][0].shape
            size = math.prod(shp)
            out[n] = flat[off:off + size].reshape(shp)
            off += size
        out["conv_sum"] = flat[off:off + conv_g.size].reshape(conv_g.shape)
        return out

    sm = [unpack(p) for p in sm]
    ch = ssd_conv_w.shape[-1]
    conv_mine = lax.dynamic_slice_in_dim(sm[0]["conv_sum"], me * ch, ch, axis=2)
    conv_out = adamw_sum(ssd_conv_w.reshape(-1, LANE), m_ssd_conv_w.reshape(-1, LANE), v_ssd_conv_w.reshape(-1, LANE),
                         conv_mine.reshape(1, -1, LANE), tr=TR, name="adamw_conv_w")
    for i in range(4):
        sm[i]["ssd_conv_w"] = conv_out[i].reshape(ssd_conv_w.shape)

    def big_named(i):
        o = {n: big_out[n][i] for n in BIG}
        br = o["w_branch"].reshape(depth, 3, d // 2, -1)
        gu = o["ffn_w_gu"].reshape(depth, 2, d, -1)
        return dict(w_in=o["w_in"].reshape(w_in.shape), w_branch_ret=br[:, 0], w_branch_sb=br[:, 1], w_branch_ssd=br[:, 2],
                    w_out=o["w_out"].reshape(w_out.shape), ffn_w_gate=gu[:, 0], ffn_w_up=gu[:, 1],
                    ffn_w_down=o["ffn_w_down"].reshape(ffn_w_down.shape))

    order = ["norm_mix_pre", "norm_mix_post", "norm_ffn_pre", "norm_ffn_post", "w_in", "b_gate", "ret_gn_w", "ssd_conv_w",
             "ssd_conv_b", "ssd_dt_bias", "ssd_a_log", "ssd_d", "ssd_norm_w", "w_branch_ret", "w_branch_sb", "w_branch_ssd",
             "w_out", "ffn_w_gate", "ffn_w_up", "ffn_w_down"]
    outs = [loss, dx.reshape(x.shape)]
    for i in range(4):
        named = {**sm[i], **big_named(i)}
        outs += [named[n] for n in order]
    return tuple(outs)
```

```python
import functools
import math

import jax
import jax.numpy as jnp
import numpy as np
from jax import lax
from jax.experimental import pallas as pl
from jax.experimental.pallas import tpu as pltpu

F32 = jnp.float32
BF16 = jnp.bfloat16

N_DEV = 8
HEAD = 128
SSD_P = 64
SSD_G = 4
SSD_N = 128
SSD_K = 4
CHUNK = 64
NORM_EPS = 1e-6
ROPE_BASE = 10000.0
LANE = 128
VMEM_LIMIT = 56 * 1024 * 1024

ADAM_LR, ADAM_B1, ADAM_B2, ADAM_EPS, ADAM_WD, ADAM_STEP = 0.001, 0.9, 0.999, 1e-08, 0.01, 10


def _cparams(sem):
    return pltpu.CompilerParams(dimension_semantics=sem, vmem_limit_bytes=VMEM_LIMIT)


def _pick(n, pref):
    if n <= pref:
        return n
    t = pref
    while t >= LANE:
        if n % t == 0:
            return t
        t -= LANE
    return n


def matmul(a, b, *, ta=False, tb=False, lead_a=None, lead_b=None, out_dtype=F32, add=None, hosted=(),
           tm=1024, tn=1024, tk=2048, name="mm"):
    la, lb = lead_a is not None, lead_b is not None
    a2, b2 = a.shape[1:] if la else a.shape, b.shape[1:] if lb else b.shape
    (kd_a, m) = a2 if ta else a2[::-1]
    (kd_b, n) = b2[::-1] if tb else b2
    assert kd_a == kd_b, (a.shape, b.shape)
    nlead = a.shape[0] if la else (b.shape[0] if lb else 1)
    batch = "batch" in (lead_a, lead_b)
    kblocks = nlead if "k" in (lead_a, lead_b) else 1
    if la and lb:
        assert lead_a == lead_b and a.shape[0] == b.shape[0]
    tm, tn, tk = _pick(m, tm), _pick(n, tn), _pick(kd_a, tk)
    kt = kd_a // tk
    nk = kt * kblocks
    grid = (m // tm, (nlead if batch else 1), n // tn, nk)

    def lead_idx(g, k):
        return g if batch else k // kt

    def a_map(i, g, j, k):
        idx = (k % kt, i) if ta else (i, k % kt)
        return ((lead_idx(g, k),) + idx) if la else idx

    def b_map(i, g, j, k):
        idx = (j, k % kt) if tb else (k % kt, j)
        return ((lead_idx(g, k),) + idx) if lb else idx

    a_blk = (tk, tm) if ta else (tm, tk)
    b_blk = (tn, tk) if tb else (tk, tn)
    a_spec = pl.BlockSpec(((None,) + a_blk) if la else a_blk, a_map)
    b_spec = pl.BlockSpec(((None,) + b_blk) if lb else b_blk, b_map)
    if batch:
        o_spec = pl.BlockSpec((None, tm, tn), lambda i, g, j, k: (g, i, j))
        o_shape = jax.ShapeDtypeStruct((nlead, m, n), out_dtype)
    else:
        o_spec = pl.BlockSpec((tm, tn), lambda i, g, j, k: (i, j))
        o_shape = jax.ShapeDtypeStruct((m, n), out_dtype)
    dims = (((0 if ta else 1,), (1 if tb else 0,)), ((), ()))

    n_add, n_host = int(add is not None), len(hosted)
    n_acc = int(nk > 1)

    def body(a_ref, b_ref, *rest):
        c_ref = rest[0] if n_add else None
        src_refs = rest[n_add:n_add + n_host]
        o_ref = rest[n_add + n_host]
        dst_refs = rest[n_add + n_host + 1:n_add + 2 * n_host + 1]
        acc_ref = rest[n_add + 2 * n_host + 1] if n_acc else None
        sems = rest[n_add + 2 * n_host + 1 + n_acc:]
        k = pl.program_id(3)
        ids = [pl.program_id(ax) for ax in range(4)]
        comms = [_COMM[kind](src_refs[c], dst_refs[c], *sems[3 * c:3 * c + 3]) for c, (kind, _) in enumerate(hosted)]

        if hosted:
            @pl.when(functools.reduce(jnp.logical_and, [i == 0 for i in ids]))
            def _():
                for start, _ in comms:
                    start()

        def product():
            return lax.dot_general(a_ref[...].astype(BF16), b_ref[...].astype(BF16), dims, preferred_element_type=F32)

        if n_acc:
            @pl.when(k == 0)
            def _():
                acc_ref[...] = jnp.zeros_like(acc_ref) if c_ref is None else c_ref[...].astype(F32)

            acc_ref[...] += product()

            @pl.when(k == nk - 1)
            def _():
                o_ref[...] = acc_ref[...].astype(o_ref.dtype)
        else:
            o_ref[...] = (product() if c_ref is None else product() + c_ref[...].astype(F32)).astype(o_ref.dtype)

        if hosted:
            @pl.when(functools.reduce(jnp.logical_and, [i == g - 1 for i, g in zip(ids, grid)]))
            def _():
                for _, finish in comms:
                    finish()

    extra = ([] if add is None else [add]) + [arr for _, arr in hosted]
    extra_specs = ([] if add is None else [o_spec]) + [_HBM] * n_host
    out_shapes = [o_shape] + [_comm_out_shape(kind, arr) for kind, arr in hosted]
    scratch = [pltpu.VMEM((tm, tn), F32)] * n_acc + _COMM_SEMS * n_host
    sem = ("arbitrary",) * 4 if hosted else ("parallel", "parallel", "parallel", "arbitrary")
    res = pl.pallas_call(
        body, out_shape=out_shapes, grid=grid, in_specs=[a_spec, b_spec] + extra_specs,
        out_specs=[o_spec] + [_HBM] * n_host, scratch_shapes=scratch, name=name, compiler_params=_cparams(sem),
    )(a, b, *extra)
    return (res[0], res[1:]) if hosted else res[0]


def _row_spec(r, tr):
    if not isinstance(r, tuple):
        return r, pl.BlockSpec((tr, r.shape[-1]), lambda i: (i, 0))
    if len(r) == 3:
        arr, w, cb = r
        return arr, pl.BlockSpec((tr, w), lambda i: (i, cb))
    arr, w, cb, ld = r
    return arr, pl.BlockSpec((None, tr, w), lambda i: (ld, i, cb))


def _full_spec(c):
    nd = c.ndim
    return pl.BlockSpec(c.shape, lambda i: (0,) * nd)


def rowwise(fn, rows, consts, outs, *, tr, name):
    arrs, specs = zip(*[_row_spec(r, tr) for r in rows])
    n_rows = arrs[0].shape[-2]
    nr, nc = len(rows), len(consts)

    def body(*refs):
        vals = [r[...] for r in refs[:nr + nc]]
        res = fn(*vals)
        for o_ref, r in zip(refs[nr + nc:], res):
            o_ref[...] = r.astype(o_ref.dtype)

    return pl.pallas_call(
        body, grid=(n_rows // tr,),
        in_specs=list(specs) + [_full_spec(c) for c in consts],
        out_specs=[pl.BlockSpec((tr, w), lambda i: (i, 0)) for w, _ in outs],
        out_shape=[jax.ShapeDtypeStruct((n_rows, w), dt) for w, dt in outs],
        name=name, compiler_params=_cparams(("parallel",)),
    )(*arrs, *consts)


def rowwise_vjp(fn, rows, consts, cts, row_grads, const_grads, *, tr, name):
    arrs, specs = zip(*[_row_spec(r, tr) for r in rows])
    n_rows = arrs[0].shape[-2]
    nr, nc = len(rows), len(consts)
    ct_present = [c for c in cts if c is not None]
    ct_arrs, ct_specs = zip(*[_row_spec(c, tr) for c in ct_present])
    add_present = [g[2] for g in row_grads if g[2] is not None]
    add_arrs, add_specs = zip(*[_row_spec(c, tr) for c in add_present]) if add_present else ((), ())
    n_ct, n_add = len(ct_present), len(add_present)
    widths = [s.block_shape[-1] for s in specs]

    def body(*refs):
        ins = refs[:nr + nc]
        ct_refs = refs[nr + nc:nr + nc + n_ct]
        add_refs = refs[nr + nc + n_ct:nr + nc + n_ct + n_add]
        out_refs = refs[nr + nc + n_ct + n_add:]
        vals = [r[...] for r in ins]
        res, f_vjp = jax.vjp(fn, *vals)
        it = iter(ct_refs)
        ct_vals = tuple(next(it)[...].astype(r.dtype) if c is not None else jnp.zeros_like(r)
                        for c, r in zip(cts, res))
        grads = f_vjp(ct_vals)
        ita = iter(add_refs)
        for o_ref, (idx, _, add) in zip(out_refs, row_grads):
            g = grads[idx].astype(F32)
            if add is not None:
                g = g + next(ita)[...].astype(F32)
            o_ref[...] = g.astype(o_ref.dtype)
        first = pl.program_id(0) == 0
        for o_ref, idx in zip(out_refs[len(row_grads):], const_grads):
            g = grads[nr + idx].astype(F32)

            @pl.when(first)
            def _():
                o_ref[...] = g

            @pl.when(jnp.logical_not(first))
            def _():
                o_ref[...] += g

    out_specs = [pl.BlockSpec((tr, widths[idx]), lambda i: (i, 0)) for idx, _, _ in row_grads]
    out_shape = [jax.ShapeDtypeStruct((n_rows, widths[idx]), dt) for idx, dt, _ in row_grads]
    out_specs += [_full_spec(consts[idx]) for idx in const_grads]
    out_shape += [jax.ShapeDtypeStruct(consts[idx].shape, F32) for idx in const_grads]
    return pl.pallas_call(
        body, grid=(n_rows // tr,),
        in_specs=list(specs) + [_full_spec(c) for c in consts] + list(ct_specs) + list(add_specs),
        out_specs=out_specs, out_shape=out_shape,
        name=name, compiler_params=_cparams(("arbitrary",)),
    )(*arrs, *consts, *ct_arrs, *add_arrs)


def f_rms(x, w):
    xf = x.astype(F32)
    return xf * lax.rsqrt(jnp.mean(xf * xf, axis=-1, keepdims=True) + NORM_EPS) * w


def f_rms_pre(x, w):
    return (f_rms(x, w),)


def f_rms_post(x, o, w):
    return (x + f_rms(o, w),)


def f_merge(u0, u1, u2, g0, g1, g2, b0, b1, b2):
    return (jax.nn.sigmoid(g0 + b0) * u0 + jax.nn.sigmoid(g1 + b1) * u1 + jax.nn.sigmoid(g2 + b2) * u2,)


def f_ssd_gate(y, z, w):
    return (f_rms(y * jax.nn.silu(z), w),)


@jax.custom_vjp
def _swap_halves(x):
    return pltpu.roll(x, HEAD // 2, 1)


_swap_halves.defvjp(lambda x: (_swap_halves(x), None), lambda _, g: (_swap_halves(g),))


def rope_tables(positions, *, tr):
    s = positions.shape[0]
    half = HEAD // 2
    inv = ROPE_BASE ** (-2.0 * jnp.arange(half, dtype=F32) / HEAD)
    inv = jnp.concatenate([inv, inv]).reshape(1, HEAD)
    sign = jnp.concatenate([-jnp.ones((half,), F32), jnp.ones((half,), F32)]).reshape(1, HEAD)

    def fn(pos, inv, sign):
        ang = pos.astype(F32) * inv
        return jnp.cos(ang), jnp.sin(ang) * sign

    return rowwise(fn, [positions.reshape(s, 1)], [inv, sign], [(HEAD, F32), (HEAD, F32)], tr=tr, name="rope_tables")


def _ret_consts(n_heads, blk):
    lg = np.log1p(-np.exp2(-5.0 - np.arange(n_heads)))[:, None, None]
    i = np.arange(blk)
    dist = np.abs(i[:, None] - i[None, :])[None]
    allowed = ((i[None, :] // CHUNK) <= (i[:, None] // CHUNK))[None]
    dm = np.where(allowed, np.exp(lg * dist), 0.0)
    qd = np.broadcast_to(np.exp(lg * (i[None, :, None] + 1.0)), (n_heads, blk, HEAD))
    kd = np.broadcast_to(np.exp(lg * (blk - 1.0 - i[None, :, None])), (n_heads, blk, HEAD))
    cd = np.broadcast_to(np.exp(lg * blk), (n_heads, 1, HEAD))
    return [jnp.asarray(a, F32) for a in (dm, qd, kd, cd)]


def _ret_block(q, k, v, g, gnw, state, cosf, sinf, dm, qd, kd, cd):
    qr = q * cosf + _swap_halves(q) * sinf
    kr = (k * cosf + _swap_halves(k) * sinf) * (HEAD ** -0.5)
    vb = v.astype(BF16)
    scores = _dot(qr.astype(BF16), kr.astype(BF16), _NT) * dm
    o = _dot(scores.astype(BF16), vb) + _dot((qr * qd).astype(BF16), state.astype(BF16))
    new_state = state * cd + _dot((kr * kd).astype(BF16), vb, _TN)
    mu = jnp.mean(o, axis=-1, keepdims=True)
    var = jnp.mean(jnp.square(o - mu), axis=-1, keepdims=True)
    y = (o - mu) * lax.rsqrt(var + NORM_EPS) * gnw * jax.nn.silu(g)
    return y, new_state


def _ret_specs(n_heads, blk, nb, cols, reverse):
    jm = (lambda j: nb - 1 - j) if reverse else (lambda j: j)
    col = lambda c0: pl.BlockSpec((blk, HEAD), lambda h, j: (jm(j), c0 + h))
    tab = pl.BlockSpec((blk, HEAD), lambda h, j: (jm(j), 0))
    specs = [col(c) for c in cols]
    specs += [pl.BlockSpec((1, HEAD), lambda h, j: (0, h)), tab, tab]
    specs += [pl.BlockSpec((None, blk, blk), lambda h, j: (h, 0, 0)),
              pl.BlockSpec((None, blk, HEAD), lambda h, j: (h, 0, 0)),
              pl.BlockSpec((None, blk, HEAD), lambda h, j: (h, 0, 0)),
              pl.BlockSpec((None, 1, HEAD), lambda h, j: (h, 0, 0))]
    state = pl.BlockSpec((None, None, HEAD, HEAD), lambda h, j: (h, jm(j), 0, 0))
    out_col = pl.BlockSpec((blk, HEAD), lambda h, j: (jm(j), h))
    return specs, state, out_col


def retention_fwd(proj, cols, gn_w, cosf, sinf, n_heads, *, blk, name):
    s = proj.shape[0]
    nb = s // blk
    consts = _ret_consts(n_heads, blk)
    specs, state_spec, out_col = _ret_specs(n_heads, blk, nb, cols, False)

    def body(q_ref, k_ref, v_ref, g_ref, gn_ref, cos_ref, sin_ref, dm_ref, qd_ref, kd_ref, cd_ref,
             y_ref, st_ref, state):
        @pl.when(pl.program_id(1) == 0)
        def _():
            state[...] = jnp.zeros_like(state)

        st = state[...]
        st_ref[...] = st
        y, new_state = _ret_block(q_ref[...], k_ref[...], v_ref[...], g_ref[...], gn_ref[...], st,
                                  cos_ref[...], sin_ref[...], dm_ref[...], qd_ref[...], kd_ref[...], cd_ref[...])
        y_ref[...] = y.astype(y_ref.dtype)
        state[...] = new_state

    return pl.pallas_call(
        body, grid=(n_heads, nb), in_specs=specs, out_specs=[out_col, state_spec],
        out_shape=[jax.ShapeDtypeStruct((s, n_heads * HEAD), BF16),
                   jax.ShapeDtypeStruct((n_heads, nb, HEAD, HEAD), F32)],
        scratch_shapes=[pltpu.VMEM((HEAD, HEAD), F32)],
        name=name, compiler_params=_cparams(("parallel", "arbitrary")),
    )(proj, proj, proj, proj, gn_w, cosf, sinf, *consts)


def retention_bwd(proj, cols, gn_w, cosf, sinf, states, dy, n_heads, *, blk, name):
    s = proj.shape[0]
    nb = s // blk
    consts = _ret_consts(n_heads, blk)
    specs, state_spec, out_col = _ret_specs(n_heads, blk, nb, cols, True)

    def body(q_ref, k_ref, v_ref, g_ref, gn_ref, cos_ref, sin_ref, dm_ref, qd_ref, kd_ref, cd_ref,
             st_ref, dy_ref, dq_ref, dk_ref, dv_ref, dg_ref, dgn_ref, dstate):
        first = pl.program_id(1) == 0

        @pl.when(first)
        def _():
            dstate[...] = jnp.zeros_like(dstate)
            dgn_ref[...] = jnp.zeros_like(dgn_ref)

        tabs = (cos_ref[...], sin_ref[...], dm_ref[...], qd_ref[...], kd_ref[...], cd_ref[...])
        fn = lambda q, k, v, g, gnw, st: _ret_block(q, k, v, g, gnw, st, *tabs)
        _, f_vjp = jax.vjp(fn, q_ref[...], k_ref[...], v_ref[...], g_ref[...], gn_ref[...], st_ref[...])
        dq, dk, dv, dg, dgn, dst = f_vjp((dy_ref[...].astype(F32), dstate[...]))
        dq_ref[...] = dq.astype(dq_ref.dtype)
        dk_ref[...] = dk.astype(dk_ref.dtype)
        dv_ref[...] = dv.astype(dv_ref.dtype)
        dg_ref[...] = dg.astype(dg_ref.dtype)
        dgn_ref[...] += dgn
        dstate[...] = dst

    o_shape = jax.ShapeDtypeStruct((s, n_heads * HEAD), BF16)
    dy_spec = out_col
    if isinstance(dy, tuple):
        dy, lead = dy
        dy_spec = pl.BlockSpec((None, blk, HEAD), lambda h, j: (lead, nb - 1 - j, h))
    return pl.pallas_call(
        body, grid=(n_heads, nb), in_specs=specs + [state_spec, dy_spec],
        out_specs=[out_col, out_col, out_col, out_col, pl.BlockSpec((1, HEAD), lambda h, j: (0, h))],
        out_shape=[o_shape, o_shape, o_shape, o_shape, jax.ShapeDtypeStruct((1, n_heads * HEAD), F32)],
        scratch_shapes=[pltpu.VMEM((HEAD, HEAD), F32)],
        name=name, compiler_params=_cparams(("parallel", "arbitrary")),
    )(proj, proj, proj, proj, gn_w, cosf, sinf, *consts, states, dy)


CONV_PAD = 8


def _conv_pre(u_ext, taps, bias, n_out):
    n = u_ext.shape[0]
    views = [pltpu.roll(u_ext, n - (k + CONV_PAD - SSD_K + 1), 0)[:n_out] for k in range(SSD_K)]
    pre = bias
    for k in range(SSD_K):
        pre = pre + taps[k] * views[k]
    return pre, views


def ssd_conv_fwd(u_pad, taps, bias, *, chunk, name):
    s, c = u_pad.shape[0] - 2 * CONV_PAD, u_pad.shape[1]

    def body(u_ref, t0, t1, t2, t3, b_ref, o_ref):
        taps_v = [t[...] for t in (t0, t1, t2, t3)]
        bias_v = b_ref[...]

        @pl.loop(0, s // chunk)
        def _(ci):
            r0 = pl.multiple_of(ci * chunk, chunk)
            pre, _ = _conv_pre(u_ref[pl.ds(r0, chunk + CONV_PAD), :], taps_v, bias_v, chunk)
            o_ref[pl.ds(r0, chunk), :] = pre * jax.nn.sigmoid(pre)

    row = pl.BlockSpec((1, LANE), lambda i: (0, i))
    return pl.pallas_call(
        body, grid=(c // LANE,),
        in_specs=[pl.BlockSpec((s + 2 * CONV_PAD, LANE), lambda i: (0, i))] + [row] * 5,
        out_specs=pl.BlockSpec((s, LANE), lambda i: (0, i)),
        out_shape=jax.ShapeDtypeStruct((s, c), F32),
        name=name, compiler_params=_cparams(("parallel",)),
    )(u_pad, *taps, bias)


def ssd_conv_bwd(u_pad, taps, bias, dxc_pad, *, chunk, name):
    s, c = u_pad.shape[0] - 2 * CONV_PAD, u_pad.shape[1]
    ext = chunk + CONV_PAD

    def body(u_ref, t0, t1, t2, t3, b_ref, d_ref, du_ref, dw_ref, db_ref):
        taps_v = [t[...] for t in (t0, t1, t2, t3)]
        bias_v = b_ref[...]
        dw_ref[...] = jnp.zeros_like(dw_ref)
        db_ref[...] = jnp.zeros_like(db_ref)

        @pl.loop(0, s // chunk)
        def _(ci):
            r0 = pl.multiple_of(ci * chunk, chunk)
            pre, views = _conv_pre(u_ref[pl.ds(r0, ext + CONV_PAD), :], taps_v, bias_v, ext)
            sig = jax.nn.sigmoid(pre)
            dpre = d_ref[pl.ds(r0, ext), :] * (sig * (1.0 + pre * (1.0 - sig)))
            du = taps_v[SSD_K - 1] * dpre[:chunk]
            for k in range(SSD_K - 1):
                du = du + taps_v[k] * pltpu.roll(dpre, ext - (SSD_K - 1 - k), 0)[:chunk]
            du_ref[pl.ds(r0, chunk), :] = du.astype(du_ref.dtype)
            own = dpre[:chunk]
            for k in range(SSD_K):
                dw_ref[k:k + 1, :] += jnp.sum(own * views[k][:chunk], axis=0, keepdims=True)
            db_ref[...] += jnp.sum(own, axis=0, keepdims=True)

    row = pl.BlockSpec((1, LANE), lambda i: (0, i))
    return pl.pallas_call(
        body, grid=(c // LANE,),
        in_specs=[pl.BlockSpec((s + 2 * CONV_PAD, LANE), lambda i: (0, i))] + [row] * 5
        + [pl.BlockSpec((s + CONV_PAD, LANE), lambda i: (0, i))],
        out_specs=[pl.BlockSpec((s, LANE), lambda i: (0, i)), pl.BlockSpec((SSD_K, LANE), lambda i: (0, i)), row],
        out_shape=[jax.ShapeDtypeStruct((s, c), BF16), jax.ShapeDtypeStruct((SSD_K, c), F32),
                   jax.ShapeDtypeStruct((1, c), F32)],
        name=name, compiler_params=_cparams(("parallel",)),
    )(u_pad, *taps, bias, dxc_pad)


def _tri_dot(tri, x, passes=3):
    out = None
    rem = x
    for _ in range(passes):
        piece = rem.astype(BF16)
        rem = rem - piece.astype(F32)
        d = _dot(tri, piece)
        out = d if out is None else out + d
    return out


def _tri(n, upper):
    rr = lax.broadcasted_iota(jnp.int32, (n, n), 0)
    cc = lax.broadcasted_iota(jnp.int32, (n, n), 1)
    return ((rr <= cc) if upper else (rr >= cc)).astype(BF16)


@jax.custom_vjp
def _cumsum_rows(a):
    return _tri_dot(_tri(a.shape[0], False), a)


_cumsum_rows.defvjp(lambda a: (_cumsum_rows(a), None), lambda _, g: (_tri_dot(_tri(g.shape[0], True), g),))


def _softplus(x):
    return jnp.maximum(x, 0.0) + jnp.log(1.0 + jnp.exp(-jnp.abs(x)))


def _ssd_block(x, bm, cm, dtraw, dtb, alog, dsk, state_t, group):
    blk, width = x.shape
    e_heads = width // SSD_P
    dt = _softplus(dtraw + dtb)
    acum = _cumsum_rows(dt * (-jnp.exp(alog)))
    acum_t = acum.T
    lane_h = lax.broadcasted_iota(jnp.int32, (1, LANE), 1)
    sub_h = lax.broadcasted_iota(jnp.int32, (LANE, 1), 0)
    lane_e = lax.broadcasted_iota(jnp.int32, (1, width), 1) // SSD_P
    causal = lax.broadcasted_iota(jnp.int32, (blk, blk), 0) >= lax.broadcasted_iota(jnp.int32, (blk, blk), 1)
    last_row = lax.broadcasted_iota(jnp.int32, (blk, 1), 0) == blk - 1
    cb = _dot(cm.astype(BF16), bm.astype(BF16), _NT)
    y = jnp.zeros((blk, width), F32)
    dt_l = jnp.zeros((blk, width), F32)
    ac_l = jnp.zeros((blk, width), F32)
    d_l = jnp.zeros((1, width), F32)
    for e in range(e_heads):
        head = group * e_heads + e
        pick = lane_h == head
        col = jnp.sum(jnp.where(pick, acum, 0.0), axis=1, keepdims=True)
        dt_e = jnp.sum(jnp.where(pick, dt, 0.0), axis=1, keepdims=True)
        d_e = jnp.sum(jnp.where(pick, dsk, 0.0), axis=1, keepdims=True)
        row = jnp.sum(jnp.where(sub_h == head, acum_t, 0.0), axis=0, keepdims=True)
        mine = lane_e == e
        decay = jnp.exp(jnp.where(causal, col - row, -jnp.inf))
        y = y + _dot((cb * decay).astype(BF16), jnp.where(mine, x * dt_e, 0.0).astype(BF16))
        dt_l = dt_l + jnp.where(mine, dt_e, 0.0)
        ac_l = ac_l + jnp.where(mine, col, 0.0)
        d_l = d_l + jnp.where(mine, d_e, 0.0)
    ac_last = jnp.sum(jnp.where(last_row, ac_l, 0.0), axis=0, keepdims=True)
    y = y + jnp.exp(ac_l) * _dot(cm.astype(BF16), state_t.astype(BF16)) + x * d_l
    inject = (x * dt_l * jnp.exp(ac_last - ac_l)).astype(BF16)
    new_state = state_t * jnp.exp(ac_last) + _dot(bm.astype(BF16), inject, _TN)
    return y, new_state


def ssd_scan_fwd(xc, proj, dt_col, dtb, alog, dsk, *, blk, name):
    s = xc.shape[0]
    nb = s // blk
    e_w = 4 * SSD_P
    b_off = SSD_G * e_w // SSD_N
    c_off = b_off + SSD_G
    row = pl.BlockSpec((1, LANE), lambda j, g: (0, 0))

    def body(x_ref, b_ref, c_ref, dt_ref, dtb_ref, al_ref, d_ref, y_ref, st_ref, state):
        j, g = pl.program_id(0), pl.program_id(1)

        @pl.when(j == 0)
        def _():
            state[g] = jnp.zeros((SSD_N, e_w), F32)

        st = state[g]
        st_ref[...] = st
        y, new_state = _ssd_block(x_ref[...], b_ref[...], c_ref[...], dt_ref[...], dtb_ref[...], al_ref[...],
                                  d_ref[...], st, g)
        y_ref[...] = y
        state[g] = new_state

    return pl.pallas_call(
        body, grid=(nb, SSD_G),
        in_specs=[pl.BlockSpec((blk, e_w), lambda j, g: (j, g)),
                  pl.BlockSpec((blk, SSD_N), lambda j, g: (j, b_off + g)),
                  pl.BlockSpec((blk, SSD_N), lambda j, g: (j, c_off + g)),
                  pl.BlockSpec((blk, LANE), lambda j, g: (j, dt_col)), row, row, row],
        out_specs=[pl.BlockSpec((blk, e_w), lambda j, g: (j, g)),
                   pl.BlockSpec((None, None, SSD_N, e_w), lambda j, g: (j, g, 0, 0))],
        out_shape=[jax.ShapeDtypeStruct((s, SSD_G * e_w), F32),
                   jax.ShapeDtypeStruct((nb, SSD_G, SSD_N, e_w), F32)],
        scratch_shapes=[pltpu.VMEM((SSD_G, SSD_N, e_w), F32)],
        name=name, compiler_params=_cparams(("arbitrary", "arbitrary")),
    )(xc, xc, xc, proj, dtb, alog, dsk)


def ssd_scan_bwd(xc, proj, dt_col, dtb, alog, dsk, states, dy, *, blk, name):
    s = xc.shape[0]
    nb = s // blk
    e_w = 4 * SSD_P
    b_off = SSD_G * e_w // SSD_N
    c_off = b_off + SSD_G
    row = pl.BlockSpec((1, LANE), lambda j, g: (0, 0))
    jm = lambda j: nb - 1 - j

    def body(x_ref, b_ref, c_ref, dt_ref, dtb_ref, al_ref, d_ref, st_ref, dy_ref,
             dx_ref, db_ref, dc_ref, ddt_ref, ddtb_ref, dal_ref, dd_ref, dstate):
        j, g = pl.program_id(0), pl.program_id(1)

        @pl.when(j == 0)
        def _():
            dstate[g] = jnp.zeros((SSD_N, e_w), F32)

        @pl.when(jnp.logical_and(j == 0, g == 0))
        def _():
            ddtb_ref[...] = jnp.zeros_like(ddtb_ref)
            dal_ref[...] = jnp.zeros_like(dal_ref)
            dd_ref[...] = jnp.zeros_like(dd_ref)

        @pl.when(g == 0)
        def _():
            ddt_ref[...] = jnp.zeros_like(ddt_ref)

        fn = functools.partial(_ssd_block, group=g)
        _, f_vjp = jax.vjp(fn, x_ref[...], b_ref[...], c_ref[...], dt_ref[...], dtb_ref[...], al_ref[...],
                           d_ref[...], st_ref[...])
        dx, db, dc, ddt, ddtb, dal, dd, dst = f_vjp((dy_ref[...], dstate[g]))
        dx_ref[...] = dx
        db_ref[...] = db
        dc_ref[...] = dc
        ddt_ref[...] += ddt
        ddtb_ref[...] += ddtb
        dal_ref[...] += dal
        dd_ref[...] += dd
        dstate[g] = dst

    return pl.pallas_call(
        body, grid=(nb, SSD_G),
        in_specs=[pl.BlockSpec((blk, e_w), lambda j, g: (jm(j), g)),
                  pl.BlockSpec((blk, SSD_N), lambda j, g: (jm(j), b_off + g)),
                  pl.BlockSpec((blk, SSD_N), lambda j, g: (jm(j), c_off + g)),
                  pl.BlockSpec((blk, LANE), lambda j, g: (jm(j), dt_col)), row, row, row,
                  pl.BlockSpec((None, None, SSD_N, e_w), lambda j, g: (jm(j), g, 0, 0)),
                  pl.BlockSpec((blk, e_w), lambda j, g: (jm(j), g))],
        out_specs=[pl.BlockSpec((blk, e_w), lambda j, g: (jm(j), g)),
                   pl.BlockSpec((blk, SSD_N), lambda j, g: (jm(j), g)),
                   pl.BlockSpec((blk, SSD_N), lambda j, g: (jm(j), g)),
                   pl.BlockSpec((blk, LANE), lambda j, g: (jm(j), 0)), row, row, row],
        out_shape=[jax.ShapeDtypeStruct((s, SSD_G * e_w), F32),
                   jax.ShapeDtypeStruct((s, SSD_G * SSD_N), F32),
                   jax.ShapeDtypeStruct((s, SSD_G * SSD_N), F32),
                   jax.ShapeDtypeStruct((s, LANE), F32)] + [jax.ShapeDtypeStruct((1, LANE), F32)] * 3,
        scratch_shapes=[pltpu.VMEM((SSD_G, SSD_N, e_w), F32)],
        name=name, compiler_params=_cparams(("arbitrary", "arbitrary")),
    )(xc, xc, xc, proj, dtb, alog, dsk, states, dy)


SB_DEAD = -104.0

_NT = (((1,), (1,)), ((), ()))
_TN = (((0,), (0,)), ((), ()))


def _dot(a, b, dims=(((1,), (0,)), ((), ()))):
    return lax.dot_general(a, b, dims, preferred_element_type=F32)


def _split_dot(x, tri, passes):
    out = None
    rem = x
    for _ in range(passes):
        piece = rem.astype(BF16)
        rem = rem - piece.astype(F32)
        d = _dot(piece, tri)
        out = d if out is None else out + d
    return out


def _sb_scores(q, k_ref, j, blk, row, scale):
    kb = k_ref[pl.ds(pl.multiple_of(j * blk, blk), blk), :].astype(BF16)
    z = _dot(q, kb, _NT) * scale
    col = j * blk + lax.broadcasted_iota(jnp.int32, (blk, blk), 1)
    mask = col < row
    sp = jnp.maximum(z, 0.0) + jnp.log(1.0 + jnp.exp(-jnp.abs(z)))
    lk = jnp.where(mask, -sp, 0.0)
    return kb, mask, lk, z - sp


def sb_attention_fwd(proj, q_col, k_col, v_col, n_heads, *, blk, name):
    s = proj.shape[0]
    nq = s // blk
    scale = HEAD ** -0.5

    def body(q_ref, k_ref, v_ref, o_ref, r_ref, n_ref):
        i = pl.program_id(1)
        q = q_ref[...].astype(BF16)
        row = i * blk + lax.broadcasted_iota(jnp.int32, (blk, blk), 0)
        rr = lax.broadcasted_iota(jnp.int32, (blk, blk), 0)
        cc = lax.broadcasted_iota(jnp.int32, (blk, blk), 1)
        tri_after = (rr > cc).astype(BF16)

        def alive(carry):
            jj, _, run = carry
            return jnp.logical_and(jj <= i, jnp.max(run) > SB_DEAD)

        def step(carry):
            jj, acc, run = carry
            j = i - jj
            _, mask, lk, ls = _sb_scores(q, k_ref, j, blk, row, scale)
            later = _split_dot(lk, tri_after, 2) + run
            w = jnp.where(mask, jnp.exp(ls + later), 0.0)
            vb = v_ref[pl.ds(pl.multiple_of(j * blk, blk), blk), :].astype(BF16)
            return jj + 1, acc + _dot(w.astype(BF16), vb), run + jnp.sum(lk, axis=1, keepdims=True)

        n, acc, run = lax.while_loop(
            alive, step, (jnp.int32(0), jnp.zeros((blk, HEAD), F32), jnp.zeros((blk, 1), F32)))
        o_ref[...] = acc.astype(o_ref.dtype)
        r_ref[...] = jnp.broadcast_to(run, (blk, HEAD))
        n_ref[pl.program_id(0), i] = n

    blk_spec = lambda c0: pl.BlockSpec((blk, HEAD), lambda h, i: (i, c0 + h))
    full_spec = lambda c0: pl.BlockSpec((s, HEAD), lambda h, i: (0, c0 + h))
    out_spec = pl.BlockSpec((blk, HEAD), lambda h, i: (i, h))
    return pl.pallas_call(
        body, grid=(n_heads, nq),
        in_specs=[blk_spec(q_col), full_spec(k_col), full_spec(v_col)],
        out_specs=[out_spec, out_spec, pl.BlockSpec(memory_space=pltpu.SMEM)],
        out_shape=[jax.ShapeDtypeStruct((s, n_heads * HEAD), BF16),
                   jax.ShapeDtypeStruct((s, n_heads * HEAD), F32),
                   jax.ShapeDtypeStruct((n_heads, nq), jnp.int32)],
        name=name, compiler_params=_cparams(("arbitrary", "arbitrary")),
    )(proj, proj, proj)


def sb_attention_bwd(proj, d_out, run_tot, visited, q_col, k_col, v_col, n_heads, *, blk, name):
    s = proj.shape[0]
    nq = s // blk
    scale = HEAD ** -0.5

    def body(n_ref, q_ref, k_ref, v_ref, do_ref, r_ref, dq_ref, dk_ref, dv_ref, dk_acc, dv_acc):
        i = pl.program_id(1)
        first = i + 1 - jnp.clip(n_ref[pl.program_id(0), i], 1, i + 1)

        @pl.when(i == 0)
        def _():
            dk_acc[...] = jnp.zeros_like(dk_acc)
            dv_acc[...] = jnp.zeros_like(dv_acc)

        q = q_ref[...].astype(BF16)
        do = do_ref[...].astype(BF16)
        rtot = r_ref[:, :1]
        row = i * blk + lax.broadcasted_iota(jnp.int32, (blk, blk), 0)
        rr = lax.broadcasted_iota(jnp.int32, (blk, blk), 0)
        cc = lax.broadcasted_iota(jnp.int32, (blk, blk), 1)
        tri_upto = (rr <= cc).astype(BF16)
        tri_before = (rr < cc).astype(BF16)

        def step(j, carry):
            dq, pre, gpre = carry
            kb, mask, lk, ls = _sb_scores(q, k_ref, j, blk, row, scale)
            rows = pl.ds(pl.multiple_of(j * blk, blk), blk)
            vb = v_ref[rows, :].astype(BF16)
            later = rtot - (pre + _split_dot(lk, tri_upto, 3))
            w = jnp.where(mask, jnp.exp(ls + later), 0.0)
            g = w * _dot(do, vb, _NT)
            g_before = _split_dot(g, tri_before, 2) + gpre
            sig = jnp.exp(ls)
            dz = (jnp.where(mask, g * (1.0 - sig) - sig * g_before, 0.0) * scale).astype(BF16)
            dk_acc[rows, :] += _dot(dz, q, _TN)
            dv_acc[rows, :] += _dot(w.astype(BF16), do, _TN)
            return (dq + _dot(dz, kb), pre + jnp.sum(lk, axis=1, keepdims=True),
                    gpre + jnp.sum(g, axis=1, keepdims=True))

        zero = jnp.zeros((blk, 1), F32)
        dq, _, _ = lax.fori_loop(first, i + 1, step, (jnp.zeros((blk, HEAD), F32), zero, zero))
        dq_ref[...] = dq.astype(dq_ref.dtype)

        @pl.when(i == nq - 1)
        def _():
            dk_ref[...] = dk_acc[...].astype(dk_ref.dtype)
            dv_ref[...] = dv_acc[...].astype(dv_ref.dtype)

    blk_spec = lambda c0: pl.BlockSpec((blk, HEAD), lambda h, i: (i, c0 + h))
    full_spec = lambda c0: pl.BlockSpec((s, HEAD), lambda h, i: (0, c0 + h))
    o_shape = jax.ShapeDtypeStruct((s, n_heads * HEAD), BF16)
    do_spec = blk_spec(0)
    if isinstance(d_out, tuple):
        d_out, lead = d_out
        do_spec = pl.BlockSpec((None, blk, HEAD), lambda h, i: (lead, i, h))
    return pl.pallas_call(
        body, grid=(n_heads, nq),
        in_specs=[pl.BlockSpec(memory_space=pltpu.SMEM), blk_spec(q_col), full_spec(k_col), full_spec(v_col),
                  do_spec, blk_spec(0)],
        out_specs=[blk_spec(0), full_spec(0), full_spec(0)],
        out_shape=[o_shape, o_shape, o_shape],
        scratch_shapes=[pltpu.VMEM((s, HEAD), F32), pltpu.VMEM((s, HEAD), F32)],
        name=name, compiler_params=_cparams(("parallel", "arbitrary")),
    )(visited, proj, proj, proj, d_out, run_tot)


def swiglu_fwd(gu, *, tr, name):
    nb, _, s, hb = gu.shape

    def body(gu_ref, a_ref):
        gate = gu_ref[0]
        a_ref[...] = (gate * jax.nn.sigmoid(gate) * gu_ref[1]).astype(a_ref.dtype)

    return pl.pallas_call(
        body, grid=(nb, s // tr),
        in_specs=[pl.BlockSpec((None, 2, tr, hb), lambda i, r: (i, 0, r, 0))],
        out_specs=pl.BlockSpec((None, tr, hb), lambda i, r: (i, r, 0)),
        out_shape=jax.ShapeDtypeStruct((nb, s, hb), BF16),
        name=name, compiler_params=_cparams(("parallel", "parallel")),
    )(gu)


def swiglu_bwd(gu, da, *, tr, name):
    nb, _, s, hb = gu.shape

    def body(gu_ref, da_ref, dgu_ref):
        gate, up, d = gu_ref[0], gu_ref[1], da_ref[...]
        sig = jax.nn.sigmoid(gate)
        dgu_ref[0] = (d * up * (sig * (1.0 + gate * (1.0 - sig)))).astype(dgu_ref.dtype)
        dgu_ref[1] = (d * gate * sig).astype(dgu_ref.dtype)

    return pl.pallas_call(
        body, grid=(nb, s // tr),
        in_specs=[pl.BlockSpec((None, 2, tr, hb), lambda i, r: (i, 0, r, 0)),
                  pl.BlockSpec((None, tr, hb), lambda i, r: (i, r, 0))],
        out_specs=pl.BlockSpec((None, 2, tr, hb), lambda i, r: (i, 0, r, 0)),
        out_shape=jax.ShapeDtypeStruct(gu.shape, BF16),
        name=name, compiler_params=_cparams(("parallel", "parallel")),
    )(gu, da)


def loss_head(y, target, *, tr, name):
    s, d = y.shape

    def body(y_ref, t_ref, l_ref, dy_ref):
        err = y_ref[...] - t_ref[...]
        dy_ref[...] = err * (1.0 / d)
        part = 0.5 * jnp.sum(jnp.mean(err * err, axis=-1, keepdims=True), axis=0, keepdims=True)

        @pl.when(pl.program_id(0) == 0)
        def _():
            l_ref[...] = jnp.zeros_like(l_ref)

        l_ref[...] += jnp.broadcast_to(part, l_ref.shape)

    row = pl.BlockSpec((tr, d), lambda i: (i, 0))
    return pl.pallas_call(
        body, grid=(s // tr,), in_specs=[row, row],
        out_specs=[pl.BlockSpec((8, LANE), lambda i: (0, 0)), row],
        out_shape=[jax.ShapeDtypeStruct((8, LANE), F32), jax.ShapeDtypeStruct((s, d), F32)],
        name=name, compiler_params=_cparams(("arbitrary",)),
    )(y, target)


def _adamw_math(w, g, m, v):
    m = ADAM_B1 * m + (1.0 - ADAM_B1) * g
    v = ADAM_B2 * v + (1.0 - ADAM_B2) * jnp.square(g)
    m_hat = m / (1.0 - ADAM_B1 ** ADAM_STEP)
    v_hat = v / (1.0 - ADAM_B2 ** ADAM_STEP)
    delta = -ADAM_LR * (m_hat / (jnp.sqrt(v_hat) + ADAM_EPS) + ADAM_WD * w)
    return delta, m, v


def adamw_sum(w, m, v, parts, *, tr, name):
    n, r, c = parts.shape
    tr = _pick_rows(r, tr)

    def body(w_ref, m_ref, v_ref, p_ref, g_ref, d_ref, nm_ref, nv_ref):
        g = p_ref[0].astype(F32)
        for i in range(1, n):
            g = g + p_ref[i].astype(F32)
        delta, nm, nv = _adamw_math(w_ref[...], g, m_ref[...], v_ref[...])
        g_ref[...] = g
        d_ref[...] = delta
        nm_ref[...] = nm
        nv_ref[...] = nv

    row = pl.BlockSpec((tr, c), lambda i: (i, 0))
    shape = jax.ShapeDtypeStruct((r, c), F32)
    return pl.pallas_call(
        body, grid=(r // tr,),
        in_specs=[row, row, row, pl.BlockSpec((n, tr, c), lambda i: (0, i, 0))],
        out_specs=[row] * 4, out_shape=[shape] * 4,
        name=name, compiler_params=_cparams(("parallel",)),
    )(w, m, v, parts)


def _pick_rows(r, pref):
    t = min(pref, r)
    while r % t or (t % 16 and t != r):
        t -= 1
    return t


_HBM = pl.BlockSpec(memory_space=pltpu.HBM)
_MESH = pl.DeviceIdType.MESH


def _flat_index(px, py, pc):
    return 4 * px + 2 * py + pc


def _gather_ops(x_ref, out_ref, send_sems, recv_sems, local_sem):
    x, y, c = lax.axis_index("x"), lax.axis_index("y"), lax.axis_index("c")
    me, sibling = (x, y, c), (x, y, 1 - c)
    chips = [(1 - x, y), (x, 1 - y), (1 - x, 1 - y)]

    def slot(p):
        return out_ref.at[_flat_index(*p)]

    def copy(k, block, to, src=None):
        return pltpu.make_async_remote_copy(
            src_ref=slot(block) if src is None else src, dst_ref=slot(block),
            send_sem=send_sems.at[k], recv_sem=recv_sems.at[k], device_id=to, device_id_type=_MESH)

    mine = pltpu.make_async_copy(x_ref, slot(me), local_sem)
    first = [copy(0, me, sibling, src=x_ref)]
    first += [copy(1 + j, me, (*chip, c), src=x_ref) for j, chip in enumerate(chips)]
    passed = [copy(4 + j, (*chip, c), sibling) for j, chip in enumerate(chips)]

    def start():
        mine.start()
        for cp in first:
            cp.start()

    def finish():
        for j, chip in enumerate(chips):
            copy(1 + j, (*chip, c), me).wait_recv()
            passed[j].start()
        copy(0, sibling, me).wait_recv()
        for j, chip in enumerate(chips):
            copy(4 + j, (*chip, 1 - c), me).wait_recv()
        for cp in first + passed:
            cp.wait_send()
        mine.wait()

    return start, finish


def _exchange_ops(p_ref, out_ref, send_sems, recv_sems, local_sem):
    x, y, c = lax.axis_index("x"), lax.axis_index("y"), lax.axis_index("c")
    me = _flat_index(x, y, c)
    peers = [(1 - x if k & 4 else x, 1 - y if k & 2 else y, 1 - c if k & 1 else c) for k in range(1, N_DEV)]

    def copy(k, peer, dst_slot):
        return pltpu.make_async_remote_copy(
            src_ref=p_ref.at[_flat_index(*peer)], dst_ref=out_ref.at[dst_slot],
            send_sem=send_sems.at[k], recv_sem=recv_sems.at[k], device_id=peer, device_id_type=_MESH)

    mine = pltpu.make_async_copy(p_ref.at[me], out_ref.at[me], local_sem)
    sends = [copy(k, peer, me) for k, peer in enumerate(peers)]

    def start():
        mine.start()
        for cp in sends:
            cp.start()

    def finish():
        for k, peer in enumerate(peers):
            copy(k, peer, _flat_index(*peer)).wait_recv()
        for cp in sends:
            cp.wait_send()
        mine.wait()

    return start, finish


_COMM = {"gather": _gather_ops, "exchange": _exchange_ops}
_COMM_SEMS = [pltpu.SemaphoreType.DMA((N_DEV - 1,)), pltpu.SemaphoreType.DMA((N_DEV - 1,)), pltpu.SemaphoreType.DMA]


def _comm_out_shape(kind, arr):
    return jax.ShapeDtypeStruct(((N_DEV,) + arr.shape) if kind == "gather" else arr.shape, arr.dtype)


def _comm_call(kind, arr, name):
    def body(src_ref, dst_ref, send_sems, recv_sems, local_sem):
        start, finish = _COMM[kind](src_ref, dst_ref, send_sems, recv_sems, local_sem)
        start()
        finish()

    return pl.pallas_call(body, out_shape=_comm_out_shape(kind, arr), in_specs=[_HBM], out_specs=_HBM,
                          scratch_shapes=_COMM_SEMS, name=name)(arr)


def all_gather(x, *, name):
    return _comm_call("gather", x, name)


def exchange(parts, *, name):
    return _comm_call("exchange", parts, name)


TR = 256
TR_WIDE = 128
BLK = 256
SB_BLK = 128
CONV_CHUNK = 512
DT_PAD = LANE

BIG = ("w_in", "w_branch", "w_out", "ffn_w_gu", "ffn_w_down")
SMALL = ("norm_mix_pre", "norm_mix_post", "norm_ffn_pre", "norm_ffn_post", "b_gate", "ret_gn_w", "ssd_conv_w",
         "ssd_conv_b", "ssd_dt_bias", "ssd_a_log", "ssd_d", "ssd_norm_w")


def _row(v):
    return v.reshape(1, -1)


def _pad_lanes(v):
    return jnp.pad(v.reshape(1, -1), ((0, 0), (0, LANE - v.shape[-1])))


def _assemble_w_in(g, d):
    full = jnp.transpose(g, (1, 0, 2)).reshape(d, -1)
    n_main, n_dt = 5 * d, full.shape[1] - 8 * d
    main, dt, gates = full[:, :n_main], full[:, n_main:n_main + n_dt], full[:, n_main + n_dt:]
    return jnp.concatenate([main, gates], axis=1), jnp.pad(dt, ((0, 0), (0, DT_PAD - n_dt)))


def _split_dw_in(dw, dw_dt, d, n_dt):
    n_main = 5 * d
    full = jnp.concatenate([dw[:, :n_main], dw_dt[:, :n_dt], dw[:, n_main:]], axis=1)
    return jnp.transpose(full.reshape(d, N_DEV, -1), (1, 0, 2))


def _layer_fwd(x, lw, cosf, sinf, next_shards):
    s, d = x.shape
    heads = d // 2 // HEAD
    nxt = {}
    (h,) = rowwise(f_rms_pre, [x], [lw["norm_mix_pre"]], [(d, BF16)], tr=TR, name="mix_pre_norm")
    if next_shards is None:
        proj = matmul(h, lw["w_cat"], tn=2048, name="in_proj")
    else:
        proj, (nxt["w_in"],) = matmul(h, lw["w_cat"], tn=2048, hosted=[("gather", next_shards["w_in"])],
                                      name="in_proj_gather")
    dt_raw = matmul(h, lw["w_dt"], name="in_proj_dt")
    ret_cols = tuple(i * heads for i in range(4))
    y_ret, ret_states = retention_fwd(proj, ret_cols, lw["ret_gn_w"], cosf, sinf, heads, blk=BLK, name="retention_fwd")
    y_sb, sb_run, sb_visited = sb_attention_fwd(proj, 4 * heads, 5 * heads, 6 * heads, heads, blk=SB_BLK,
                                                name="stickbreak_fwd")
    u_pad = jnp.pad(proj[:, 4 * d:5 * d], ((CONV_PAD, CONV_PAD), (0, 0)))
    xc = ssd_conv_fwd(u_pad, lw["conv_taps"], lw["ssd_conv_b"], chunk=CONV_CHUNK, name="ssd_conv_fwd")
    y_scan, ssd_states = ssd_scan_fwd(xc, dt_raw, 0, lw["ssd_dt_bias"], lw["ssd_a_log"], lw["ssd_d"], blk=BLK,
                                      name="ssd_scan_fwd")
    z_spec = (proj, d // 2, 7)
    (y_ssd,) = rowwise(f_ssd_gate, [y_scan, z_spec], [lw["ssd_norm_w"]], [(d // 2, BF16)], tr=TR, name="ssd_gate_norm")
    y3 = jnp.stack([y_ret, y_sb, y_ssd])
    u3 = matmul(y3, lw["w_branch"], lead_a="batch", lead_b="batch", name="branch_proj")
    merge_rows = [(u3, d, 0, i) for i in range(3)] + [(proj, d, 5 + i) for i in range(3)]
    (merged,) = rowwise(f_merge, merge_rows, lw["b_gate"], [(d, BF16)], tr=TR_WIDE, name="gate_merge")
    o = matmul(merged, lw["w_out"], name="out_proj")
    (x1,) = rowwise(f_rms_post, [x, o], [lw["norm_mix_post"]], [(d, F32)], tr=TR, name="mix_post_norm")
    (h2,) = rowwise(f_rms_pre, [x1], [lw["norm_ffn_pre"]], [(d, BF16)], tr=TR, name="ffn_pre_norm")
    if next_shards is None:
        gu = matmul(h2, lw["ffn_w_gu"], lead_b="batch", name="ffn_up")
    else:
        rest = [n for n in BIG if n != "w_in"]
        gu, got = matmul(h2, lw["ffn_w_gu"], lead_b="batch", hosted=[("gather", next_shards[n]) for n in rest],
                         name="ffn_up_gather")
        nxt.update(zip(rest, got))
    gu = gu.reshape(N_DEV, 2, s, -1)
    act = swiglu_fwd(gu, tr=TR, name="swiglu_fwd")
    f = matmul(act, lw["ffn_w_down"], lead_a="k", lead_b="k", name="ffn_down")
    (x2,) = rowwise(f_rms_post, [x1, f], [lw["norm_ffn_post"]], [(d, F32)], tr=TR, name="ffn_post_norm")
    res = dict(x=x, h=h, proj=proj, dt_raw=dt_raw, ret_states=ret_states, sb_run=sb_run, sb_visited=sb_visited, xc=xc,
               u_pad=u_pad,
               y_scan=y_scan, ssd_states=ssd_states, y3=y3, u3=u3, merged=merged, o=o, x1=x1, h2=h2, gu=gu, act=act, f=f)
    return x2, res, (nxt if next_shards is not None else None)


def _layer_bwd(dx2, res, lw, cosf, sinf, pending):
    x, proj = res["x"], res["proj"]
    s, d = x.shape
    heads = d // 2 // HEAD
    n_dt = d // 2 // SSD_P
    got = {}
    df, dn_ffn_post = rowwise_vjp(f_rms_post, [res["x1"], res["f"]], [lw["norm_ffn_post"]], [dx2],
                                  [(1, BF16, None)], [0], tr=TR, name="ffn_post_norm_bwd")
    d_act = matmul(df, lw["ffn_w_down"], tb=True, lead_b="batch", name="ffn_down_dx")
    dw_down = matmul(res["act"], df, ta=True, lead_a="batch", out_dtype=BF16, name="ffn_down_dw")
    dgu = swiglu_bwd(res["gu"], d_act, tr=TR, name="swiglu_bwd").reshape(2 * N_DEV, s, -1)
    if pending is None:
        dh2 = matmul(dgu, lw["ffn_w_gu"], tb=True, lead_a="k", lead_b="k", name="ffn_up_dx")
        dw_gu = matmul(res["h2"], dgu, ta=True, lead_b="batch", out_dtype=BF16, name="ffn_up_dw")
    else:
        dh2, (got["prev_w_branch"], got["prev_w_out"]) = matmul(
            dgu, lw["ffn_w_gu"], tb=True, lead_a="k", lead_b="k",
            hosted=[("exchange", pending["w_branch"]), ("exchange", pending["w_out"])], name="ffn_up_dx_exchange")
        dw_gu, (got["prev_w_in"],) = matmul(res["h2"], dgu, ta=True, lead_b="batch", out_dtype=BF16,
                                            hosted=[("exchange", pending["w_in"])], name="ffn_up_dw_exchange")
    dw_gu = dw_gu.reshape(N_DEV, 2 * d, -1)
    dx1, dn_ffn_pre = rowwise_vjp(f_rms_pre, [res["x1"]], [lw["norm_ffn_pre"]], [dh2], [(0, F32, dx2)], [0],
                                  tr=TR, name="ffn_pre_norm_bwd")
    do, dn_mix_post = rowwise_vjp(f_rms_post, [x, res["o"]], [lw["norm_mix_post"]], [dx1], [(1, BF16, None)], [0],
                                  tr=TR, name="mix_post_norm_bwd")
    dmerged = matmul(do, lw["w_out"], tb=True, name="out_proj_dx")
    dw_out = matmul(res["merged"], do, ta=True, out_dtype=BF16, name="out_proj_dw")
    merge_rows = [(res["u3"], d, 0, i) for i in range(3)] + [(proj, d, 5 + i) for i in range(3)]
    mg = rowwise_vjp(f_merge, merge_rows, lw["b_gate"], [dmerged], [(i, BF16, None) for i in range(6)], [0, 1, 2],
                     tr=TR_WIDE, name="gate_merge_bwd")
    du3 = jnp.stack(mg[:3])
    d_gate_logits, db_gate = mg[3:6], mg[6:9]
    dy3 = matmul(du3, lw["w_branch"], tb=True, lead_a="batch", lead_b="batch", name="branch_proj_dx")
    dw_branch = matmul(res["y3"], du3, ta=True, lead_a="batch", lead_b="batch", out_dtype=BF16, name="branch_proj_dw")
    ret_cols = tuple(i * heads for i in range(4))
    dq, dk, dv, dg, d_gn = retention_bwd(proj, ret_cols, lw["ret_gn_w"], cosf, sinf, res["ret_states"], (dy3, 0), heads,
                                         blk=BLK, name="retention_bwd")
    dsq, dsk, dsv = sb_attention_bwd(proj, (dy3, 1), res["sb_run"], res["sb_visited"], 4 * heads, 5 * heads, 6 * heads,
                                     heads, blk=SB_BLK, name="stickbreak_bwd")
    z_spec = (proj, d // 2, 7)
    dy_scan, dz, d_ssd_norm = rowwise_vjp(f_ssd_gate, [res["y_scan"], z_spec], [lw["ssd_norm_w"]], [(dy3, d // 2, 0, 2)],
                                          [(0, F32, None), (1, BF16, None)], [0], tr=TR, name="ssd_gate_norm_bwd")
    dxs, dbm, dcm, ddt, d_dtb, d_alog, d_dskip = ssd_scan_bwd(
        res["xc"], res["dt_raw"], 0, lw["ssd_dt_bias"], lw["ssd_a_log"], lw["ssd_d"], res["ssd_states"], dy_scan,
        blk=BLK, name="ssd_scan_bwd")
    dxc_pad = jnp.pad(jnp.concatenate([dxs, dbm, dcm], axis=1), ((0, CONV_PAD), (0, 0)))
    du, d_taps, d_conv_b = ssd_conv_bwd(res["u_pad"], lw["conv_taps"], lw["ssd_conv_b"], dxc_pad, chunk=CONV_CHUNK,
                                        name="ssd_conv_bwd")
    dproj = jnp.concatenate([dq, dk, dv, dg, dsq, dsk, dsv, dz, du, *d_gate_logits], axis=1)
    dh_dt = matmul(ddt, lw["w_dt"], tb=True, name="in_proj_dt_dx")
    dh, (got["ffn_w_down"], got["ffn_w_gu"]) = matmul(
        dproj, lw["w_cat"], tb=True, add=dh_dt, hosted=[("exchange", dw_down), ("exchange", dw_gu)],
        name="in_proj_dx_exchange")
    dw_cat = matmul(res["h"], dproj, ta=True, tn=2048, out_dtype=BF16, name="in_proj_dw")
    dw_dt = matmul(res["h"], ddt, ta=True, out_dtype=BF16, name="in_proj_dt_dw")
    dx, dn_mix_pre = rowwise_vjp(f_rms_pre, [x], [lw["norm_mix_pre"]], [dh], [(0, F32, dx1)], [0], tr=TR,
                                 name="mix_pre_norm_bwd")
    mine = dict(
        w_in=_split_dw_in(dw_cat, dw_dt, d, n_dt),
        w_branch=jnp.transpose(dw_branch.reshape(3, d // 2, N_DEV, -1), (2, 0, 1, 3)).reshape(N_DEV, 3 * d // 2, -1),
        w_out=dw_out.reshape(N_DEV, d // N_DEV, d),
    )
    small = dict(
        norm_mix_pre=dn_mix_pre[0], norm_mix_post=dn_mix_post[0], norm_ffn_pre=dn_ffn_pre[0],
        norm_ffn_post=dn_ffn_post[0], b_gate=jnp.concatenate([b[0] for b in db_gate]), ret_gn_w=d_gn[0],
        ssd_conv_w=d_taps, ssd_conv_b=d_conv_b[0], ssd_dt_bias=d_dtb[0, :n_dt], ssd_a_log=d_alog[0, :n_dt],
        ssd_d=d_dskip[0, :n_dt], ssd_norm_w=d_ssd_norm[0],
    )
    return dx, got, mine, small


def _adam_rows(cols):
    return max(8, (1 << 17) // cols // 8 * 8)


def kernel(x, positions, norm_mix_pre, norm_mix_post, norm_ffn_pre, norm_ffn_post, w_in, b_gate, ret_gn_w, ssd_conv_w, ssd_conv_b, ssd_dt_bias, ssd_a_log, ssd_d, ssd_norm_w, w_branch_ret, w_branch_sb, w_branch_ssd, w_out, ffn_w_gate, ffn_w_up, ffn_w_down, loss_target, m_norm_mix_pre, m_norm_mix_post, m_norm_ffn_pre, m_norm_ffn_post, m_w_in, m_b_gate, m_ret_gn_w, m_ssd_conv_w, m_ssd_conv_b, m_ssd_dt_bias, m_ssd_a_log, m_ssd_d, m_ssd_norm_w, m_w_branch_ret, m_w_branch_sb, m_w_branch_ssd, m_w_out, m_ffn_w_gate, m_ffn_w_up, m_ffn_w_down, v_norm_mix_pre, v_norm_mix_post, v_norm_ffn_pre, v_norm_ffn_post, v_w_in, v_b_gate, v_ret_gn_w, v_ssd_conv_w, v_ssd_conv_b, v_ssd_dt_bias, v_ssd_a_log, v_ssd_d, v_ssd_norm_w, v_w_branch_ret, v_w_branch_sb, v_w_branch_ssd, v_w_out, v_ffn_w_gate, v_ffn_w_up, v_ffn_w_down):
    depth = w_in.shape[0]
    s, d = x.shape[1], x.shape[2]
    axes = ("x", "y", "c")
    me = _flat_index(lax.axis_index("x"), lax.axis_index("y"), lax.axis_index("c"))

    def big_shards(w_in_, br_ret, br_sb, br_ssd, w_out_, gate, up, down):
        return dict(w_in=w_in_, w_branch=jnp.concatenate([br_ret, br_sb, br_ssd], axis=1), w_out=w_out_,
                    ffn_w_gu=jnp.concatenate([gate, up], axis=1), ffn_w_down=down)

    big_w = big_shards(w_in, w_branch_ret, w_branch_sb, w_branch_ssd, w_out, ffn_w_gate, ffn_w_up, ffn_w_down)
    big_m = big_shards(m_w_in, m_w_branch_ret, m_w_branch_sb, m_w_branch_ssd, m_w_out, m_ffn_w_gate, m_ffn_w_up, m_ffn_w_down)
    big_v = big_shards(v_w_in, v_w_branch_ret, v_w_branch_sb, v_w_branch_ssd, v_w_out, v_ffn_w_gate, v_ffn_w_up, v_ffn_w_down)

    taps_all = all_gather(ssd_conv_w.reshape(-1, LANE), name="gather_conv_w")
    taps_all = jnp.transpose(taps_all.reshape(N_DEV, depth, SSD_K, -1), (1, 2, 0, 3)).reshape(depth, SSD_K, -1)

    cosf, sinf = rope_tables(positions.reshape(s), tr=TR)
    small_w = dict(norm_mix_pre=norm_mix_pre, norm_mix_post=norm_mix_post, norm_ffn_pre=norm_ffn_pre,
                   norm_ffn_post=norm_ffn_post, b_gate=b_gate, ret_gn_w=ret_gn_w, ssd_conv_b=ssd_conv_b,
                   ssd_dt_bias=ssd_dt_bias, ssd_a_log=ssd_a_log, ssd_d=ssd_d, ssd_norm_w=ssd_norm_w, taps=taps_all)

    def layer_weights(sw, gathered):
        lw = dict(gathered)
        for n in ("norm_mix_pre", "norm_mix_post", "norm_ffn_pre", "norm_ffn_post", "ret_gn_w", "ssd_conv_b", "ssd_norm_w"):
            lw[n] = _row(sw[n])
        for n in ("ssd_dt_bias", "ssd_a_log", "ssd_d"):
            lw[n] = _pad_lanes(sw[n])
        lw["b_gate"] = [_row(sw["b_gate"][i * d:(i + 1) * d]) for i in range(3)]
        lw["conv_taps"] = [sw["taps"][k:k + 1] for k in range(SSD_K)]
        return lw

    def layer_slice(t, l):
        return {n: a[l] for n, a in t.items()}

    def bf16_shards(l):
        return {n: big_w[n][l].astype(BF16) for n in BIG}

    def arrange(g):
        w_cat, w_dt = _assemble_w_in(g["w_in"], d)
        return dict(
            w_cat=w_cat, w_dt=w_dt,
            w_branch=jnp.transpose(g["w_branch"].reshape(N_DEV, 3, d // 2, -1), (1, 2, 0, 3)).reshape(3, d // 2, d),
            w_out=g["w_out"].reshape(d, d),
            ffn_w_gu=g["ffn_w_gu"].reshape(2 * N_DEV, d, -1),
            ffn_w_down=g["ffn_w_down"],
        )

    def adam(l, n, parts):
        w2 = big_w[n][l]
        return adamw_sum(w2, big_m[n][l], big_v[n][l], parts.reshape(N_DEV, *w2.shape), tr=_adam_rows(w2.shape[1]),
                         name="adamw_" + n)

    xs, saved = x.reshape(s, d), []
    gathered = {n: all_gather(a, name="gather_" + n) for n, a in bf16_shards(0).items()}
    for l in range(depth):
        lw = layer_weights(layer_slice(small_w, l), arrange(gathered))
        xs, res, gathered = _layer_fwd(xs, lw, cosf, sinf, bf16_shards(l + 1) if l + 1 < depth else None)
        saved.append((res, lw))
    loss_tile, dy = loss_head(xs, loss_target.reshape(s, d), tr=TR, name="loss_head")
    loss = lax.psum(loss_tile[0, 0], axes)

    dx, pending, small_layers = dy, None, [None] * depth
    big_layers = [dict() for _ in range(depth)]
    for l in reversed(range(depth)):
        res, lw = saved[l]
        dx, got, pending, small_layers[l] = _layer_bwd(dx, res, lw, cosf, sinf, pending)
        for n in ("ffn_w_gu", "ffn_w_down"):
            big_layers[l][n] = adam(l, n, got[n])
        if l + 1 < depth:
            for n in ("w_in", "w_branch", "w_out"):
                big_layers[l + 1][n] = adam(l + 1, n, got["prev_" + n])
    for n in ("w_in", "w_branch", "w_out"):
        big_layers[0][n] = adam(0, n, exchange(pending[n], name="exchange_" + n))
    big_out = {n: [jnp.stack([big_layers[l][n][i] for l in range(depth)]) for i in range(4)] for n in BIG}
    small_g = {n: jnp.stack([small_layers[l][n] for l in range(depth)]) for n in SMALL}

    n_dt = ssd_dt_bias.shape[-1]
    small_in = dict(norm_mix_pre=(norm_mix_pre, m_norm_mix_pre, v_norm_mix_pre), norm_mix_post=(norm_mix_post, m_norm_mix_post, v_norm_mix_post),
                    norm_ffn_pre=(norm_ffn_pre, m_norm_ffn_pre, v_norm_ffn_pre), norm_ffn_post=(norm_ffn_post, m_norm_ffn_post, v_norm_ffn_post),
                    b_gate=(b_gate, m_b_gate, v_b_gate), ret_gn_w=(ret_gn_w, m_ret_gn_w, v_ret_gn_w),
                    ssd_conv_b=(ssd_conv_b, m_ssd_conv_b, v_ssd_conv_b), ssd_dt_bias=(ssd_dt_bias, m_ssd_dt_bias, v_ssd_dt_bias),
                    ssd_a_log=(ssd_a_log, m_ssd_a_log, v_ssd_a_log), ssd_d=(ssd_d, m_ssd_d, v_ssd_d),
                    ssd_norm_w=(ssd_norm_w, m_ssd_norm_w, v_ssd_norm_w))
    rep = [n for n in SMALL if n != "ssd_conv_w"]

    def pack(arrs):
        flat = jnp.concatenate([a.reshape(-1) for a in arrs])
        rows = -(-flat.shape[0] // (16 * LANE)) * 16
        return jnp.pad(flat, (0, rows * LANE - flat.shape[0])).reshape(rows, LANE)

    conv_g = small_g["ssd_conv_w"]
    g_pack = pack([small_g[n] for n in rep] + [conv_g])
    g_all = all_gather(g_pack, name="gather_small_grads")
    zeros_conv = jnp.zeros_like(conv_g)
    w_pack, m_pack, v_pack = (pack([small_in[n][i] for n in rep] + [zeros_conv]) for i in range(3))
    sm = adamw_sum(w_pack, m_pack, v_pack, g_all, tr=TR, name="adamw_small")

    def unpack(p):
        flat, out, off = p.reshape(-1), {}, 0
        for n in rep:
            shp = small_in[n][0].shape
            size = math.prod(shp)
            out[n] = flat[off:off + size].reshape(shp)
            off += size
        out["conv_sum"] = flat[off:off + conv_g.size].reshape(conv_g.shape)
        return out

    sm = [unpack(p) for p in sm]
    ch = ssd_conv_w.shape[-1]
    conv_mine = lax.dynamic_slice_in_dim(sm[0]["conv_sum"], me * ch, ch, axis=2)
    conv_out = adamw_sum(ssd_conv_w.reshape(-1, LANE), m_ssd_conv_w.reshape(-1, LANE), v_ssd_conv_w.reshape(-1, LANE),
                         conv_mine.reshape(1, -1, LANE), tr=TR, name="adamw_conv_w")
    for i in range(4):
        sm[i]["ssd_conv_w"] = conv_out[i].reshape(ssd_conv_w.shape)

    def big_named(i):
        o = {n: big_out[n][i] for n in BIG}
        br = o["w_branch"].reshape(depth, 3, d // 2, -1)
        gu = o["ffn_w_gu"].reshape(depth, 2, d, -1)
        return dict(w_in=o["w_in"].reshape(w_in.shape), w_branch_ret=br[:, 0], w_branch_sb=br[:, 1], w_branch_ssd=br[:, 2],
                    w_out=o["w_out"].reshape(w_out.shape), ffn_w_gate=gu[:, 0], ffn_w_up=gu[:, 1],
                    ffn_w_down=o["ffn_w_down"].reshape(ffn_w_down.shape))

    order = ["norm_mix_pre", "norm_mix_post", "norm_ffn_pre", "norm_ffn_post", "w_in", "b_gate", "ret_gn_w", "ssd_conv_w",
             "ssd_conv_b", "ssd_dt_bias", "ssd_a_log", "ssd_d", "ssd_norm_w", "w_branch_ret", "w_branch_sb", "w_branch_ssd",
             "w_out", "ffn_w_gate", "ffn_w_up", "ffn_w_down"]
    outs = [loss, dx.reshape(x.shape)]
    for i in range(4):
        named = {**sm[i], **big_named(i)}
        outs += [named[n] for n in order]
    return tuple(outs)
```

```python
import functools
import math

import jax
import jax.numpy as jnp
import numpy as np
from jax import lax
from jax.experimental import pallas as pl
from jax.experimental.pallas import tpu as pltpu

F32 = jnp.float32
BF16 = jnp.bfloat16

N_DEV = 8
HEAD = 128
SSD_P = 64
SSD_G = 4
SSD_N = 128
SSD_K = 4
CHUNK = 64
NORM_EPS = 1e-6
ROPE_BASE = 10000.0
LANE = 128
VMEM_LIMIT = 56 * 1024 * 1024

ADAM_LR, ADAM_B1, ADAM_B2, ADAM_EPS, ADAM_WD, ADAM_STEP = 0.001, 0.9, 0.999, 1e-08, 0.01, 10


def _cparams(sem):
    return pltpu.CompilerParams(dimension_semantics=sem, vmem_limit_bytes=VMEM_LIMIT)


def _pick(n, pref):
    if n <= pref:
        return n
    t = pref
    while t >= LANE:
        if n % t == 0:
            return t
        t -= LANE
    return n


def matmul(a, b, *, ta=False, tb=False, lead_a=None, lead_b=None, out_dtype=F32, add=None, hosted=(),
           tm=1024, tn=1024, tk=2048, name="mm"):
    la, lb = lead_a is not None, lead_b is not None
    a2, b2 = a.shape[1:] if la else a.shape, b.shape[1:] if lb else b.shape
    (kd_a, m) = a2 if ta else a2[::-1]
    (kd_b, n) = b2[::-1] if tb else b2
    assert kd_a == kd_b, (a.shape, b.shape)
    nlead = a.shape[0] if la else (b.shape[0] if lb else 1)
    batch = "batch" in (lead_a, lead_b)
    kblocks = nlead if "k" in (lead_a, lead_b) else 1
    if la and lb:
        assert lead_a == lead_b and a.shape[0] == b.shape[0]
    tm, tn, tk = _pick(m, tm), _pick(n, tn), _pick(kd_a, tk)
    kt = kd_a // tk
    nk = kt * kblocks
    grid = (m // tm, (nlead if batch else 1), n // tn, nk)

    def lead_idx(g, k):
        return g if batch else k // kt

    def a_map(i, g, j, k):
        idx = (k % kt, i) if ta else (i, k % kt)
        return ((lead_idx(g, k),) + idx) if la else idx

    def b_map(i, g, j, k):
        idx = (j, k % kt) if tb else (k % kt, j)
        return ((lead_idx(g, k),) + idx) if lb else idx

    a_blk = (tk, tm) if ta else (tm, tk)
    b_blk = (tn, tk) if tb else (tk, tn)
    a_spec = pl.BlockSpec(((None,) + a_blk) if la else a_blk, a_map)
    b_spec = pl.BlockSpec(((None,) + b_blk) if lb else b_blk, b_map)
    if batch:
        o_spec = pl.BlockSpec((None, tm, tn), lambda i, g, j, k: (g, i, j))
        o_shape = jax.ShapeDtypeStruct((nlead, m, n), out_dtype)
    else:
        o_spec = pl.BlockSpec((tm, tn), lambda i, g, j, k: (i, j))
        o_shape = jax.ShapeDtypeStruct((m, n), out_dtype)
    dims = (((0 if ta else 1,), (1 if tb else 0,)), ((), ()))

    n_add, n_host = int(add is not None), len(hosted)
    n_acc = int(nk > 1)

    def body(a_ref, b_ref, *rest):
        c_ref = rest[0] if n_add else None
        src_refs = rest[n_add:n_add + n_host]
        o_ref = rest[n_add + n_host]
        dst_refs = rest[n_add + n_host + 1:n_add + 2 * n_host + 1]
        acc_ref = rest[n_add + 2 * n_host + 1] if n_acc else None
        sems = rest[n_add + 2 * n_host + 1 + n_acc:]
        k = pl.program_id(3)
        ids = [pl.program_id(ax) for ax in range(4)]
        comms = [_COMM[kind](src_refs[c], dst_refs[c], *sems[3 * c:3 * c + 3]) for c, (kind, _) in enumerate(hosted)]

        if hosted:
            @pl.when(functools.reduce(jnp.logical_and, [i == 0 for i in ids]))
            def _():
                for start, _ in comms:
                    start()

        def product():
            return lax.dot_general(a_ref[...].astype(BF16), b_ref[...].astype(BF16), dims, preferred_element_type=F32)

        if n_acc:
            @pl.when(k == 0)
            def _():
                acc_ref[...] = jnp.zeros_like(acc_ref) if c_ref is None else c_ref[...].astype(F32)

            acc_ref[...] += product()

            @pl.when(k == nk - 1)
            def _():
                o_ref[...] = acc_ref[...].astype(o_ref.dtype)
        else:
            o_ref[...] = (product() if c_ref is None else product() + c_ref[...].astype(F32)).astype(o_ref.dtype)

        if hosted:
            @pl.when(functools.reduce(jnp.logical_and, [i == g - 1 for i, g in zip(ids, grid)]))
            def _():
                for _, finish in comms:
                    finish()

    extra = ([] if add is None else [add]) + [arr for _, arr in hosted]
    extra_specs = ([] if add is None else [o_spec]) + [_HBM] * n_host
    out_shapes = [o_shape] + [_comm_out_shape(kind, arr) for kind, arr in hosted]
    scratch = [pltpu.VMEM((tm, tn), F32)] * n_acc + _COMM_SEMS * n_host
    sem = ("arbitrary",) * 4 if hosted else ("parallel", "parallel", "parallel", "arbitrary")
    res = pl.pallas_call(
        body, out_shape=out_shapes, grid=grid, in_specs=[a_spec, b_spec] + extra_specs,
        out_specs=[o_spec] + [_HBM] * n_host, scratch_shapes=scratch, name=name, compiler_params=_cparams(sem),
    )(a, b, *extra)
    return (res[0], res[1:]) if hosted else res[0]


def _row_spec(r, tr):
    if not isinstance(r, tuple):
        return r, pl.BlockSpec((tr, r.shape[-1]), lambda i: (i, 0))
    if len(r) == 3:
        arr, w, cb = r
        return arr, pl.BlockSpec((tr, w), lambda i: (i, cb))
    arr, w, cb, ld = r
    return arr, pl.BlockSpec((None, tr, w), lambda i: (ld, i, cb))


def _full_spec(c):
    nd = c.ndim
    return pl.BlockSpec(c.shape, lambda i: (0,) * nd)


def rowwise(fn, rows, consts, outs, *, tr, name):
    arrs, specs = zip(*[_row_spec(r, tr) for r in rows])
    n_rows = arrs[0].shape[-2]
    nr, nc = len(rows), len(consts)

    def body(*refs):
        vals = [r[...] for r in refs[:nr + nc]]
        res = fn(*vals)
        for o_ref, r in zip(refs[nr + nc:], res):
            o_ref[...] = r.astype(o_ref.dtype)

    return pl.pallas_call(
        body, grid=(n_rows // tr,),
        in_specs=list(specs) + [_full_spec(c) for c in consts],
        out_specs=[pl.BlockSpec((tr, w), lambda i: (i, 0)) for w, _ in outs],
        out_shape=[jax.ShapeDtypeStruct((n_rows, w), dt) for w, dt in outs],
        name=name, compiler_params=_cparams(("parallel",)),
    )(*arrs, *consts)


def rowwise_vjp(fn, rows, consts, cts, row_grads, const_grads, *, tr, name):
    arrs, specs = zip(*[_row_spec(r, tr) for r in rows])
    n_rows = arrs[0].shape[-2]
    nr, nc = len(rows), len(consts)
    ct_present = [c for c in cts if c is not None]
    ct_arrs, ct_specs = zip(*[_row_spec(c, tr) for c in ct_present])
    add_present = [g[2] for g in row_grads if g[2] is not None]
    add_arrs, add_specs = zip(*[_row_spec(c, tr) for c in add_present]) if add_present else ((), ())
    n_ct, n_add = len(ct_present), len(add_present)
    widths = [s.block_shape[-1] for s in specs]

    def body(*refs):
        ins = refs[:nr + nc]
        ct_refs = refs[nr + nc:nr + nc + n_ct]
        add_refs = refs[nr + nc + n_ct:nr + nc + n_ct + n_add]
        out_refs = refs[nr + nc + n_ct + n_add:]
        vals = [r[...] for r in ins]
        res, f_vjp = jax.vjp(fn, *vals)
        it = iter(ct_refs)
        ct_vals = tuple(next(it)[...].astype(r.dtype) if c is not None else jnp.zeros_like(r)
                        for c, r in zip(cts, res))
        grads = f_vjp(ct_vals)
        ita = iter(add_refs)
        for o_ref, (idx, _, add) in zip(out_refs, row_grads):
            g = grads[idx].astype(F32)
            if add is not None:
                g = g + next(ita)[...].astype(F32)
            o_ref[...] = g.astype(o_ref.dtype)
        first = pl.program_id(0) == 0
        for o_ref, idx in zip(out_refs[len(row_grads):], const_grads):
            g = grads[nr + idx].astype(F32)

            @pl.when(first)
            def _():
                o_ref[...] = g

            @pl.when(jnp.logical_not(first))
            def _():
                o_ref[...] += g

    out_specs = [pl.BlockSpec((tr, widths[idx]), lambda i: (i, 0)) for idx, _, _ in row_grads]
    out_shape = [jax.ShapeDtypeStruct((n_rows, widths[idx]), dt) for idx, dt, _ in row_grads]
    out_specs += [_full_spec(consts[idx]) for idx in const_grads]
    out_shape += [jax.ShapeDtypeStruct(consts[idx].shape, F32) for idx in const_grads]
    return pl.pallas_call(
        body, grid=(n_rows // tr,),
        in_specs=list(specs) + [_full_spec(c) for c in consts] + list(ct_specs) + list(add_specs),
        out_specs=out_specs, out_shape=out_shape,
        name=name, compiler_params=_cparams(("arbitrary",)),
    )(*arrs, *consts, *ct_arrs, *add_arrs)


def f_rms(x, w):
    xf = x.astype(F32)
    return xf * lax.rsqrt(jnp.mean(xf * xf, axis=-1, keepdims=True) + NORM_EPS) * w


def f_rms_pre(x, w):
    return (f_rms(x, w),)


def f_rms_post(x, o, w):
    return (x + f_rms(o, w),)


def f_merge(u0, u1, u2, g0, g1, g2, b0, b1, b2):
    return (jax.nn.sigmoid(g0 + b0) * u0 + jax.nn.sigmoid(g1 + b1) * u1 + jax.nn.sigmoid(g2 + b2) * u2,)


def f_ssd_gate(y, z, w):
    return (f_rms(y * jax.nn.silu(z), w),)


@jax.custom_vjp
def _swap_halves(x):
    return pltpu.roll(x, HEAD // 2, 1)


_swap_halves.defvjp(lambda x: (_swap_halves(x), None), lambda _, g: (_swap_halves(g),))


def rope_tables(positions, *, tr):
    s = positions.shape[0]
    half = HEAD // 2
    inv = ROPE_BASE ** (-2.0 * jnp.arange(half, dtype=F32) / HEAD)
    inv = jnp.concatenate([inv, inv]).reshape(1, HEAD)
    sign = jnp.concatenate([-jnp.ones((half,), F32), jnp.ones((half,), F32)]).reshape(1, HEAD)

    def fn(pos, inv, sign):
        ang = pos.astype(F32) * inv
        return jnp.cos(ang), jnp.sin(ang) * sign

    return rowwise(fn, [positions.reshape(s, 1)], [inv, sign], [(HEAD, F32), (HEAD, F32)], tr=tr, name="rope_tables")


def _ret_consts(n_heads, blk):
    lg = np.log1p(-np.exp2(-5.0 - np.arange(n_heads)))[:, None, None]
    i = np.arange(blk)
    dist = np.abs(i[:, None] - i[None, :])[None]
    allowed = ((i[None, :] // CHUNK) <= (i[:, None] // CHUNK))[None]
    dm = np.where(allowed, np.exp(lg * dist), 0.0)
    qd = np.broadcast_to(np.exp(lg * (i[None, :, None] + 1.0)), (n_heads, blk, HEAD))
    kd = np.broadcast_to(np.exp(lg * (blk - 1.0 - i[None, :, None])), (n_heads, blk, HEAD))
    cd = np.broadcast_to(np.exp(lg * blk), (n_heads, 1, HEAD))
    return [jnp.asarray(a, F32) for a in (dm, qd, kd, cd)]


def _ret_block(q, k, v, g, gnw, state, cosf, sinf, dm, qd, kd, cd):
    qr = q * cosf + _swap_halves(q) * sinf
    kr = (k * cosf + _swap_halves(k) * sinf) * (HEAD ** -0.5)
    vb = v.astype(BF16)
    scores = _dot(qr.astype(BF16), kr.astype(BF16), _NT) * dm
    o = _dot(scores.astype(BF16), vb) + _dot((qr * qd).astype(BF16), state.astype(BF16))
    new_state = state * cd + _dot((kr * kd).astype(BF16), vb, _TN)
    mu = jnp.mean(o, axis=-1, keepdims=True)
    var = jnp.mean(jnp.square(o - mu), axis=-1, keepdims=True)
    y = (o - mu) * lax.rsqrt(var + NORM_EPS) * gnw * jax.nn.silu(g)
    return y, new_state


def _ret_specs(n_heads, blk, nb, cols, reverse):
    jm = (lambda j: nb - 1 - j) if reverse else (lambda j: j)
    col = lambda c0: pl.BlockSpec((blk, HEAD), lambda h, j: (jm(j), c0 + h))
    tab = pl.BlockSpec((blk, HEAD), lambda h, j: (jm(j), 0))
    specs = [col(c) for c in cols]
    specs += [pl.BlockSpec((1, HEAD), lambda h, j: (0, h)), tab, tab]
    specs += [pl.BlockSpec((None, blk, blk), lambda h, j: (h, 0, 0)),
              pl.BlockSpec((None, blk, HEAD), lambda h, j: (h, 0, 0)),
              pl.BlockSpec((None, blk, HEAD), lambda h, j: (h, 0, 0)),
              pl.BlockSpec((None, 1, HEAD), lambda h, j: (h, 0, 0))]
    state = pl.BlockSpec((None, None, HEAD, HEAD), lambda h, j: (h, jm(j), 0, 0))
    out_col = pl.BlockSpec((blk, HEAD), lambda h, j: (jm(j), h))
    return specs, state, out_col


def retention_fwd(proj, cols, gn_w, cosf, sinf, n_heads, *, blk, name):
    s = proj.shape[0]
    nb = s // blk
    consts = _ret_consts(n_heads, blk)
    specs, state_spec, out_col = _ret_specs(n_heads, blk, nb, cols, False)

    def body(q_ref, k_ref, v_ref, g_ref, gn_ref, cos_ref, sin_ref, dm_ref, qd_ref, kd_ref, cd_ref,
             y_ref, st_ref, state):
        @pl.when(pl.program_id(1) == 0)
        def _():
            state[...] = jnp.zeros_like(state)

        st = state[...]
        st_ref[...] = st
        y, new_state = _ret_block(q_ref[...], k_ref[...], v_ref[...], g_ref[...], gn_ref[...], st,
                                  cos_ref[...], sin_ref[...], dm_ref[...], qd_ref[...], kd_ref[...], cd_ref[...])
        y_ref[...] = y.astype(y_ref.dtype)
        state[...] = new_state

    return pl.pallas_call(
        body, grid=(n_heads, nb), in_specs=specs, out_specs=[out_col, state_spec],
        out_shape=[jax.ShapeDtypeStruct((s, n_heads * HEAD), BF16),
                   jax.ShapeDtypeStruct((n_heads, nb, HEAD, HEAD), F32)],
        scratch_shapes=[pltpu.VMEM((HEAD, HEAD), F32)],
        name=name, compiler_params=_cparams(("parallel", "arbitrary")),
    )(proj, proj, proj, proj, gn_w, cosf, sinf, *consts)


def retention_bwd(proj, cols, gn_w, cosf, sinf, states, dy, n_heads, *, blk, name):
    s = proj.shape[0]
    nb = s // blk
    consts = _ret_consts(n_heads, blk)
    specs, state_spec, out_col = _ret_specs(n_heads, blk, nb, cols, True)

    def body(q_ref, k_ref, v_ref, g_ref, gn_ref, cos_ref, sin_ref, dm_ref, qd_ref, kd_ref, cd_ref,
             st_ref, dy_ref, dq_ref, dk_ref, dv_ref, dg_ref, dgn_ref, dstate):
        first = pl.program_id(1) == 0

        @pl.when(first)
        def _():
            dstate[...] = jnp.zeros_like(dstate)
            dgn_ref[...] = jnp.zeros_like(dgn_ref)

        tabs = (cos_ref[...], sin_ref[...], dm_ref[...], qd_ref[...], kd_ref[...], cd_ref[...])
        fn = lambda q, k, v, g, gnw, st: _ret_block(q, k, v, g, gnw, st, *tabs)
        _, f_vjp = jax.vjp(fn, q_ref[...], k_ref[...], v_ref[...], g_ref[...], gn_ref[...], st_ref[...])
        dq, dk, dv, dg, dgn, dst = f_vjp((dy_ref[...].astype(F32), dstate[...]))
        dq_ref[...] = dq.astype(dq_ref.dtype)
        dk_ref[...] = dk.astype(dk_ref.dtype)
        dv_ref[...] = dv.astype(dv_ref.dtype)
        dg_ref[...] = dg.astype(dg_ref.dtype)
        dgn_ref[...] += dgn
        dstate[...] = dst

    o_shape = jax.ShapeDtypeStruct((s, n_heads * HEAD), BF16)
    dy_spec = out_col
    if isinstance(dy, tuple):
        dy, lead = dy
        dy_spec = pl.BlockSpec((None, blk, HEAD), lambda h, j: (lead, nb - 1 - j, h))
    return pl.pallas_call(
        body, grid=(n_heads, nb), in_specs=specs + [state_spec, dy_spec],
        out_specs=[out_col, out_col, out_col, out_col, pl.BlockSpec((1, HEAD), lambda h, j: (0, h))],
        out_shape=[o_shape, o_shape, o_shape, o_shape, jax.ShapeDtypeStruct((1, n_heads * HEAD), F32)],
        scratch_shapes=[pltpu.VMEM((HEAD, HEAD), F32)],
        name=name, compiler_params=_cparams(("parallel", "arbitrary")),
    )(proj, proj, proj, proj, gn_w, cosf, sinf, *consts, states, dy)


CONV_PAD = 8


def _conv_pre(u_ext, taps, bias, n_out):
    n = u_ext.shape[0]
    views = [pltpu.roll(u_ext, n - (k + CONV_PAD - SSD_K + 1), 0)[:n_out] for k in range(SSD_K)]
    pre = bias
    for k in range(SSD_K):
        pre = pre + taps[k] * views[k]
    return pre, views


def ssd_conv_fwd(u_pad, taps, bias, *, chunk, name):
    s, c = u_pad.shape[0] - 2 * CONV_PAD, u_pad.shape[1]

    def body(u_ref, t0, t1, t2, t3, b_ref, o_ref):
        taps_v = [t[...] for t in (t0, t1, t2, t3)]
        bias_v = b_ref[...]

        @pl.loop(0, s // chunk)
        def _(ci):
            r0 = pl.multiple_of(ci * chunk, chunk)
            pre, _ = _conv_pre(u_ref[pl.ds(r0, chunk + CONV_PAD), :], taps_v, bias_v, chunk)
            o_ref[pl.ds(r0, chunk), :] = pre * jax.nn.sigmoid(pre)

    row = pl.BlockSpec((1, LANE), lambda i: (0, i))
    return pl.pallas_call(
        body, grid=(c // LANE,),
        in_specs=[pl.BlockSpec((s + 2 * CONV_PAD, LANE), lambda i: (0, i))] + [row] * 5,
        out_specs=pl.BlockSpec((s, LANE), lambda i: (0, i)),
        out_shape=jax.ShapeDtypeStruct((s, c), F32),
        name=name, compiler_params=_cparams(("parallel",)),
    )(u_pad, *taps, bias)


def ssd_conv_bwd(u_pad, taps, bias, dxc_pad, *, chunk, name):
    s, c = u_pad.shape[0] - 2 * CONV_PAD, u_pad.shape[1]
    ext = chunk + CONV_PAD

    def body(u_ref, t0, t1, t2, t3, b_ref, d_ref, du_ref, dw_ref, db_ref):
        taps_v = [t[...] for t in (t0, t1, t2, t3)]
        bias_v = b_ref[...]
        dw_ref[...] = jnp.zeros_like(dw_ref)
        db_ref[...] = jnp.zeros_like(db_ref)

        @pl.loop(0, s // chunk)
        def _(ci):
            r0 = pl.multiple_of(ci * chunk, chunk)
            pre, views = _conv_pre(u_ref[pl.ds(r0, ext + CONV_PAD), :], taps_v, bias_v, ext)
            sig = jax.nn.sigmoid(pre)
            dpre = d_ref[pl.ds(r0, ext), :] * (sig * (1.0 + pre * (1.0 - sig)))
            du = taps_v[SSD_K - 1] * dpre[:chunk]
            for k in range(SSD_K - 1):
                du = du + taps_v[k] * pltpu.roll(dpre, ext - (SSD_K - 1 - k), 0)[:chunk]
            du_ref[pl.ds(r0, chunk), :] = du.astype(du_ref.dtype)
            own = dpre[:chunk]
            for k in range(SSD_K):
                dw_ref[k:k + 1, :] += jnp.sum(own * views[k][:chunk], axis=0, keepdims=True)
            db_ref[...] += jnp.sum(own, axis=0, keepdims=True)

    row = pl.BlockSpec((1, LANE), lambda i: (0, i))
    return pl.pallas_call(
        body, grid=(c // LANE,),
        in_specs=[pl.BlockSpec((s + 2 * CONV_PAD, LANE), lambda i: (0, i))] + [row] * 5
        + [pl.BlockSpec((s + CONV_PAD, LANE), lambda i: (0, i))],
        out_specs=[pl.BlockSpec((s, LANE), lambda i: (0, i)), pl.BlockSpec((SSD_K, LANE), lambda i: (0, i)), row],
        out_shape=[jax.ShapeDtypeStruct((s, c), BF16), jax.ShapeDtypeStruct((SSD_K, c), F32),
                   jax.ShapeDtypeStruct((1, c), F32)],
        name=name, compiler_params=_cparams(("parallel",)),
    )(u_pad, *taps, bias, dxc_pad)


def _tri_dot(tri, x, passes=3):
    out = None
    rem = x
    for _ in range(passes):
        piece = rem.astype(BF16)
        rem = rem - piece.astype(F32)
        d = _dot(tri, piece)
        out = d if out is None else out + d
    return out


def _tri(n, upper):
    rr = lax.broadcasted_iota(jnp.int32, (n, n), 0)
    cc = lax.broadcasted_iota(jnp.int32, (n, n), 1)
    return ((rr <= cc) if upper else (rr >= cc)).astype(BF16)


@jax.custom_vjp
def _cumsum_rows(a):
    return _tri_dot(_tri(a.shape[0], False), a)


_cumsum_rows.defvjp(lambda a: (_cumsum_rows(a), None), lambda _, g: (_tri_dot(_tri(g.shape[0], True), g),))


def _softplus(x):
    return jnp.maximum(x, 0.0) + jnp.log(1.0 + jnp.exp(-jnp.abs(x)))


def _ssd_block(x, bm, cm, dtraw, dtb, alog, dsk, state_t, group):
    blk, width = x.shape
    e_heads = width // SSD_P
    dt = _softplus(dtraw + dtb)
    acum = _cumsum_rows(dt * (-jnp.exp(alog)))
    acum_t = acum.T
    lane_h = lax.broadcasted_iota(jnp.int32, (1, LANE), 1)
    sub_h = lax.broadcasted_iota(jnp.int32, (LANE, 1), 0)
    lane_e = lax.broadcasted_iota(jnp.int32, (1, width), 1) // SSD_P
    causal = lax.broadcasted_iota(jnp.int32, (blk, blk), 0) >= lax.broadcasted_iota(jnp.int32, (blk, blk), 1)
    last_row = lax.broadcasted_iota(jnp.int32, (blk, 1), 0) == blk - 1
    cb = _dot(cm.astype(BF16), bm.astype(BF16), _NT)
    y = jnp.zeros((blk, width), F32)
    dt_l = jnp.zeros((blk, width), F32)
    ac_l = jnp.zeros((blk, width), F32)
    d_l = jnp.zeros((1, width), F32)
    for e in range(e_heads):
        head = group * e_heads + e
        pick = lane_h == head
        col = jnp.sum(jnp.where(pick, acum, 0.0), axis=1, keepdims=True)
        dt_e = jnp.sum(jnp.where(pick, dt, 0.0), axis=1, keepdims=True)
        d_e = jnp.sum(jnp.where(pick, dsk, 0.0), axis=1, keepdims=True)
        row = jnp.sum(jnp.where(sub_h == head, acum_t, 0.0), axis=0, keepdims=True)
        mine = lane_e == e
        decay = jnp.exp(jnp.where(causal, col - row, -jnp.inf))
        y = y + _dot((cb * decay).astype(BF16), jnp.where(mine, x * dt_e, 0.0).astype(BF16))
        dt_l = dt_l + jnp.where(mine, dt_e, 0.0)
        ac_l = ac_l + jnp.where(mine, col, 0.0)
        d_l = d_l + jnp.where(mine, d_e, 0.0)
    ac_last = jnp.sum(jnp.where(last_row, ac_l, 0.0), axis=0, keepdims=True)
    y = y + jnp.exp(ac_l) * _dot(cm.astype(BF16), state_t.astype(BF16)) + x * d_l
    inject = (x * dt_l * jnp.exp(ac_last - ac_l)).astype(BF16)
    new_state = state_t * jnp.exp(ac_last) + _dot(bm.astype(BF16), inject, _TN)
    return y, new_state


def ssd_scan_fwd(xc, proj, dt_col, dtb, alog, dsk, *, blk, name):
    s = xc.shape[0]
    nb = s // blk
    e_w = 4 * SSD_P
    b_off = SSD_G * e_w // SSD_N
    c_off = b_off + SSD_G
    row = pl.BlockSpec((1, LANE), lambda j, g: (0, 0))

    def body(x_ref, b_ref, c_ref, dt_ref, dtb_ref, al_ref, d_ref, y_ref, st_ref, state):
        j, g = pl.program_id(0), pl.program_id(1)

        @pl.when(j == 0)
        def _():
            state[g] = jnp.zeros((SSD_N, e_w), F32)

        st = state[g]
        st_ref[...] = st
        y, new_state = _ssd_block(x_ref[...], b_ref[...], c_ref[...], dt_ref[...], dtb_ref[...], al_ref[...],
                                  d_ref[...], st, g)
        y_ref[...] = y
        state[g] = new_state

    return pl.pallas_call(
        body, grid=(nb, SSD_G),
        in_specs=[pl.BlockSpec((blk, e_w), lambda j, g: (j, g)),
                  pl.BlockSpec((blk, SSD_N), lambda j, g: (j, b_off + g)),
                  pl.BlockSpec((blk, SSD_N), lambda j, g: (j, c_off + g)),
                  pl.BlockSpec((blk, LANE), lambda j, g: (j, dt_col)), row, row, row],
        out_specs=[pl.BlockSpec((blk, e_w), lambda j, g: (j, g)),
                   pl.BlockSpec((None, None, SSD_N, e_w), lambda j, g: (j, g, 0, 0))],
        out_shape=[jax.ShapeDtypeStruct((s, SSD_G * e_w), F32),
                   jax.ShapeDtypeStruct((nb, SSD_G, SSD_N, e_w), F32)],
        scratch_shapes=[pltpu.VMEM((SSD_G, SSD_N, e_w), F32)],
        name=name, compiler_params=_cparams(("arbitrary", "arbitrary")),
    )(xc, xc, xc, proj, dtb, alog, dsk)


def ssd_scan_bwd(xc, proj, dt_col, dtb, alog, dsk, states, dy, *, blk, name):
    s = xc.shape[0]
    nb = s // blk
    e_w = 4 * SSD_P
    b_off = SSD_G * e_w // SSD_N
    c_off = b_off + SSD_G
    row = pl.BlockSpec((1, LANE), lambda j, g: (0, 0))
    jm = lambda j: nb - 1 - j

    def body(x_ref, b_ref, c_ref, dt_ref, dtb_ref, al_ref, d_ref, st_ref, dy_ref,
             dx_ref, db_ref, dc_ref, ddt_ref, ddtb_ref, dal_ref, dd_ref, dstate):
        j, g = pl.program_id(0), pl.program_id(1)

        @pl.when(j == 0)
        def _():
            dstate[g] = jnp.zeros((SSD_N, e_w), F32)

        @pl.when(jnp.logical_and(j == 0, g == 0))
        def _():
            ddtb_ref[...] = jnp.zeros_like(ddtb_ref)
            dal_ref[...] = jnp.zeros_like(dal_ref)
            dd_ref[...] = jnp.zeros_like(dd_ref)

        @pl.when(g == 0)
        def _():
            ddt_ref[...] = jnp.zeros_like(ddt_ref)

        fn = functools.partial(_ssd_block, group=g)
        _, f_vjp = jax.vjp(fn, x_ref[...], b_ref[...], c_ref[...], dt_ref[...], dtb_ref[...], al_ref[...],
                           d_ref[...], st_ref[...])
        dx, db, dc, ddt, ddtb, dal, dd, dst = f_vjp((dy_ref[...], dstate[g]))
        dx_ref[...] = dx
        db_ref[...] = db
        dc_ref[...] = dc
        ddt_ref[...] += ddt
        ddtb_ref[...] += ddtb
        dal_ref[...] += dal
        dd_ref[...] += dd
        dstate[g] = dst

    return pl.pallas_call(
        body, grid=(nb, SSD_G),
        in_specs=[pl.BlockSpec((blk, e_w), lambda j, g: (jm(j), g)),
                  pl.BlockSpec((blk, SSD_N), lambda j, g: (jm(j), b_off + g)),
                  pl.BlockSpec((blk, SSD_N), lambda j, g: (jm(j), c_off + g)),
                  pl.BlockSpec((blk, LANE), lambda j, g: (jm(j), dt_col)), row, row, row,
                  pl.BlockSpec((None, None, SSD_N, e_w), lambda j, g: (jm(j), g, 0, 0)),
                  pl.BlockSpec((blk, e_w), lambda j, g: (jm(j), g))],
        out_specs=[pl.BlockSpec((blk, e_w), lambda j, g: (jm(j), g)),
                   pl.BlockSpec((blk, SSD_N), lambda j, g: (jm(j), g)),
                   pl.BlockSpec((blk, SSD_N), lambda j, g: (jm(j), g)),
                   pl.BlockSpec((blk, LANE), lambda j, g: (jm(j), 0)), row, row, row],
        out_shape=[jax.ShapeDtypeStruct((s, SSD_G * e_w), F32),
                   jax.ShapeDtypeStruct((s, SSD_G * SSD_N), F32),
                   jax.ShapeDtypeStruct((s, SSD_G * SSD_N), F32),
                   jax.ShapeDtypeStruct((s, LANE), F32)] + [jax.ShapeDtypeStruct((1, LANE), F32)] * 3,
        scratch_shapes=[pltpu.VMEM((SSD_G, SSD_N, e_w), F32)],
        name=name, compiler_params=_cparams(("arbitrary", "arbitrary")),
    )(xc, xc, xc, proj, dtb, alog, dsk, states, dy)


SB_DEAD = -104.0

_NT = (((1,), (1,)), ((), ()))
_TN = (((0,), (0,)), ((), ()))


def _dot(a, b, dims=(((1,), (0,)), ((), ()))):
    return lax.dot_general(a, b, dims, preferred_element_type=F32)


def _split_dot(x, tri, passes):
    out = None
    rem = x
    for _ in range(passes):
        piece = rem.astype(BF16)
        rem = rem - piece.astype(F32)
        d = _dot(piece, tri)
        out = d if out is None else out + d
    return out


def _sb_scores(q, k_ref, j, blk, row, scale):
    kb = k_ref[pl.ds(pl.multiple_of(j * blk, blk), blk), :].astype(BF16)
    z = _dot(q, kb, _NT) * scale
    col = j * blk + lax.broadcasted_iota(jnp.int32, (blk, blk), 1)
    mask = col < row
    sp = jnp.maximum(z, 0.0) + jnp.log(1.0 + jnp.exp(-jnp.abs(z)))
    lk = jnp.where(mask, -sp, 0.0)
    return kb, mask, lk, z - sp


def sb_attention_fwd(proj, q_col, k_col, v_col, n_heads, *, blk, name):
    s = proj.shape[0]
    nq = s // blk
    scale = HEAD ** -0.5

    def body(q_ref, k_ref, v_ref, o_ref, r_ref, n_ref):
        i = pl.program_id(1)
        q = q_ref[...].astype(BF16)
        row = i * blk + lax.broadcasted_iota(jnp.int32, (blk, blk), 0)
        rr = lax.broadcasted_iota(jnp.int32, (blk, blk), 0)
        cc = lax.broadcasted_iota(jnp.int32, (blk, blk), 1)
        tri_after = (rr > cc).astype(BF16)

        def alive(carry):
            jj, _, run = carry
            return jnp.logical_and(jj <= i, jnp.max(run) > SB_DEAD)

        def step(carry):
            jj, acc, run = carry
            j = i - jj
            _, mask, lk, ls = _sb_scores(q, k_ref, j, blk, row, scale)
            later = _split_dot(lk, tri_after, 2) + run
            w = jnp.where(mask, jnp.exp(ls + later), 0.0)
            vb = v_ref[pl.ds(pl.multiple_of(j * blk, blk), blk), :].astype(BF16)
            return jj + 1, acc + _dot(w.astype(BF16), vb), run + jnp.sum(lk, axis=1, keepdims=True)

        n, acc, run = lax.while_loop(
            alive, step, (jnp.int32(0), jnp.zeros((blk, HEAD), F32), jnp.zeros((blk, 1), F32)))
        o_ref[...] = acc.astype(o_ref.dtype)
        r_ref[...] = jnp.broadcast_to(run, (blk, HEAD))
        n_ref[pl.program_id(0), i] = n

    blk_spec = lambda c0: pl.BlockSpec((blk, HEAD), lambda h, i: (i, c0 + h))
    full_spec = lambda c0: pl.BlockSpec((s, HEAD), lambda h, i: (0, c0 + h))
    out_spec = pl.BlockSpec((blk, HEAD), lambda h, i: (i, h))
    return pl.pallas_call(
        body, grid=(n_heads, nq),
        in_specs=[blk_spec(q_col), full_spec(k_col), full_spec(v_col)],
        out_specs=[out_spec, out_spec, pl.BlockSpec(memory_space=pltpu.SMEM)],
        out_shape=[jax.ShapeDtypeStruct((s, n_heads * HEAD), BF16),
                   jax.ShapeDtypeStruct((s, n_heads * HEAD), F32),
                   jax.ShapeDtypeStruct((n_heads, nq), jnp.int32)],
        name=name, compiler_params=_cparams(("arbitrary", "arbitrary")),
    )(proj, proj, proj)


def sb_attention_bwd(proj, d_out, run_tot, visited, q_col, k_col, v_col, n_heads, *, blk, name):
    s = proj.shape[0]
    nq = s // blk
    scale = HEAD ** -0.5

    def body(n_ref, q_ref, k_ref, v_ref, do_ref, r_ref, dq_ref, dk_ref, dv_ref, dk_acc, dv_acc):
        i = pl.program_id(1)
        first = i + 1 - jnp.clip(n_ref[pl.program_id(0), i], 1, i + 1)

        @pl.when(i == 0)
        def _():
            dk_acc[...] = jnp.zeros_like(dk_acc)
            dv_acc[...] = jnp.zeros_like(dv_acc)

        q = q_ref[...].astype(BF16)
        do = do_ref[...].astype(BF16)
        rtot = r_ref[:, :1]
        row = i * blk + lax.broadcasted_iota(jnp.int32, (blk, blk), 0)
        rr = lax.broadcasted_iota(jnp.int32, (blk, blk), 0)
        cc = lax.broadcasted_iota(jnp.int32, (blk, blk), 1)
        tri_upto = (rr <= cc).astype(BF16)
        tri_before = (rr < cc).astype(BF16)

        def step(j, carry):
            dq, pre, gpre = carry
            kb, mask, lk, ls = _sb_scores(q, k_ref, j, blk, row, scale)
            rows = pl.ds(pl.multiple_of(j * blk, blk), blk)
            vb = v_ref[rows, :].astype(BF16)
            later = rtot - (pre + _split_dot(lk, tri_upto, 3))
            w = jnp.where(mask, jnp.exp(ls + later), 0.0)
            g = w * _dot(do, vb, _NT)
            g_before = _split_dot(g, tri_before, 2) + gpre
            sig = jnp.exp(ls)
            dz = (jnp.where(mask, g * (1.0 - sig) - sig * g_before, 0.0) * scale).astype(BF16)
            dk_acc[rows, :] += _dot(dz, q, _TN)
            dv_acc[rows, :] += _dot(w.astype(BF16), do, _TN)
            return (dq + _dot(dz, kb), pre + jnp.sum(lk, axis=1, keepdims=True),
                    gpre + jnp.sum(g, axis=1, keepdims=True))

        zero = jnp.zeros((blk, 1), F32)
        dq, _, _ = lax.fori_loop(first, i + 1, step, (jnp.zeros((blk, HEAD), F32), zero, zero))
        dq_ref[...] = dq.astype(dq_ref.dtype)

        @pl.when(i == nq - 1)
        def _():
            dk_ref[...] = dk_acc[...].astype(dk_ref.dtype)
            dv_ref[...] = dv_acc[...].astype(dv_ref.dtype)

    blk_spec = lambda c0: pl.BlockSpec((blk, HEAD), lambda h, i: (i, c0 + h))
    full_spec = lambda c0: pl.BlockSpec((s, HEAD), lambda h, i: (0, c0 + h))
    o_shape = jax.ShapeDtypeStruct((s, n_heads * HEAD), BF16)
    do_spec = blk_spec(0)
    if isinstance(d_out, tuple):
        d_out, lead = d_out
        do_spec = pl.BlockSpec((None, blk, HEAD), lambda h, i: (lead, i, h))
    return pl.pallas_call(
        body, grid=(n_heads, nq),
        in_specs=[pl.BlockSpec(memory_space=pltpu.SMEM), blk_spec(q_col), full_spec(k_col), full_spec(v_col),
                  do_spec, blk_spec(0)],
        out_specs=[blk_spec(0), full_spec(0), full_spec(0)],
        out_shape=[o_shape, o_shape, o_shape],
        scratch_shapes=[pltpu.VMEM((s, HEAD), F32), pltpu.VMEM((s, HEAD), F32)],
        name=name, compiler_params=_cparams(("parallel", "arbitrary")),
    )(visited, proj, proj, proj, d_out, run_tot)


def _hosting(hosted):
    arrs = [arr for _, arr in hosted]
    shapes = [_comm_out_shape(kind, arr) for kind, arr in hosted]

    def ops(src_refs, dst_refs, sems):
        return [_COMM[kind](src_refs[c], dst_refs[c], *sems[3 * c:3 * c + 3]) for c, (kind, _) in enumerate(hosted)]

    return arrs, [_HBM] * len(hosted), shapes, _COMM_SEMS * len(hosted), ops


def ffn_up_swiglu(h, w_gu, *, tm, hosted=(), name):
    s, d = h.shape
    tm = _pick(s, tm)
    nb, _, _, hb = w_gu.shape
    n_host = len(hosted)
    h_arrs, h_specs, h_shapes, h_sems, h_ops = _hosting(hosted)
    grid = (s // tm, nb)

    def body(h_ref, w_ref, *rest):
        src_refs, (gu_ref, act_ref) = rest[:n_host], rest[n_host:n_host + 2]
        dst_refs, sems = rest[n_host + 2:2 * n_host + 2], rest[2 * n_host + 2:]
        comms = h_ops(src_refs, dst_refs, sems)
        i, g = pl.program_id(0), pl.program_id(1)

        if hosted:
            @pl.when(jnp.logical_and(i == 0, g == 0))
            def _():
                for start, _ in comms:
                    start()

        a = h_ref[...]
        gate = _dot(a, w_ref[0])
        up = _dot(a, w_ref[1])
        gu_ref[0] = gate
        gu_ref[1] = up
        act_ref[...] = (gate * jax.nn.sigmoid(gate) * up).astype(act_ref.dtype)

        if hosted:
            @pl.when(jnp.logical_and(i == grid[0] - 1, g == grid[1] - 1))
            def _():
                for _, finish in comms:
                    finish()

    res = pl.pallas_call(
        body, grid=grid,
        in_specs=[pl.BlockSpec((tm, d), lambda i, g: (i, 0)),
                  pl.BlockSpec((None, 2, d, hb), lambda i, g: (g, 0, 0, 0))] + h_specs,
        out_specs=[pl.BlockSpec((None, 2, tm, hb), lambda i, g: (g, 0, i, 0)),
                   pl.BlockSpec((None, tm, hb), lambda i, g: (g, i, 0))] + h_specs,
        out_shape=[jax.ShapeDtypeStruct((nb, 2, s, hb), F32), jax.ShapeDtypeStruct((nb, s, hb), BF16)] + h_shapes,
        scratch_shapes=h_sems, name=name,
        compiler_params=_cparams(("arbitrary", "arbitrary") if hosted else ("parallel", "parallel")),
    )(h, w_gu, *h_arrs)
    return res[0], res[1], res[2:]


def ffn_down_dx_swiglu(df, w_down, gu, *, tm, name):
    s, d = df.shape
    tm = _pick(s, tm)
    nb, hb, _ = w_down.shape

    def body(df_ref, w_ref, gu_ref, dgu_ref):
        dact = _dot(df_ref[...], w_ref[...], _NT)
        gate, up = gu_ref[0], gu_ref[1]
        sig = jax.nn.sigmoid(gate)
        dgu_ref[0] = (dact * up * (sig * (1.0 + gate * (1.0 - sig)))).astype(dgu_ref.dtype)
        dgu_ref[1] = (dact * gate * sig).astype(dgu_ref.dtype)

    blk = pl.BlockSpec((None, 2, tm, hb), lambda i, g: (g, 0, i, 0))
    return pl.pallas_call(
        body, grid=(s // tm, nb),
        in_specs=[pl.BlockSpec((tm, d), lambda i, g: (i, 0)), pl.BlockSpec((None, hb, d), lambda i, g: (g, 0, 0)), blk],
        out_specs=blk, out_shape=jax.ShapeDtypeStruct(gu.shape, BF16),
        name=name, compiler_params=_cparams(("parallel", "parallel")),
    )(df, w_down, gu)


def loss_head(y, target, *, tr, name):
    s, d = y.shape

    def body(y_ref, t_ref, l_ref, dy_ref):
        err = y_ref[...] - t_ref[...]
        dy_ref[...] = err * (1.0 / d)
        part = 0.5 * jnp.sum(jnp.mean(err * err, axis=-1, keepdims=True), axis=0, keepdims=True)

        @pl.when(pl.program_id(0) == 0)
        def _():
            l_ref[...] = jnp.zeros_like(l_ref)

        l_ref[...] += jnp.broadcast_to(part, l_ref.shape)

    row = pl.BlockSpec((tr, d), lambda i: (i, 0))
    return pl.pallas_call(
        body, grid=(s // tr,), in_specs=[row, row],
        out_specs=[pl.BlockSpec((8, LANE), lambda i: (0, 0)), row],
        out_shape=[jax.ShapeDtypeStruct((8, LANE), F32), jax.ShapeDtypeStruct((s, d), F32)],
        name=name, compiler_params=_cparams(("arbitrary",)),
    )(y, target)


def _adamw_math(w, g, m, v):
    m = ADAM_B1 * m + (1.0 - ADAM_B1) * g
    v = ADAM_B2 * v + (1.0 - ADAM_B2) * jnp.square(g)
    m_hat = m / (1.0 - ADAM_B1 ** ADAM_STEP)
    v_hat = v / (1.0 - ADAM_B2 ** ADAM_STEP)
    delta = -ADAM_LR * (m_hat / (jnp.sqrt(v_hat) + ADAM_EPS) + ADAM_WD * w)
    return delta, m, v


def adamw_sum(w, m, v, parts, *, tr, name):
    n, r, c = parts.shape
    tr = _pick_rows(r, tr)

    def body(w_ref, m_ref, v_ref, p_ref, g_ref, d_ref, nm_ref, nv_ref):
        g = p_ref[0].astype(F32)
        for i in range(1, n):
            g = g + p_ref[i].astype(F32)
        delta, nm, nv = _adamw_math(w_ref[...], g, m_ref[...], v_ref[...])
        g_ref[...] = g
        d_ref[...] = delta
        nm_ref[...] = nm
        nv_ref[...] = nv

    row = pl.BlockSpec((tr, c), lambda i: (i, 0))
    shape = jax.ShapeDtypeStruct((r, c), F32)
    return pl.pallas_call(
        body, grid=(r // tr,),
        in_specs=[row, row, row, pl.BlockSpec((n, tr, c), lambda i: (0, i, 0))],
        out_specs=[row] * 4, out_shape=[shape] * 4,
        name=name, compiler_params=_cparams(("parallel",)),
    )(w, m, v, parts)


def adamw_into(w_all, m_all, v_all, parts, layer, row_off, prev, *, tr, name):
    depth, r, c = w_all.shape
    n = parts.shape[0]
    tr = _pick_rows(r, tr)
    assert row_off % tr == 0
    off = row_off // tr
    if prev is None:
        prev = [lax.empty(w_all.shape, F32) for _ in range(4)]

    def body(w_ref, m_ref, v_ref, p_ref, _g, _d, _m, _v, g_ref, d_ref, nm_ref, nv_ref):
        g = p_ref[0].astype(F32)
        for i in range(1, n):
            g = g + p_ref[i].astype(F32)
        delta, nm, nv = _adamw_math(w_ref[...], g, m_ref[...], v_ref[...])
        g_ref[...] = g
        d_ref[...] = delta
        nm_ref[...] = nm
        nv_ref[...] = nv

    lay = pl.BlockSpec((None, tr, c), lambda i: (layer, i, 0))
    untouched = pl.BlockSpec(memory_space=pl.ANY)
    return pl.pallas_call(
        body, grid=(r // tr,),
        in_specs=[lay, lay, lay, pl.BlockSpec((n, tr, c), lambda i: (0, off + i, 0))] + [untouched] * 4,
        out_specs=[lay] * 4, out_shape=[jax.ShapeDtypeStruct(w_all.shape, F32)] * 4,
        input_output_aliases={4: 0, 5: 1, 6: 2, 7: 3},
        name=name, compiler_params=_cparams(("parallel",)),
    )(w_all, m_all, v_all, parts, *prev)


def _pick_rows(r, pref):
    t = min(pref, r)
    while r % t or (t % 16 and t != r):
        t -= 1
    return t


_HBM = pl.BlockSpec(memory_space=pltpu.HBM)
_MESH = pl.DeviceIdType.MESH


def _flat_index(px, py, pc):
    return 4 * px + 2 * py + pc


def _gather_ops(x_ref, out_ref, send_sems, recv_sems, local_sem):
    x, y, c = lax.axis_index("x"), lax.axis_index("y"), lax.axis_index("c")
    me, sibling = (x, y, c), (x, y, 1 - c)
    chips = [(1 - x, y), (x, 1 - y), (1 - x, 1 - y)]

    def slot(p):
        return out_ref.at[_flat_index(*p)]

    def copy(k, block, to, src=None):
        return pltpu.make_async_remote_copy(
            src_ref=slot(block) if src is None else src, dst_ref=slot(block),
            send_sem=send_sems.at[k], recv_sem=recv_sems.at[k], device_id=to, device_id_type=_MESH)

    mine = pltpu.make_async_copy(x_ref, slot(me), local_sem)
    first = [copy(0, me, sibling, src=x_ref)]
    first += [copy(1 + j, me, (*chip, c), src=x_ref) for j, chip in enumerate(chips)]
    passed = [copy(4 + j, (*chip, c), sibling) for j, chip in enumerate(chips)]

    def start():
        mine.start()
        for cp in first:
            cp.start()

    def finish():
        for j, chip in enumerate(chips):
            copy(1 + j, (*chip, c), me).wait_recv()
            passed[j].start()
        copy(0, sibling, me).wait_recv()
        for j, chip in enumerate(chips):
            copy(4 + j, (*chip, 1 - c), me).wait_recv()
        for cp in first + passed:
            cp.wait_send()
        mine.wait()

    return start, finish


def _exchange_ops(p_ref, out_ref, send_sems, recv_sems, local_sem):
    x, y, c = lax.axis_index("x"), lax.axis_index("y"), lax.axis_index("c")
    me = _flat_index(x, y, c)
    peers = [(1 - x if k & 4 else x, 1 - y if k & 2 else y, 1 - c if k & 1 else c) for k in range(1, N_DEV)]

    def copy(k, peer, dst_slot):
        return pltpu.make_async_remote_copy(
            src_ref=p_ref.at[_flat_index(*peer)], dst_ref=out_ref.at[dst_slot],
            send_sem=send_sems.at[k], recv_sem=recv_sems.at[k], device_id=peer, device_id_type=_MESH)

    mine = pltpu.make_async_copy(p_ref.at[me], out_ref.at[me], local_sem)
    sends = [copy(k, peer, me) for k, peer in enumerate(peers)]

    def start():
        mine.start()
        for cp in sends:
            cp.start()

    def finish():
        for k, peer in enumerate(peers):
            copy(k, peer, _flat_index(*peer)).wait_recv()
        for cp in sends:
            cp.wait_send()
        mine.wait()

    return start, finish


_COMM = {"gather": _gather_ops, "exchange": _exchange_ops}
_COMM_SEMS = [pltpu.SemaphoreType.DMA((N_DEV - 1,)), pltpu.SemaphoreType.DMA((N_DEV - 1,)), pltpu.SemaphoreType.DMA]


def _comm_out_shape(kind, arr):
    return jax.ShapeDtypeStruct(((N_DEV,) + arr.shape) if kind == "gather" else arr.shape, arr.dtype)


def _comm_call(kind, arr, name):
    def body(src_ref, dst_ref, send_sems, recv_sems, local_sem):
        start, finish = _COMM[kind](src_ref, dst_ref, send_sems, recv_sems, local_sem)
        start()
        finish()

    return pl.pallas_call(body, out_shape=_comm_out_shape(kind, arr), in_specs=[_HBM], out_specs=_HBM,
                          scratch_shapes=_COMM_SEMS, name=name)(arr)


def all_gather(x, *, name):
    return _comm_call("gather", x, name)


def exchange(parts, *, name):
    return _comm_call("exchange", parts, name)


TR = 256
TR_WIDE = 128
BLK = 256
SB_BLK = 256
FFN_TM = 1024
CONV_CHUNK = 512
DT_PAD = LANE

BIG = ("w_in", "w_branch", "w_out", "ffn_w_gu", "ffn_w_down")
SMALL = ("norm_mix_pre", "norm_mix_post", "norm_ffn_pre", "norm_ffn_post", "b_gate", "ret_gn_w", "ssd_conv_w",
         "ssd_conv_b", "ssd_dt_bias", "ssd_a_log", "ssd_d", "ssd_norm_w")


def _row(v):
    return v.reshape(1, -1)


def _pad_lanes(v):
    return jnp.pad(v.reshape(1, -1), ((0, 0), (0, LANE - v.shape[-1])))


def _assemble_w_in(g, d):
    full = jnp.transpose(g, (1, 0, 2)).reshape(d, -1)
    n_main, n_dt = 5 * d, full.shape[1] - 8 * d
    main, dt, gates = full[:, :n_main], full[:, n_main:n_main + n_dt], full[:, n_main + n_dt:]
    return jnp.concatenate([main, gates], axis=1), jnp.pad(dt, ((0, 0), (0, DT_PAD - n_dt)))


def _split_dw_in(dw, dw_dt, d, n_dt):
    n_main = 5 * d
    full = jnp.concatenate([dw[:, :n_main], dw_dt[:, :n_dt], dw[:, n_main:]], axis=1)
    return jnp.transpose(full.reshape(d, N_DEV, -1), (1, 0, 2))


def _layer_fwd(x, lw, cosf, sinf, next_shards):
    s, d = x.shape
    heads = d // 2 // HEAD
    nxt = {}
    (h,) = rowwise(f_rms_pre, [x], [lw["norm_mix_pre"]], [(d, BF16)], tr=TR, name="mix_pre_norm")
    if next_shards is None:
        proj = matmul(h, lw["w_cat"], tn=2048, name="in_proj")
    else:
        proj, (nxt["w_in"],) = matmul(h, lw["w_cat"], tn=2048, hosted=[("gather", next_shards["w_in"])],
                                      name="in_proj_gather")
    dt_raw = matmul(h, lw["w_dt"], name="in_proj_dt")
    ret_cols = tuple(i * heads for i in range(4))
    y_ret, ret_states = retention_fwd(proj, ret_cols, lw["ret_gn_w"], cosf, sinf, heads, blk=BLK, name="retention_fwd")
    y_sb, sb_run, sb_visited = sb_attention_fwd(proj, 4 * heads, 5 * heads, 6 * heads, heads, blk=SB_BLK,
                                                name="stickbreak_fwd")
    u_pad = jnp.pad(proj[:, 4 * d:5 * d], ((CONV_PAD, CONV_PAD), (0, 0)))
    xc = ssd_conv_fwd(u_pad, lw["conv_taps"], lw["ssd_conv_b"], chunk=CONV_CHUNK, name="ssd_conv_fwd")
    y_scan, ssd_states = ssd_scan_fwd(xc, dt_raw, 0, lw["ssd_dt_bias"], lw["ssd_a_log"], lw["ssd_d"], blk=BLK,
                                      name="ssd_scan_fwd")
    z_spec = (proj, d // 2, 7)
    (y_ssd,) = rowwise(f_ssd_gate, [y_scan, z_spec], [lw["ssd_norm_w"]], [(d // 2, BF16)], tr=TR, name="ssd_gate_norm")
    y3 = jnp.stack([y_ret, y_sb, y_ssd])
    u3 = matmul(y3, lw["w_branch"], lead_a="batch", lead_b="batch", name="branch_proj")
    merge_rows = [(u3, d, 0, i) for i in range(3)] + [(proj, d, 5 + i) for i in range(3)]
    (merged,) = rowwise(f_merge, merge_rows, lw["b_gate"], [(d, BF16)], tr=TR_WIDE, name="gate_merge")
    o = matmul(merged, lw["w_out"], name="out_proj")
    (x1,) = rowwise(f_rms_post, [x, o], [lw["norm_mix_post"]], [(d, F32)], tr=TR, name="mix_post_norm")
    (h2,) = rowwise(f_rms_pre, [x1], [lw["norm_ffn_pre"]], [(d, BF16)], tr=TR, name="ffn_pre_norm")
    w_gu = lw["ffn_w_gu"].reshape(N_DEV, 2, d, -1)
    if next_shards is None:
        gu, act, _ = ffn_up_swiglu(h2, w_gu, tm=FFN_TM, name="ffn_up_swiglu")
    else:
        rest = [n for n in BIG if n != "w_in"]
        gu, act, got = ffn_up_swiglu(h2, w_gu, tm=FFN_TM, hosted=[("gather", next_shards[n]) for n in rest],
                                     name="ffn_up_swiglu_gather")
        nxt.update(zip(rest, got))
    f = matmul(act, lw["ffn_w_down"], lead_a="k", lead_b="k", name="ffn_down")
    (x2,) = rowwise(f_rms_post, [x1, f], [lw["norm_ffn_post"]], [(d, F32)], tr=TR, name="ffn_post_norm")
    res = dict(x=x, h=h, proj=proj, dt_raw=dt_raw, ret_states=ret_states, sb_run=sb_run, sb_visited=sb_visited, xc=xc,
               u_pad=u_pad,
               y_scan=y_scan, ssd_states=ssd_states, y3=y3, u3=u3, merged=merged, o=o, x1=x1, h2=h2, gu=gu, act=act, f=f)
    return x2, res, (nxt if next_shards is not None else None)


def _layer_bwd(dx2, res, lw, cosf, sinf, pending):
    x, proj = res["x"], res["proj"]
    s, d = x.shape
    heads = d // 2 // HEAD
    n_dt = d // 2 // SSD_P
    got = {}
    df, dn_ffn_post = rowwise_vjp(f_rms_post, [res["x1"], res["f"]], [lw["norm_ffn_post"]], [dx2],
                                  [(1, BF16, None)], [0], tr=TR, name="ffn_post_norm_bwd")
    dw_down = matmul(res["act"], df, ta=True, lead_a="batch", out_dtype=BF16, name="ffn_down_dw")
    dgu = ffn_down_dx_swiglu(df, lw["ffn_w_down"], res["gu"], tm=FFN_TM, name="ffn_down_dx_swiglu")
    dgu = dgu.reshape(2 * N_DEV, s, -1)
    if pending is None:
        dh2 = matmul(dgu, lw["ffn_w_gu"], tb=True, lead_a="k", lead_b="k", name="ffn_up_dx")
        dw_gu = matmul(res["h2"], dgu, ta=True, lead_b="batch", out_dtype=BF16, name="ffn_up_dw")
    else:
        dh2, (got["prev_w_branch"], got["prev_w_out"]) = matmul(
            dgu, lw["ffn_w_gu"], tb=True, lead_a="k", lead_b="k",
            hosted=[("exchange", pending["w_branch"]), ("exchange", pending["w_out"])], name="ffn_up_dx_exchange")
        dw_gu, (got["prev_w_in"],) = matmul(res["h2"], dgu, ta=True, lead_b="batch", out_dtype=BF16,
                                            hosted=[("exchange", pending["w_in"])], name="ffn_up_dw_exchange")
    dw_gu = dw_gu.reshape(N_DEV, 2 * d, -1)
    dx1, dn_ffn_pre = rowwise_vjp(f_rms_pre, [res["x1"]], [lw["norm_ffn_pre"]], [dh2], [(0, F32, dx2)], [0],
                                  tr=TR, name="ffn_pre_norm_bwd")
    do, dn_mix_post = rowwise_vjp(f_rms_post, [x, res["o"]], [lw["norm_mix_post"]], [dx1], [(1, BF16, None)], [0],
                                  tr=TR, name="mix_post_norm_bwd")
    dmerged = matmul(do, lw["w_out"], tb=True, name="out_proj_dx")
    dw_out = matmul(res["merged"], do, ta=True, out_dtype=BF16, name="out_proj_dw")
    merge_rows = [(res["u3"], d, 0, i) for i in range(3)] + [(proj, d, 5 + i) for i in range(3)]
    mg = rowwise_vjp(f_merge, merge_rows, lw["b_gate"], [dmerged], [(i, BF16, None) for i in range(6)], [0, 1, 2],
                     tr=TR_WIDE, name="gate_merge_bwd")
    du3 = jnp.stack(mg[:3])
    d_gate_logits, db_gate = mg[3:6], mg[6:9]
    dy3 = matmul(du3, lw["w_branch"], tb=True, lead_a="batch", lead_b="batch", name="branch_proj_dx")
    dw_branch = matmul(res["y3"], du3, ta=True, lead_a="batch", lead_b="batch", out_dtype=BF16, name="branch_proj_dw")
    ret_cols = tuple(i * heads for i in range(4))
    dq, dk, dv, dg, d_gn = retention_bwd(proj, ret_cols, lw["ret_gn_w"], cosf, sinf, res["ret_states"], (dy3, 0), heads,
                                         blk=BLK, name="retention_bwd")
    dsq, dsk, dsv = sb_attention_bwd(proj, (dy3, 1), res["sb_run"], res["sb_visited"], 4 * heads, 5 * heads, 6 * heads,
                                     heads, blk=SB_BLK, name="stickbreak_bwd")
    z_spec = (proj, d // 2, 7)
    dy_scan, dz, d_ssd_norm = rowwise_vjp(f_ssd_gate, [res["y_scan"], z_spec], [lw["ssd_norm_w"]], [(dy3, d // 2, 0, 2)],
                                          [(0, F32, None), (1, BF16, None)], [0], tr=TR, name="ssd_gate_norm_bwd")
    dxs, dbm, dcm, ddt, d_dtb, d_alog, d_dskip = ssd_scan_bwd(
        res["xc"], res["dt_raw"], 0, lw["ssd_dt_bias"], lw["ssd_a_log"], lw["ssd_d"], res["ssd_states"], dy_scan,
        blk=BLK, name="ssd_scan_bwd")
    dxc_pad = jnp.pad(jnp.concatenate([dxs, dbm, dcm], axis=1), ((0, CONV_PAD), (0, 0)))
    du, d_taps, d_conv_b = ssd_conv_bwd(res["u_pad"], lw["conv_taps"], lw["ssd_conv_b"], dxc_pad, chunk=CONV_CHUNK,
                                        name="ssd_conv_bwd")
    dproj = jnp.concatenate([dq, dk, dv, dg, dsq, dsk, dsv, dz, du, *d_gate_logits], axis=1)
    dh_dt = matmul(ddt, lw["w_dt"], tb=True, name="in_proj_dt_dx")
    dh, (got["ffn_w_down"], got["ffn_w_gu"]) = matmul(
        dproj, lw["w_cat"], tb=True, add=dh_dt, hosted=[("exchange", dw_down), ("exchange", dw_gu)],
        name="in_proj_dx_exchange")
    dw_cat = matmul(res["h"], dproj, ta=True, tn=2048, out_dtype=BF16, name="in_proj_dw")
    dw_dt = matmul(res["h"], ddt, ta=True, out_dtype=BF16, name="in_proj_dt_dw")
    dx, dn_mix_pre = rowwise_vjp(f_rms_pre, [x], [lw["norm_mix_pre"]], [dh], [(0, F32, dx1)], [0], tr=TR,
                                 name="mix_pre_norm_bwd")
    mine = dict(
        w_in=_split_dw_in(dw_cat, dw_dt, d, n_dt),
        w_branch=jnp.transpose(dw_branch.reshape(3, d // 2, N_DEV, -1), (2, 0, 1, 3)).reshape(N_DEV, 3 * d // 2, -1),
        w_out=dw_out.reshape(N_DEV, d // N_DEV, d),
    )
    small = dict(
        norm_mix_pre=dn_mix_pre[0], norm_mix_post=dn_mix_post[0], norm_ffn_pre=dn_ffn_pre[0],
        norm_ffn_post=dn_ffn_post[0], b_gate=jnp.concatenate([b[0] for b in db_gate]), ret_gn_w=d_gn[0],
        ssd_conv_w=d_taps, ssd_conv_b=d_conv_b[0], ssd_dt_bias=d_dtb[0, :n_dt], ssd_a_log=d_alog[0, :n_dt],
        ssd_d=d_dskip[0, :n_dt], ssd_norm_w=d_ssd_norm[0],
    )
    return dx, got, mine, small


def _adam_rows(cols):
    return max(8, (1 << 17) // cols // 8 * 8)


def kernel(x, positions, norm_mix_pre, norm_mix_post, norm_ffn_pre, norm_ffn_post, w_in, b_gate, ret_gn_w, ssd_conv_w, ssd_conv_b, ssd_dt_bias, ssd_a_log, ssd_d, ssd_norm_w, w_branch_ret, w_branch_sb, w_branch_ssd, w_out, ffn_w_gate, ffn_w_up, ffn_w_down, loss_target, m_norm_mix_pre, m_norm_mix_post, m_norm_ffn_pre, m_norm_ffn_post, m_w_in, m_b_gate, m_ret_gn_w, m_ssd_conv_w, m_ssd_conv_b, m_ssd_dt_bias, m_ssd_a_log, m_ssd_d, m_ssd_norm_w, m_w_branch_ret, m_w_branch_sb, m_w_branch_ssd, m_w_out, m_ffn_w_gate, m_ffn_w_up, m_ffn_w_down, v_norm_mix_pre, v_norm_mix_post, v_norm_ffn_pre, v_norm_ffn_post, v_w_in, v_b_gate, v_ret_gn_w, v_ssd_conv_w, v_ssd_conv_b, v_ssd_dt_bias, v_ssd_a_log, v_ssd_d, v_ssd_norm_w, v_w_branch_ret, v_w_branch_sb, v_w_branch_ssd, v_w_out, v_ffn_w_gate, v_ffn_w_up, v_ffn_w_down):
    depth = w_in.shape[0]
    s, d = x.shape[1], x.shape[2]
    axes = ("x", "y", "c")
    me = _flat_index(lax.axis_index("x"), lax.axis_index("y"), lax.axis_index("c"))

    big_w = dict(w_in=w_in, w_branch=jnp.concatenate([w_branch_ret, w_branch_sb, w_branch_ssd], axis=1), w_out=w_out,
                 ffn_w_gu=jnp.concatenate([ffn_w_gate, ffn_w_up], axis=1), ffn_w_down=ffn_w_down)
    members = dict(w_in=["w_in"], w_branch=["w_branch_ret", "w_branch_sb", "w_branch_ssd"], w_out=["w_out"],
                   ffn_w_gu=["ffn_w_gate", "ffn_w_up"], ffn_w_down=["ffn_w_down"])
    wmv = dict(w_in=(w_in, m_w_in, v_w_in), w_branch_ret=(w_branch_ret, m_w_branch_ret, v_w_branch_ret),
               w_branch_sb=(w_branch_sb, m_w_branch_sb, v_w_branch_sb),
               w_branch_ssd=(w_branch_ssd, m_w_branch_ssd, v_w_branch_ssd), w_out=(w_out, m_w_out, v_w_out),
               ffn_w_gate=(ffn_w_gate, m_ffn_w_gate, v_ffn_w_gate), ffn_w_up=(ffn_w_up, m_ffn_w_up, v_ffn_w_up),
               ffn_w_down=(ffn_w_down, m_ffn_w_down, v_ffn_w_down))

    taps_all = all_gather(ssd_conv_w.reshape(-1, LANE), name="gather_conv_w")
    taps_all = jnp.transpose(taps_all.reshape(N_DEV, depth, SSD_K, -1), (1, 2, 0, 3)).reshape(depth, SSD_K, -1)

    cosf, sinf = rope_tables(positions.reshape(s), tr=TR)
    small_w = dict(norm_mix_pre=norm_mix_pre, norm_mix_post=norm_mix_post, norm_ffn_pre=norm_ffn_pre,
                   norm_ffn_post=norm_ffn_post, b_gate=b_gate, ret_gn_w=ret_gn_w, ssd_conv_b=ssd_conv_b,
                   ssd_dt_bias=ssd_dt_bias, ssd_a_log=ssd_a_log, ssd_d=ssd_d, ssd_norm_w=ssd_norm_w, taps=taps_all)

    def layer_weights(sw, gathered):
        lw = dict(gathered)
        for n in ("norm_mix_pre", "norm_mix_post", "norm_ffn_pre", "norm_ffn_post", "ret_gn_w", "ssd_conv_b", "ssd_norm_w"):
            lw[n] = _row(sw[n])
        for n in ("ssd_dt_bias", "ssd_a_log", "ssd_d"):
            lw[n] = _pad_lanes(sw[n])
        lw["b_gate"] = [_row(sw["b_gate"][i * d:(i + 1) * d]) for i in range(3)]
        lw["conv_taps"] = [sw["taps"][k:k + 1] for k in range(SSD_K)]
        return lw

    def layer_slice(t, l):
        return {n: a[l] for n, a in t.items()}

    def bf16_shards(l):
        return {n: big_w[n][l].astype(BF16) for n in BIG}

    def arrange(g):
        w_cat, w_dt = _assemble_w_in(g["w_in"], d)
        return dict(
            w_cat=w_cat, w_dt=w_dt,
            w_branch=jnp.transpose(g["w_branch"].reshape(N_DEV, 3, d // 2, -1), (1, 2, 0, 3)).reshape(3, d // 2, d),
            w_out=g["w_out"].reshape(d, d),
            ffn_w_gu=g["ffn_w_gu"].reshape(2 * N_DEV, d, -1),
            ffn_w_down=g["ffn_w_down"],
        )

    big_out = {}

    def adam(l, group, parts):
        rows, cols = big_w[group].shape[1:]
        parts = parts.reshape(N_DEV, rows, cols)
        off = 0
        for n in members[group]:
            big_out[n] = adamw_into(*wmv[n], parts, l, off, big_out.get(n), tr=_adam_rows(cols), name="adamw_" + n)
            off += wmv[n][0].shape[1]

    xs, saved = x.reshape(s, d), []
    gathered = {n: all_gather(a, name="gather_" + n) for n, a in bf16_shards(0).items()}
    for l in range(depth):
        lw = layer_weights(layer_slice(small_w, l), arrange(gathered))
        xs, res, gathered = _layer_fwd(xs, lw, cosf, sinf, bf16_shards(l + 1) if l + 1 < depth else None)
        saved.append((res, lw))
    loss_tile, dy = loss_head(xs, loss_target.reshape(s, d), tr=TR, name="loss_head")
    loss = lax.psum(loss_tile[0, 0], axes)

    dx, pending, small_layers = dy, None, [None] * depth
    for l in reversed(range(depth)):
        res, lw = saved[l]
        dx, got, pending, small_layers[l] = _layer_bwd(dx, res, lw, cosf, sinf, pending)
        for n in ("ffn_w_gu", "ffn_w_down"):
            adam(l, n, got[n])
        if l + 1 < depth:
            for n in ("w_in", "w_branch", "w_out"):
                adam(l + 1, n, got["prev_" + n])
    for n in ("w_in", "w_branch", "w_out"):
        adam(0, n, exchange(pending[n], name="exchange_" + n))
    small_g = {n: jnp.stack([small_layers[l][n] for l in range(depth)]) for n in SMALL}

    n_dt = ssd_dt_bias.shape[-1]
    small_in = dict(norm_mix_pre=(norm_mix_pre, m_norm_mix_pre, v_norm_mix_pre), norm_mix_post=(norm_mix_post, m_norm_mix_post, v_norm_mix_post),
                    norm_ffn_pre=(norm_ffn_pre, m_norm_ffn_pre, v_norm_ffn_pre), norm_ffn_post=(norm_ffn_post, m_norm_ffn_post, v_norm_ffn_post),
                    b_gate=(b_gate, m_b_gate, v_b_gate), ret_gn_w=(ret_gn_w, m_ret_gn_w, v_ret_gn_w),
                    ssd_conv_b=(ssd_conv_b, m_ssd_conv_b, v_ssd_conv_b), ssd_dt_bias=(ssd_dt_bias, m_ssd_dt_bias, v_ssd_dt_bias),
                    ssd_a_log=(ssd_a_log, m_ssd_a_log, v_ssd_a_log), ssd_d=(ssd_d, m_ssd_d, v_ssd_d),
                    ssd_norm_w=(ssd_norm_w, m_ssd_norm_w, v_ssd_norm_w))
    rep = [n for n in SMALL if n != "ssd_conv_w"]

    def pack(arrs):
        flat = jnp.concatenate([a.reshape(-1) for a in arrs])
        rows = -(-flat.shape[0] // (16 * LANE)) * 16
        return jnp.pad(flat, (0, rows * LANE - flat.shape[0])).reshape(rows, LANE)

    conv_g = small_g["ssd_conv_w"]
    g_pack = pack([small_g[n] for n in rep] + [conv_g])
    g_all = all_gather(g_pack, name="gather_small_grads")
    zeros_conv = jnp.zeros_like(conv_g)
    w_pack, m_pack, v_pack = (pack([small_in[n][i] for n in rep] + [zeros_conv]) for i in range(3))
    sm = adamw_sum(w_pack, m_pack, v_pack, g_all, tr=TR, name="adamw_small")

    def unpack(p):
        flat, out, off = p.reshape(-1), {}, 0
        for n in rep:
            shp = small_in[n][0].shape
            size = math.prod(shp)
            out[n] = flat[off:off + size].reshape(shp)
            off += size
        out["conv_sum"] = flat[off:off + conv_g.size].reshape(conv_g.shape)
        return out

    sm = [unpack(p) for p in sm]
    ch = ssd_conv_w.shape[-1]
    conv_mine = lax.dynamic_slice_in_dim(sm[0]["conv_sum"], me * ch, ch, axis=2)
    conv_out = adamw_sum(ssd_conv_w.reshape(-1, LANE), m_ssd_conv_w.reshape(-1, LANE), v_ssd_conv_w.reshape(-1, LANE),
                         conv_mine.reshape(1, -1, LANE), tr=TR, name="adamw_conv_w")
    for i in range(4):
        sm[i]["ssd_conv_w"] = conv_out[i].reshape(ssd_conv_w.shape)

    def big_named(i):
        return {n: out[i] for n, out in big_out.items()}

    order = ["norm_mix_pre", "norm_mix_post", "norm_ffn_pre", "norm_ffn_post", "w_in", "b_gate", "ret_gn_w", "ssd_conv_w",
             "ssd_conv_b", "ssd_dt_bias", "ssd_a_log", "ssd_d", "ssd_norm_w", "w_branch_ret", "w_branch_sb", "w_branch_ssd",
             "w_out", "ffn_w_gate", "ffn_w_up", "ffn_w_down"]
    outs = [loss, dx.reshape(x.shape)]
    for i in range(4):
        named = {**sm[i], **big_named(i)}
        outs += [named[n] for n in order]
    return tuple(outs)
```

```python
import functools
import math

import jax
import jax.numpy as jnp
import numpy as np
from jax import lax
from jax.experimental import pallas as pl
from jax.experimental.pallas import tpu as pltpu

F32 = jnp.float32
BF16 = jnp.bfloat16

N_DEV = 8
HEAD = 128
SSD_P = 64
SSD_G = 4
SSD_N = 128
SSD_K = 4
CHUNK = 64
NORM_EPS = 1e-6
ROPE_BASE = 10000.0
LANE = 128
VMEM_LIMIT = 56 * 1024 * 1024

ADAM_LR, ADAM_B1, ADAM_B2, ADAM_EPS, ADAM_WD, ADAM_STEP = 0.001, 0.9, 0.999, 1e-08, 0.01, 10


def _cparams(sem):
    return pltpu.CompilerParams(dimension_semantics=sem, vmem_limit_bytes=VMEM_LIMIT)


def _pick(n, pref):
    if n <= pref:
        return n
    t = pref
    while t >= LANE:
        if n % t == 0:
            return t
        t -= LANE
    return n


def matmul(a, b, *, ta=False, tb=False, lead_a=None, lead_b=None, out_dtype=F32, add=None, hosted=(),
           tm=1024, tn=1024, tk=2048, name="mm"):
    la, lb = lead_a is not None, lead_b is not None
    a2, b2 = a.shape[1:] if la else a.shape, b.shape[1:] if lb else b.shape
    (kd_a, m) = a2 if ta else a2[::-1]
    (kd_b, n) = b2[::-1] if tb else b2
    assert kd_a == kd_b, (a.shape, b.shape)
    nlead = a.shape[0] if la else (b.shape[0] if lb else 1)
    batch = "batch" in (lead_a, lead_b)
    kblocks = nlead if "k" in (lead_a, lead_b) else 1
    if la and lb:
        assert lead_a == lead_b and a.shape[0] == b.shape[0]
    tm, tn, tk = _pick(m, tm), _pick(n, tn), _pick(kd_a, tk)
    kt = kd_a // tk
    nk = kt * kblocks
    grid = (m // tm, (nlead if batch else 1), n // tn, nk)

    def lead_idx(g, k):
        return g if batch else k // kt

    def a_map(i, g, j, k):
        idx = (k % kt, i) if ta else (i, k % kt)
        return ((lead_idx(g, k),) + idx) if la else idx

    def b_map(i, g, j, k):
        idx = (j, k % kt) if tb else (k % kt, j)
        return ((lead_idx(g, k),) + idx) if lb else idx

    a_blk = (tk, tm) if ta else (tm, tk)
    b_blk = (tn, tk) if tb else (tk, tn)
    a_spec = pl.BlockSpec(((None,) + a_blk) if la else a_blk, a_map)
    b_spec = pl.BlockSpec(((None,) + b_blk) if lb else b_blk, b_map)
    if batch:
        o_spec = pl.BlockSpec((None, tm, tn), lambda i, g, j, k: (g, i, j))
        o_shape = jax.ShapeDtypeStruct((nlead, m, n), out_dtype)
    else:
        o_spec = pl.BlockSpec((tm, tn), lambda i, g, j, k: (i, j))
        o_shape = jax.ShapeDtypeStruct((m, n), out_dtype)
    dims = (((0 if ta else 1,), (1 if tb else 0,)), ((), ()))

    n_add, n_host = int(add is not None), len(hosted)
    n_acc = int(nk > 1)

    def body(a_ref, b_ref, *rest):
        c_ref = rest[0] if n_add else None
        src_refs = rest[n_add:n_add + n_host]
        o_ref = rest[n_add + n_host]
        dst_refs = rest[n_add + n_host + 1:n_add + 2 * n_host + 1]
        acc_ref = rest[n_add + 2 * n_host + 1] if n_acc else None
        sems = rest[n_add + 2 * n_host + 1 + n_acc:]
        k = pl.program_id(3)
        ids = [pl.program_id(ax) for ax in range(4)]
        comms = [_COMM[kind](src_refs[c], dst_refs[c], *sems[3 * c:3 * c + 3]) for c, (kind, _) in enumerate(hosted)]

        if hosted:
            @pl.when(functools.reduce(jnp.logical_and, [i == 0 for i in ids]))
            def _():
                for start, _ in comms:
                    start()

        def product():
            return lax.dot_general(a_ref[...].astype(BF16), b_ref[...].astype(BF16), dims, preferred_element_type=F32)

        if n_acc:
            @pl.when(k == 0)
            def _():
                acc_ref[...] = jnp.zeros_like(acc_ref) if c_ref is None else c_ref[...].astype(F32)

            acc_ref[...] += product()

            @pl.when(k == nk - 1)
            def _():
                o_ref[...] = acc_ref[...].astype(o_ref.dtype)
        else:
            o_ref[...] = (product() if c_ref is None else product() + c_ref[...].astype(F32)).astype(o_ref.dtype)

        if hosted:
            @pl.when(functools.reduce(jnp.logical_and, [i == g - 1 for i, g in zip(ids, grid)]))
            def _():
                for _, finish in comms:
                    finish()

    extra = ([] if add is None else [add]) + [arr for _, arr in hosted]
    extra_specs = ([] if add is None else [o_spec]) + [_HBM] * n_host
    out_shapes = [o_shape] + [_comm_out_shape(kind, arr) for kind, arr in hosted]
    scratch = [pltpu.VMEM((tm, tn), F32)] * n_acc + _COMM_SEMS * n_host
    sem = ("arbitrary",) * 4 if hosted else ("parallel", "parallel", "parallel", "arbitrary")
    res = pl.pallas_call(
        body, out_shape=out_shapes, grid=grid, in_specs=[a_spec, b_spec] + extra_specs,
        out_specs=[o_spec] + [_HBM] * n_host, scratch_shapes=scratch, name=name, compiler_params=_cparams(sem),
    )(a, b, *extra)
    return (res[0], res[1:]) if hosted else res[0]


def _row_spec(r, tr):
    if not isinstance(r, tuple):
        return r, pl.BlockSpec((tr, r.shape[-1]), lambda i: (i, 0))
    if len(r) == 3:
        arr, w, cb = r
        return arr, pl.BlockSpec((tr, w), lambda i: (i, cb))
    arr, w, cb, ld = r
    return arr, pl.BlockSpec((None, tr, w), lambda i: (ld, i, cb))


def _full_spec(c):
    nd = c.ndim
    return pl.BlockSpec(c.shape, lambda i: (0,) * nd)


def rowwise(fn, rows, consts, outs, *, tr, name):
    arrs, specs = zip(*[_row_spec(r, tr) for r in rows])
    n_rows = arrs[0].shape[-2]
    nr, nc = len(rows), len(consts)

    def body(*refs):
        vals = [r[...] for r in refs[:nr + nc]]
        res = fn(*vals)
        for o_ref, r in zip(refs[nr + nc:], res):
            o_ref[...] = r.astype(o_ref.dtype)

    return pl.pallas_call(
        body, grid=(n_rows // tr,),
        in_specs=list(specs) + [_full_spec(c) for c in consts],
        out_specs=[pl.BlockSpec((tr, w), lambda i: (i, 0)) for w, _ in outs],
        out_shape=[jax.ShapeDtypeStruct((n_rows, w), dt) for w, dt in outs],
        name=name, compiler_params=_cparams(("parallel",)),
    )(*arrs, *consts)


def rowwise_vjp(fn, rows, consts, cts, row_grads, const_grads, *, tr, name):
    arrs, specs = zip(*[_row_spec(r, tr) for r in rows])
    n_rows = arrs[0].shape[-2]
    nr, nc = len(rows), len(consts)
    ct_present = [c for c in cts if c is not None]
    ct_arrs, ct_specs = zip(*[_row_spec(c, tr) for c in ct_present])
    add_present = [g[2] for g in row_grads if g[2] is not None]
    add_arrs, add_specs = zip(*[_row_spec(c, tr) for c in add_present]) if add_present else ((), ())
    n_ct, n_add = len(ct_present), len(add_present)
    widths = [s.block_shape[-1] for s in specs]

    def body(*refs):
        ins = refs[:nr + nc]
        ct_refs = refs[nr + nc:nr + nc + n_ct]
        add_refs = refs[nr + nc + n_ct:nr + nc + n_ct + n_add]
        out_refs = refs[nr + nc + n_ct + n_add:]
        vals = [r[...] for r in ins]
        res, f_vjp = jax.vjp(fn, *vals)
        it = iter(ct_refs)
        ct_vals = tuple(next(it)[...].astype(r.dtype) if c is not None else jnp.zeros_like(r)
                        for c, r in zip(cts, res))
        grads = f_vjp(ct_vals)
        ita = iter(add_refs)
        for o_ref, (idx, _, add) in zip(out_refs, row_grads):
            g = grads[idx].astype(F32)
            if add is not None:
                g = g + next(ita)[...].astype(F32)
            o_ref[...] = g.astype(o_ref.dtype)
        first = pl.program_id(0) == 0
        for o_ref, idx in zip(out_refs[len(row_grads):], const_grads):
            g = grads[nr + idx].astype(F32)

            @pl.when(first)
            def _():
                o_ref[...] = g

            @pl.when(jnp.logical_not(first))
            def _():
                o_ref[...] += g

    out_specs = [pl.BlockSpec((tr, widths[idx]), lambda i: (i, 0)) for idx, _, _ in row_grads]
    out_shape = [jax.ShapeDtypeStruct((n_rows, widths[idx]), dt) for idx, dt, _ in row_grads]
    out_specs += [_full_spec(consts[idx]) for idx in const_grads]
    out_shape += [jax.ShapeDtypeStruct(consts[idx].shape, F32) for idx in const_grads]
    return pl.pallas_call(
        body, grid=(n_rows // tr,),
        in_specs=list(specs) + [_full_spec(c) for c in consts] + list(ct_specs) + list(add_specs),
        out_specs=out_specs, out_shape=out_shape,
        name=name, compiler_params=_cparams(("arbitrary",)),
    )(*arrs, *consts, *ct_arrs, *add_arrs)


def f_rms(x, w):
    xf = x.astype(F32)
    return xf * lax.rsqrt(jnp.mean(xf * xf, axis=-1, keepdims=True) + NORM_EPS) * w


def f_rms_pre(x, w):
    return (f_rms(x, w),)


def f_rms_post(x, o, w):
    return (x + f_rms(o, w),)


def f_merge(u0, u1, u2, g0, g1, g2, b0, b1, b2):
    return (jax.nn.sigmoid(g0 + b0) * u0 + jax.nn.sigmoid(g1 + b1) * u1 + jax.nn.sigmoid(g2 + b2) * u2,)


def f_ssd_gate(y, z, w):
    return (f_rms(y * jax.nn.silu(z), w),)


@jax.custom_vjp
def _swap_halves(x):
    return pltpu.roll(x, HEAD // 2, 1)


_swap_halves.defvjp(lambda x: (_swap_halves(x), None), lambda _, g: (_swap_halves(g),))


def rope_tables(positions, *, tr):
    s = positions.shape[0]
    half = HEAD // 2
    inv = ROPE_BASE ** (-2.0 * jnp.arange(half, dtype=F32) / HEAD)
    inv = jnp.concatenate([inv, inv]).reshape(1, HEAD)
    sign = jnp.concatenate([-jnp.ones((half,), F32), jnp.ones((half,), F32)]).reshape(1, HEAD)

    def fn(pos, inv, sign):
        ang = pos.astype(F32) * inv
        return jnp.cos(ang), jnp.sin(ang) * sign

    return rowwise(fn, [positions.reshape(s, 1)], [inv, sign], [(HEAD, F32), (HEAD, F32)], tr=tr, name="rope_tables")


def _ret_consts(n_heads, blk):
    lg = np.log1p(-np.exp2(-5.0 - np.arange(n_heads)))[:, None, None]
    i = np.arange(blk)
    dist = np.abs(i[:, None] - i[None, :])[None]
    allowed = ((i[None, :] // CHUNK) <= (i[:, None] // CHUNK))[None]
    dm = np.where(allowed, np.exp(lg * dist), 0.0)
    qd = np.broadcast_to(np.exp(lg * (i[None, :, None] + 1.0)), (n_heads, blk, HEAD))
    kd = np.broadcast_to(np.exp(lg * (blk - 1.0 - i[None, :, None])), (n_heads, blk, HEAD))
    cd = np.broadcast_to(np.exp(lg * blk), (n_heads, 1, HEAD))
    return [jnp.asarray(a, F32) for a in (dm, qd, kd, cd)]


def _ret_block(q, k, v, g, gnw, state, cosf, sinf, dm, qd, kd, cd):
    qr = q * cosf + _swap_halves(q) * sinf
    kr = (k * cosf + _swap_halves(k) * sinf) * (HEAD ** -0.5)
    vb = v.astype(BF16)
    scores = _dot(qr.astype(BF16), kr.astype(BF16), _NT) * dm
    o = _dot(scores.astype(BF16), vb) + _dot((qr * qd).astype(BF16), state.astype(BF16))
    new_state = state * cd + _dot((kr * kd).astype(BF16), vb, _TN)
    mu = jnp.mean(o, axis=-1, keepdims=True)
    var = jnp.mean(jnp.square(o - mu), axis=-1, keepdims=True)
    y = (o - mu) * lax.rsqrt(var + NORM_EPS) * gnw * jax.nn.silu(g)
    return y, new_state


def _ret_specs(n_heads, blk, nb, cols, reverse):
    jm = (lambda j: nb - 1 - j) if reverse else (lambda j: j)
    col = lambda c0: pl.BlockSpec((blk, HEAD), lambda h, j: (jm(j), c0 + h))
    tab = pl.BlockSpec((blk, HEAD), lambda h, j: (jm(j), 0))
    specs = [col(c) for c in cols]
    specs += [pl.BlockSpec((1, HEAD), lambda h, j: (0, h)), tab, tab]
    specs += [pl.BlockSpec((None, blk, blk), lambda h, j: (h, 0, 0)),
              pl.BlockSpec((None, blk, HEAD), lambda h, j: (h, 0, 0)),
              pl.BlockSpec((None, blk, HEAD), lambda h, j: (h, 0, 0)),
              pl.BlockSpec((None, 1, HEAD), lambda h, j: (h, 0, 0))]
    state = pl.BlockSpec((None, None, HEAD, HEAD), lambda h, j: (h, jm(j), 0, 0))
    out_col = pl.BlockSpec((blk, HEAD), lambda h, j: (jm(j), h))
    return specs, state, out_col


def retention_fwd(proj, cols, gn_w, cosf, sinf, n_heads, *, blk, name):
    s = proj.shape[0]
    nb = s // blk
    consts = _ret_consts(n_heads, blk)
    specs, state_spec, out_col = _ret_specs(n_heads, blk, nb, cols, False)

    def body(q_ref, k_ref, v_ref, g_ref, gn_ref, cos_ref, sin_ref, dm_ref, qd_ref, kd_ref, cd_ref,
             y_ref, st_ref, state):
        @pl.when(pl.program_id(1) == 0)
        def _():
            state[...] = jnp.zeros_like(state)

        st = state[...]
        st_ref[...] = st
        y, new_state = _ret_block(q_ref[...], k_ref[...], v_ref[...], g_ref[...], gn_ref[...], st,
                                  cos_ref[...], sin_ref[...], dm_ref[...], qd_ref[...], kd_ref[...], cd_ref[...])
        y_ref[...] = y.astype(y_ref.dtype)
        state[...] = new_state

    return pl.pallas_call(
        body, grid=(n_heads, nb), in_specs=specs, out_specs=[out_col, state_spec],
        out_shape=[jax.ShapeDtypeStruct((s, n_heads * HEAD), BF16),
                   jax.ShapeDtypeStruct((n_heads, nb, HEAD, HEAD), F32)],
        scratch_shapes=[pltpu.VMEM((HEAD, HEAD), F32)],
        name=name, compiler_params=_cparams(("parallel", "arbitrary")),
    )(proj, proj, proj, proj, gn_w, cosf, sinf, *consts)


def retention_bwd(proj, cols, gn_w, cosf, sinf, states, dy, n_heads, *, blk, name):
    s = proj.shape[0]
    nb = s // blk
    consts = _ret_consts(n_heads, blk)
    specs, state_spec, out_col = _ret_specs(n_heads, blk, nb, cols, True)

    def body(q_ref, k_ref, v_ref, g_ref, gn_ref, cos_ref, sin_ref, dm_ref, qd_ref, kd_ref, cd_ref,
             st_ref, dy_ref, dq_ref, dk_ref, dv_ref, dg_ref, dgn_ref, dstate):
        first = pl.program_id(1) == 0

        @pl.when(first)
        def _():
            dstate[...] = jnp.zeros_like(dstate)
            dgn_ref[...] = jnp.zeros_like(dgn_ref)

        tabs = (cos_ref[...], sin_ref[...], dm_ref[...], qd_ref[...], kd_ref[...], cd_ref[...])
        fn = lambda q, k, v, g, gnw, st: _ret_block(q, k, v, g, gnw, st, *tabs)
        _, f_vjp = jax.vjp(fn, q_ref[...], k_ref[...], v_ref[...], g_ref[...], gn_ref[...], st_ref[...])
        dq, dk, dv, dg, dgn, dst = f_vjp((dy_ref[...].astype(F32), dstate[...]))
        dq_ref[...] = dq.astype(dq_ref.dtype)
        dk_ref[...] = dk.astype(dk_ref.dtype)
        dv_ref[...] = dv.astype(dv_ref.dtype)
        dg_ref[...] = dg.astype(dg_ref.dtype)
        dgn_ref[...] += dgn
        dstate[...] = dst

    o_shape = jax.ShapeDtypeStruct((s, n_heads * HEAD), BF16)
    dy_spec = out_col
    if isinstance(dy, tuple):
        dy, lead = dy
        dy_spec = pl.BlockSpec((None, blk, HEAD), lambda h, j: (lead, nb - 1 - j, h))
    return pl.pallas_call(
        body, grid=(n_heads, nb), in_specs=specs + [state_spec, dy_spec],
        out_specs=[out_col, out_col, out_col, out_col, pl.BlockSpec((1, HEAD), lambda h, j: (0, h))],
        out_shape=[o_shape, o_shape, o_shape, o_shape, jax.ShapeDtypeStruct((1, n_heads * HEAD), F32)],
        scratch_shapes=[pltpu.VMEM((HEAD, HEAD), F32)],
        name=name, compiler_params=_cparams(("parallel", "arbitrary")),
    )(proj, proj, proj, proj, gn_w, cosf, sinf, *consts, states, dy)


CONV_PAD = 8


def _conv_pre(u_ext, taps, bias, n_out):
    n = u_ext.shape[0]
    views = [pltpu.roll(u_ext, n - (k + CONV_PAD - SSD_K + 1), 0)[:n_out] for k in range(SSD_K)]
    pre = bias
    for k in range(SSD_K):
        pre = pre + taps[k] * views[k]
    return pre, views


def ssd_conv_fwd(u_pad, taps, bias, *, chunk, name):
    s, c = u_pad.shape[0] - 2 * CONV_PAD, u_pad.shape[1]

    def body(u_ref, t0, t1, t2, t3, b_ref, o_ref):
        taps_v = [t[...] for t in (t0, t1, t2, t3)]
        bias_v = b_ref[...]

        @pl.loop(0, s // chunk)
        def _(ci):
            r0 = pl.multiple_of(ci * chunk, chunk)
            pre, _ = _conv_pre(u_ref[pl.ds(r0, chunk + CONV_PAD), :], taps_v, bias_v, chunk)
            o_ref[pl.ds(r0, chunk), :] = pre * jax.nn.sigmoid(pre)

    row = pl.BlockSpec((1, LANE), lambda i: (0, i))
    return pl.pallas_call(
        body, grid=(c // LANE,),
        in_specs=[pl.BlockSpec((s + 2 * CONV_PAD, LANE), lambda i: (0, i))] + [row] * 5,
        out_specs=pl.BlockSpec((s, LANE), lambda i: (0, i)),
        out_shape=jax.ShapeDtypeStruct((s, c), F32),
        name=name, compiler_params=_cparams(("parallel",)),
    )(u_pad, *taps, bias)


def ssd_conv_bwd(u_pad, taps, bias, dxc_pad, *, chunk, name):
    s, c = u_pad.shape[0] - 2 * CONV_PAD, u_pad.shape[1]
    ext = chunk + CONV_PAD

    def body(u_ref, t0, t1, t2, t3, b_ref, d_ref, du_ref, dw_ref, db_ref):
        taps_v = [t[...] for t in (t0, t1, t2, t3)]
        bias_v = b_ref[...]
        dw_ref[...] = jnp.zeros_like(dw_ref)
        db_ref[...] = jnp.zeros_like(db_ref)

        @pl.loop(0, s // chunk)
        def _(ci):
            r0 = pl.multiple_of(ci * chunk, chunk)
            pre, views = _conv_pre(u_ref[pl.ds(r0, ext + CONV_PAD), :], taps_v, bias_v, ext)
            sig = jax.nn.sigmoid(pre)
            dpre = d_ref[pl.ds(r0, ext), :] * (sig * (1.0 + pre * (1.0 - sig)))
            du = taps_v[SSD_K - 1] * dpre[:chunk]
            for k in range(SSD_K - 1):
                du = du + taps_v[k] * pltpu.roll(dpre, ext - (SSD_K - 1 - k), 0)[:chunk]
            du_ref[pl.ds(r0, chunk), :] = du.astype(du_ref.dtype)
            own = dpre[:chunk]
            for k in range(SSD_K):
                dw_ref[k:k + 1, :] += jnp.sum(own * views[k][:chunk], axis=0, keepdims=True)
            db_ref[...] += jnp.sum(own, axis=0, keepdims=True)

    row = pl.BlockSpec((1, LANE), lambda i: (0, i))
    return pl.pallas_call(
        body, grid=(c // LANE,),
        in_specs=[pl.BlockSpec((s + 2 * CONV_PAD, LANE), lambda i: (0, i))] + [row] * 5
        + [pl.BlockSpec((s + CONV_PAD, LANE), lambda i: (0, i))],
        out_specs=[pl.BlockSpec((s, LANE), lambda i: (0, i)), pl.BlockSpec((SSD_K, LANE), lambda i: (0, i)), row],
        out_shape=[jax.ShapeDtypeStruct((s, c), BF16), jax.ShapeDtypeStruct((SSD_K, c), F32),
                   jax.ShapeDtypeStruct((1, c), F32)],
        name=name, compiler_params=_cparams(("parallel",)),
    )(u_pad, *taps, bias, dxc_pad)


def _tri_dot(tri, x, passes=3):
    out = None
    rem = x
    for _ in range(passes):
        piece = rem.astype(BF16)
        rem = rem - piece.astype(F32)
        d = _dot(tri, piece)
        out = d if out is None else out + d
    return out


def _tri(n, upper):
    rr = lax.broadcasted_iota(jnp.int32, (n, n), 0)
    cc = lax.broadcasted_iota(jnp.int32, (n, n), 1)
    return ((rr <= cc) if upper else (rr >= cc)).astype(BF16)


@jax.custom_vjp
def _cumsum_rows(a):
    return _tri_dot(_tri(a.shape[0], False), a)


_cumsum_rows.defvjp(lambda a: (_cumsum_rows(a), None), lambda _, g: (_tri_dot(_tri(g.shape[0], True), g),))


def _softplus(x):
    return jnp.maximum(x, 0.0) + jnp.log(1.0 + jnp.exp(-jnp.abs(x)))


def _ssd_block(x, bm, cm, dtraw, dtb, alog, dsk, state_t, group):
    blk, width = x.shape
    e_heads = width // SSD_P
    dt = _softplus(dtraw + dtb)
    acum = _cumsum_rows(dt * (-jnp.exp(alog)))
    acum_t = acum.T
    lane_h = lax.broadcasted_iota(jnp.int32, (1, LANE), 1)
    sub_h = lax.broadcasted_iota(jnp.int32, (LANE, 1), 0)
    lane_e = lax.broadcasted_iota(jnp.int32, (1, width), 1) // SSD_P
    causal = lax.broadcasted_iota(jnp.int32, (blk, blk), 0) >= lax.broadcasted_iota(jnp.int32, (blk, blk), 1)
    last_row = lax.broadcasted_iota(jnp.int32, (blk, 1), 0) == blk - 1
    cb = _dot(cm.astype(BF16), bm.astype(BF16), _NT)
    y = jnp.zeros((blk, width), F32)
    dt_l = jnp.zeros((blk, width), F32)
    ac_l = jnp.zeros((blk, width), F32)
    d_l = jnp.zeros((1, width), F32)
    for e in range(e_heads):
        head = group * e_heads + e
        pick = lane_h == head
        col = jnp.sum(jnp.where(pick, acum, 0.0), axis=1, keepdims=True)
        dt_e = jnp.sum(jnp.where(pick, dt, 0.0), axis=1, keepdims=True)
        d_e = jnp.sum(jnp.where(pick, dsk, 0.0), axis=1, keepdims=True)
        row = jnp.sum(jnp.where(sub_h == head, acum_t, 0.0), axis=0, keepdims=True)
        mine = lane_e == e
        decay = jnp.exp(jnp.where(causal, col - row, -jnp.inf))
        y = y + _dot((cb * decay).astype(BF16), jnp.where(mine, x * dt_e, 0.0).astype(BF16))
        dt_l = dt_l + jnp.where(mine, dt_e, 0.0)
        ac_l = ac_l + jnp.where(mine, col, 0.0)
        d_l = d_l + jnp.where(mine, d_e, 0.0)
    ac_last = jnp.sum(jnp.where(last_row, ac_l, 0.0), axis=0, keepdims=True)
    y = y + jnp.exp(ac_l) * _dot(cm.astype(BF16), state_t.astype(BF16)) + x * d_l
    inject = (x * dt_l * jnp.exp(ac_last - ac_l)).astype(BF16)
    new_state = state_t * jnp.exp(ac_last) + _dot(bm.astype(BF16), inject, _TN)
    return y, new_state


def ssd_scan_fwd(xc, proj, dt_col, dtb, alog, dsk, *, blk, name):
    s = xc.shape[0]
    nb = s // blk
    e_w = 4 * SSD_P
    b_off = SSD_G * e_w // SSD_N
    c_off = b_off + SSD_G
    row = pl.BlockSpec((1, LANE), lambda j, g: (0, 0))

    def body(x_ref, b_ref, c_ref, dt_ref, dtb_ref, al_ref, d_ref, y_ref, st_ref, state):
        j, g = pl.program_id(0), pl.program_id(1)

        @pl.when(j == 0)
        def _():
            state[g] = jnp.zeros((SSD_N, e_w), F32)

        st = state[g]
        st_ref[...] = st
        y, new_state = _ssd_block(x_ref[...], b_ref[...], c_ref[...], dt_ref[...], dtb_ref[...], al_ref[...],
                                  d_ref[...], st, g)
        y_ref[...] = y
        state[g] = new_state

    return pl.pallas_call(
        body, grid=(nb, SSD_G),
        in_specs=[pl.BlockSpec((blk, e_w), lambda j, g: (j, g)),
                  pl.BlockSpec((blk, SSD_N), lambda j, g: (j, b_off + g)),
                  pl.BlockSpec((blk, SSD_N), lambda j, g: (j, c_off + g)),
                  pl.BlockSpec((blk, LANE), lambda j, g: (j, dt_col)), row, row, row],
        out_specs=[pl.BlockSpec((blk, e_w), lambda j, g: (j, g)),
                   pl.BlockSpec((None, None, SSD_N, e_w), lambda j, g: (j, g, 0, 0))],
        out_shape=[jax.ShapeDtypeStruct((s, SSD_G * e_w), F32),
                   jax.ShapeDtypeStruct((nb, SSD_G, SSD_N, e_w), F32)],
        scratch_shapes=[pltpu.VMEM((SSD_G, SSD_N, e_w), F32)],
        name=name, compiler_params=_cparams(("arbitrary", "arbitrary")),
    )(xc, xc, xc, proj, dtb, alog, dsk)


def ssd_scan_bwd(xc, proj, dt_col, dtb, alog, dsk, states, dy, *, blk, name):
    s = xc.shape[0]
    nb = s // blk
    e_w = 4 * SSD_P
    b_off = SSD_G * e_w // SSD_N
    c_off = b_off + SSD_G
    row = pl.BlockSpec((1, LANE), lambda j, g: (0, 0))
    jm = lambda j: nb - 1 - j

    def body(x_ref, b_ref, c_ref, dt_ref, dtb_ref, al_ref, d_ref, st_ref, dy_ref,
             dx_ref, db_ref, dc_ref, ddt_ref, ddtb_ref, dal_ref, dd_ref, dstate):
        j, g = pl.program_id(0), pl.program_id(1)

        @pl.when(j == 0)
        def _():
            dstate[g] = jnp.zeros((SSD_N, e_w), F32)

        @pl.when(jnp.logical_and(j == 0, g == 0))
        def _():
            ddtb_ref[...] = jnp.zeros_like(ddtb_ref)
            dal_ref[...] = jnp.zeros_like(dal_ref)
            dd_ref[...] = jnp.zeros_like(dd_ref)

        @pl.when(g == 0)
        def _():
            ddt_ref[...] = jnp.zeros_like(ddt_ref)

        fn = functools.partial(_ssd_block, group=g)
        _, f_vjp = jax.vjp(fn, x_ref[...], b_ref[...], c_ref[...], dt_ref[...], dtb_ref[...], al_ref[...],
                           d_ref[...], st_ref[...])
        dx, db, dc, ddt, ddtb, dal, dd, dst = f_vjp((dy_ref[...], dstate[g]))
        dx_ref[...] = dx
        db_ref[...] = db
        dc_ref[...] = dc
        ddt_ref[...] += ddt
        ddtb_ref[...] += ddtb
        dal_ref[...] += dal
        dd_ref[...] += dd
        dstate[g] = dst

    return pl.pallas_call(
        body, grid=(nb, SSD_G),
        in_specs=[pl.BlockSpec((blk, e_w), lambda j, g: (jm(j), g)),
                  pl.BlockSpec((blk, SSD_N), lambda j, g: (jm(j), b_off + g)),
                  pl.BlockSpec((blk, SSD_N), lambda j, g: (jm(j), c_off + g)),
                  pl.BlockSpec((blk, LANE), lambda j, g: (jm(j), dt_col)), row, row, row,
                  pl.BlockSpec((None, None, SSD_N, e_w), lambda j, g: (jm(j), g, 0, 0)),
                  pl.BlockSpec((blk, e_w), lambda j, g: (jm(j), g))],
        out_specs=[pl.BlockSpec((blk, e_w), lambda j, g: (jm(j), g)),
                   pl.BlockSpec((blk, SSD_N), lambda j, g: (jm(j), g)),
                   pl.BlockSpec((blk, SSD_N), lambda j, g: (jm(j), g)),
                   pl.BlockSpec((blk, LANE), lambda j, g: (jm(j), 0)), row, row, row],
        out_shape=[jax.ShapeDtypeStruct((s, SSD_G * e_w), F32),
                   jax.ShapeDtypeStruct((s, SSD_G * SSD_N), F32),
                   jax.ShapeDtypeStruct((s, SSD_G * SSD_N), F32),
                   jax.ShapeDtypeStruct((s, LANE), F32)] + [jax.ShapeDtypeStruct((1, LANE), F32)] * 3,
        scratch_shapes=[pltpu.VMEM((SSD_G, SSD_N, e_w), F32)],
        name=name, compiler_params=_cparams(("arbitrary", "arbitrary")),
    )(xc, xc, xc, proj, dtb, alog, dsk, states, dy)


SB_DEAD = -104.0

_NT = (((1,), (1,)), ((), ()))
_TN = (((0,), (0,)), ((), ()))


def _dot(a, b, dims=(((1,), (0,)), ((), ()))):
    return lax.dot_general(a, b, dims, preferred_element_type=F32)


def _split_dot(x, tri, passes):
    out = None
    rem = x
    for _ in range(passes):
        piece = rem.astype(BF16)
        rem = rem - piece.astype(F32)
        d = _dot(piece, tri)
        out = d if out is None else out + d
    return out


def _sb_scores(q, k_ref, j, blk, row, scale):
    kb = k_ref[pl.ds(pl.multiple_of(j * blk, blk), blk), :].astype(BF16)
    z = _dot(q, kb, _NT) * scale
    col = j * blk + lax.broadcasted_iota(jnp.int32, (blk, blk), 1)
    mask = col < row
    sp = jnp.maximum(z, 0.0) + jnp.log(1.0 + jnp.exp(-jnp.abs(z)))
    lk = jnp.where(mask, -sp, 0.0)
    return kb, mask, lk, z - sp


def sb_attention_fwd(proj, q_col, k_col, v_col, n_heads, *, blk, name):
    s = proj.shape[0]
    nq = s // blk
    scale = HEAD ** -0.5

    def body(q_ref, k_ref, v_ref, o_ref, r_ref, n_ref):
        i = pl.program_id(1)
        q = q_ref[...].astype(BF16)
        row = i * blk + lax.broadcasted_iota(jnp.int32, (blk, blk), 0)
        rr = lax.broadcasted_iota(jnp.int32, (blk, blk), 0)
        cc = lax.broadcasted_iota(jnp.int32, (blk, blk), 1)
        tri_after = (rr > cc).astype(BF16)

        def alive(carry):
            jj, _, run = carry
            return jnp.logical_and(jj <= i, jnp.max(run) > SB_DEAD)

        def step(carry):
            jj, acc, run = carry
            j = i - jj
            _, mask, lk, ls = _sb_scores(q, k_ref, j, blk, row, scale)
            later = _split_dot(lk, tri_after, 2) + run
            w = jnp.where(mask, jnp.exp(ls + later), 0.0)
            vb = v_ref[pl.ds(pl.multiple_of(j * blk, blk), blk), :].astype(BF16)
            return jj + 1, acc + _dot(w.astype(BF16), vb), run + jnp.sum(lk, axis=1, keepdims=True)

        n, acc, run = lax.while_loop(
            alive, step, (jnp.int32(0), jnp.zeros((blk, HEAD), F32), jnp.zeros((blk, 1), F32)))
        o_ref[...] = acc.astype(o_ref.dtype)
        r_ref[...] = jnp.broadcast_to(run, (blk, HEAD))
        n_ref[pl.program_id(0), i] = n

    blk_spec = lambda c0: pl.BlockSpec((blk, HEAD), lambda h, i: (i, c0 + h))
    full_spec = lambda c0: pl.BlockSpec((s, HEAD), lambda h, i: (0, c0 + h))
    out_spec = pl.BlockSpec((blk, HEAD), lambda h, i: (i, h))
    return pl.pallas_call(
        body, grid=(n_heads, nq),
        in_specs=[blk_spec(q_col), full_spec(k_col), full_spec(v_col)],
        out_specs=[out_spec, out_spec, pl.BlockSpec(memory_space=pltpu.SMEM)],
        out_shape=[jax.ShapeDtypeStruct((s, n_heads * HEAD), BF16),
                   jax.ShapeDtypeStruct((s, n_heads * HEAD), F32),
                   jax.ShapeDtypeStruct((n_heads, nq), jnp.int32)],
        name=name, compiler_params=_cparams(("arbitrary", "arbitrary")),
    )(proj, proj, proj)


def sb_attention_bwd(proj, d_out, run_tot, visited, q_col, k_col, v_col, n_heads, *, blk, name):
    s = proj.shape[0]
    nq = s // blk
    scale = HEAD ** -0.5

    def body(n_ref, q_ref, k_ref, v_ref, do_ref, r_ref, dq_ref, dk_ref, dv_ref, dk_acc, dv_acc):
        i = pl.program_id(1)
        first = i + 1 - jnp.clip(n_ref[pl.program_id(0), i], 1, i + 1)

        @pl.when(i == 0)
        def _():
            dk_acc[...] = jnp.zeros_like(dk_acc)
            dv_acc[...] = jnp.zeros_like(dv_acc)

        q = q_ref[...].astype(BF16)
        do = do_ref[...].astype(BF16)
        rtot = r_ref[:, :1]
        row = i * blk + lax.broadcasted_iota(jnp.int32, (blk, blk), 0)
        rr = lax.broadcasted_iota(jnp.int32, (blk, blk), 0)
        cc = lax.broadcasted_iota(jnp.int32, (blk, blk), 1)
        tri_upto = (rr <= cc).astype(BF16)
        tri_before = (rr < cc).astype(BF16)

        def step(j, carry):
            dq, pre, gpre = carry
            kb, mask, lk, ls = _sb_scores(q, k_ref, j, blk, row, scale)
            rows = pl.ds(pl.multiple_of(j * blk, blk), blk)
            vb = v_ref[rows, :].astype(BF16)
            later = rtot - (pre + _split_dot(lk, tri_upto, 3))
            w = jnp.where(mask, jnp.exp(ls + later), 0.0)
            g = w * _dot(do, vb, _NT)
            g_before = _split_dot(g, tri_before, 2) + gpre
            sig = jnp.exp(ls)
            dz = (jnp.where(mask, g * (1.0 - sig) - sig * g_before, 0.0) * scale).astype(BF16)
            dk_acc[rows, :] += _dot(dz, q, _TN)
            dv_acc[rows, :] += _dot(w.astype(BF16), do, _TN)
            return (dq + _dot(dz, kb), pre + jnp.sum(lk, axis=1, keepdims=True),
                    gpre + jnp.sum(g, axis=1, keepdims=True))

        zero = jnp.zeros((blk, 1), F32)
        dq, _, _ = lax.fori_loop(first, i + 1, step, (jnp.zeros((blk, HEAD), F32), zero, zero))
        dq_ref[...] = dq.astype(dq_ref.dtype)

        @pl.when(i == nq - 1)
        def _():
            dk_ref[...] = dk_acc[...].astype(dk_ref.dtype)
            dv_ref[...] = dv_acc[...].astype(dv_ref.dtype)

    blk_spec = lambda c0: pl.BlockSpec((blk, HEAD), lambda h, i: (i, c0 + h))
    full_spec = lambda c0: pl.BlockSpec((s, HEAD), lambda h, i: (0, c0 + h))
    o_shape = jax.ShapeDtypeStruct((s, n_heads * HEAD), BF16)
    do_spec = blk_spec(0)
    if isinstance(d_out, tuple):
        d_out, lead = d_out
        do_spec = pl.BlockSpec((None, blk, HEAD), lambda h, i: (lead, i, h))
    return pl.pallas_call(
        body, grid=(n_heads, nq),
        in_specs=[pl.BlockSpec(memory_space=pltpu.SMEM), blk_spec(q_col), full_spec(k_col), full_spec(v_col),
                  do_spec, blk_spec(0)],
        out_specs=[blk_spec(0), full_spec(0), full_spec(0)],
        out_shape=[o_shape, o_shape, o_shape],
        scratch_shapes=[pltpu.VMEM((s, HEAD), F32), pltpu.VMEM((s, HEAD), F32)],
        name=name, compiler_params=_cparams(("parallel", "arbitrary")),
    )(visited, proj, proj, proj, d_out, run_tot)


def _hosting(hosted):
    arrs = [arr for _, arr in hosted]
    shapes = [_comm_out_shape(kind, arr) for kind, arr in hosted]

    def ops(src_refs, dst_refs, sems):
        return [_COMM[kind](src_refs[c], dst_refs[c], *sems[3 * c:3 * c + 3]) for c, (kind, _) in enumerate(hosted)]

    return arrs, [_HBM] * len(hosted), shapes, _COMM_SEMS * len(hosted), ops


def ffn_up_swiglu(h, w_gu, *, tm, hosted=(), name):
    s, d = h.shape
    tm = _pick(s, tm)
    nb, _, hb, _ = w_gu.shape
    n_host = len(hosted)
    h_arrs, h_specs, h_shapes, h_sems, h_ops = _hosting(hosted)
    grid = (s // tm, nb)

    def body(h_ref, w_ref, *rest):
        src_refs, (gu_ref, act_ref) = rest[:n_host], rest[n_host:n_host + 2]
        dst_refs, sems = rest[n_host + 2:2 * n_host + 2], rest[2 * n_host + 2:]
        comms = h_ops(src_refs, dst_refs, sems)
        i, g = pl.program_id(0), pl.program_id(1)

        if hosted:
            @pl.when(jnp.logical_and(i == 0, g == 0))
            def _():
                for start, _ in comms:
                    start()

        a = h_ref[...]
        gate = _dot(a, w_ref[0], _NT)
        up = _dot(a, w_ref[1], _NT)
        gu_ref[0] = gate
        gu_ref[1] = up
        act_ref[...] = (gate * jax.nn.sigmoid(gate) * up).astype(act_ref.dtype)

        if hosted:
            @pl.when(jnp.logical_and(i == grid[0] - 1, g == grid[1] - 1))
            def _():
                for _, finish in comms:
                    finish()

    res = pl.pallas_call(
        body, grid=grid,
        in_specs=[pl.BlockSpec((tm, d), lambda i, g: (i, 0)),
                  pl.BlockSpec((None, 2, hb, d), lambda i, g: (g, 0, 0, 0))] + h_specs,
        out_specs=[pl.BlockSpec((None, 2, tm, hb), lambda i, g: (g, 0, i, 0)),
                   pl.BlockSpec((None, tm, hb), lambda i, g: (g, i, 0))] + h_specs,
        out_shape=[jax.ShapeDtypeStruct((nb, 2, s, hb), F32), jax.ShapeDtypeStruct((nb, s, hb), BF16)] + h_shapes,
        scratch_shapes=h_sems, name=name,
        compiler_params=_cparams(("arbitrary", "arbitrary") if hosted else ("parallel", "parallel")),
    )(h, w_gu, *h_arrs)
    return res[0], res[1], res[2:]


def ffn_down_dx_swiglu(df, w_down, gu, *, tm, name):
    s, d = df.shape
    tm = _pick(s, tm)
    nb, hb, _ = w_down.shape

    def body(df_ref, w_ref, gu_ref, dgu_ref):
        dact = _dot(df_ref[...], w_ref[...], _NT)
        gate, up = gu_ref[0], gu_ref[1]
        sig = jax.nn.sigmoid(gate)
        dgu_ref[0] = (dact * up * (sig * (1.0 + gate * (1.0 - sig)))).astype(dgu_ref.dtype)
        dgu_ref[1] = (dact * gate * sig).astype(dgu_ref.dtype)

    blk = pl.BlockSpec((None, 2, tm, hb), lambda i, g: (g, 0, i, 0))
    return pl.pallas_call(
        body, grid=(s // tm, nb),
        in_specs=[pl.BlockSpec((tm, d), lambda i, g: (i, 0)), pl.BlockSpec((None, hb, d), lambda i, g: (g, 0, 0)), blk],
        out_specs=blk, out_shape=jax.ShapeDtypeStruct(gu.shape, BF16),
        name=name, compiler_params=_cparams(("parallel", "parallel")),
    )(df, w_down, gu)


def loss_head(y, target, *, tr, name):
    s, d = y.shape

    def body(y_ref, t_ref, l_ref, dy_ref):
        err = y_ref[...] - t_ref[...]
        dy_ref[...] = err * (1.0 / d)
        part = 0.5 * jnp.sum(jnp.mean(err * err, axis=-1, keepdims=True), axis=0, keepdims=True)

        @pl.when(pl.program_id(0) == 0)
        def _():
            l_ref[...] = jnp.zeros_like(l_ref)

        l_ref[...] += jnp.broadcast_to(part, l_ref.shape)

    row = pl.BlockSpec((tr, d), lambda i: (i, 0))
    return pl.pallas_call(
        body, grid=(s // tr,), in_specs=[row, row],
        out_specs=[pl.BlockSpec((8, LANE), lambda i: (0, 0)), row],
        out_shape=[jax.ShapeDtypeStruct((8, LANE), F32), jax.ShapeDtypeStruct((s, d), F32)],
        name=name, compiler_params=_cparams(("arbitrary",)),
    )(y, target)


def _adamw_math(w, g, m, v):
    m = ADAM_B1 * m + (1.0 - ADAM_B1) * g
    v = ADAM_B2 * v + (1.0 - ADAM_B2) * jnp.square(g)
    m_hat = m / (1.0 - ADAM_B1 ** ADAM_STEP)
    v_hat = v / (1.0 - ADAM_B2 ** ADAM_STEP)
    delta = -ADAM_LR * (m_hat / (jnp.sqrt(v_hat) + ADAM_EPS) + ADAM_WD * w)
    return delta, m, v


def adamw_sum(w, m, v, parts, *, tr, name):
    n, r, c = parts.shape
    tr = _pick_rows(r, tr)

    def body(w_ref, m_ref, v_ref, p_ref, g_ref, d_ref, nm_ref, nv_ref):
        g = p_ref[0].astype(F32)
        for i in range(1, n):
            g = g + p_ref[i].astype(F32)
        delta, nm, nv = _adamw_math(w_ref[...], g, m_ref[...], v_ref[...])
        g_ref[...] = g
        d_ref[...] = delta
        nm_ref[...] = nm
        nv_ref[...] = nv

    row = pl.BlockSpec((tr, c), lambda i: (i, 0))
    shape = jax.ShapeDtypeStruct((r, c), F32)
    return pl.pallas_call(
        body, grid=(r // tr,),
        in_specs=[row, row, row, pl.BlockSpec((n, tr, c), lambda i: (0, i, 0))],
        out_specs=[row] * 4, out_shape=[shape] * 4,
        name=name, compiler_params=_cparams(("parallel",)),
    )(w, m, v, parts)


def adamw_into(w_all, m_all, v_all, parts, layer, row_off, prev, *, tr, name):
    depth, r, c = w_all.shape
    n = parts.shape[0]
    if r % 16 == 0:
        tr = _pick_rows(r, tr)
        assert row_off % tr == 0
        off = row_off // tr
        grid = (r // tr,)
        lay = pl.BlockSpec((None, tr, c), lambda i: (layer, i, 0))
        p_spec = pl.BlockSpec((n, tr, c), lambda i: (0, off + i, 0))
    else:
        assert row_off == 0 and parts.shape[1] == r
        grid = (c // LANE,)
        lay = pl.BlockSpec((None, r, LANE), lambda j: (layer, 0, j))
        p_spec = pl.BlockSpec((n, r, LANE), lambda j: (0, 0, j))
    if prev is None:
        prev = [lax.empty(w_all.shape, F32) for _ in range(4)]

    def body(w_ref, m_ref, v_ref, p_ref, _g, _d, _m, _v, g_ref, d_ref, nm_ref, nv_ref):
        g = p_ref[0].astype(F32)
        for i in range(1, n):
            g = g + p_ref[i].astype(F32)
        delta, nm, nv = _adamw_math(w_ref[...], g, m_ref[...], v_ref[...])
        g_ref[...] = g
        d_ref[...] = delta
        nm_ref[...] = nm
        nv_ref[...] = nv

    untouched = pl.BlockSpec(memory_space=pl.ANY)
    return pl.pallas_call(
        body, grid=grid,
        in_specs=[lay, lay, lay, p_spec] + [untouched] * 4,
        out_specs=[lay] * 4, out_shape=[jax.ShapeDtypeStruct(w_all.shape, F32)] * 4,
        input_output_aliases={4: 0, 5: 1, 6: 2, 7: 3},
        name=name, compiler_params=_cparams(("parallel",)),
    )(w_all, m_all, v_all, parts, *prev)


def _pick_rows(r, pref):
    t = min(pref, r)
    while r % t or (t % 16 and t != r):
        t -= 1
    return t


_HBM = pl.BlockSpec(memory_space=pltpu.HBM)
_MESH = pl.DeviceIdType.MESH


def _flat_index(px, py, pc):
    return 4 * px + 2 * py + pc


def _gather_ops(x_ref, out_ref, send_sems, recv_sems, local_sem):
    x, y, c = lax.axis_index("x"), lax.axis_index("y"), lax.axis_index("c")
    me, sibling = (x, y, c), (x, y, 1 - c)
    chips = [(1 - x, y), (x, 1 - y), (1 - x, 1 - y)]

    def slot(p):
        return out_ref.at[_flat_index(*p)]

    def copy(k, block, to, src=None):
        return pltpu.make_async_remote_copy(
            src_ref=slot(block) if src is None else src, dst_ref=slot(block),
            send_sem=send_sems.at[k], recv_sem=recv_sems.at[k], device_id=to, device_id_type=_MESH)

    mine = pltpu.make_async_copy(x_ref, slot(me), local_sem)
    first = [copy(0, me, sibling, src=x_ref)]
    first += [copy(1 + j, me, (*chip, c), src=x_ref) for j, chip in enumerate(chips)]
    passed = [copy(4 + j, (*chip, c), sibling) for j, chip in enumerate(chips)]

    def start():
        mine.start()
        for cp in first:
            cp.start()

    def finish():
        for j, chip in enumerate(chips):
            copy(1 + j, (*chip, c), me).wait_recv()
            passed[j].start()
        copy(0, sibling, me).wait_recv()
        for j, chip in enumerate(chips):
            copy(4 + j, (*chip, 1 - c), me).wait_recv()
        for cp in first + passed:
            cp.wait_send()
        mine.wait()

    return start, finish


def _exchange_ops(p_ref, out_ref, send_sems, recv_sems, local_sem):
    x, y, c = lax.axis_index("x"), lax.axis_index("y"), lax.axis_index("c")
    me = _flat_index(x, y, c)
    peers = [(1 - x if k & 4 else x, 1 - y if k & 2 else y, 1 - c if k & 1 else c) for k in range(1, N_DEV)]

    def copy(k, peer, dst_slot):
        return pltpu.make_async_remote_copy(
            src_ref=p_ref.at[_flat_index(*peer)], dst_ref=out_ref.at[dst_slot],
            send_sem=send_sems.at[k], recv_sem=recv_sems.at[k], device_id=peer, device_id_type=_MESH)

    mine = pltpu.make_async_copy(p_ref.at[me], out_ref.at[me], local_sem)
    sends = [copy(k, peer, me) for k, peer in enumerate(peers)]

    def start():
        mine.start()
        for cp in sends:
            cp.start()

    def finish():
        for k, peer in enumerate(peers):
            copy(k, peer, _flat_index(*peer)).wait_recv()
        for cp in sends:
            cp.wait_send()
        mine.wait()

    return start, finish


_COMM = {"gather": _gather_ops, "exchange": _exchange_ops}
_COMM_SEMS = [pltpu.SemaphoreType.DMA((N_DEV - 1,)), pltpu.SemaphoreType.DMA((N_DEV - 1,)), pltpu.SemaphoreType.DMA]


def _comm_out_shape(kind, arr):
    return jax.ShapeDtypeStruct(((N_DEV,) + arr.shape) if kind == "gather" else arr.shape, arr.dtype)


def _comm_call(kind, arr, name):
    def body(src_ref, dst_ref, send_sems, recv_sems, local_sem):
        start, finish = _COMM[kind](src_ref, dst_ref, send_sems, recv_sems, local_sem)
        start()
        finish()

    return pl.pallas_call(body, out_shape=_comm_out_shape(kind, arr), in_specs=[_HBM], out_specs=_HBM,
                          scratch_shapes=_COMM_SEMS, name=name)(arr)


def all_gather(x, *, name):
    return _comm_call("gather", x, name)


def exchange(parts, *, name):
    return _comm_call("exchange", parts, name)


TR = 256
TR_WIDE = 128
BLK = 256
SB_BLK = 256
FFN_TM = 1024
CONV_CHUNK = 512
DT_PAD = LANE

BIG = ("w_in", "w_branch", "w_out", "ffn_w_gu", "ffn_w_down")
SMALL = ("norm_mix_pre", "norm_mix_post", "norm_ffn_pre", "norm_ffn_post", "b_gate", "ret_gn_w", "ssd_conv_w",
         "ssd_conv_b", "ssd_dt_bias", "ssd_a_log", "ssd_d", "ssd_norm_w")


def _row(v):
    return v.reshape(1, -1)


def _pad_lanes(v):
    return jnp.pad(v.reshape(1, -1), ((0, 0), (0, LANE - v.shape[-1])))


def _assemble_w_in(g, d):
    full = g.reshape(-1, d)
    n_main, n_dt = 5 * d, full.shape[0] - 8 * d
    main, dt, gates = full[:n_main], full[n_main:n_main + n_dt], full[n_main + n_dt:]
    return jnp.concatenate([main, gates], axis=0), jnp.pad(dt, ((0, DT_PAD - n_dt), (0, 0)))


def _split_dw_in(dw, dw_dt, d, n_dt):
    n_main = 5 * d
    full = jnp.concatenate([dw[:n_main], dw_dt[:n_dt], dw[n_main:]], axis=0)
    return full.reshape(N_DEV, -1, d)


def _layer_fwd(x, lw, cosf, sinf, next_shards):
    s, d = x.shape
    heads = d // 2 // HEAD
    nxt = {}
    (h,) = rowwise(f_rms_pre, [x], [lw["norm_mix_pre"]], [(d, BF16)], tr=TR, name="mix_pre_norm")
    if next_shards is None:
        proj = matmul(h, lw["w_cat"], tb=True, tn=2048, name="in_proj")
    else:
        proj, (nxt["w_in"],) = matmul(h, lw["w_cat"], tb=True, tn=2048, hosted=[("gather", next_shards["w_in"])],
                                      name="in_proj_gather")
    dt_raw = matmul(h, lw["w_dt"], tb=True, name="in_proj_dt")
    ret_cols = tuple(i * heads for i in range(4))
    y_ret, ret_states = retention_fwd(proj, ret_cols, lw["ret_gn_w"], cosf, sinf, heads, blk=BLK, name="retention_fwd")
    y_sb, sb_run, sb_visited = sb_attention_fwd(proj, 4 * heads, 5 * heads, 6 * heads, heads, blk=SB_BLK,
                                                name="stickbreak_fwd")
    u_pad = jnp.pad(proj[:, 4 * d:5 * d], ((CONV_PAD, CONV_PAD), (0, 0)))
    xc = ssd_conv_fwd(u_pad, lw["conv_taps"], lw["ssd_conv_b"], chunk=CONV_CHUNK, name="ssd_conv_fwd")
    y_scan, ssd_states = ssd_scan_fwd(xc, dt_raw, 0, lw["ssd_dt_bias"], lw["ssd_a_log"], lw["ssd_d"], blk=BLK,
                                      name="ssd_scan_fwd")
    z_spec = (proj, d // 2, 7)
    (y_ssd,) = rowwise(f_ssd_gate, [y_scan, z_spec], [lw["ssd_norm_w"]], [(d // 2, BF16)], tr=TR, name="ssd_gate_norm")
    y3 = jnp.stack([y_ret, y_sb, y_ssd])
    u3 = matmul(y3, lw["w_branch"], lead_a="batch", lead_b="batch", name="branch_proj")
    merge_rows = [(u3, d, 0, i) for i in range(3)] + [(proj, d, 5 + i) for i in range(3)]
    (merged,) = rowwise(f_merge, merge_rows, lw["b_gate"], [(d, BF16)], tr=TR_WIDE, name="gate_merge")
    o = matmul(merged, lw["w_out"], name="out_proj")
    (x1,) = rowwise(f_rms_post, [x, o], [lw["norm_mix_post"]], [(d, F32)], tr=TR, name="mix_post_norm")
    (h2,) = rowwise(f_rms_pre, [x1], [lw["norm_ffn_pre"]], [(d, BF16)], tr=TR, name="ffn_pre_norm")
    w_gu = lw["ffn_w_gu"].reshape(N_DEV, 2, -1, d)
    if next_shards is None:
        gu, act, _ = ffn_up_swiglu(h2, w_gu, tm=FFN_TM, name="ffn_up_swiglu")
    else:
        rest = [n for n in BIG if n != "w_in"]
        gu, act, got = ffn_up_swiglu(h2, w_gu, tm=FFN_TM, hosted=[("gather", next_shards[n]) for n in rest],
                                     name="ffn_up_swiglu_gather")
        nxt.update(zip(rest, got))
    f = matmul(act, lw["ffn_w_down"], lead_a="k", lead_b="k", name="ffn_down")
    (x2,) = rowwise(f_rms_post, [x1, f], [lw["norm_ffn_post"]], [(d, F32)], tr=TR, name="ffn_post_norm")
    res = dict(x=x, h=h, proj=proj, dt_raw=dt_raw, ret_states=ret_states, sb_run=sb_run, sb_visited=sb_visited, xc=xc,
               u_pad=u_pad,
               y_scan=y_scan, ssd_states=ssd_states, y3=y3, u3=u3, merged=merged, o=o, x1=x1, h2=h2, gu=gu, act=act, f=f)
    return x2, res, (nxt if next_shards is not None else None)


def _layer_bwd(dx2, res, lw, cosf, sinf, pending):
    x, proj = res["x"], res["proj"]
    s, d = x.shape
    heads = d // 2 // HEAD
    n_dt = d // 2 // SSD_P
    got = {}
    df, dn_ffn_post = rowwise_vjp(f_rms_post, [res["x1"], res["f"]], [lw["norm_ffn_post"]], [dx2],
                                  [(1, BF16, None)], [0], tr=TR, name="ffn_post_norm_bwd")
    dw_down = matmul(res["act"], df, ta=True, lead_a="batch", out_dtype=BF16, name="ffn_down_dw")
    dgu = ffn_down_dx_swiglu(df, lw["ffn_w_down"], res["gu"], tm=FFN_TM, name="ffn_down_dx_swiglu")
    dgu = dgu.reshape(2 * N_DEV, s, -1)
    if pending is None:
        dh2 = matmul(dgu, lw["ffn_w_gu"], lead_a="k", lead_b="k", name="ffn_up_dx")
        dw_gu = matmul(dgu, res["h2"], ta=True, lead_a="batch", out_dtype=BF16, name="ffn_up_dw")
    else:
        dh2, (got["prev_w_branch"], got["prev_w_out"]) = matmul(
            dgu, lw["ffn_w_gu"], lead_a="k", lead_b="k",
            hosted=[("exchange", pending["w_branch"]), ("exchange", pending["w_out"])], name="ffn_up_dx_exchange")
        dw_gu, (got["prev_w_in"],) = matmul(dgu, res["h2"], ta=True, lead_a="batch", out_dtype=BF16,
                                            hosted=[("exchange", pending["w_in"])], name="ffn_up_dw_exchange")
    dw_gu = dw_gu.reshape(N_DEV, -1, d)
    dx1, dn_ffn_pre = rowwise_vjp(f_rms_pre, [res["x1"]], [lw["norm_ffn_pre"]], [dh2], [(0, F32, dx2)], [0],
                                  tr=TR, name="ffn_pre_norm_bwd")
    do, dn_mix_post = rowwise_vjp(f_rms_post, [x, res["o"]], [lw["norm_mix_post"]], [dx1], [(1, BF16, None)], [0],
                                  tr=TR, name="mix_post_norm_bwd")
    dmerged = matmul(do, lw["w_out"], tb=True, name="out_proj_dx")
    dw_out = matmul(res["merged"], do, ta=True, out_dtype=BF16, name="out_proj_dw")
    merge_rows = [(res["u3"], d, 0, i) for i in range(3)] + [(proj, d, 5 + i) for i in range(3)]
    mg = rowwise_vjp(f_merge, merge_rows, lw["b_gate"], [dmerged], [(i, BF16, None) for i in range(6)], [0, 1, 2],
                     tr=TR_WIDE, name="gate_merge_bwd")
    du3 = jnp.stack(mg[:3])
    d_gate_logits, db_gate = mg[3:6], mg[6:9]
    dy3 = matmul(du3, lw["w_branch"], tb=True, lead_a="batch", lead_b="batch", name="branch_proj_dx")
    dw_branch = matmul(res["y3"], du3, ta=True, lead_a="batch", lead_b="batch", out_dtype=BF16, name="branch_proj_dw")
    ret_cols = tuple(i * heads for i in range(4))
    dq, dk, dv, dg, d_gn = retention_bwd(proj, ret_cols, lw["ret_gn_w"], cosf, sinf, res["ret_states"], (dy3, 0), heads,
                                         blk=BLK, name="retention_bwd")
    dsq, dsk, dsv = sb_attention_bwd(proj, (dy3, 1), res["sb_run"], res["sb_visited"], 4 * heads, 5 * heads, 6 * heads,
                                     heads, blk=SB_BLK, name="stickbreak_bwd")
    z_spec = (proj, d // 2, 7)
    dy_scan, dz, d_ssd_norm = rowwise_vjp(f_ssd_gate, [res["y_scan"], z_spec], [lw["ssd_norm_w"]], [(dy3, d // 2, 0, 2)],
                                          [(0, F32, None), (1, BF16, None)], [0], tr=TR, name="ssd_gate_norm_bwd")
    dxs, dbm, dcm, ddt, d_dtb, d_alog, d_dskip = ssd_scan_bwd(
        res["xc"], res["dt_raw"], 0, lw["ssd_dt_bias"], lw["ssd_a_log"], lw["ssd_d"], res["ssd_states"], dy_scan,
        blk=BLK, name="ssd_scan_bwd")
    dxc_pad = jnp.pad(jnp.concatenate([dxs, dbm, dcm], axis=1), ((0, CONV_PAD), (0, 0)))
    du, d_taps, d_conv_b = ssd_conv_bwd(res["u_pad"], lw["conv_taps"], lw["ssd_conv_b"], dxc_pad, chunk=CONV_CHUNK,
                                        name="ssd_conv_bwd")
    dproj = jnp.concatenate([dq, dk, dv, dg, dsq, dsk, dsv, dz, du, *d_gate_logits], axis=1)
    dh_dt = matmul(ddt, lw["w_dt"], name="in_proj_dt_dx")
    dh, (got["ffn_w_down"], got["ffn_w_gu"]) = matmul(
        dproj, lw["w_cat"], add=dh_dt, hosted=[("exchange", dw_down), ("exchange", dw_gu)],
        name="in_proj_dx_exchange")
    dw_cat = matmul(dproj, res["h"], ta=True, tn=2048, out_dtype=BF16, name="in_proj_dw")
    dw_dt = matmul(ddt, res["h"], ta=True, out_dtype=BF16, name="in_proj_dt_dw")
    dx, dn_mix_pre = rowwise_vjp(f_rms_pre, [x], [lw["norm_mix_pre"]], [dh], [(0, F32, dx1)], [0], tr=TR,
                                 name="mix_pre_norm_bwd")
    mine = dict(
        w_in=_split_dw_in(dw_cat, dw_dt, d, n_dt),
        w_branch=jnp.transpose(dw_branch.reshape(3, d // 2, N_DEV, -1), (2, 0, 1, 3)).reshape(N_DEV, 3 * d // 2, -1),
        w_out=dw_out.reshape(N_DEV, d // N_DEV, d),
    )
    small = dict(
        norm_mix_pre=dn_mix_pre[0], norm_mix_post=dn_mix_post[0], norm_ffn_pre=dn_ffn_pre[0],
        norm_ffn_post=dn_ffn_post[0], b_gate=jnp.concatenate([b[0] for b in db_gate]), ret_gn_w=d_gn[0],
        ssd_conv_w=d_taps, ssd_conv_b=d_conv_b[0], ssd_dt_bias=d_dtb[0, :n_dt], ssd_a_log=d_alog[0, :n_dt],
        ssd_d=d_dskip[0, :n_dt], ssd_norm_w=d_ssd_norm[0],
    )
    return dx, got, mine, small


def _adam_rows(cols):
    return max(8, (1 << 17) // cols // 8 * 8)


def kernel(x, positions, norm_mix_pre, norm_mix_post, norm_ffn_pre, norm_ffn_post, w_in, b_gate, ret_gn_w, ssd_conv_w, ssd_conv_b, ssd_dt_bias, ssd_a_log, ssd_d, ssd_norm_w, w_branch_ret, w_branch_sb, w_branch_ssd, w_out, ffn_w_gate, ffn_w_up, ffn_w_down, loss_target, m_norm_mix_pre, m_norm_mix_post, m_norm_ffn_pre, m_norm_ffn_post, m_w_in, m_b_gate, m_ret_gn_w, m_ssd_conv_w, m_ssd_conv_b, m_ssd_dt_bias, m_ssd_a_log, m_ssd_d, m_ssd_norm_w, m_w_branch_ret, m_w_branch_sb, m_w_branch_ssd, m_w_out, m_ffn_w_gate, m_ffn_w_up, m_ffn_w_down, v_norm_mix_pre, v_norm_mix_post, v_norm_ffn_pre, v_norm_ffn_post, v_w_in, v_b_gate, v_ret_gn_w, v_ssd_conv_w, v_ssd_conv_b, v_ssd_dt_bias, v_ssd_a_log, v_ssd_d, v_ssd_norm_w, v_w_branch_ret, v_w_branch_sb, v_w_branch_ssd, v_w_out, v_ffn_w_gate, v_ffn_w_up, v_ffn_w_down):
    depth = w_in.shape[0]
    s, d = x.shape[1], x.shape[2]
    axes = ("x", "y", "c")
    me = _flat_index(lax.axis_index("x"), lax.axis_index("y"), lax.axis_index("c"))

    def tr_(a):
        return jnp.swapaxes(a, 1, 2)

    transposed = ("w_in", "ffn_w_gate", "ffn_w_up")
    wmv = dict(w_in=tuple(map(tr_, (w_in, m_w_in, v_w_in))), w_branch_ret=(w_branch_ret, m_w_branch_ret, v_w_branch_ret),
               w_branch_sb=(w_branch_sb, m_w_branch_sb, v_w_branch_sb),
               w_branch_ssd=(w_branch_ssd, m_w_branch_ssd, v_w_branch_ssd), w_out=(w_out, m_w_out, v_w_out),
               ffn_w_gate=tuple(map(tr_, (ffn_w_gate, m_ffn_w_gate, v_ffn_w_gate))),
               ffn_w_up=tuple(map(tr_, (ffn_w_up, m_ffn_w_up, v_ffn_w_up))),
               ffn_w_down=(ffn_w_down, m_ffn_w_down, v_ffn_w_down))
    members = dict(w_in=["w_in"], w_branch=["w_branch_ret", "w_branch_sb", "w_branch_ssd"], w_out=["w_out"],
                   ffn_w_gu=["ffn_w_gate", "ffn_w_up"], ffn_w_down=["ffn_w_down"])
    big_w = {g: (wmv[ns[0]][0] if len(ns) == 1 else jnp.concatenate([wmv[n][0] for n in ns], axis=1))
             for g, ns in members.items()}

    taps_all = all_gather(ssd_conv_w.reshape(-1, LANE), name="gather_conv_w")
    taps_all = jnp.transpose(taps_all.reshape(N_DEV, depth, SSD_K, -1), (1, 2, 0, 3)).reshape(depth, SSD_K, -1)

    cosf, sinf = rope_tables(positions.reshape(s), tr=TR)
    small_w = dict(norm_mix_pre=norm_mix_pre, norm_mix_post=norm_mix_post, norm_ffn_pre=norm_ffn_pre,
                   norm_ffn_post=norm_ffn_post, b_gate=b_gate, ret_gn_w=ret_gn_w, ssd_conv_b=ssd_conv_b,
                   ssd_dt_bias=ssd_dt_bias, ssd_a_log=ssd_a_log, ssd_d=ssd_d, ssd_norm_w=ssd_norm_w, taps=taps_all)

    def layer_weights(sw, gathered):
        lw = dict(gathered)
        for n in ("norm_mix_pre", "norm_mix_post", "norm_ffn_pre", "norm_ffn_post", "ret_gn_w", "ssd_conv_b", "ssd_norm_w"):
            lw[n] = _row(sw[n])
        for n in ("ssd_dt_bias", "ssd_a_log", "ssd_d"):
            lw[n] = _pad_lanes(sw[n])
        lw["b_gate"] = [_row(sw["b_gate"][i * d:(i + 1) * d]) for i in range(3)]
        lw["conv_taps"] = [sw["taps"][k:k + 1] for k in range(SSD_K)]
        return lw

    def layer_slice(t, l):
        return {n: a[l] for n, a in t.items()}

    def bf16_shards(l):
        return {n: big_w[n][l].astype(BF16) for n in BIG}

    def arrange(g):
        w_cat, w_dt = _assemble_w_in(g["w_in"], d)
        return dict(
            w_cat=w_cat, w_dt=w_dt,
            w_branch=jnp.transpose(g["w_branch"].reshape(N_DEV, 3, d // 2, -1), (1, 2, 0, 3)).reshape(3, d // 2, d),
            w_out=g["w_out"].reshape(d, d),
            ffn_w_gu=g["ffn_w_gu"].reshape(2 * N_DEV, -1, d),
            ffn_w_down=g["ffn_w_down"],
        )

    big_out = {}

    def adam(l, group, parts):
        rows, cols = big_w[group].shape[1:]
        parts = parts.reshape(N_DEV, rows, cols)
        off = 0
        for n in members[group]:
            big_out[n] = adamw_into(*wmv[n], parts, l, off, big_out.get(n), tr=_adam_rows(cols), name="adamw_" + n)
            off += wmv[n][0].shape[1]

    xs, saved = x.reshape(s, d), []
    gathered = {n: all_gather(a, name="gather_" + n) for n, a in bf16_shards(0).items()}
    for l in range(depth):
        lw = layer_weights(layer_slice(small_w, l), arrange(gathered))
        xs, res, gathered = _layer_fwd(xs, lw, cosf, sinf, bf16_shards(l + 1) if l + 1 < depth else None)
        saved.append((res, lw))
    loss_tile, dy = loss_head(xs, loss_target.reshape(s, d), tr=TR, name="loss_head")
    loss = lax.psum(loss_tile[0, 0], axes)

    dx, pending, small_layers = dy, None, [None] * depth
    for l in reversed(range(depth)):
        res, lw = saved[l]
        dx, got, pending, small_layers[l] = _layer_bwd(dx, res, lw, cosf, sinf, pending)
        for n in ("ffn_w_gu", "ffn_w_down"):
            adam(l, n, got[n])
        if l + 1 < depth:
            for n in ("w_in", "w_branch", "w_out"):
                adam(l + 1, n, got["prev_" + n])
    for n in ("w_in", "w_branch", "w_out"):
        adam(0, n, exchange(pending[n], name="exchange_" + n))
    small_g = {n: jnp.stack([small_layers[l][n] for l in range(depth)]) for n in SMALL}

    n_dt = ssd_dt_bias.shape[-1]
    small_in = dict(norm_mix_pre=(norm_mix_pre, m_norm_mix_pre, v_norm_mix_pre), norm_mix_post=(norm_mix_post, m_norm_mix_post, v_norm_mix_post),
                    norm_ffn_pre=(norm_ffn_pre, m_norm_ffn_pre, v_norm_ffn_pre), norm_ffn_post=(norm_ffn_post, m_norm_ffn_post, v_norm_ffn_post),
                    b_gate=(b_gate, m_b_gate, v_b_gate), ret_gn_w=(ret_gn_w, m_ret_gn_w, v_ret_gn_w),
                    ssd_conv_b=(ssd_conv_b, m_ssd_conv_b, v_ssd_conv_b), ssd_dt_bias=(ssd_dt_bias, m_ssd_dt_bias, v_ssd_dt_bias),
                    ssd_a_log=(ssd_a_log, m_ssd_a_log, v_ssd_a_log), ssd_d=(ssd_d, m_ssd_d, v_ssd_d),
                    ssd_norm_w=(ssd_norm_w, m_ssd_norm_w, v_ssd_norm_w))
    rep = [n for n in SMALL if n != "ssd_conv_w"]

    def pack(arrs):
        flat = jnp.concatenate([a.reshape(-1) for a in arrs])
        rows = -(-flat.shape[0] // (16 * LANE)) * 16
        return jnp.pad(flat, (0, rows * LANE - flat.shape[0])).reshape(rows, LANE)

    conv_g = small_g["ssd_conv_w"]
    g_pack = pack([small_g[n] for n in rep] + [conv_g])
    g_all = all_gather(g_pack, name="gather_small_grads")
    zeros_conv = jnp.zeros_like(conv_g)
    w_pack, m_pack, v_pack = (pack([small_in[n][i] for n in rep] + [zeros_conv]) for i in range(3))
    sm = adamw_sum(w_pack, m_pack, v_pack, g_all, tr=TR, name="adamw_small")

    def unpack(p):
        flat, out, off = p.reshape(-1), {}, 0
        for n in rep:
            shp = small_in[n][0].shape
            size = math.prod(shp)
            out[n] = flat[off:off + size].reshape(shp)
            off += size
        out["conv_sum"] = flat[off:off + conv_g.size].reshape(conv_g.shape)
        return out

    sm = [unpack(p) for p in sm]
    ch = ssd_conv_w.shape[-1]
    conv_mine = lax.dynamic_slice_in_dim(sm[0]["conv_sum"], me * ch, ch, axis=2)
    conv_out = adamw_sum(ssd_conv_w.reshape(-1, LANE), m_ssd_conv_w.reshape(-1, LANE), v_ssd_conv_w.reshape(-1, LANE),
                         conv_mine.reshape(1, -1, LANE), tr=TR, name="adamw_conv_w")
    for i in range(4):
        sm[i]["ssd_conv_w"] = conv_out[i].reshape(ssd_conv_w.shape)

    def big_named(i):
        return {n: (tr_(out[i]) if n in transposed else out[i]) for n, out in big_out.items()}

    order = ["norm_mix_pre", "norm_mix_post", "norm_ffn_pre", "norm_ffn_post", "w_in", "b_gate", "ret_gn_w", "ssd_conv_w",
             "ssd_conv_b", "ssd_dt_bias", "ssd_a_log", "ssd_d", "ssd_norm_w", "w_branch_ret", "w_branch_sb", "w_branch_ssd",
             "w_out", "ffn_w_gate", "ffn_w_up", "ffn_w_down"]
    outs = [loss, dx.reshape(x.shape)]
    for i in range(4):
        named = {**sm[i], **big_named(i)}
        outs += [named[n] for n in order]
    return tuple(outs)
```

```python
import functools
import math

import jax
import jax.numpy as jnp
import numpy as np
from jax import lax
from jax.experimental import pallas as pl
from jax.experimental.pallas import tpu as pltpu

F32 = jnp.float32
BF16 = jnp.bfloat16

N_DEV = 8
HEAD = 128
SSD_P = 64
SSD_G = 4
SSD_N = 128
SSD_K = 4
CHUNK = 64
NORM_EPS = 1e-6
ROPE_BASE = 10000.0
LANE = 128
VMEM_LIMIT = 56 * 1024 * 1024

ADAM_LR, ADAM_B1, ADAM_B2, ADAM_EPS, ADAM_WD, ADAM_STEP = 0.001, 0.9, 0.999, 1e-08, 0.01, 10


def _cparams(sem):
    return pltpu.CompilerParams(dimension_semantics=sem, vmem_limit_bytes=VMEM_LIMIT)


def _pick(n, pref):
    if n <= pref:
        return n
    t = pref
    while t >= LANE:
        if n % t == 0:
            return t
        t -= LANE
    return n


def matmul(a, b, *, ta=False, tb=False, lead_a=None, lead_b=None, out_dtype=F32, add=None, hosted=(),
           tm=1024, tn=1024, tk=2048, name="mm"):
    la, lb = lead_a is not None, lead_b is not None
    a2, b2 = a.shape[1:] if la else a.shape, b.shape[1:] if lb else b.shape
    (kd_a, m) = a2 if ta else a2[::-1]
    (kd_b, n) = b2[::-1] if tb else b2
    assert kd_a == kd_b, (a.shape, b.shape)
    nlead = a.shape[0] if la else (b.shape[0] if lb else 1)
    batch = "batch" in (lead_a, lead_b)
    kblocks = nlead if "k" in (lead_a, lead_b) else 1
    if la and lb:
        assert lead_a == lead_b and a.shape[0] == b.shape[0]
    tm, tn, tk = _pick(m, tm), _pick(n, tn), _pick(kd_a, tk)
    kt = kd_a // tk
    nk = kt * kblocks
    grid = (m // tm, (nlead if batch else 1), n // tn, nk)

    def lead_idx(g, k):
        return g if batch else k // kt

    def a_map(i, g, j, k):
        idx = (k % kt, i) if ta else (i, k % kt)
        return ((lead_idx(g, k),) + idx) if la else idx

    def b_map(i, g, j, k):
        idx = (j, k % kt) if tb else (k % kt, j)
        return ((lead_idx(g, k),) + idx) if lb else idx

    a_blk = (tk, tm) if ta else (tm, tk)
    b_blk = (tn, tk) if tb else (tk, tn)
    a_spec = pl.BlockSpec(((None,) + a_blk) if la else a_blk, a_map)
    b_spec = pl.BlockSpec(((None,) + b_blk) if lb else b_blk, b_map)
    if batch:
        o_spec = pl.BlockSpec((None, tm, tn), lambda i, g, j, k: (g, i, j))
        o_shape = jax.ShapeDtypeStruct((nlead, m, n), out_dtype)
    else:
        o_spec = pl.BlockSpec((tm, tn), lambda i, g, j, k: (i, j))
        o_shape = jax.ShapeDtypeStruct((m, n), out_dtype)
    dims = (((0 if ta else 1,), (1 if tb else 0,)), ((), ()))

    n_add, n_host = int(add is not None), len(hosted)
    n_acc = int(nk > 1)

    def body(a_ref, b_ref, *rest):
        c_ref = rest[0] if n_add else None
        src_refs = rest[n_add:n_add + n_host]
        o_ref = rest[n_add + n_host]
        dst_refs = rest[n_add + n_host + 1:n_add + 2 * n_host + 1]
        acc_ref = rest[n_add + 2 * n_host + 1] if n_acc else None
        sems = rest[n_add + 2 * n_host + 1 + n_acc:]
        k = pl.program_id(3)
        ids = [pl.program_id(ax) for ax in range(4)]
        comms = [_COMM[kind](src_refs[c], dst_refs[c], *sems[3 * c:3 * c + 3]) for c, (kind, _) in enumerate(hosted)]

        if hosted:
            @pl.when(functools.reduce(jnp.logical_and, [i == 0 for i in ids]))
            def _():
                for start, _ in comms:
                    start()

        def product():
            return lax.dot_general(a_ref[...].astype(BF16), b_ref[...].astype(BF16), dims, preferred_element_type=F32)

        if n_acc:
            @pl.when(k == 0)
            def _():
                acc_ref[...] = jnp.zeros_like(acc_ref) if c_ref is None else c_ref[...].astype(F32)

            acc_ref[...] += product()

            @pl.when(k == nk - 1)
            def _():
                o_ref[...] = acc_ref[...].astype(o_ref.dtype)
        else:
            o_ref[...] = (product() if c_ref is None else product() + c_ref[...].astype(F32)).astype(o_ref.dtype)

        if hosted:
            @pl.when(functools.reduce(jnp.logical_and, [i == g - 1 for i, g in zip(ids, grid)]))
            def _():
                for _, finish in comms:
                    finish()

    extra = ([] if add is None else [add]) + [arr for _, arr in hosted]
    extra_specs = ([] if add is None else [o_spec]) + [_HBM] * n_host
    out_shapes = [o_shape] + [_comm_out_shape(kind, arr) for kind, arr in hosted]
    scratch = [pltpu.VMEM((tm, tn), F32)] * n_acc + _COMM_SEMS * n_host
    sem = ("arbitrary",) * 4 if hosted else ("parallel", "parallel", "parallel", "arbitrary")
    res = pl.pallas_call(
        body, out_shape=out_shapes, grid=grid, in_specs=[a_spec, b_spec] + extra_specs,
        out_specs=[o_spec] + [_HBM] * n_host, scratch_shapes=scratch, name=name, compiler_params=_cparams(sem),
    )(a, b, *extra)
    return (res[0], res[1:]) if hosted else res[0]


def _row_spec(r, tr):
    if not isinstance(r, tuple):
        return r, pl.BlockSpec((tr, r.shape[-1]), lambda i: (i, 0))
    if len(r) == 3:
        arr, w, cb = r
        return arr, pl.BlockSpec((tr, w), lambda i: (i, cb))
    arr, w, cb, ld = r
    return arr, pl.BlockSpec((None, tr, w), lambda i: (ld, i, cb))


def _full_spec(c):
    nd = c.ndim
    return pl.BlockSpec(c.shape, lambda i: (0,) * nd)


def rowwise(fn, rows, consts, outs, *, tr, name):
    arrs, specs = zip(*[_row_spec(r, tr) for r in rows])
    n_rows = arrs[0].shape[-2]
    nr, nc = len(rows), len(consts)

    def body(*refs):
        vals = [r[...] for r in refs[:nr + nc]]
        res = fn(*vals)
        for o_ref, r in zip(refs[nr + nc:], res):
            o_ref[...] = r.astype(o_ref.dtype)

    return pl.pallas_call(
        body, grid=(n_rows // tr,),
        in_specs=list(specs) + [_full_spec(c) for c in consts],
        out_specs=[pl.BlockSpec((tr, w), lambda i: (i, 0)) for w, _ in outs],
        out_shape=[jax.ShapeDtypeStruct((n_rows, w), dt) for w, dt in outs],
        name=name, compiler_params=_cparams(("parallel",)),
    )(*arrs, *consts)


def rowwise_vjp(fn, rows, consts, cts, row_grads, const_grads, *, tr, name):
    arrs, specs = zip(*[_row_spec(r, tr) for r in rows])
    n_rows = arrs[0].shape[-2]
    nr, nc = len(rows), len(consts)
    ct_present = [c for c in cts if c is not None]
    ct_arrs, ct_specs = zip(*[_row_spec(c, tr) for c in ct_present])
    add_present = [g[2] for g in row_grads if g[2] is not None]
    add_arrs, add_specs = zip(*[_row_spec(c, tr) for c in add_present]) if add_present else ((), ())
    n_ct, n_add = len(ct_present), len(add_present)
    widths = [s.block_shape[-1] for s in specs]

    def body(*refs):
        ins = refs[:nr + nc]
        ct_refs = refs[nr + nc:nr + nc + n_ct]
        add_refs = refs[nr + nc + n_ct:nr + nc + n_ct + n_add]
        out_refs = refs[nr + nc + n_ct + n_add:]
        vals = [r[...] for r in ins]
        res, f_vjp = jax.vjp(fn, *vals)
        it = iter(ct_refs)
        ct_vals = tuple(next(it)[...].astype(r.dtype) if c is not None else jnp.zeros_like(r)
                        for c, r in zip(cts, res))
        grads = f_vjp(ct_vals)
        ita = iter(add_refs)
        for o_ref, (idx, _, add) in zip(out_refs, row_grads):
            g = grads[idx].astype(F32)
            if add is not None:
                g = g + next(ita)[...].astype(F32)
            o_ref[...] = g.astype(o_ref.dtype)
        first = pl.program_id(0) == 0
        for o_ref, idx in zip(out_refs[len(row_grads):], const_grads):
            g = grads[nr + idx].astype(F32)

            @pl.when(first)
            def _():
                o_ref[...] = g

            @pl.when(jnp.logical_not(first))
            def _():
                o_ref[...] += g

    out_specs = [pl.BlockSpec((tr, widths[idx]), lambda i: (i, 0)) for idx, _, _ in row_grads]
    out_shape = [jax.ShapeDtypeStruct((n_rows, widths[idx]), dt) for idx, dt, _ in row_grads]
    out_specs += [_full_spec(consts[idx]) for idx in const_grads]
    out_shape += [jax.ShapeDtypeStruct(consts[idx].shape, F32) for idx in const_grads]
    return pl.pallas_call(
        body, grid=(n_rows // tr,),
        in_specs=list(specs) + [_full_spec(c) for c in consts] + list(ct_specs) + list(add_specs),
        out_specs=out_specs, out_shape=out_shape,
        name=name, compiler_params=_cparams(("arbitrary",)),
    )(*arrs, *consts, *ct_arrs, *add_arrs)


def f_rms(x, w):
    xf = x.astype(F32)
    return xf * lax.rsqrt(jnp.mean(xf * xf, axis=-1, keepdims=True) + NORM_EPS) * w


def f_rms_pre(x, w):
    return (f_rms(x, w),)


def f_rms_post(x, o, w):
    return (x + f_rms(o, w),)


def f_merge(u0, u1, u2, g0, g1, g2, b0, b1, b2):
    return (jax.nn.sigmoid(g0 + b0) * u0 + jax.nn.sigmoid(g1 + b1) * u1 + jax.nn.sigmoid(g2 + b2) * u2,)


def f_ssd_gate(y, z, w):
    return (f_rms(y * jax.nn.silu(z), w),)


@jax.custom_vjp
def _swap_halves(x):
    return pltpu.roll(x, HEAD // 2, 1)


_swap_halves.defvjp(lambda x: (_swap_halves(x), None), lambda _, g: (_swap_halves(g),))


def rope_tables(positions, *, tr):
    s = positions.shape[0]
    half = HEAD // 2
    inv = ROPE_BASE ** (-2.0 * jnp.arange(half, dtype=F32) / HEAD)
    inv = jnp.concatenate([inv, inv]).reshape(1, HEAD)
    sign = jnp.concatenate([-jnp.ones((half,), F32), jnp.ones((half,), F32)]).reshape(1, HEAD)

    def fn(pos, inv, sign):
        ang = pos.astype(F32) * inv
        return jnp.cos(ang), jnp.sin(ang) * sign

    return rowwise(fn, [positions.reshape(s, 1)], [inv, sign], [(HEAD, F32), (HEAD, F32)], tr=tr, name="rope_tables")


def _ret_consts(n_heads, blk):
    lg = np.log1p(-np.exp2(-5.0 - np.arange(n_heads)))[:, None, None]
    i = np.arange(blk)
    dist = np.abs(i[:, None] - i[None, :])[None]
    allowed = ((i[None, :] // CHUNK) <= (i[:, None] // CHUNK))[None]
    dm = np.where(allowed, np.exp(lg * dist), 0.0)
    qd = np.broadcast_to(np.exp(lg * (i[None, :, None] + 1.0)), (n_heads, blk, HEAD))
    kd = np.broadcast_to(np.exp(lg * (blk - 1.0 - i[None, :, None])), (n_heads, blk, HEAD))
    cd = np.broadcast_to(np.exp(lg * blk), (n_heads, 1, HEAD))
    return [jnp.asarray(a, F32) for a in (dm, qd, kd, cd)]


def _ret_block(q, k, v, g, gnw, state, cosf, sinf, dm, qd, kd, cd):
    qr = q * cosf + _swap_halves(q) * sinf
    kr = (k * cosf + _swap_halves(k) * sinf) * (HEAD ** -0.5)
    vb = v.astype(BF16)
    scores = _dot(qr.astype(BF16), kr.astype(BF16), _NT) * dm
    o = _dot(scores.astype(BF16), vb) + _dot((qr * qd).astype(BF16), state.astype(BF16))
    new_state = state * cd + _dot((kr * kd).astype(BF16), vb, _TN)
    mu = jnp.mean(o, axis=-1, keepdims=True)
    var = jnp.mean(jnp.square(o - mu), axis=-1, keepdims=True)
    y = (o - mu) * lax.rsqrt(var + NORM_EPS) * gnw * jax.nn.silu(g)
    return y, new_state


def _ret_specs(n_heads, blk, nb, cols, reverse):
    jm = (lambda j: nb - 1 - j) if reverse else (lambda j: j)
    col = lambda c0: pl.BlockSpec((blk, HEAD), lambda h, j: (jm(j), c0 + h))
    tab = pl.BlockSpec((blk, HEAD), lambda h, j: (jm(j), 0))
    specs = [col(c) for c in cols]
    specs += [pl.BlockSpec((1, HEAD), lambda h, j: (0, h)), tab, tab]
    specs += [pl.BlockSpec((None, blk, blk), lambda h, j: (h, 0, 0)),
              pl.BlockSpec((None, blk, HEAD), lambda h, j: (h, 0, 0)),
              pl.BlockSpec((None, blk, HEAD), lambda h, j: (h, 0, 0)),
              pl.BlockSpec((None, 1, HEAD), lambda h, j: (h, 0, 0))]
    state = pl.BlockSpec((None, None, HEAD, HEAD), lambda h, j: (h, jm(j), 0, 0))
    out_col = pl.BlockSpec((blk, HEAD), lambda h, j: (jm(j), h))
    return specs, state, out_col


def retention_fwd(proj, cols, gn_w, cosf, sinf, n_heads, *, blk, name):
    s = proj.shape[0]
    nb = s // blk
    consts = _ret_consts(n_heads, blk)
    specs, state_spec, out_col = _ret_specs(n_heads, blk, nb, cols, False)

    def body(q_ref, k_ref, v_ref, g_ref, gn_ref, cos_ref, sin_ref, dm_ref, qd_ref, kd_ref, cd_ref,
             y_ref, st_ref, state):
        @pl.when(pl.program_id(1) == 0)
        def _():
            state[...] = jnp.zeros_like(state)

        st = state[...]
        st_ref[...] = st
        y, new_state = _ret_block(q_ref[...], k_ref[...], v_ref[...], g_ref[...], gn_ref[...], st,
                                  cos_ref[...], sin_ref[...], dm_ref[...], qd_ref[...], kd_ref[...], cd_ref[...])
        y_ref[...] = y.astype(y_ref.dtype)
        state[...] = new_state

    return pl.pallas_call(
        body, grid=(n_heads, nb), in_specs=specs, out_specs=[out_col, state_spec],
        out_shape=[jax.ShapeDtypeStruct((s, n_heads * HEAD), BF16),
                   jax.ShapeDtypeStruct((n_heads, nb, HEAD, HEAD), F32)],
        scratch_shapes=[pltpu.VMEM((HEAD, HEAD), F32)],
        name=name, compiler_params=_cparams(("parallel", "arbitrary")),
    )(proj, proj, proj, proj, gn_w, cosf, sinf, *consts)


def retention_bwd(proj, cols, gn_w, cosf, sinf, states, dy, n_heads, *, blk, name):
    s = proj.shape[0]
    nb = s // blk
    consts = _ret_consts(n_heads, blk)
    specs, state_spec, out_col = _ret_specs(n_heads, blk, nb, cols, True)

    def body(q_ref, k_ref, v_ref, g_ref, gn_ref, cos_ref, sin_ref, dm_ref, qd_ref, kd_ref, cd_ref,
             st_ref, dy_ref, dq_ref, dk_ref, dv_ref, dg_ref, dgn_ref, dstate):
        first = pl.program_id(1) == 0

        @pl.when(first)
        def _():
            dstate[...] = jnp.zeros_like(dstate)
            dgn_ref[...] = jnp.zeros_like(dgn_ref)

        tabs = (cos_ref[...], sin_ref[...], dm_ref[...], qd_ref[...], kd_ref[...], cd_ref[...])
        fn = lambda q, k, v, g, gnw, st: _ret_block(q, k, v, g, gnw, st, *tabs)
        _, f_vjp = jax.vjp(fn, q_ref[...], k_ref[...], v_ref[...], g_ref[...], gn_ref[...], st_ref[...])
        dq, dk, dv, dg, dgn, dst = f_vjp((dy_ref[...].astype(F32), dstate[...]))
        dq_ref[...] = dq.astype(dq_ref.dtype)
        dk_ref[...] = dk.astype(dk_ref.dtype)
        dv_ref[...] = dv.astype(dv_ref.dtype)
        dg_ref[...] = dg.astype(dg_ref.dtype)
        dgn_ref[...] += dgn
        dstate[...] = dst

    o_shape = jax.ShapeDtypeStruct((s, n_heads * HEAD), BF16)
    dy_spec = out_col
    if isinstance(dy, tuple):
        dy, lead = dy
        dy_spec = pl.BlockSpec((None, blk, HEAD), lambda h, j: (lead, nb - 1 - j, h))
    return pl.pallas_call(
        body, grid=(n_heads, nb), in_specs=specs + [state_spec, dy_spec],
        out_specs=[out_col, out_col, out_col, out_col, pl.BlockSpec((1, HEAD), lambda h, j: (0, h))],
        out_shape=[o_shape, o_shape, o_shape, o_shape, jax.ShapeDtypeStruct((1, n_heads * HEAD), F32)],
        scratch_shapes=[pltpu.VMEM((HEAD, HEAD), F32)],
        name=name, compiler_params=_cparams(("parallel", "arbitrary")),
    )(proj, proj, proj, proj, gn_w, cosf, sinf, *consts, states, dy)


CONV_PAD = 8


def _conv_pre(u_ext, taps, bias, n_out):
    n = u_ext.shape[0]
    views = [pltpu.roll(u_ext, n - (k + CONV_PAD - SSD_K + 1), 0)[:n_out] for k in range(SSD_K)]
    pre = bias
    for k in range(SSD_K):
        pre = pre + taps[k] * views[k]
    return pre, views


def ssd_conv_fwd(u_pad, taps, bias, *, chunk, name):
    s, c = u_pad.shape[0] - 2 * CONV_PAD, u_pad.shape[1]

    def body(u_ref, t0, t1, t2, t3, b_ref, o_ref):
        taps_v = [t[...] for t in (t0, t1, t2, t3)]
        bias_v = b_ref[...]

        @pl.loop(0, s // chunk)
        def _(ci):
            r0 = pl.multiple_of(ci * chunk, chunk)
            pre, _ = _conv_pre(u_ref[pl.ds(r0, chunk + CONV_PAD), :], taps_v, bias_v, chunk)
            o_ref[pl.ds(r0, chunk), :] = pre * jax.nn.sigmoid(pre)

    row = pl.BlockSpec((1, LANE), lambda i: (0, i))
    return pl.pallas_call(
        body, grid=(c // LANE,),
        in_specs=[pl.BlockSpec((s + 2 * CONV_PAD, LANE), lambda i: (0, i))] + [row] * 5,
        out_specs=pl.BlockSpec((s, LANE), lambda i: (0, i)),
        out_shape=jax.ShapeDtypeStruct((s, c), F32),
        name=name, compiler_params=_cparams(("parallel",)),
    )(u_pad, *taps, bias)


def ssd_conv_bwd(u_pad, taps, bias, dxc_pad, *, chunk, name):
    s, c = u_pad.shape[0] - 2 * CONV_PAD, u_pad.shape[1]
    ext = chunk + CONV_PAD

    def body(u_ref, t0, t1, t2, t3, b_ref, d_ref, du_ref, dw_ref, db_ref):
        taps_v = [t[...] for t in (t0, t1, t2, t3)]
        bias_v = b_ref[...]
        dw_ref[...] = jnp.zeros_like(dw_ref)
        db_ref[...] = jnp.zeros_like(db_ref)

        @pl.loop(0, s // chunk)
        def _(ci):
            r0 = pl.multiple_of(ci * chunk, chunk)
            pre, views = _conv_pre(u_ref[pl.ds(r0, ext + CONV_PAD), :], taps_v, bias_v, ext)
            sig = jax.nn.sigmoid(pre)
            dpre = d_ref[pl.ds(r0, ext), :] * (sig * (1.0 + pre * (1.0 - sig)))
            du = taps_v[SSD_K - 1] * dpre[:chunk]
            for k in range(SSD_K - 1):
                du = du + taps_v[k] * pltpu.roll(dpre, ext - (SSD_K - 1 - k), 0)[:chunk]
            du_ref[pl.ds(r0, chunk), :] = du.astype(du_ref.dtype)
            own = dpre[:chunk]
            for k in range(SSD_K):
                dw_ref[k:k + 1, :] += jnp.sum(own * views[k][:chunk], axis=0, keepdims=True)
            db_ref[...] += jnp.sum(own, axis=0, keepdims=True)

    row = pl.BlockSpec((1, LANE), lambda i: (0, i))
    return pl.pallas_call(
        body, grid=(c // LANE,),
        in_specs=[pl.BlockSpec((s + 2 * CONV_PAD, LANE), lambda i: (0, i))] + [row] * 5
        + [pl.BlockSpec((s + CONV_PAD, LANE), lambda i: (0, i))],
        out_specs=[pl.BlockSpec((s, LANE), lambda i: (0, i)), pl.BlockSpec((SSD_K, LANE), lambda i: (0, i)), row],
        out_shape=[jax.ShapeDtypeStruct((s, c), BF16), jax.ShapeDtypeStruct((SSD_K, c), F32),
                   jax.ShapeDtypeStruct((1, c), F32)],
        name=name, compiler_params=_cparams(("parallel",)),
    )(u_pad, *taps, bias, dxc_pad)


def _tri_dot(tri, x, passes=3):
    out = None
    rem = x
    for _ in range(passes):
        piece = rem.astype(BF16)
        rem = rem - piece.astype(F32)
        d = _dot(tri, piece)
        out = d if out is None else out + d
    return out


def _tri(n, upper):
    rr = lax.broadcasted_iota(jnp.int32, (n, n), 0)
    cc = lax.broadcasted_iota(jnp.int32, (n, n), 1)
    return ((rr <= cc) if upper else (rr >= cc)).astype(BF16)


@jax.custom_vjp
def _cumsum_rows(a):
    return _tri_dot(_tri(a.shape[0], False), a)


_cumsum_rows.defvjp(lambda a: (_cumsum_rows(a), None), lambda _, g: (_tri_dot(_tri(g.shape[0], True), g),))


def _softplus(x):
    return jnp.maximum(x, 0.0) + jnp.log(1.0 + jnp.exp(-jnp.abs(x)))


def _ssd_block(x, bm, cm, dtraw, dtb, alog, dsk, state_t, group):
    blk, width = x.shape
    e_heads = width // SSD_P
    dt = _softplus(dtraw + dtb)
    acum = _cumsum_rows(dt * (-jnp.exp(alog)))
    acum_t = acum.T
    lane_h = lax.broadcasted_iota(jnp.int32, (1, LANE), 1)
    sub_h = lax.broadcasted_iota(jnp.int32, (LANE, 1), 0)
    lane_e = lax.broadcasted_iota(jnp.int32, (1, width), 1) // SSD_P
    causal = lax.broadcasted_iota(jnp.int32, (blk, blk), 0) >= lax.broadcasted_iota(jnp.int32, (blk, blk), 1)
    last_row = lax.broadcasted_iota(jnp.int32, (blk, 1), 0) == blk - 1
    cb = _dot(cm.astype(BF16), bm.astype(BF16), _NT)
    y = jnp.zeros((blk, width), F32)
    dt_l = jnp.zeros((blk, width), F32)
    ac_l = jnp.zeros((blk, width), F32)
    d_l = jnp.zeros((1, width), F32)
    for e in range(e_heads):
        head = group * e_heads + e
        pick = lane_h == head
        col = jnp.sum(jnp.where(pick, acum, 0.0), axis=1, keepdims=True)
        dt_e = jnp.sum(jnp.where(pick, dt, 0.0), axis=1, keepdims=True)
        d_e = jnp.sum(jnp.where(pick, dsk, 0.0), axis=1, keepdims=True)
        row = jnp.sum(jnp.where(sub_h == head, acum_t, 0.0), axis=0, keepdims=True)
        mine = lane_e == e
        decay = jnp.exp(jnp.where(causal, col - row, -jnp.inf))
        y = y + _dot((cb * decay).astype(BF16), jnp.where(mine, x * dt_e, 0.0).astype(BF16))
        dt_l = dt_l + jnp.where(mine, dt_e, 0.0)
        ac_l = ac_l + jnp.where(mine, col, 0.0)
        d_l = d_l + jnp.where(mine, d_e, 0.0)
    ac_last = jnp.sum(jnp.where(last_row, ac_l, 0.0), axis=0, keepdims=True)
    y = y + jnp.exp(ac_l) * _dot(cm.astype(BF16), state_t.astype(BF16)) + x * d_l
    inject = (x * dt_l * jnp.exp(ac_last - ac_l)).astype(BF16)
    new_state = state_t * jnp.exp(ac_last) + _dot(bm.astype(BF16), inject, _TN)
    return y, new_state


def ssd_scan_fwd(xc, proj, dt_col, dtb, alog, dsk, *, blk, name):
    s = xc.shape[0]
    nb = s // blk
    e_w = 4 * SSD_P
    b_off = SSD_G * e_w // SSD_N
    c_off = b_off + SSD_G
    row = pl.BlockSpec((1, LANE), lambda j, g: (0, 0))

    def body(x_ref, b_ref, c_ref, dt_ref, dtb_ref, al_ref, d_ref, y_ref, st_ref, state):
        j, g = pl.program_id(0), pl.program_id(1)

        @pl.when(j == 0)
        def _():
            state[g] = jnp.zeros((SSD_N, e_w), F32)

        st = state[g]
        st_ref[...] = st
        y, new_state = _ssd_block(x_ref[...], b_ref[...], c_ref[...], dt_ref[...], dtb_ref[...], al_ref[...],
                                  d_ref[...], st, g)
        y_ref[...] = y
        state[g] = new_state

    return pl.pallas_call(
        body, grid=(nb, SSD_G),
        in_specs=[pl.BlockSpec((blk, e_w), lambda j, g: (j, g)),
                  pl.BlockSpec((blk, SSD_N), lambda j, g: (j, b_off + g)),
                  pl.BlockSpec((blk, SSD_N), lambda j, g: (j, c_off + g)),
                  pl.BlockSpec((blk, LANE), lambda j, g: (j, dt_col)), row, row, row],
        out_specs=[pl.BlockSpec((blk, e_w), lambda j, g: (j, g)),
                   pl.BlockSpec((None, None, SSD_N, e_w), lambda j, g: (j, g, 0, 0))],
        out_shape=[jax.ShapeDtypeStruct((s, SSD_G * e_w), F32),
                   jax.ShapeDtypeStruct((nb, SSD_G, SSD_N, e_w), F32)],
        scratch_shapes=[pltpu.VMEM((SSD_G, SSD_N, e_w), F32)],
        name=name, compiler_params=_cparams(("arbitrary", "arbitrary")),
    )(xc, xc, xc, proj, dtb, alog, dsk)


def ssd_scan_bwd(xc, proj, dt_col, dtb, alog, dsk, states, dy, *, blk, name):
    s = xc.shape[0]
    nb = s // blk
    e_w = 4 * SSD_P
    b_off = SSD_G * e_w // SSD_N
    c_off = b_off + SSD_G
    row = pl.BlockSpec((1, LANE), lambda j, g: (0, 0))
    jm = lambda j: nb - 1 - j

    def body(x_ref, b_ref, c_ref, dt_ref, dtb_ref, al_ref, d_ref, st_ref, dy_ref,
             dx_ref, db_ref, dc_ref, ddt_ref, ddtb_ref, dal_ref, dd_ref, dstate):
        j, g = pl.program_id(0), pl.program_id(1)

        @pl.when(j == 0)
        def _():
            dstate[g] = jnp.zeros((SSD_N, e_w), F32)

        @pl.when(jnp.logical_and(j == 0, g == 0))
        def _():
            ddtb_ref[...] = jnp.zeros_like(ddtb_ref)
            dal_ref[...] = jnp.zeros_like(dal_ref)
            dd_ref[...] = jnp.zeros_like(dd_ref)

        @pl.when(g == 0)
        def _():
            ddt_ref[...] = jnp.zeros_like(ddt_ref)

        fn = functools.partial(_ssd_block, group=g)
        _, f_vjp = jax.vjp(fn, x_ref[...], b_ref[...], c_ref[...], dt_ref[...], dtb_ref[...], al_ref[...],
                           d_ref[...], st_ref[...])
        dx, db, dc, ddt, ddtb, dal, dd, dst = f_vjp((dy_ref[...], dstate[g]))
        dx_ref[...] = dx
        db_ref[...] = db
        dc_ref[...] = dc
        ddt_ref[...] += ddt
        ddtb_ref[...] += ddtb
        dal_ref[...] += dal
        dd_ref[...] += dd
        dstate[g] = dst

    return pl.pallas_call(
        body, grid=(nb, SSD_G),
        in_specs=[pl.BlockSpec((blk, e_w), lambda j, g: (jm(j), g)),
                  pl.BlockSpec((blk, SSD_N), lambda j, g: (jm(j), b_off + g)),
                  pl.BlockSpec((blk, SSD_N), lambda j, g: (jm(j), c_off + g)),
                  pl.BlockSpec((blk, LANE), lambda j, g: (jm(j), dt_col)), row, row, row,
                  pl.BlockSpec((None, None, SSD_N, e_w), lambda j, g: (jm(j), g, 0, 0)),
                  pl.BlockSpec((blk, e_w), lambda j, g: (jm(j), g))],
        out_specs=[pl.BlockSpec((blk, e_w), lambda j, g: (jm(j), g)),
                   pl.BlockSpec((blk, SSD_N), lambda j, g: (jm(j), g)),
                   pl.BlockSpec((blk, SSD_N), lambda j, g: (jm(j), g)),
                   pl.BlockSpec((blk, LANE), lambda j, g: (jm(j), 0)), row, row, row],
        out_shape=[jax.ShapeDtypeStruct((s, SSD_G * e_w), F32),
                   jax.ShapeDtypeStruct((s, SSD_G * SSD_N), F32),
                   jax.ShapeDtypeStruct((s, SSD_G * SSD_N), F32),
                   jax.ShapeDtypeStruct((s, LANE), F32)] + [jax.ShapeDtypeStruct((1, LANE), F32)] * 3,
        scratch_shapes=[pltpu.VMEM((SSD_G, SSD_N, e_w), F32)],
        name=name, compiler_params=_cparams(("arbitrary", "arbitrary")),
    )(xc, xc, xc, proj, dtb, alog, dsk, states, dy)


SB_DEAD = -104.0

_NT = (((1,), (1,)), ((), ()))
_TN = (((0,), (0,)), ((), ()))


def _dot(a, b, dims=(((1,), (0,)), ((), ()))):
    return lax.dot_general(a, b, dims, preferred_element_type=F32)


def _split_dot(x, tri, passes):
    out = None
    rem = x
    for _ in range(passes):
        piece = rem.astype(BF16)
        rem = rem - piece.astype(F32)
        d = _dot(piece, tri)
        out = d if out is None else out + d
    return out


def _sb_scores(q, k_ref, j, blk, row, scale):
    kb = k_ref[pl.ds(pl.multiple_of(j * blk, blk), blk), :].astype(BF16)
    z = _dot(q, kb, _NT) * scale
    col = j * blk + lax.broadcasted_iota(jnp.int32, (blk, blk), 1)
    mask = col < row
    sp = jnp.maximum(z, 0.0) + jnp.log(1.0 + jnp.exp(-jnp.abs(z)))
    lk = jnp.where(mask, -sp, 0.0)
    return kb, mask, lk, z - sp


def sb_attention_fwd(proj, q_col, k_col, v_col, n_heads, *, blk, name):
    s = proj.shape[0]
    nq = s // blk
    scale = HEAD ** -0.5

    def body(q_ref, k_ref, v_ref, o_ref, r_ref, n_ref):
        i = pl.program_id(1)
        q = q_ref[...].astype(BF16)
        row = i * blk + lax.broadcasted_iota(jnp.int32, (blk, blk), 0)
        rr = lax.broadcasted_iota(jnp.int32, (blk, blk), 0)
        cc = lax.broadcasted_iota(jnp.int32, (blk, blk), 1)
        tri_after = (rr > cc).astype(BF16)

        def alive(carry):
            jj, _, run = carry
            return jnp.logical_and(jj <= i, jnp.max(run) > SB_DEAD)

        def step(carry):
            jj, acc, run = carry
            j = i - jj
            _, mask, lk, ls = _sb_scores(q, k_ref, j, blk, row, scale)
            later = _split_dot(lk, tri_after, 2) + run
            w = jnp.where(mask, jnp.exp(ls + later), 0.0)
            vb = v_ref[pl.ds(pl.multiple_of(j * blk, blk), blk), :].astype(BF16)
            return jj + 1, acc + _dot(w.astype(BF16), vb), run + jnp.sum(lk, axis=1, keepdims=True)

        n, acc, run = lax.while_loop(
            alive, step, (jnp.int32(0), jnp.zeros((blk, HEAD), F32), jnp.zeros((blk, 1), F32)))
        o_ref[...] = acc.astype(o_ref.dtype)
        r_ref[...] = jnp.broadcast_to(run, (blk, HEAD))
        n_ref[pl.program_id(0), i] = n

    blk_spec = lambda c0: pl.BlockSpec((blk, HEAD), lambda h, i: (i, c0 + h))
    full_spec = lambda c0: pl.BlockSpec((s, HEAD), lambda h, i: (0, c0 + h))
    out_spec = pl.BlockSpec((blk, HEAD), lambda h, i: (i, h))
    return pl.pallas_call(
        body, grid=(n_heads, nq),
        in_specs=[blk_spec(q_col), full_spec(k_col), full_spec(v_col)],
        out_specs=[out_spec, out_spec, pl.BlockSpec(memory_space=pltpu.SMEM)],
        out_shape=[jax.ShapeDtypeStruct((s, n_heads * HEAD), BF16),
                   jax.ShapeDtypeStruct((s, n_heads * HEAD), F32),
                   jax.ShapeDtypeStruct((n_heads, nq), jnp.int32)],
        name=name, compiler_params=_cparams(("arbitrary", "arbitrary")),
    )(proj, proj, proj)


def sb_attention_bwd(proj, d_out, run_tot, visited, q_col, k_col, v_col, n_heads, *, blk, name):
    s = proj.shape[0]
    nq = s // blk
    scale = HEAD ** -0.5

    def body(n_ref, q_ref, k_ref, v_ref, do_ref, r_ref, dq_ref, dk_ref, dv_ref, dk_acc, dv_acc):
        i = pl.program_id(1)
        first = i + 1 - jnp.clip(n_ref[pl.program_id(0), i], 1, i + 1)

        @pl.when(i == 0)
        def _():
            dk_acc[...] = jnp.zeros_like(dk_acc)
            dv_acc[...] = jnp.zeros_like(dv_acc)

        q = q_ref[...].astype(BF16)
        do = do_ref[...].astype(BF16)
        rtot = r_ref[:, :1]
        row = i * blk + lax.broadcasted_iota(jnp.int32, (blk, blk), 0)
        rr = lax.broadcasted_iota(jnp.int32, (blk, blk), 0)
        cc = lax.broadcasted_iota(jnp.int32, (blk, blk), 1)
        tri_upto = (rr <= cc).astype(BF16)
        tri_before = (rr < cc).astype(BF16)

        def step(j, carry):
            dq, pre, gpre = carry
            kb, mask, lk, ls = _sb_scores(q, k_ref, j, blk, row, scale)
            rows = pl.ds(pl.multiple_of(j * blk, blk), blk)
            vb = v_ref[rows, :].astype(BF16)
            later = rtot - (pre + _split_dot(lk, tri_upto, 2))
            w = jnp.where(mask, jnp.exp(ls + later), 0.0)
            g = w * _dot(do, vb, _NT)
            g_before = _split_dot(g, tri_before, 1) + gpre
            sig = jnp.exp(ls)
            dz = (jnp.where(mask, g * (1.0 - sig) - sig * g_before, 0.0) * scale).astype(BF16)
            dk_acc[rows, :] += _dot(dz, q, _TN)
            dv_acc[rows, :] += _dot(w.astype(BF16), do, _TN)
            return (dq + _dot(dz, kb), pre + jnp.sum(lk, axis=1, keepdims=True),
                    gpre + jnp.sum(g, axis=1, keepdims=True))

        zero = jnp.zeros((blk, 1), F32)
        dq, _, _ = lax.fori_loop(first, i + 1, step, (jnp.zeros((blk, HEAD), F32), zero, zero))
        dq_ref[...] = dq.astype(dq_ref.dtype)

        @pl.when(i == nq - 1)
        def _():
            dk_ref[...] = dk_acc[...].astype(dk_ref.dtype)
            dv_ref[...] = dv_acc[...].astype(dv_ref.dtype)

    blk_spec = lambda c0: pl.BlockSpec((blk, HEAD), lambda h, i: (i, c0 + h))
    full_spec = lambda c0: pl.BlockSpec((s, HEAD), lambda h, i: (0, c0 + h))
    o_shape = jax.ShapeDtypeStruct((s, n_heads * HEAD), BF16)
    do_spec = blk_spec(0)
    if isinstance(d_out, tuple):
        d_out, lead = d_out
        do_spec = pl.BlockSpec((None, blk, HEAD), lambda h, i: (lead, i, h))
    return pl.pallas_call(
        body, grid=(n_heads, nq),
        in_specs=[pl.BlockSpec(memory_space=pltpu.SMEM), blk_spec(q_col), full_spec(k_col), full_spec(v_col),
                  do_spec, blk_spec(0)],
        out_specs=[blk_spec(0), full_spec(0), full_spec(0)],
        out_shape=[o_shape, o_shape, o_shape],
        scratch_shapes=[pltpu.VMEM((s, HEAD), F32), pltpu.VMEM((s, HEAD), F32)],
        name=name, compiler_params=_cparams(("parallel", "arbitrary")),
    )(visited, proj, proj, proj, d_out, run_tot)


def _hosting(hosted):
    arrs = [arr for _, arr in hosted]
    shapes = [_comm_out_shape(kind, arr) for kind, arr in hosted]

    def ops(src_refs, dst_refs, sems):
        return [_COMM[kind](src_refs[c], dst_refs[c], *sems[3 * c:3 * c + 3]) for c, (kind, _) in enumerate(hosted)]

    return arrs, [_HBM] * len(hosted), shapes, _COMM_SEMS * len(hosted), ops


def ffn_up_swiglu(h, w_gu, *, tm, hosted=(), name):
    s, d = h.shape
    tm = _pick(s, tm)
    nb, _, hb, _ = w_gu.shape
    n_host = len(hosted)
    h_arrs, h_specs, h_shapes, h_sems, h_ops = _hosting(hosted)
    grid = (s // tm, nb)

    def body(h_ref, w_ref, *rest):
        src_refs, (gu_ref, act_ref) = rest[:n_host], rest[n_host:n_host + 2]
        dst_refs, sems = rest[n_host + 2:2 * n_host + 2], rest[2 * n_host + 2:]
        comms = h_ops(src_refs, dst_refs, sems)
        i, g = pl.program_id(0), pl.program_id(1)

        if hosted:
            @pl.when(jnp.logical_and(i == 0, g == 0))
            def _():
                for start, _ in comms:
                    start()

        a = h_ref[...]
        gate = _dot(a, w_ref[0], _NT)
        up = _dot(a, w_ref[1], _NT)
        gu_ref[0] = gate
        gu_ref[1] = up
        act_ref[...] = (gate * jax.nn.sigmoid(gate) * up).astype(act_ref.dtype)

        if hosted:
            @pl.when(jnp.logical_and(i == grid[0] - 1, g == grid[1] - 1))
            def _():
                for _, finish in comms:
                    finish()

    res = pl.pallas_call(
        body, grid=grid,
        in_specs=[pl.BlockSpec((tm, d), lambda i, g: (i, 0)),
                  pl.BlockSpec((None, 2, hb, d), lambda i, g: (g, 0, 0, 0))] + h_specs,
        out_specs=[pl.BlockSpec((None, 2, tm, hb), lambda i, g: (g, 0, i, 0)),
                   pl.BlockSpec((None, tm, hb), lambda i, g: (g, i, 0))] + h_specs,
        out_shape=[jax.ShapeDtypeStruct((nb, 2, s, hb), F32), jax.ShapeDtypeStruct((nb, s, hb), BF16)] + h_shapes,
        scratch_shapes=h_sems, name=name,
        compiler_params=_cparams(("arbitrary", "arbitrary") if hosted else ("parallel", "parallel")),
    )(h, w_gu, *h_arrs)
    return res[0], res[1], res[2:]


def branch_merge(y3, w3, proj, gate_col, biases, *, tm, tn, name):
    nbr, s, k = y3.shape
    n = w3.shape[2]
    tm, tn = _pick(s, tm), _pick(n, tn)
    per = n // tn

    def body(y_ref, w_ref, g0, g1, g2, b0, b1, b2, m_ref, u_ref):
        merged = None
        for i, (g_ref, b_ref) in enumerate(((g0, b0), (g1, b1), (g2, b2))):
            u = _dot(y_ref[i], w_ref[i])
            u_ref[i] = u
            part = jax.nn.sigmoid(g_ref[...] + b_ref[...]) * u
            merged = part if merged is None else merged + part
        m_ref[...] = merged.astype(m_ref.dtype)

    gate_spec = lambda i: pl.BlockSpec((tm, tn), lambda r, j: (r, (gate_col + i) * per + j))
    bias_spec = pl.BlockSpec((1, tn), lambda r, j: (0, j))
    return pl.pallas_call(
        body, grid=(s // tm, per),
        in_specs=[pl.BlockSpec((nbr, tm, k), lambda r, j: (0, r, 0)), pl.BlockSpec((nbr, k, tn), lambda r, j: (0, 0, j)),
                  gate_spec(0), gate_spec(1), gate_spec(2), bias_spec, bias_spec, bias_spec],
        out_specs=[pl.BlockSpec((tm, tn), lambda r, j: (r, j)), pl.BlockSpec((nbr, tm, tn), lambda r, j: (0, r, j))],
        out_shape=[jax.ShapeDtypeStruct((s, n), BF16), jax.ShapeDtypeStruct((nbr, s, n), F32)],
        name=name, compiler_params=_cparams(("parallel", "parallel")),
    )(y3, w3, proj, proj, proj, *biases)


def ffn_down_dx_swiglu(df, w_down, gu, *, tm, name):
    s, d = df.shape
    tm = _pick(s, tm)
    nb, hb, _ = w_down.shape

    def body(df_ref, w_ref, gu_ref, dgu_ref):
        dact = _dot(df_ref[...], w_ref[...], _NT)
        gate, up = gu_ref[0], gu_ref[1]
        sig = jax.nn.sigmoid(gate)
        dgu_ref[0] = (dact * up * (sig * (1.0 + gate * (1.0 - sig)))).astype(dgu_ref.dtype)
        dgu_ref[1] = (dact * gate * sig).astype(dgu_ref.dtype)

    blk = pl.BlockSpec((None, 2, tm, hb), lambda i, g: (g, 0, i, 0))
    return pl.pallas_call(
        body, grid=(s // tm, nb),
        in_specs=[pl.BlockSpec((tm, d), lambda i, g: (i, 0)), pl.BlockSpec((None, hb, d), lambda i, g: (g, 0, 0)), blk],
        out_specs=blk, out_shape=jax.ShapeDtypeStruct(gu.shape, BF16),
        name=name, compiler_params=_cparams(("parallel", "parallel")),
    )(df, w_down, gu)


def loss_head(y, target, *, tr, name):
    s, d = y.shape

    def body(y_ref, t_ref, l_ref, dy_ref):
        err = y_ref[...] - t_ref[...]
        dy_ref[...] = err * (1.0 / d)
        part = 0.5 * jnp.sum(jnp.mean(err * err, axis=-1, keepdims=True), axis=0, keepdims=True)

        @pl.when(pl.program_id(0) == 0)
        def _():
            l_ref[...] = jnp.zeros_like(l_ref)

        l_ref[...] += jnp.broadcast_to(part, l_ref.shape)

    row = pl.BlockSpec((tr, d), lambda i: (i, 0))
    return pl.pallas_call(
        body, grid=(s // tr,), in_specs=[row, row],
        out_specs=[pl.BlockSpec((8, LANE), lambda i: (0, 0)), row],
        out_shape=[jax.ShapeDtypeStruct((8, LANE), F32), jax.ShapeDtypeStruct((s, d), F32)],
        name=name, compiler_params=_cparams(("arbitrary",)),
    )(y, target)


def _adamw_math(w, g, m, v):
    m = ADAM_B1 * m + (1.0 - ADAM_B1) * g
    v = ADAM_B2 * v + (1.0 - ADAM_B2) * jnp.square(g)
    m_hat = m / (1.0 - ADAM_B1 ** ADAM_STEP)
    v_hat = v / (1.0 - ADAM_B2 ** ADAM_STEP)
    delta = -ADAM_LR * (m_hat / (jnp.sqrt(v_hat) + ADAM_EPS) + ADAM_WD * w)
    return delta, m, v


def adamw_sum(w, m, v, parts, *, tr, name):
    n, r, c = parts.shape
    tr = _pick_rows(r, tr)

    def body(w_ref, m_ref, v_ref, p_ref, g_ref, d_ref, nm_ref, nv_ref):
        g = p_ref[0].astype(F32)
        for i in range(1, n):
            g = g + p_ref[i].astype(F32)
        delta, nm, nv = _adamw_math(w_ref[...], g, m_ref[...], v_ref[...])
        g_ref[...] = g
        d_ref[...] = delta
        nm_ref[...] = nm
        nv_ref[...] = nv

    row = pl.BlockSpec((tr, c), lambda i: (i, 0))
    shape = jax.ShapeDtypeStruct((r, c), F32)
    return pl.pallas_call(
        body, grid=(r // tr,),
        in_specs=[row, row, row, pl.BlockSpec((n, tr, c), lambda i: (0, i, 0))],
        out_specs=[row] * 4, out_shape=[shape] * 4,
        name=name, compiler_params=_cparams(("parallel",)),
    )(w, m, v, parts)


def adamw_into(w_all, m_all, v_all, parts, layer, row_off, prev, *, tr, name):
    depth, r, c = w_all.shape
    n = parts.shape[0]
    if r % 16 == 0:
        tr = _pick_rows(r, tr)
        assert row_off % tr == 0
        off = row_off // tr
        grid = (r // tr,)
        lay = pl.BlockSpec((None, tr, c), lambda i: (layer, i, 0))
        p_spec = pl.BlockSpec((n, tr, c), lambda i: (0, off + i, 0))
    else:
        assert row_off == 0 and parts.shape[1] == r
        grid = (c // LANE,)
        lay = pl.BlockSpec((None, r, LANE), lambda j: (layer, 0, j))
        p_spec = pl.BlockSpec((n, r, LANE), lambda j: (0, 0, j))
    if prev is None:
        prev = [lax.empty(w_all.shape, F32) for _ in range(4)]

    def body(w_ref, m_ref, v_ref, p_ref, _g, _d, _m, _v, g_ref, d_ref, nm_ref, nv_ref):
        g = p_ref[0].astype(F32)
        for i in range(1, n):
            g = g + p_ref[i].astype(F32)
        delta, nm, nv = _adamw_math(w_ref[...], g, m_ref[...], v_ref[...])
        g_ref[...] = g
        d_ref[...] = delta
        nm_ref[...] = nm
        nv_ref[...] = nv

    untouched = pl.BlockSpec(memory_space=pl.ANY)
    return pl.pallas_call(
        body, grid=grid,
        in_specs=[lay, lay, lay, p_spec] + [untouched] * 4,
        out_specs=[lay] * 4, out_shape=[jax.ShapeDtypeStruct(w_all.shape, F32)] * 4,
        input_output_aliases={4: 0, 5: 1, 6: 2, 7: 3},
        name=name, compiler_params=_cparams(("parallel",)),
    )(w_all, m_all, v_all, parts, *prev)


def _pick_rows(r, pref):
    t = min(pref, r)
    while r % t or (t % 16 and t != r):
        t -= 1
    return t


_HBM = pl.BlockSpec(memory_space=pltpu.HBM)
_MESH = pl.DeviceIdType.MESH


def _flat_index(px, py, pc):
    return 4 * px + 2 * py + pc


def _gather_ops(x_ref, out_ref, send_sems, recv_sems, local_sem):
    x, y, c = lax.axis_index("x"), lax.axis_index("y"), lax.axis_index("c")
    me, sibling = (x, y, c), (x, y, 1 - c)
    chips = [(1 - x, y), (x, 1 - y), (1 - x, 1 - y)]

    def slot(p):
        return out_ref.at[_flat_index(*p)]

    def copy(k, block, to, src=None):
        return pltpu.make_async_remote_copy(
            src_ref=slot(block) if src is None else src, dst_ref=slot(block),
            send_sem=send_sems.at[k], recv_sem=recv_sems.at[k], device_id=to, device_id_type=_MESH)

    mine = pltpu.make_async_copy(x_ref, slot(me), local_sem)
    first = [copy(0, me, sibling, src=x_ref)]
    first += [copy(1 + j, me, (*chip, c), src=x_ref) for j, chip in enumerate(chips)]
    passed = [copy(4 + j, (*chip, c), sibling) for j, chip in enumerate(chips)]

    def start():
        mine.start()
        for cp in first:
            cp.start()

    def finish():
        for j, chip in enumerate(chips):
            copy(1 + j, (*chip, c), me).wait_recv()
            passed[j].start()
        copy(0, sibling, me).wait_recv()
        for j, chip in enumerate(chips):
            copy(4 + j, (*chip, 1 - c), me).wait_recv()
        for cp in first + passed:
            cp.wait_send()
        mine.wait()

    return start, finish


def _exchange_ops(p_ref, out_ref, send_sems, recv_sems, local_sem):
    x, y, c = lax.axis_index("x"), lax.axis_index("y"), lax.axis_index("c")
    me = _flat_index(x, y, c)
    peers = [(1 - x if k & 4 else x, 1 - y if k & 2 else y, 1 - c if k & 1 else c) for k in range(1, N_DEV)]

    def copy(k, peer, dst_slot):
        return pltpu.make_async_remote_copy(
            src_ref=p_ref.at[_flat_index(*peer)], dst_ref=out_ref.at[dst_slot],
            send_sem=send_sems.at[k], recv_sem=recv_sems.at[k], device_id=peer, device_id_type=_MESH)

    mine = pltpu.make_async_copy(p_ref.at[me], out_ref.at[me], local_sem)
    sends = [copy(k, peer, me) for k, peer in enumerate(peers)]

    def start():
        mine.start()
        for cp in sends:
            cp.start()

    def finish():
        for k, peer in enumerate(peers):
            copy(k, peer, _flat_index(*peer)).wait_recv()
        for cp in sends:
            cp.wait_send()
        mine.wait()

    return start, finish


_COMM = {"gather": _gather_ops, "exchange": _exchange_ops}
_COMM_SEMS = [pltpu.SemaphoreType.DMA((N_DEV - 1,)), pltpu.SemaphoreType.DMA((N_DEV - 1,)), pltpu.SemaphoreType.DMA]


def _comm_out_shape(kind, arr):
    return jax.ShapeDtypeStruct(((N_DEV,) + arr.shape) if kind == "gather" else arr.shape, arr.dtype)


def _comm_call(kind, arr, name):
    def body(src_ref, dst_ref, send_sems, recv_sems, local_sem):
        start, finish = _COMM[kind](src_ref, dst_ref, send_sems, recv_sems, local_sem)
        start()
        finish()

    return pl.pallas_call(body, out_shape=_comm_out_shape(kind, arr), in_specs=[_HBM], out_specs=_HBM,
                          scratch_shapes=_COMM_SEMS, name=name)(arr)


def all_gather(x, *, name):
    return _comm_call("gather", x, name)


def exchange(parts, *, name):
    return _comm_call("exchange", parts, name)


TR = 256
TR_WIDE = 128
BLK = 256
SB_BLK = 256
FFN_TM = 1024
CONV_CHUNK = 512
DT_PAD = LANE

BIG = ("w_in", "w_branch", "w_out", "ffn_w_gu", "ffn_w_down")
SMALL = ("norm_mix_pre", "norm_mix_post", "norm_ffn_pre", "norm_ffn_post", "b_gate", "ret_gn_w", "ssd_conv_w",
         "ssd_conv_b", "ssd_dt_bias", "ssd_a_log", "ssd_d", "ssd_norm_w")


def _row(v):
    return v.reshape(1, -1)


def _pad_lanes(v):
    return jnp.pad(v.reshape(1, -1), ((0, 0), (0, LANE - v.shape[-1])))


def _assemble_w_in(g, d):
    full = g.reshape(-1, d)
    n_main, n_dt = 5 * d, full.shape[0] - 8 * d
    main, dt, gates = full[:n_main], full[n_main:n_main + n_dt], full[n_main + n_dt:]
    return jnp.concatenate([main, gates], axis=0), jnp.pad(dt, ((0, DT_PAD - n_dt), (0, 0)))


def _split_dw_in(dw, dw_dt, d, n_dt):
    n_main = 5 * d
    full = jnp.concatenate([dw[:n_main], dw_dt[:n_dt], dw[n_main:]], axis=0)
    return full.reshape(N_DEV, -1, d)


def _layer_fwd(x, lw, cosf, sinf, next_shards):
    s, d = x.shape
    heads = d // 2 // HEAD
    nxt = {}
    (h,) = rowwise(f_rms_pre, [x], [lw["norm_mix_pre"]], [(d, BF16)], tr=TR, name="mix_pre_norm")
    if next_shards is None:
        proj = matmul(h, lw["w_cat"], tb=True, tn=2048, name="in_proj")
    else:
        proj, (nxt["w_in"],) = matmul(h, lw["w_cat"], tb=True, tn=2048, hosted=[("gather", next_shards["w_in"])],
                                      name="in_proj_gather")
    dt_raw = matmul(h, lw["w_dt"], tb=True, name="in_proj_dt")
    ret_cols = tuple(i * heads for i in range(4))
    y_ret, ret_states = retention_fwd(proj, ret_cols, lw["ret_gn_w"], cosf, sinf, heads, blk=BLK, name="retention_fwd")
    y_sb, sb_run, sb_visited = sb_attention_fwd(proj, 4 * heads, 5 * heads, 6 * heads, heads, blk=SB_BLK,
                                                name="stickbreak_fwd")
    u_pad = jnp.pad(proj[:, 4 * d:5 * d], ((CONV_PAD, CONV_PAD), (0, 0)))
    xc = ssd_conv_fwd(u_pad, lw["conv_taps"], lw["ssd_conv_b"], chunk=CONV_CHUNK, name="ssd_conv_fwd")
    y_scan, ssd_states = ssd_scan_fwd(xc, dt_raw, 0, lw["ssd_dt_bias"], lw["ssd_a_log"], lw["ssd_d"], blk=BLK,
                                      name="ssd_scan_fwd")
    z_spec = (proj, d // 2, 7)
    (y_ssd,) = rowwise(f_ssd_gate, [y_scan, z_spec], [lw["ssd_norm_w"]], [(d // 2, BF16)], tr=TR, name="ssd_gate_norm")
    y3 = jnp.stack([y_ret, y_sb, y_ssd])
    merged, u3 = branch_merge(y3, lw["w_branch"], proj, 5, lw["b_gate"], tm=512, tn=1024, name="branch_merge")
    o = matmul(merged, lw["w_out"], name="out_proj")
    (x1,) = rowwise(f_rms_post, [x, o], [lw["norm_mix_post"]], [(d, F32)], tr=TR, name="mix_post_norm")
    (h2,) = rowwise(f_rms_pre, [x1], [lw["norm_ffn_pre"]], [(d, BF16)], tr=TR, name="ffn_pre_norm")
    w_gu = lw["ffn_w_gu"].reshape(N_DEV, 2, -1, d)
    if next_shards is None:
        gu, act, _ = ffn_up_swiglu(h2, w_gu, tm=FFN_TM, name="ffn_up_swiglu")
    else:
        rest = [n for n in BIG if n != "w_in"]
        gu, act, got = ffn_up_swiglu(h2, w_gu, tm=FFN_TM, hosted=[("gather", next_shards[n]) for n in rest],
                                     name="ffn_up_swiglu_gather")
        nxt.update(zip(rest, got))
    f = matmul(act, lw["ffn_w_down"], lead_a="k", lead_b="k", name="ffn_down")
    (x2,) = rowwise(f_rms_post, [x1, f], [lw["norm_ffn_post"]], [(d, F32)], tr=TR, name="ffn_post_norm")
    res = dict(x=x, h=h, proj=proj, dt_raw=dt_raw, ret_states=ret_states, sb_run=sb_run, sb_visited=sb_visited, xc=xc,
               u_pad=u_pad,
               y_scan=y_scan, ssd_states=ssd_states, y3=y3, u3=u3, merged=merged, o=o, x1=x1, h2=h2, gu=gu, act=act, f=f)
    return x2, res, (nxt if next_shards is not None else None)


def _layer_bwd(dx2, res, lw, cosf, sinf, pending):
    x, proj = res["x"], res["proj"]
    s, d = x.shape
    heads = d // 2 // HEAD
    n_dt = d // 2 // SSD_P
    got = {}
    df, dn_ffn_post = rowwise_vjp(f_rms_post, [res["x1"], res["f"]], [lw["norm_ffn_post"]], [dx2],
                                  [(1, BF16, None)], [0], tr=TR, name="ffn_post_norm_bwd")
    dw_down = matmul(res["act"], df, ta=True, lead_a="batch", out_dtype=BF16, name="ffn_down_dw")
    dgu = ffn_down_dx_swiglu(df, lw["ffn_w_down"], res["gu"], tm=FFN_TM, name="ffn_down_dx_swiglu")
    dgu = dgu.reshape(2 * N_DEV, s, -1)
    if pending is None:
        dh2 = matmul(dgu, lw["ffn_w_gu"], lead_a="k", lead_b="k", name="ffn_up_dx")
        dw_gu = matmul(dgu, res["h2"], ta=True, lead_a="batch", out_dtype=BF16, name="ffn_up_dw")
    else:
        dh2, (got["prev_w_branch"], got["prev_w_out"]) = matmul(
            dgu, lw["ffn_w_gu"], lead_a="k", lead_b="k",
            hosted=[("exchange", pending["w_branch"]), ("exchange", pending["w_out"])], name="ffn_up_dx_exchange")
        dw_gu, (got["prev_w_in"],) = matmul(dgu, res["h2"], ta=True, lead_a="batch", out_dtype=BF16,
                                            hosted=[("exchange", pending["w_in"])], name="ffn_up_dw_exchange")
    dw_gu = dw_gu.reshape(N_DEV, -1, d)
    dx1, dn_ffn_pre = rowwise_vjp(f_rms_pre, [res["x1"]], [lw["norm_ffn_pre"]], [dh2], [(0, F32, dx2)], [0],
                                  tr=TR, name="ffn_pre_norm_bwd")
    do, dn_mix_post = rowwise_vjp(f_rms_post, [x, res["o"]], [lw["norm_mix_post"]], [dx1], [(1, BF16, None)], [0],
                                  tr=TR, name="mix_post_norm_bwd")
    dmerged = matmul(do, lw["w_out"], tb=True, name="out_proj_dx")
    dw_out = matmul(res["merged"], do, ta=True, out_dtype=BF16, name="out_proj_dw")
    merge_rows = [(res["u3"], d, 0, i) for i in range(3)] + [(proj, d, 5 + i) for i in range(3)]
    mg = rowwise_vjp(f_merge, merge_rows, lw["b_gate"], [dmerged], [(i, BF16, None) for i in range(6)], [0, 1, 2],
                     tr=TR_WIDE, name="gate_merge_bwd")
    du3 = jnp.stack(mg[:3])
    d_gate_logits, db_gate = mg[3:6], mg[6:9]
    dy3 = matmul(du3, lw["w_branch"], tb=True, lead_a="batch", lead_b="batch", name="branch_proj_dx")
    dw_branch = matmul(res["y3"], du3, ta=True, lead_a="batch", lead_b="batch", out_dtype=BF16, name="branch_proj_dw")
    ret_cols = tuple(i * heads for i in range(4))
    dq, dk, dv, dg, d_gn = retention_bwd(proj, ret_cols, lw["ret_gn_w"], cosf, sinf, res["ret_states"], (dy3, 0), heads,
                                         blk=BLK, name="retention_bwd")
    dsq, dsk, dsv = sb_attention_bwd(proj, (dy3, 1), res["sb_run"], res["sb_visited"], 4 * heads, 5 * heads, 6 * heads,
                                     heads, blk=SB_BLK, name="stickbreak_bwd")
    z_spec = (proj, d // 2, 7)
    dy_scan, dz, d_ssd_norm = rowwise_vjp(f_ssd_gate, [res["y_scan"], z_spec], [lw["ssd_norm_w"]], [(dy3, d // 2, 0, 2)],
                                          [(0, F32, None), (1, BF16, None)], [0], tr=TR, name="ssd_gate_norm_bwd")
    dxs, dbm, dcm, ddt, d_dtb, d_alog, d_dskip = ssd_scan_bwd(
        res["xc"], res["dt_raw"], 0, lw["ssd_dt_bias"], lw["ssd_a_log"], lw["ssd_d"], res["ssd_states"], dy_scan,
        blk=BLK, name="ssd_scan_bwd")
    dxc_pad = jnp.pad(jnp.concatenate([dxs, dbm, dcm], axis=1), ((0, CONV_PAD), (0, 0)))
    du, d_taps, d_conv_b = ssd_conv_bwd(res["u_pad"], lw["conv_taps"], lw["ssd_conv_b"], dxc_pad, chunk=CONV_CHUNK,
                                        name="ssd_conv_bwd")
    dproj = jnp.concatenate([dq, dk, dv, dg, dsq, dsk, dsv, dz, du, *d_gate_logits], axis=1)
    dh_dt = matmul(ddt, lw["w_dt"], name="in_proj_dt_dx")
    dh, (got["ffn_w_down"], got["ffn_w_gu"]) = matmul(
        dproj, lw["w_cat"], add=dh_dt, hosted=[("exchange", dw_down), ("exchange", dw_gu)],
        name="in_proj_dx_exchange")
    dw_cat = matmul(dproj, res["h"], ta=True, tn=2048, out_dtype=BF16, name="in_proj_dw")
    dw_dt = matmul(ddt, res["h"], ta=True, out_dtype=BF16, name="in_proj_dt_dw")
    dx, dn_mix_pre = rowwise_vjp(f_rms_pre, [x], [lw["norm_mix_pre"]], [dh], [(0, F32, dx1)], [0], tr=TR,
                                 name="mix_pre_norm_bwd")
    mine = dict(
        w_in=_split_dw_in(dw_cat, dw_dt, d, n_dt),
        w_branch=jnp.transpose(dw_branch.reshape(3, d // 2, N_DEV, -1), (2, 0, 1, 3)).reshape(N_DEV, 3 * d // 2, -1),
        w_out=dw_out.reshape(N_DEV, d // N_DEV, d),
    )
    small = dict(
        norm_mix_pre=dn_mix_pre[0], norm_mix_post=dn_mix_post[0], norm_ffn_pre=dn_ffn_pre[0],
        norm_ffn_post=dn_ffn_post[0], b_gate=jnp.concatenate([b[0] for b in db_gate]), ret_gn_w=d_gn[0],
        ssd_conv_w=d_taps, ssd_conv_b=d_conv_b[0], ssd_dt_bias=d_dtb[0, :n_dt], ssd_a_log=d_alog[0, :n_dt],
        ssd_d=d_dskip[0, :n_dt], ssd_norm_w=d_ssd_norm[0],
    )
    return dx, got, mine, small


def _adam_rows(cols):
    return max(8, (1 << 17) // cols // 8 * 8)


def kernel(x, positions, norm_mix_pre, norm_mix_post, norm_ffn_pre, norm_ffn_post, w_in, b_gate, ret_gn_w, ssd_conv_w, ssd_conv_b, ssd_dt_bias, ssd_a_log, ssd_d, ssd_norm_w, w_branch_ret, w_branch_sb, w_branch_ssd, w_out, ffn_w_gate, ffn_w_up, ffn_w_down, loss_target, m_norm_mix_pre, m_norm_mix_post, m_norm_ffn_pre, m_norm_ffn_post, m_w_in, m_b_gate, m_ret_gn_w, m_ssd_conv_w, m_ssd_conv_b, m_ssd_dt_bias, m_ssd_a_log, m_ssd_d, m_ssd_norm_w, m_w_branch_ret, m_w_branch_sb, m_w_branch_ssd, m_w_out, m_ffn_w_gate, m_ffn_w_up, m_ffn_w_down, v_norm_mix_pre, v_norm_mix_post, v_norm_ffn_pre, v_norm_ffn_post, v_w_in, v_b_gate, v_ret_gn_w, v_ssd_conv_w, v_ssd_conv_b, v_ssd_dt_bias, v_ssd_a_log, v_ssd_d, v_ssd_norm_w, v_w_branch_ret, v_w_branch_sb, v_w_branch_ssd, v_w_out, v_ffn_w_gate, v_ffn_w_up, v_ffn_w_down):
    depth = w_in.shape[0]
    s, d = x.shape[1], x.shape[2]
    axes = ("x", "y", "c")
    me = _flat_index(lax.axis_index("x"), lax.axis_index("y"), lax.axis_index("c"))

    def tr_(a):
        return jnp.swapaxes(a, 1, 2)

    transposed = ("w_in", "ffn_w_gate", "ffn_w_up")
    wmv = dict(w_in=tuple(map(tr_, (w_in, m_w_in, v_w_in))), w_branch_ret=(w_branch_ret, m_w_branch_ret, v_w_branch_ret),
               w_branch_sb=(w_branch_sb, m_w_branch_sb, v_w_branch_sb),
               w_branch_ssd=(w_branch_ssd, m_w_branch_ssd, v_w_branch_ssd), w_out=(w_out, m_w_out, v_w_out),
               ffn_w_gate=tuple(map(tr_, (ffn_w_gate, m_ffn_w_gate, v_ffn_w_gate))),
               ffn_w_up=tuple(map(tr_, (ffn_w_up, m_ffn_w_up, v_ffn_w_up))),
               ffn_w_down=(ffn_w_down, m_ffn_w_down, v_ffn_w_down))
    members = dict(w_in=["w_in"], w_branch=["w_branch_ret", "w_branch_sb", "w_branch_ssd"], w_out=["w_out"],
                   ffn_w_gu=["ffn_w_gate", "ffn_w_up"], ffn_w_down=["ffn_w_down"])
    big_w = {g: (wmv[ns[0]][0] if len(ns) == 1 else jnp.concatenate([wmv[n][0] for n in ns], axis=1))
             for g, ns in members.items()}

    taps_all = all_gather(ssd_conv_w.reshape(-1, LANE), name="gather_conv_w")
    taps_all = jnp.transpose(taps_all.reshape(N_DEV, depth, SSD_K, -1), (1, 2, 0, 3)).reshape(depth, SSD_K, -1)

    cosf, sinf = rope_tables(positions.reshape(s), tr=TR)
    small_w = dict(norm_mix_pre=norm_mix_pre, norm_mix_post=norm_mix_post, norm_ffn_pre=norm_ffn_pre,
                   norm_ffn_post=norm_ffn_post, b_gate=b_gate, ret_gn_w=ret_gn_w, ssd_conv_b=ssd_conv_b,
                   ssd_dt_bias=ssd_dt_bias, ssd_a_log=ssd_a_log, ssd_d=ssd_d, ssd_norm_w=ssd_norm_w, taps=taps_all)

    def layer_weights(sw, gathered):
        lw = dict(gathered)
        for n in ("norm_mix_pre", "norm_mix_post", "norm_ffn_pre", "norm_ffn_post", "ret_gn_w", "ssd_conv_b", "ssd_norm_w"):
            lw[n] = _row(sw[n])
        for n in ("ssd_dt_bias", "ssd_a_log", "ssd_d"):
            lw[n] = _pad_lanes(sw[n])
        lw["b_gate"] = [_row(sw["b_gate"][i * d:(i + 1) * d]) for i in range(3)]
        lw["conv_taps"] = [sw["taps"][k:k + 1] for k in range(SSD_K)]
        return lw

    def layer_slice(t, l):
        return {n: a[l] for n, a in t.items()}

    def bf16_shards(l):
        return {n: big_w[n][l].astype(BF16) for n in BIG}

    def arrange(g):
        w_cat, w_dt = _assemble_w_in(g["w_in"], d)
        return dict(
            w_cat=w_cat, w_dt=w_dt,
            w_branch=jnp.transpose(g["w_branch"].reshape(N_DEV, 3, d // 2, -1), (1, 2, 0, 3)).reshape(3, d // 2, d),
            w_out=g["w_out"].reshape(d, d),
            ffn_w_gu=g["ffn_w_gu"].reshape(2 * N_DEV, -1, d),
            ffn_w_down=g["ffn_w_down"],
        )

    big_out = {}

    def adam(l, group, parts):
        rows, cols = big_w[group].shape[1:]
        parts = parts.reshape(N_DEV, rows, cols)
        off = 0
        for n in members[group]:
            big_out[n] = adamw_into(*wmv[n], parts, l, off, big_out.get(n), tr=_adam_rows(cols), name="adamw_" + n)
            off += wmv[n][0].shape[1]

    xs, saved = x.reshape(s, d), []
    gathered = {n: all_gather(a, name="gather_" + n) for n, a in bf16_shards(0).items()}
    for l in range(depth):
        lw = layer_weights(layer_slice(small_w, l), arrange(gathered))
        xs, res, gathered = _layer_fwd(xs, lw, cosf, sinf, bf16_shards(l + 1) if l + 1 < depth else None)
        saved.append((res, lw))
    loss_tile, dy = loss_head(xs, loss_target.reshape(s, d), tr=TR, name="loss_head")
    loss = lax.psum(loss_tile[0, 0], axes)

    dx, pending, small_layers = dy, None, [None] * depth
    for l in reversed(range(depth)):
        res, lw = saved[l]
        dx, got, pending, small_layers[l] = _layer_bwd(dx, res, lw, cosf, sinf, pending)
        for n in ("ffn_w_gu", "ffn_w_down"):
            adam(l, n, got[n])
        if l + 1 < depth:
            for n in ("w_in", "w_branch", "w_out"):
                adam(l + 1, n, got["prev_" + n])
    for n in ("w_in", "w_branch", "w_out"):
        adam(0, n, exchange(pending[n], name="exchange_" + n))
    small_g = {n: jnp.stack([small_layers[l][n] for l in range(depth)]) for n in SMALL}

    n_dt = ssd_dt_bias.shape[-1]
    small_in = dict(norm_mix_pre=(norm_mix_pre, m_norm_mix_pre, v_norm_mix_pre), norm_mix_post=(norm_mix_post, m_norm_mix_post, v_norm_mix_post),
                    norm_ffn_pre=(norm_ffn_pre, m_norm_ffn_pre, v_norm_ffn_pre), norm_ffn_post=(norm_ffn_post, m_norm_ffn_post, v_norm_ffn_post),
                    b_gate=(b_gate, m_b_gate, v_b_gate), ret_gn_w=(ret_gn_w, m_ret_gn_w, v_ret_gn_w),
                    ssd_conv_b=(ssd_conv_b, m_ssd_conv_b, v_ssd_conv_b), ssd_dt_bias=(ssd_dt_bias, m_ssd_dt_bias, v_ssd_dt_bias),
                    ssd_a_log=(ssd_a_log, m_ssd_a_log, v_ssd_a_log), ssd_d=(ssd_d, m_ssd_d, v_ssd_d),
                    ssd_norm_w=(ssd_norm_w, m_ssd_norm_w, v_ssd_norm_w))
    rep = [n for n in SMALL if n != "ssd_conv_w"]

    def pack(arrs):
        flat = jnp.concatenate([a.reshape(-1) for a in arrs])
        rows = -(-flat.shape[0] // (16 * LANE)) * 16
        return jnp.pad(flat, (0, rows * LANE - flat.shape[0])).reshape(rows, LANE)

    conv_g = small_g["ssd_conv_w"]
    g_pack = pack([small_g[n] for n in rep] + [conv_g])
    g_all = all_gather(g_pack, name="gather_small_grads")
    zeros_conv = jnp.zeros_like(conv_g)
    w_pack, m_pack, v_pack = (pack([small_in[n][i] for n in rep] + [zeros_conv]) for i in range(3))
    sm = adamw_sum(w_pack, m_pack, v_pack, g_all, tr=TR, name="adamw_small")

    def unpack(p):
        flat, out, off = p.reshape(-1), {}, 0
        for n in rep:
            shp = small_in[n][0].shape
            size = math.prod(shp)
            out[n] = flat[off:off + size].reshape(shp)
            off += size
        out["conv_sum"] = flat[off:off + conv_g.size].reshape(conv_g.shape)
        return out

    sm = [unpack(p) for p in sm]
    ch = ssd_conv_w.shape[-1]
    conv_mine = lax.dynamic_slice_in_dim(sm[0]["conv_sum"], me * ch, ch, axis=2)
    conv_out = adamw_sum(ssd_conv_w.reshape(-1, LANE), m_ssd_conv_w.reshape(-1, LANE), v_ssd_conv_w.reshape(-1, LANE),
                         conv_mine.reshape(1, -1, LANE), tr=TR, name="adamw_conv_w")
    for i in range(4):
        sm[i]["ssd_conv_w"] = conv_out[i].reshape(ssd_conv_w.shape)

    def big_named(i):
        return {n: (tr_(out[i]) if n in transposed else out[i]) for n, out in big_out.items()}

    order = ["norm_mix_pre", "norm_mix_post", "norm_ffn_pre", "norm_ffn_post", "w_in", "b_gate", "ret_gn_w", "ssd_conv_w",
             "ssd_conv_b", "ssd_dt_bias", "ssd_a_log", "ssd_d", "ssd_norm_w", "w_branch_ret", "w_branch_sb", "w_branch_ssd",
             "w_out", "ffn_w_gate", "ffn_w_up", "ffn_w_down"]
    outs = [loss, dx.reshape(x.shape)]
    for i in range(4):
        named = {**sm[i], **big_named(i)}
        outs += [named[n] for n in order]
    return tuple(outs)
```

```python
import functools
import math

import jax
import jax.numpy as jnp
import numpy as np
from jax import lax
from jax.experimental import pallas as pl
from jax.experimental.pallas import tpu as pltpu

F32 = jnp.float32
BF16 = jnp.bfloat16

N_DEV = 8
HEAD = 128
SSD_P = 64
SSD_G = 4
SSD_N = 128
SSD_K = 4
CHUNK = 64
NORM_EPS = 1e-6
ROPE_BASE = 10000.0
LANE = 128
VMEM_LIMIT = 56 * 1024 * 1024

ADAM_LR, ADAM_B1, ADAM_B2, ADAM_EPS, ADAM_WD, ADAM_STEP = 0.001, 0.9, 0.999, 1e-08, 0.01, 10


def _cparams(sem):
    return pltpu.CompilerParams(dimension_semantics=sem, vmem_limit_bytes=VMEM_LIMIT)


def _pick(n, pref):
    if n <= pref:
        return n
    t = pref
    while t >= LANE:
        if n % t == 0:
            return t
        t -= LANE
    return n


def matmul(a, b, *, ta=False, tb=False, lead_a=None, lead_b=None, out_dtype=F32, add=None, hosted=(),
           tm=1024, tn=1024, tk=2048, name="mm"):
    la, lb = lead_a is not None, lead_b is not None
    a2, b2 = a.shape[1:] if la else a.shape, b.shape[1:] if lb else b.shape
    (kd_a, m) = a2 if ta else a2[::-1]
    (kd_b, n) = b2[::-1] if tb else b2
    assert kd_a == kd_b, (a.shape, b.shape)
    nlead = a.shape[0] if la else (b.shape[0] if lb else 1)
    batch = "batch" in (lead_a, lead_b)
    kblocks = nlead if "k" in (lead_a, lead_b) else 1
    if la and lb:
        assert lead_a == lead_b and a.shape[0] == b.shape[0]
    tm, tn, tk = _pick(m, tm), _pick(n, tn), _pick(kd_a, tk)
    kt = kd_a // tk
    nk = kt * kblocks
    grid = (m // tm, (nlead if batch else 1), n // tn, nk)

    def lead_idx(g, k):
        return g if batch else k // kt

    def a_map(i, g, j, k):
        idx = (k % kt, i) if ta else (i, k % kt)
        return ((lead_idx(g, k),) + idx) if la else idx

    def b_map(i, g, j, k):
        idx = (j, k % kt) if tb else (k % kt, j)
        return ((lead_idx(g, k),) + idx) if lb else idx

    a_blk = (tk, tm) if ta else (tm, tk)
    b_blk = (tn, tk) if tb else (tk, tn)
    a_spec = pl.BlockSpec(((None,) + a_blk) if la else a_blk, a_map)
    b_spec = pl.BlockSpec(((None,) + b_blk) if lb else b_blk, b_map)
    if batch:
        o_spec = pl.BlockSpec((None, tm, tn), lambda i, g, j, k: (g, i, j))
        o_shape = jax.ShapeDtypeStruct((nlead, m, n), out_dtype)
    else:
        o_spec = pl.BlockSpec((tm, tn), lambda i, g, j, k: (i, j))
        o_shape = jax.ShapeDtypeStruct((m, n), out_dtype)
    dims = (((0 if ta else 1,), (1 if tb else 0,)), ((), ()))

    n_add, n_host = int(add is not None), len(hosted)
    n_acc = int(nk > 1)

    def body(a_ref, b_ref, *rest):
        c_ref = rest[0] if n_add else None
        src_refs = rest[n_add:n_add + n_host]
        o_ref = rest[n_add + n_host]
        dst_refs = rest[n_add + n_host + 1:n_add + 2 * n_host + 1]
        acc_ref = rest[n_add + 2 * n_host + 1] if n_acc else None
        sems = rest[n_add + 2 * n_host + 1 + n_acc:]
        k = pl.program_id(3)
        ids = [pl.program_id(ax) for ax in range(4)]
        comms = [_COMM[kind](src_refs[c], dst_refs[c], *sems[3 * c:3 * c + 3]) for c, (kind, _) in enumerate(hosted)]

        if hosted:
            @pl.when(functools.reduce(jnp.logical_and, [i == 0 for i in ids]))
            def _():
                for start, _ in comms:
                    start()

        def product():
            return lax.dot_general(a_ref[...].astype(BF16), b_ref[...].astype(BF16), dims, preferred_element_type=F32)

        if n_acc:
            @pl.when(k == 0)
            def _():
                acc_ref[...] = jnp.zeros_like(acc_ref) if c_ref is None else c_ref[...].astype(F32)

            acc_ref[...] += product()

            @pl.when(k == nk - 1)
            def _():
                o_ref[...] = acc_ref[...].astype(o_ref.dtype)
        else:
            o_ref[...] = (product() if c_ref is None else product() + c_ref[...].astype(F32)).astype(o_ref.dtype)

        if hosted:
            @pl.when(functools.reduce(jnp.logical_and, [i == g - 1 for i, g in zip(ids, grid)]))
            def _():
                for _, finish in comms:
                    finish()

    extra = ([] if add is None else [add]) + [arr for _, arr in hosted]
    extra_specs = ([] if add is None else [o_spec]) + [_HBM] * n_host
    out_shapes = [o_shape] + [_comm_out_shape(kind, arr) for kind, arr in hosted]
    scratch = [pltpu.VMEM((tm, tn), F32)] * n_acc + _COMM_SEMS * n_host
    sem = ("arbitrary",) * 4 if hosted else ("parallel", "parallel", "parallel", "arbitrary")
    res = pl.pallas_call(
        body, out_shape=out_shapes, grid=grid, in_specs=[a_spec, b_spec] + extra_specs,
        out_specs=[o_spec] + [_HBM] * n_host, scratch_shapes=scratch, name=name, compiler_params=_cparams(sem),
    )(a, b, *extra)
    return (res[0], res[1:]) if hosted else res[0]


def _row_spec(r, tr):
    if not isinstance(r, tuple):
        return r, pl.BlockSpec((tr, r.shape[-1]), lambda i: (i, 0))
    if len(r) == 3:
        arr, w, cb = r
        return arr, pl.BlockSpec((tr, w), lambda i: (i, cb))
    arr, w, cb, ld = r
    return arr, pl.BlockSpec((None, tr, w), lambda i: (ld, i, cb))


def _full_spec(c):
    nd = c.ndim
    return pl.BlockSpec(c.shape, lambda i: (0,) * nd)


def rowwise(fn, rows, consts, outs, *, tr, name):
    arrs, specs = zip(*[_row_spec(r, tr) for r in rows])
    n_rows = arrs[0].shape[-2]
    nr, nc = len(rows), len(consts)

    def body(*refs):
        vals = [r[...] for r in refs[:nr + nc]]
        res = fn(*vals)
        for o_ref, r in zip(refs[nr + nc:], res):
            o_ref[...] = r.astype(o_ref.dtype)

    return pl.pallas_call(
        body, grid=(n_rows // tr,),
        in_specs=list(specs) + [_full_spec(c) for c in consts],
        out_specs=[pl.BlockSpec((tr, w), lambda i: (i, 0)) for w, _ in outs],
        out_shape=[jax.ShapeDtypeStruct((n_rows, w), dt) for w, dt in outs],
        name=name, compiler_params=_cparams(("parallel",)),
    )(*arrs, *consts)


def rowwise_vjp(fn, rows, consts, cts, row_grads, const_grads, *, tr, name):
    arrs, specs = zip(*[_row_spec(r, tr) for r in rows])
    n_rows = arrs[0].shape[-2]
    nr, nc = len(rows), len(consts)
    ct_present = [c for c in cts if c is not None]
    ct_arrs, ct_specs = zip(*[_row_spec(c, tr) for c in ct_present])
    add_present = [g[2] for g in row_grads if g[2] is not None]
    add_arrs, add_specs = zip(*[_row_spec(c, tr) for c in add_present]) if add_present else ((), ())
    n_ct, n_add = len(ct_present), len(add_present)
    widths = [s.block_shape[-1] for s in specs]

    def body(*refs):
        ins = refs[:nr + nc]
        ct_refs = refs[nr + nc:nr + nc + n_ct]
        add_refs = refs[nr + nc + n_ct:nr + nc + n_ct + n_add]
        out_refs = refs[nr + nc + n_ct + n_add:]
        vals = [r[...] for r in ins]
        res, f_vjp = jax.vjp(fn, *vals)
        it = iter(ct_refs)
        ct_vals = tuple(next(it)[...].astype(r.dtype) if c is not None else jnp.zeros_like(r)
                        for c, r in zip(cts, res))
        grads = f_vjp(ct_vals)
        ita = iter(add_refs)
        for o_ref, (idx, _, add) in zip(out_refs, row_grads):
            g = grads[idx].astype(F32)
            if add is not None:
                g = g + next(ita)[...].astype(F32)
            o_ref[...] = g.astype(o_ref.dtype)
        first = pl.program_id(0) == 0
        for o_ref, idx in zip(out_refs[len(row_grads):], const_grads):
            g = grads[nr + idx].astype(F32)

            @pl.when(first)
            def _():
                o_ref[...] = g

            @pl.when(jnp.logical_not(first))
            def _():
                o_ref[...] += g

    out_specs = [pl.BlockSpec((tr, widths[idx]), lambda i: (i, 0)) for idx, _, _ in row_grads]
    out_shape = [jax.ShapeDtypeStruct((n_rows, widths[idx]), dt) for idx, dt, _ in row_grads]
    out_specs += [_full_spec(consts[idx]) for idx in const_grads]
    out_shape += [jax.ShapeDtypeStruct(consts[idx].shape, F32) for idx in const_grads]
    return pl.pallas_call(
        body, grid=(n_rows // tr,),
        in_specs=list(specs) + [_full_spec(c) for c in consts] + list(ct_specs) + list(add_specs),
        out_specs=out_specs, out_shape=out_shape,
        name=name, compiler_params=_cparams(("arbitrary",)),
    )(*arrs, *consts, *ct_arrs, *add_arrs)


def f_rms(x, w):
    xf = x.astype(F32)
    return xf * lax.rsqrt(jnp.mean(xf * xf, axis=-1, keepdims=True) + NORM_EPS) * w


def f_rms_pre(x, w):
    return (f_rms(x, w),)


def f_rms_post(x, o, w):
    return (x + f_rms(o, w),)


def f_ssd_gate(y, z, w):
    return (f_rms(y * jax.nn.silu(z), w),)


@jax.custom_vjp
def _swap_halves(x):
    return pltpu.roll(x, HEAD // 2, 1)


_swap_halves.defvjp(lambda x: (_swap_halves(x), None), lambda _, g: (_swap_halves(g),))


def rope_tables(positions, *, tr):
    s = positions.shape[0]
    half = HEAD // 2
    inv = ROPE_BASE ** (-2.0 * jnp.arange(half, dtype=F32) / HEAD)
    inv = jnp.concatenate([inv, inv]).reshape(1, HEAD)
    sign = jnp.concatenate([-jnp.ones((half,), F32), jnp.ones((half,), F32)]).reshape(1, HEAD)

    def fn(pos, inv, sign):
        ang = pos.astype(F32) * inv
        return jnp.cos(ang), jnp.sin(ang) * sign

    return rowwise(fn, [positions.reshape(s, 1)], [inv, sign], [(HEAD, F32), (HEAD, F32)], tr=tr, name="rope_tables")


def _ret_consts(n_heads, blk):
    lg = np.log1p(-np.exp2(-5.0 - np.arange(n_heads)))[:, None, None]
    i = np.arange(blk)
    dist = np.abs(i[:, None] - i[None, :])[None]
    allowed = ((i[None, :] // CHUNK) <= (i[:, None] // CHUNK))[None]
    dm = np.where(allowed, np.exp(lg * dist), 0.0)
    qd = np.broadcast_to(np.exp(lg * (i[None, :, None] + 1.0)), (n_heads, blk, HEAD))
    kd = np.broadcast_to(np.exp(lg * (blk - 1.0 - i[None, :, None])), (n_heads, blk, HEAD))
    cd = np.broadcast_to(np.exp(lg * blk), (n_heads, 1, HEAD))
    return [jnp.asarray(a, F32) for a in (dm, qd, kd, cd)]


def _ret_block(q, k, v, g, gnw, state, cosf, sinf, dm, qd, kd, cd):
    qr = q * cosf + _swap_halves(q) * sinf
    kr = (k * cosf + _swap_halves(k) * sinf) * (HEAD ** -0.5)
    vb = v.astype(BF16)
    scores = _dot(qr.astype(BF16), kr.astype(BF16), _NT) * dm
    o = _dot(scores.astype(BF16), vb) + _dot((qr * qd).astype(BF16), state.astype(BF16))
    new_state = state * cd + _dot((kr * kd).astype(BF16), vb, _TN)
    mu = jnp.mean(o, axis=-1, keepdims=True)
    var = jnp.mean(jnp.square(o - mu), axis=-1, keepdims=True)
    y = (o - mu) * lax.rsqrt(var + NORM_EPS) * gnw * jax.nn.silu(g)
    return y, new_state


def _ret_specs(n_heads, blk, nb, cols, reverse):
    jm = (lambda j: nb - 1 - j) if reverse else (lambda j: j)
    col = lambda c0: pl.BlockSpec((blk, HEAD), lambda h, j: (jm(j), c0 + h))
    tab = pl.BlockSpec((blk, HEAD), lambda h, j: (jm(j), 0))
    specs = [col(c) for c in cols]
    specs += [pl.BlockSpec((1, HEAD), lambda h, j: (0, h)), tab, tab]
    specs += [pl.BlockSpec((None, blk, blk), lambda h, j: (h, 0, 0)),
              pl.BlockSpec((None, blk, HEAD), lambda h, j: (h, 0, 0)),
              pl.BlockSpec((None, blk, HEAD), lambda h, j: (h, 0, 0)),
              pl.BlockSpec((None, 1, HEAD), lambda h, j: (h, 0, 0))]
    state = pl.BlockSpec((None, None, HEAD, HEAD), lambda h, j: (h, jm(j), 0, 0))
    out_col = pl.BlockSpec((blk, HEAD), lambda h, j: (jm(j), h))
    return specs, state, out_col


def retention_fwd(proj, cols, gn_w, cosf, sinf, n_heads, *, blk, name):
    s = proj.shape[0]
    nb = s // blk
    consts = _ret_consts(n_heads, blk)
    specs, state_spec, out_col = _ret_specs(n_heads, blk, nb, cols, False)

    def body(q_ref, k_ref, v_ref, g_ref, gn_ref, cos_ref, sin_ref, dm_ref, qd_ref, kd_ref, cd_ref,
             y_ref, st_ref, state):
        @pl.when(pl.program_id(1) == 0)
        def _():
            state[...] = jnp.zeros_like(state)

        st = state[...]
        st_ref[...] = st
        y, new_state = _ret_block(q_ref[...], k_ref[...], v_ref[...], g_ref[...], gn_ref[...], st,
                                  cos_ref[...], sin_ref[...], dm_ref[...], qd_ref[...], kd_ref[...], cd_ref[...])
        y_ref[...] = y.astype(y_ref.dtype)
        state[...] = new_state

    return pl.pallas_call(
        body, grid=(n_heads, nb), in_specs=specs, out_specs=[out_col, state_spec],
        out_shape=[jax.ShapeDtypeStruct((s, n_heads * HEAD), BF16),
                   jax.ShapeDtypeStruct((n_heads, nb, HEAD, HEAD), F32)],
        scratch_shapes=[pltpu.VMEM((HEAD, HEAD), F32)],
        name=name, compiler_params=_cparams(("parallel", "arbitrary")),
    )(proj, proj, proj, proj, gn_w, cosf, sinf, *consts)


def retention_bwd(proj, cols, gn_w, cosf, sinf, states, dy, n_heads, *, blk, name):
    s = proj.shape[0]
    nb = s // blk
    consts = _ret_consts(n_heads, blk)
    specs, state_spec, out_col = _ret_specs(n_heads, blk, nb, cols, True)

    def body(q_ref, k_ref, v_ref, g_ref, gn_ref, cos_ref, sin_ref, dm_ref, qd_ref, kd_ref, cd_ref,
             st_ref, dy_ref, dq_ref, dk_ref, dv_ref, dg_ref, dgn_ref, dstate):
        first = pl.program_id(1) == 0

        @pl.when(first)
        def _():
            dstate[...] = jnp.zeros_like(dstate)
            dgn_ref[...] = jnp.zeros_like(dgn_ref)

        tabs = (cos_ref[...], sin_ref[...], dm_ref[...], qd_ref[...], kd_ref[...], cd_ref[...])
        fn = lambda q, k, v, g, gnw, st: _ret_block(q, k, v, g, gnw, st, *tabs)
        _, f_vjp = jax.vjp(fn, q_ref[...], k_ref[...], v_ref[...], g_ref[...], gn_ref[...], st_ref[...])
        dq, dk, dv, dg, dgn, dst = f_vjp((dy_ref[...].astype(F32), dstate[...]))
        dq_ref[...] = dq.astype(dq_ref.dtype)
        dk_ref[...] = dk.astype(dk_ref.dtype)
        dv_ref[...] = dv.astype(dv_ref.dtype)
        dg_ref[...] = dg.astype(dg_ref.dtype)
        dgn_ref[...] += dgn
        dstate[...] = dst

    o_shape = jax.ShapeDtypeStruct((s, n_heads * HEAD), BF16)
    dy_spec = out_col
    if isinstance(dy, tuple):
        dy, lead = dy
        dy_spec = pl.BlockSpec((None, blk, HEAD), lambda h, j: (lead, nb - 1 - j, h))
    return pl.pallas_call(
        body, grid=(n_heads, nb), in_specs=specs + [state_spec, dy_spec],
        out_specs=[out_col, out_col, out_col, out_col, pl.BlockSpec((1, HEAD), lambda h, j: (0, h))],
        out_shape=[o_shape, o_shape, o_shape, o_shape, jax.ShapeDtypeStruct((1, n_heads * HEAD), F32)],
        scratch_shapes=[pltpu.VMEM((HEAD, HEAD), F32)],
        name=name, compiler_params=_cparams(("parallel", "arbitrary")),
    )(proj, proj, proj, proj, gn_w, cosf, sinf, *consts, states, dy)


CONV_PAD = 8


def _conv_pre(u_ext, taps, bias, n_out):
    n = u_ext.shape[0]
    views = [pltpu.roll(u_ext, n - (k + CONV_PAD - SSD_K + 1), 0)[:n_out] for k in range(SSD_K)]
    pre = bias
    for k in range(SSD_K):
        pre = pre + taps[k] * views[k]
    return pre, views


def ssd_conv_fwd(u_pad, taps, bias, *, chunk, name):
    s, c = u_pad.shape[0] - 2 * CONV_PAD, u_pad.shape[1]

    def body(u_ref, t0, t1, t2, t3, b_ref, o_ref):
        taps_v = [t[...] for t in (t0, t1, t2, t3)]
        bias_v = b_ref[...]

        @pl.loop(0, s // chunk)
        def _(ci):
            r0 = pl.multiple_of(ci * chunk, chunk)
            pre, _ = _conv_pre(u_ref[pl.ds(r0, chunk + CONV_PAD), :], taps_v, bias_v, chunk)
            o_ref[pl.ds(r0, chunk), :] = pre * jax.nn.sigmoid(pre)

    row = pl.BlockSpec((1, LANE), lambda i: (0, i))
    return pl.pallas_call(
        body, grid=(c // LANE,),
        in_specs=[pl.BlockSpec((s + 2 * CONV_PAD, LANE), lambda i: (0, i))] + [row] * 5,
        out_specs=pl.BlockSpec((s, LANE), lambda i: (0, i)),
        out_shape=jax.ShapeDtypeStruct((s, c), F32),
        name=name, compiler_params=_cparams(("parallel",)),
    )(u_pad, *taps, bias)


def ssd_conv_bwd(u_pad, taps, bias, dxc_pad, *, chunk, name):
    s, c = u_pad.shape[0] - 2 * CONV_PAD, u_pad.shape[1]
    ext = chunk + CONV_PAD

    def body(u_ref, t0, t1, t2, t3, b_ref, d_ref, du_ref, dw_ref, db_ref):
        taps_v = [t[...] for t in (t0, t1, t2, t3)]
        bias_v = b_ref[...]
        dw_ref[...] = jnp.zeros_like(dw_ref)
        db_ref[...] = jnp.zeros_like(db_ref)

        @pl.loop(0, s // chunk)
        def _(ci):
            r0 = pl.multiple_of(ci * chunk, chunk)
            pre, views = _conv_pre(u_ref[pl.ds(r0, ext + CONV_PAD), :], taps_v, bias_v, ext)
            sig = jax.nn.sigmoid(pre)
            dpre = d_ref[pl.ds(r0, ext), :] * (sig * (1.0 + pre * (1.0 - sig)))
            du = taps_v[SSD_K - 1] * dpre[:chunk]
            for k in range(SSD_K - 1):
                du = du + taps_v[k] * pltpu.roll(dpre, ext - (SSD_K - 1 - k), 0)[:chunk]
            du_ref[pl.ds(r0, chunk), :] = du.astype(du_ref.dtype)
            own = dpre[:chunk]
            for k in range(SSD_K):
                dw_ref[k:k + 1, :] += jnp.sum(own * views[k][:chunk], axis=0, keepdims=True)
            db_ref[...] += jnp.sum(own, axis=0, keepdims=True)

    row = pl.BlockSpec((1, LANE), lambda i: (0, i))
    return pl.pallas_call(
        body, grid=(c // LANE,),
        in_specs=[pl.BlockSpec((s + 2 * CONV_PAD, LANE), lambda i: (0, i))] + [row] * 5
        + [pl.BlockSpec((s + CONV_PAD, LANE), lambda i: (0, i))],
        out_specs=[pl.BlockSpec((s, LANE), lambda i: (0, i)), pl.BlockSpec((SSD_K, LANE), lambda i: (0, i)), row],
        out_shape=[jax.ShapeDtypeStruct((s, c), BF16), jax.ShapeDtypeStruct((SSD_K, c), F32),
                   jax.ShapeDtypeStruct((1, c), F32)],
        name=name, compiler_params=_cparams(("parallel",)),
    )(u_pad, *taps, bias, dxc_pad)


def _tri_dot(tri, x, passes=3):
    out = None
    rem = x
    for _ in range(passes):
        piece = rem.astype(BF16)
        rem = rem - piece.astype(F32)
        d = _dot(tri, piece)
        out = d if out is None else out + d
    return out


def _tri(n, upper):
    rr = lax.broadcasted_iota(jnp.int32, (n, n), 0)
    cc = lax.broadcasted_iota(jnp.int32, (n, n), 1)
    return ((rr <= cc) if upper else (rr >= cc)).astype(BF16)


@jax.custom_vjp
def _cumsum_rows(a):
    return _tri_dot(_tri(a.shape[0], False), a)


_cumsum_rows.defvjp(lambda a: (_cumsum_rows(a), None), lambda _, g: (_tri_dot(_tri(g.shape[0], True), g),))


def _softplus(x):
    return jnp.maximum(x, 0.0) + jnp.log(1.0 + jnp.exp(-jnp.abs(x)))


def _ssd_block(x, bm, cm, dtraw, dtb, alog, dsk, state_t, group):
    blk, width = x.shape
    e_heads = width // SSD_P
    dt = _softplus(dtraw + dtb)
    acum = _cumsum_rows(dt * (-jnp.exp(alog)))
    acum_t = acum.T
    lane_h = lax.broadcasted_iota(jnp.int32, (1, LANE), 1)
    sub_h = lax.broadcasted_iota(jnp.int32, (LANE, 1), 0)
    lane_e = lax.broadcasted_iota(jnp.int32, (1, width), 1) // SSD_P
    causal = lax.broadcasted_iota(jnp.int32, (blk, blk), 0) >= lax.broadcasted_iota(jnp.int32, (blk, blk), 1)
    last_row = lax.broadcasted_iota(jnp.int32, (blk, 1), 0) == blk - 1
    cb = _dot(cm.astype(BF16), bm.astype(BF16), _NT)
    y = jnp.zeros((blk, width), F32)
    dt_l = jnp.zeros((blk, width), F32)
    ac_l = jnp.zeros((blk, width), F32)
    d_l = jnp.zeros((1, width), F32)
    for e in range(e_heads):
        head = group * e_heads + e
        pick = lane_h == head
        col = jnp.sum(jnp.where(pick, acum, 0.0), axis=1, keepdims=True)
        dt_e = jnp.sum(jnp.where(pick, dt, 0.0), axis=1, keepdims=True)
        d_e = jnp.sum(jnp.where(pick, dsk, 0.0), axis=1, keepdims=True)
        row = jnp.sum(jnp.where(sub_h == head, acum_t, 0.0), axis=0, keepdims=True)
        mine = lane_e == e
        decay = jnp.exp(jnp.where(causal, col - row, -jnp.inf))
        y = y + _dot((cb * decay).astype(BF16), jnp.where(mine, x * dt_e, 0.0).astype(BF16))
        dt_l = dt_l + jnp.where(mine, dt_e, 0.0)
        ac_l = ac_l + jnp.where(mine, col, 0.0)
        d_l = d_l + jnp.where(mine, d_e, 0.0)
    ac_last = jnp.sum(jnp.where(last_row, ac_l, 0.0), axis=0, keepdims=True)
    y = y + jnp.exp(ac_l) * _dot(cm.astype(BF16), state_t.astype(BF16)) + x * d_l
    inject = (x * dt_l * jnp.exp(ac_last - ac_l)).astype(BF16)
    new_state = state_t * jnp.exp(ac_last) + _dot(bm.astype(BF16), inject, _TN)
    return y, new_state


def ssd_scan_fwd(xc, proj, dt_col, dtb, alog, dsk, *, blk, name):
    s = xc.shape[0]
    nb = s // blk
    e_w = 4 * SSD_P
    b_off = SSD_G * e_w // SSD_N
    c_off = b_off + SSD_G
    row = pl.BlockSpec((1, LANE), lambda j, g: (0, 0))

    def body(x_ref, b_ref, c_ref, dt_ref, dtb_ref, al_ref, d_ref, y_ref, st_ref, state):
        j, g = pl.program_id(0), pl.program_id(1)

        @pl.when(j == 0)
        def _():
            state[g] = jnp.zeros((SSD_N, e_w), F32)

        st = state[g]
        st_ref[...] = st
        y, new_state = _ssd_block(x_ref[...], b_ref[...], c_ref[...], dt_ref[...], dtb_ref[...], al_ref[...],
                                  d_ref[...], st, g)
        y_ref[...] = y
        state[g] = new_state

    return pl.pallas_call(
        body, grid=(nb, SSD_G),
        in_specs=[pl.BlockSpec((blk, e_w), lambda j, g: (j, g)),
                  pl.BlockSpec((blk, SSD_N), lambda j, g: (j, b_off + g)),
                  pl.BlockSpec((blk, SSD_N), lambda j, g: (j, c_off + g)),
                  pl.BlockSpec((blk, LANE), lambda j, g: (j, dt_col)), row, row, row],
        out_specs=[pl.BlockSpec((blk, e_w), lambda j, g: (j, g)),
                   pl.BlockSpec((None, None, SSD_N, e_w), lambda j, g: (j, g, 0, 0))],
        out_shape=[jax.ShapeDtypeStruct((s, SSD_G * e_w), F32),
                   jax.ShapeDtypeStruct((nb, SSD_G, SSD_N, e_w), F32)],
        scratch_shapes=[pltpu.VMEM((SSD_G, SSD_N, e_w), F32)],
        name=name, compiler_params=_cparams(("arbitrary", "arbitrary")),
    )(xc, xc, xc, proj, dtb, alog, dsk)


def ssd_scan_bwd(xc, proj, dt_col, dtb, alog, dsk, states, dy, *, blk, name):
    s = xc.shape[0]
    nb = s // blk
    e_w = 4 * SSD_P
    b_off = SSD_G * e_w // SSD_N
    c_off = b_off + SSD_G
    row = pl.BlockSpec((1, LANE), lambda j, g: (0, 0))
    jm = lambda j: nb - 1 - j

    def body(x_ref, b_ref, c_ref, dt_ref, dtb_ref, al_ref, d_ref, st_ref, dy_ref,
             dx_ref, db_ref, dc_ref, ddt_ref, ddtb_ref, dal_ref, dd_ref, dstate):
        j, g = pl.program_id(0), pl.program_id(1)

        @pl.when(j == 0)
        def _():
            dstate[g] = jnp.zeros((SSD_N, e_w), F32)

        @pl.when(jnp.logical_and(j == 0, g == 0))
        def _():
            ddtb_ref[...] = jnp.zeros_like(ddtb_ref)
            dal_ref[...] = jnp.zeros_like(dal_ref)
            dd_ref[...] = jnp.zeros_like(dd_ref)

        @pl.when(g == 0)
        def _():
            ddt_ref[...] = jnp.zeros_like(ddt_ref)

        fn = functools.partial(_ssd_block, group=g)
        _, f_vjp = jax.vjp(fn, x_ref[...], b_ref[...], c_ref[...], dt_ref[...], dtb_ref[...], al_ref[...],
                           d_ref[...], st_ref[...])
        dx, db, dc, ddt, ddtb, dal, dd, dst = f_vjp((dy_ref[...], dstate[g]))
        dx_ref[...] = dx
        db_ref[...] = db
        dc_ref[...] = dc
        ddt_ref[...] += ddt
        ddtb_ref[...] += ddtb
        dal_ref[...] += dal
        dd_ref[...] += dd
        dstate[g] = dst

    return pl.pallas_call(
        body, grid=(nb, SSD_G),
        in_specs=[pl.BlockSpec((blk, e_w), lambda j, g: (jm(j), g)),
                  pl.BlockSpec((blk, SSD_N), lambda j, g: (jm(j), b_off + g)),
                  pl.BlockSpec((blk, SSD_N), lambda j, g: (jm(j), c_off + g)),
                  pl.BlockSpec((blk, LANE), lambda j, g: (jm(j), dt_col)), row, row, row,
                  pl.BlockSpec((None, None, SSD_N, e_w), lambda j, g: (jm(j), g, 0, 0)),
                  pl.BlockSpec((blk, e_w), lambda j, g: (jm(j), g))],
        out_specs=[pl.BlockSpec((blk, e_w), lambda j, g: (jm(j), g)),
                   pl.BlockSpec((blk, SSD_N), lambda j, g: (jm(j), g)),
                   pl.BlockSpec((blk, SSD_N), lambda j, g: (jm(j), g)),
                   pl.BlockSpec((blk, LANE), lambda j, g: (jm(j), 0)), row, row, row],
        out_shape=[jax.ShapeDtypeStruct((s, SSD_G * e_w), F32),
                   jax.ShapeDtypeStruct((s, SSD_G * SSD_N), F32),
                   jax.ShapeDtypeStruct((s, SSD_G * SSD_N), F32),
                   jax.ShapeDtypeStruct((s, LANE), F32)] + [jax.ShapeDtypeStruct((1, LANE), F32)] * 3,
        scratch_shapes=[pltpu.VMEM((SSD_G, SSD_N, e_w), F32)],
        name=name, compiler_params=_cparams(("arbitrary", "arbitrary")),
    )(xc, xc, xc, proj, dtb, alog, dsk, states, dy)


SB_DEAD = -104.0

_NT = (((1,), (1,)), ((), ()))
_TN = (((0,), (0,)), ((), ()))


def _dot(a, b, dims=(((1,), (0,)), ((), ()))):
    return lax.dot_general(a, b, dims, preferred_element_type=F32)


def _split_dot(x, tri, passes):
    out = None
    rem = x
    for _ in range(passes):
        piece = rem.astype(BF16)
        rem = rem - piece.astype(F32)
        d = _dot(piece, tri)
        out = d if out is None else out + d
    return out


def _sb_scores(q, k_ref, j, blk, row, scale):
    kb = k_ref[pl.ds(pl.multiple_of(j * blk, blk), blk), :].astype(BF16)
    z = _dot(q, kb, _NT) * scale
    col = j * blk + lax.broadcasted_iota(jnp.int32, (blk, blk), 1)
    mask = col < row
    sp = jnp.maximum(z, 0.0) + jnp.log(1.0 + jnp.exp(-jnp.abs(z)))
    lk = jnp.where(mask, -sp, 0.0)
    return kb, mask, lk, z - sp


def sb_attention_fwd(proj, q_col, k_col, v_col, n_heads, *, blk, name):
    s = proj.shape[0]
    nq = s // blk
    scale = HEAD ** -0.5

    def body(q_ref, k_ref, v_ref, o_ref, r_ref, n_ref):
        i = pl.program_id(1)
        q = q_ref[...].astype(BF16)
        row = i * blk + lax.broadcasted_iota(jnp.int32, (blk, blk), 0)
        rr = lax.broadcasted_iota(jnp.int32, (blk, blk), 0)
        cc = lax.broadcasted_iota(jnp.int32, (blk, blk), 1)
        tri_after = (rr > cc).astype(BF16)

        def alive(carry):
            jj, _, run = carry
            return jnp.logical_and(jj <= i, jnp.max(run) > SB_DEAD)

        def step(carry):
            jj, acc, run = carry
            j = i - jj
            _, mask, lk, ls = _sb_scores(q, k_ref, j, blk, row, scale)
            later = _split_dot(lk, tri_after, 2) + run
            w = jnp.where(mask, jnp.exp(ls + later), 0.0)
            vb = v_ref[pl.ds(pl.multiple_of(j * blk, blk), blk), :].astype(BF16)
            return jj + 1, acc + _dot(w.astype(BF16), vb), run + jnp.sum(lk, axis=1, keepdims=True)

        n, acc, run = lax.while_loop(
            alive, step, (jnp.int32(0), jnp.zeros((blk, HEAD), F32), jnp.zeros((blk, 1), F32)))
        o_ref[...] = acc.astype(o_ref.dtype)
        r_ref[...] = jnp.broadcast_to(run, (blk, HEAD))
        n_ref[pl.program_id(0), i] = n

    blk_spec = lambda c0: pl.BlockSpec((blk, HEAD), lambda h, i: (i, c0 + h))
    full_spec = lambda c0: pl.BlockSpec((s, HEAD), lambda h, i: (0, c0 + h))
    out_spec = pl.BlockSpec((blk, HEAD), lambda h, i: (i, h))
    return pl.pallas_call(
        body, grid=(n_heads, nq),
        in_specs=[blk_spec(q_col), full_spec(k_col), full_spec(v_col)],
        out_specs=[out_spec, out_spec, pl.BlockSpec(memory_space=pltpu.SMEM)],
        out_shape=[jax.ShapeDtypeStruct((s, n_heads * HEAD), BF16),
                   jax.ShapeDtypeStruct((s, n_heads * HEAD), F32),
                   jax.ShapeDtypeStruct((n_heads, nq), jnp.int32)],
        name=name, compiler_params=_cparams(("arbitrary", "arbitrary")),
    )(proj, proj, proj)


def sb_attention_bwd(proj, d_out, run_tot, visited, q_col, k_col, v_col, n_heads, *, blk, name):
    s = proj.shape[0]
    nq = s // blk
    scale = HEAD ** -0.5

    def body(n_ref, q_ref, k_ref, v_ref, do_ref, r_ref, dq_ref, dk_ref, dv_ref, dk_acc, dv_acc):
        i = pl.program_id(1)
        first = i + 1 - jnp.clip(n_ref[pl.program_id(0), i], 1, i + 1)

        @pl.when(i == 0)
        def _():
            dk_acc[...] = jnp.zeros_like(dk_acc)
            dv_acc[...] = jnp.zeros_like(dv_acc)

        q = q_ref[...].astype(BF16)
        do = do_ref[...].astype(BF16)
        rtot = r_ref[:, :1]
        row = i * blk + lax.broadcasted_iota(jnp.int32, (blk, blk), 0)
        rr = lax.broadcasted_iota(jnp.int32, (blk, blk), 0)
        cc = lax.broadcasted_iota(jnp.int32, (blk, blk), 1)
        tri_upto = (rr <= cc).astype(BF16)
        tri_before = (rr < cc).astype(BF16)

        def step(j, carry):
            dq, pre, gpre = carry
            kb, mask, lk, ls = _sb_scores(q, k_ref, j, blk, row, scale)
            rows = pl.ds(pl.multiple_of(j * blk, blk), blk)
            vb = v_ref[rows, :].astype(BF16)
            later = rtot - (pre + _split_dot(lk, tri_upto, 2))
            w = jnp.where(mask, jnp.exp(ls + later), 0.0)
            g = w * _dot(do, vb, _NT)
            g_before = _split_dot(g, tri_before, 1) + gpre
            sig = jnp.exp(ls)
            dz = (jnp.where(mask, g * (1.0 - sig) - sig * g_before, 0.0) * scale).astype(BF16)
            dk_acc[rows, :] += _dot(dz, q, _TN)
            dv_acc[rows, :] += _dot(w.astype(BF16), do, _TN)
            return (dq + _dot(dz, kb), pre + jnp.sum(lk, axis=1, keepdims=True),
                    gpre + jnp.sum(g, axis=1, keepdims=True))

        zero = jnp.zeros((blk, 1), F32)
        dq, _, _ = lax.fori_loop(first, i + 1, step, (jnp.zeros((blk, HEAD), F32), zero, zero))
        dq_ref[...] = dq.astype(dq_ref.dtype)

        @pl.when(i == nq - 1)
        def _():
            dk_ref[...] = dk_acc[...].astype(dk_ref.dtype)
            dv_ref[...] = dv_acc[...].astype(dv_ref.dtype)

    blk_spec = lambda c0: pl.BlockSpec((blk, HEAD), lambda h, i: (i, c0 + h))
    full_spec = lambda c0: pl.BlockSpec((s, HEAD), lambda h, i: (0, c0 + h))
    o_shape = jax.ShapeDtypeStruct((s, n_heads * HEAD), BF16)
    do_spec = blk_spec(0)
    if isinstance(d_out, tuple):
        d_out, lead = d_out
        do_spec = pl.BlockSpec((None, blk, HEAD), lambda h, i: (lead, i, h))
    return pl.pallas_call(
        body, grid=(n_heads, nq),
        in_specs=[pl.BlockSpec(memory_space=pltpu.SMEM), blk_spec(q_col), full_spec(k_col), full_spec(v_col),
                  do_spec, blk_spec(0)],
        out_specs=[blk_spec(0), full_spec(0), full_spec(0)],
        out_shape=[o_shape, o_shape, o_shape],
        scratch_shapes=[pltpu.VMEM((s, HEAD), F32), pltpu.VMEM((s, HEAD), F32)],
        name=name, compiler_params=_cparams(("parallel", "arbitrary")),
    )(visited, proj, proj, proj, d_out, run_tot)


def _hosting(hosted):
    arrs = [arr for _, arr in hosted]
    shapes = [_comm_out_shape(kind, arr) for kind, arr in hosted]

    def ops(src_refs, dst_refs, sems):
        return [_COMM[kind](src_refs[c], dst_refs[c], *sems[3 * c:3 * c + 3]) for c, (kind, _) in enumerate(hosted)]

    return arrs, [_HBM] * len(hosted), shapes, _COMM_SEMS * len(hosted), ops


def ffn_up_swiglu(h, w_gu, *, tm, hosted=(), name):
    s, d = h.shape
    tm = _pick(s, tm)
    nb, _, hb, _ = w_gu.shape
    n_host = len(hosted)
    h_arrs, h_specs, h_shapes, h_sems, h_ops = _hosting(hosted)
    grid = (s // tm, nb)

    def body(h_ref, w_ref, *rest):
        src_refs, (gu_ref, act_ref) = rest[:n_host], rest[n_host:n_host + 2]
        dst_refs, sems = rest[n_host + 2:2 * n_host + 2], rest[2 * n_host + 2:]
        comms = h_ops(src_refs, dst_refs, sems)
        i, g = pl.program_id(0), pl.program_id(1)

        if hosted:
            @pl.when(jnp.logical_and(i == 0, g == 0))
            def _():
                for start, _ in comms:
                    start()

        a = h_ref[...]
        gate = _dot(a, w_ref[0], _NT)
        up = _dot(a, w_ref[1], _NT)
        gu_ref[0] = gate
        gu_ref[1] = up
        act_ref[...] = (gate * jax.nn.sigmoid(gate) * up).astype(act_ref.dtype)

        if hosted:
            @pl.when(jnp.logical_and(i == grid[0] - 1, g == grid[1] - 1))
            def _():
                for _, finish in comms:
                    finish()

    res = pl.pallas_call(
        body, grid=grid,
        in_specs=[pl.BlockSpec((tm, d), lambda i, g: (i, 0)),
                  pl.BlockSpec((None, 2, hb, d), lambda i, g: (g, 0, 0, 0))] + h_specs,
        out_specs=[pl.BlockSpec((None, 2, tm, hb), lambda i, g: (g, 0, i, 0)),
                   pl.BlockSpec((None, tm, hb), lambda i, g: (g, i, 0))] + h_specs,
        out_shape=[jax.ShapeDtypeStruct((nb, 2, s, hb), F32), jax.ShapeDtypeStruct((nb, s, hb), BF16)] + h_shapes,
        scratch_shapes=h_sems, name=name,
        compiler_params=_cparams(("arbitrary", "arbitrary") if hosted else ("parallel", "parallel")),
    )(h, w_gu, *h_arrs)
    return res[0], res[1], res[2:]


def branch_merge(y3, w3, proj, gate_col, biases, *, tm, tn, name):
    nbr, s, k = y3.shape
    n = w3.shape[2]
    tm, tn = _pick(s, tm), _pick(n, tn)
    per = n // tn

    def body(y_ref, w_ref, g0, g1, g2, b0, b1, b2, m_ref, u_ref):
        merged = None
        for i, (g_ref, b_ref) in enumerate(((g0, b0), (g1, b1), (g2, b2))):
            u = _dot(y_ref[i], w_ref[i])
            u_ref[i] = u
            part = jax.nn.sigmoid(g_ref[...] + b_ref[...]) * u
            merged = part if merged is None else merged + part
        m_ref[...] = merged.astype(m_ref.dtype)

    gate_spec = lambda i: pl.BlockSpec((tm, tn), lambda r, j: (r, (gate_col + i) * per + j))
    bias_spec = pl.BlockSpec((1, tn), lambda r, j: (0, j))
    return pl.pallas_call(
        body, grid=(s // tm, per),
        in_specs=[pl.BlockSpec((nbr, tm, k), lambda r, j: (0, r, 0)), pl.BlockSpec((nbr, k, tn), lambda r, j: (0, 0, j)),
                  gate_spec(0), gate_spec(1), gate_spec(2), bias_spec, bias_spec, bias_spec],
        out_specs=[pl.BlockSpec((tm, tn), lambda r, j: (r, j)), pl.BlockSpec((nbr, tm, tn), lambda r, j: (0, r, j))],
        out_shape=[jax.ShapeDtypeStruct((s, n), BF16), jax.ShapeDtypeStruct((nbr, s, n), F32)],
        name=name, compiler_params=_cparams(("parallel", "parallel")),
    )(y3, w3, proj, proj, proj, *biases)


def gate_merge_bwd(u3, proj, gate_col, biases, dmerged, *, tr, name):
    nbr, s, n = u3.shape

    def body(u_ref, g0, g1, g2, b0, b1, b2, dm_ref, du_ref, dg_ref, db0, db1, db2):
        first = pl.program_id(0) == 0
        dm = dm_ref[...]
        for i, (g_ref, b_ref, db_ref) in enumerate(((g0, b0, db0), (g1, b1, db1), (g2, b2, db2))):
            sig = jax.nn.sigmoid(g_ref[...] + b_ref[...])
            du_ref[i] = (dm * sig).astype(du_ref.dtype)
            dlogit = dm * u_ref[i] * (sig * (1.0 - sig))
            dg_ref[:, i * n:(i + 1) * n] = dlogit.astype(dg_ref.dtype)
            part = jnp.sum(dlogit, axis=0, keepdims=True)

            @pl.when(first)
            def _():
                db_ref[...] = part

            @pl.when(jnp.logical_not(first))
            def _():
                db_ref[...] += part

    gate_spec = lambda i: pl.BlockSpec((tr, n), lambda r: (r, gate_col + i))
    bias_spec = pl.BlockSpec((1, n), lambda r: (0, 0))
    u_spec = pl.BlockSpec((nbr, tr, n), lambda r: (0, r, 0))
    return pl.pallas_call(
        body, grid=(s // tr,),
        in_specs=[u_spec, gate_spec(0), gate_spec(1), gate_spec(2), bias_spec, bias_spec, bias_spec,
                  pl.BlockSpec((tr, n), lambda r: (r, 0))],
        out_specs=[u_spec, pl.BlockSpec((tr, nbr * n), lambda r: (r, 0)), bias_spec, bias_spec, bias_spec],
        out_shape=[jax.ShapeDtypeStruct(u3.shape, BF16), jax.ShapeDtypeStruct((s, nbr * n), BF16)]
        + [jax.ShapeDtypeStruct((1, n), F32)] * 3,
        name=name, compiler_params=_cparams(("arbitrary",)),
    )(u3, proj, proj, proj, *biases, dmerged)


def ffn_down_dx_swiglu(df, w_down, gu, *, tm, name):
    s, d = df.shape
    tm = _pick(s, tm)
    nb, hb, _ = w_down.shape

    def body(df_ref, w_ref, gu_ref, dgu_ref):
        dact = _dot(df_ref[...], w_ref[...], _NT)
        gate, up = gu_ref[0], gu_ref[1]
        sig = jax.nn.sigmoid(gate)
        dgu_ref[0] = (dact * up * (sig * (1.0 + gate * (1.0 - sig)))).astype(dgu_ref.dtype)
        dgu_ref[1] = (dact * gate * sig).astype(dgu_ref.dtype)

    blk = pl.BlockSpec((None, 2, tm, hb), lambda i, g: (g, 0, i, 0))
    return pl.pallas_call(
        body, grid=(s // tm, nb),
        in_specs=[pl.BlockSpec((tm, d), lambda i, g: (i, 0)), pl.BlockSpec((None, hb, d), lambda i, g: (g, 0, 0)), blk],
        out_specs=blk, out_shape=jax.ShapeDtypeStruct(gu.shape, BF16),
        name=name, compiler_params=_cparams(("parallel", "parallel")),
    )(df, w_down, gu)


def loss_head(y, target, *, tr, name):
    s, d = y.shape

    def body(y_ref, t_ref, l_ref, dy_ref):
        err = y_ref[...] - t_ref[...]
        dy_ref[...] = err * (1.0 / d)
        part = 0.5 * jnp.sum(jnp.mean(err * err, axis=-1, keepdims=True), axis=0, keepdims=True)

        @pl.when(pl.program_id(0) == 0)
        def _():
            l_ref[...] = jnp.zeros_like(l_ref)

        l_ref[...] += jnp.broadcast_to(part, l_ref.shape)

    row = pl.BlockSpec((tr, d), lambda i: (i, 0))
    return pl.pallas_call(
        body, grid=(s // tr,), in_specs=[row, row],
        out_specs=[pl.BlockSpec((8, LANE), lambda i: (0, 0)), row],
        out_shape=[jax.ShapeDtypeStruct((8, LANE), F32), jax.ShapeDtypeStruct((s, d), F32)],
        name=name, compiler_params=_cparams(("arbitrary",)),
    )(y, target)


def _adamw_math(w, g, m, v):
    m = ADAM_B1 * m + (1.0 - ADAM_B1) * g
    v = ADAM_B2 * v + (1.0 - ADAM_B2) * jnp.square(g)
    m_hat = m / (1.0 - ADAM_B1 ** ADAM_STEP)
    v_hat = v / (1.0 - ADAM_B2 ** ADAM_STEP)
    delta = -ADAM_LR * (m_hat / (jnp.sqrt(v_hat) + ADAM_EPS) + ADAM_WD * w)
    return delta, m, v


def adamw_sum(w, m, v, parts, *, tr, name):
    n, r, c = parts.shape
    tr = _pick_rows(r, tr)

    def body(w_ref, m_ref, v_ref, p_ref, g_ref, d_ref, nm_ref, nv_ref):
        g = p_ref[0].astype(F32)
        for i in range(1, n):
            g = g + p_ref[i].astype(F32)
        delta, nm, nv = _adamw_math(w_ref[...], g, m_ref[...], v_ref[...])
        g_ref[...] = g
        d_ref[...] = delta
        nm_ref[...] = nm
        nv_ref[...] = nv

    row = pl.BlockSpec((tr, c), lambda i: (i, 0))
    shape = jax.ShapeDtypeStruct((r, c), F32)
    return pl.pallas_call(
        body, grid=(r // tr,),
        in_specs=[row, row, row, pl.BlockSpec((n, tr, c), lambda i: (0, i, 0))],
        out_specs=[row] * 4, out_shape=[shape] * 4,
        name=name, compiler_params=_cparams(("parallel",)),
    )(w, m, v, parts)


def adamw_into(w_all, m_all, v_all, parts, layer, row_off, prev, *, tr, name):
    depth, r, c = w_all.shape
    n = parts.shape[0]
    if r % 16 == 0:
        tr = _pick_rows(r, tr)
        assert row_off % tr == 0
        off = row_off // tr
        grid = (r // tr,)
        lay = pl.BlockSpec((None, tr, c), lambda i: (layer, i, 0))
        p_spec = pl.BlockSpec((n, tr, c), lambda i: (0, off + i, 0))
    else:
        assert row_off == 0 and parts.shape[1] == r
        grid = (c // LANE,)
        lay = pl.BlockSpec((None, r, LANE), lambda j: (layer, 0, j))
        p_spec = pl.BlockSpec((n, r, LANE), lambda j: (0, 0, j))
    if prev is None:
        prev = [lax.empty(w_all.shape, F32) for _ in range(4)]

    def body(w_ref, m_ref, v_ref, p_ref, _g, _d, _m, _v, g_ref, d_ref, nm_ref, nv_ref):
        g = p_ref[0].astype(F32)
        for i in range(1, n):
            g = g + p_ref[i].astype(F32)
        delta, nm, nv = _adamw_math(w_ref[...], g, m_ref[...], v_ref[...])
        g_ref[...] = g
        d_ref[...] = delta
        nm_ref[...] = nm
        nv_ref[...] = nv

    untouched = pl.BlockSpec(memory_space=pl.ANY)
    return pl.pallas_call(
        body, grid=grid,
        in_specs=[lay, lay, lay, p_spec] + [untouched] * 4,
        out_specs=[lay] * 4, out_shape=[jax.ShapeDtypeStruct(w_all.shape, F32)] * 4,
        input_output_aliases={4: 0, 5: 1, 6: 2, 7: 3},
        name=name, compiler_params=_cparams(("parallel",)),
    )(w_all, m_all, v_all, parts, *prev)


def _pick_rows(r, pref):
    t = min(pref, r)
    while r % t or (t % 16 and t != r):
        t -= 1
    return t


_HBM = pl.BlockSpec(memory_space=pltpu.HBM)
_MESH = pl.DeviceIdType.MESH


def _flat_index(px, py, pc):
    return 4 * px + 2 * py + pc


def _gather_ops(x_ref, out_ref, send_sems, recv_sems, local_sem):
    x, y, c = lax.axis_index("x"), lax.axis_index("y"), lax.axis_index("c")
    me, sibling = (x, y, c), (x, y, 1 - c)
    chips = [(1 - x, y), (x, 1 - y), (1 - x, 1 - y)]

    def slot(p):
        return out_ref.at[_flat_index(*p)]

    def copy(k, block, to, src=None):
        return pltpu.make_async_remote_copy(
            src_ref=slot(block) if src is None else src, dst_ref=slot(block),
            send_sem=send_sems.at[k], recv_sem=recv_sems.at[k], device_id=to, device_id_type=_MESH)

    mine = pltpu.make_async_copy(x_ref, slot(me), local_sem)
    first = [copy(0, me, sibling, src=x_ref)]
    first += [copy(1 + j, me, (*chip, c), src=x_ref) for j, chip in enumerate(chips)]
    passed = [copy(4 + j, (*chip, c), sibling) for j, chip in enumerate(chips)]

    def start():
        mine.start()
        for cp in first:
            cp.start()

    def finish():
        for j, chip in enumerate(chips):
            copy(1 + j, (*chip, c), me).wait_recv()
            passed[j].start()
        copy(0, sibling, me).wait_recv()
        for j, chip in enumerate(chips):
            copy(4 + j, (*chip, 1 - c), me).wait_recv()
        for cp in first + passed:
            cp.wait_send()
        mine.wait()

    return start, finish


def _exchange_ops(p_ref, out_ref, send_sems, recv_sems, local_sem):
    x, y, c = lax.axis_index("x"), lax.axis_index("y"), lax.axis_index("c")
    me = _flat_index(x, y, c)
    peers = [(1 - x if k & 4 else x, 1 - y if k & 2 else y, 1 - c if k & 1 else c) for k in range(1, N_DEV)]

    def copy(k, peer, dst_slot):
        return pltpu.make_async_remote_copy(
            src_ref=p_ref.at[_flat_index(*peer)], dst_ref=out_ref.at[dst_slot],
            send_sem=send_sems.at[k], recv_sem=recv_sems.at[k], device_id=peer, device_id_type=_MESH)

    mine = pltpu.make_async_copy(p_ref.at[me], out_ref.at[me], local_sem)
    sends = [copy(k, peer, me) for k, peer in enumerate(peers)]

    def start():
        mine.start()
        for cp in sends:
            cp.start()

    def finish():
        for k, peer in enumerate(peers):
            copy(k, peer, _flat_index(*peer)).wait_recv()
        for cp in sends:
            cp.wait_send()
        mine.wait()

    return start, finish


_COMM = {"gather": _gather_ops, "exchange": _exchange_ops}
_COMM_SEMS = [pltpu.SemaphoreType.DMA((N_DEV - 1,)), pltpu.SemaphoreType.DMA((N_DEV - 1,)), pltpu.SemaphoreType.DMA]


def _comm_out_shape(kind, arr):
    return jax.ShapeDtypeStruct(((N_DEV,) + arr.shape) if kind == "gather" else arr.shape, arr.dtype)


def _comm_call(kind, arr, name):
    def body(src_ref, dst_ref, send_sems, recv_sems, local_sem):
        start, finish = _COMM[kind](src_ref, dst_ref, send_sems, recv_sems, local_sem)
        start()
        finish()

    return pl.pallas_call(body, out_shape=_comm_out_shape(kind, arr), in_specs=[_HBM], out_specs=_HBM,
                          scratch_shapes=_COMM_SEMS, name=name)(arr)


def all_gather(x, *, name):
    return _comm_call("gather", x, name)


def exchange(parts, *, name):
    return _comm_call("exchange", parts, name)


TR = 256
TR_WIDE = 128
BLK = 256
SB_BLK = 256
FFN_TM = 1024
CONV_CHUNK = 512
DT_PAD = LANE

BIG = ("w_in", "w_branch", "w_out", "ffn_w_gu", "ffn_w_down")
SMALL = ("norm_mix_pre", "norm_mix_post", "norm_ffn_pre", "norm_ffn_post", "b_gate", "ret_gn_w", "ssd_conv_w",
         "ssd_conv_b", "ssd_dt_bias", "ssd_a_log", "ssd_d", "ssd_norm_w")


def _row(v):
    return v.reshape(1, -1)


def _pad_lanes(v):
    return jnp.pad(v.reshape(1, -1), ((0, 0), (0, LANE - v.shape[-1])))


def _rows_of(segments, lo, hi):
    out, start = [], 0
    for seg in segments:
        a, b = max(lo, start), min(hi, start + seg.shape[0])
        if a < b:
            out.append(seg[a - start:b - start])
        start += seg.shape[0]
    return out


def _assemble_w_in(g, d):
    blocks = [g[j] for j in range(g.shape[0])]
    total = g.shape[0] * g.shape[1]
    n_main, n_dt = 5 * d, total - 8 * d
    w_cat = jnp.concatenate(_rows_of(blocks, 0, n_main) + _rows_of(blocks, n_main + n_dt, total), axis=0)
    dt = jnp.concatenate(_rows_of(blocks, n_main, n_main + n_dt), axis=0)
    return w_cat, jnp.pad(dt, ((0, DT_PAD - n_dt), (0, 0)))


def _split_dw_in(dw, dw_dt, d, n_dt):
    n_main = 5 * d
    segments = [dw[:n_main], dw_dt[:n_dt], dw[n_main:]]
    rows = (8 * d + n_dt) // N_DEV
    return jnp.stack([jnp.concatenate(_rows_of(segments, j * rows, (j + 1) * rows), axis=0) for j in range(N_DEV)])


def _layer_fwd(x, lw, cosf, sinf, next_shards):
    s, d = x.shape
    heads = d // 2 // HEAD
    nxt = {}
    (h,) = rowwise(f_rms_pre, [x], [lw["norm_mix_pre"]], [(d, BF16)], tr=TR, name="mix_pre_norm")
    if next_shards is None:
        proj = matmul(h, lw["w_cat"], tb=True, tn=2048, name="in_proj")
    else:
        proj, (nxt["w_in"],) = matmul(h, lw["w_cat"], tb=True, tn=2048, hosted=[("gather", next_shards["w_in"])],
                                      name="in_proj_gather")
    dt_raw = matmul(h, lw["w_dt"], tb=True, name="in_proj_dt")
    ret_cols = tuple(i * heads for i in range(4))
    y_ret, ret_states = retention_fwd(proj, ret_cols, lw["ret_gn_w"], cosf, sinf, heads, blk=BLK, name="retention_fwd")
    y_sb, sb_run, sb_visited = sb_attention_fwd(proj, 4 * heads, 5 * heads, 6 * heads, heads, blk=SB_BLK,
                                                name="stickbreak_fwd")
    u_pad = jnp.pad(proj[:, 4 * d:5 * d], ((CONV_PAD, CONV_PAD), (0, 0)))
    xc = ssd_conv_fwd(u_pad, lw["conv_taps"], lw["ssd_conv_b"], chunk=CONV_CHUNK, name="ssd_conv_fwd")
    y_scan, ssd_states = ssd_scan_fwd(xc, dt_raw, 0, lw["ssd_dt_bias"], lw["ssd_a_log"], lw["ssd_d"], blk=BLK,
                                      name="ssd_scan_fwd")
    z_spec = (proj, d // 2, 7)
    (y_ssd,) = rowwise(f_ssd_gate, [y_scan, z_spec], [lw["ssd_norm_w"]], [(d // 2, BF16)], tr=TR, name="ssd_gate_norm")
    y3 = jnp.stack([y_ret, y_sb, y_ssd])
    merged, u3 = branch_merge(y3, lw["w_branch"], proj, 5, lw["b_gate"], tm=512, tn=1024, name="branch_merge")
    o = matmul(merged, lw["w_out"], name="out_proj")
    (x1,) = rowwise(f_rms_post, [x, o], [lw["norm_mix_post"]], [(d, F32)], tr=TR, name="mix_post_norm")
    (h2,) = rowwise(f_rms_pre, [x1], [lw["norm_ffn_pre"]], [(d, BF16)], tr=TR, name="ffn_pre_norm")
    w_gu = lw["ffn_w_gu"].reshape(N_DEV, 2, -1, d)
    if next_shards is None:
        gu, act, _ = ffn_up_swiglu(h2, w_gu, tm=FFN_TM, name="ffn_up_swiglu")
    else:
        rest = [n for n in BIG if n != "w_in"]
        gu, act, got = ffn_up_swiglu(h2, w_gu, tm=FFN_TM, hosted=[("gather", next_shards[n]) for n in rest],
                                     name="ffn_up_swiglu_gather")
        nxt.update(zip(rest, got))
    f = matmul(act, lw["ffn_w_down"], lead_a="k", lead_b="k", name="ffn_down")
    (x2,) = rowwise(f_rms_post, [x1, f], [lw["norm_ffn_post"]], [(d, F32)], tr=TR, name="ffn_post_norm")
    res = dict(x=x, h=h, proj=proj, dt_raw=dt_raw, ret_states=ret_states, sb_run=sb_run, sb_visited=sb_visited, xc=xc,
               u_pad=u_pad,
               y_scan=y_scan, ssd_states=ssd_states, y3=y3, u3=u3, merged=merged, o=o, x1=x1, h2=h2, gu=gu, act=act, f=f)
    return x2, res, (nxt if next_shards is not None else None)


def _layer_bwd(dx2, res, lw, cosf, sinf, pending):
    x, proj = res["x"], res["proj"]
    s, d = x.shape
    heads = d // 2 // HEAD
    n_dt = d // 2 // SSD_P
    got = {}
    df, dn_ffn_post = rowwise_vjp(f_rms_post, [res["x1"], res["f"]], [lw["norm_ffn_post"]], [dx2],
                                  [(1, BF16, None)], [0], tr=TR, name="ffn_post_norm_bwd")
    dw_down = matmul(res["act"], df, ta=True, lead_a="batch", out_dtype=BF16, name="ffn_down_dw")
    dgu = ffn_down_dx_swiglu(df, lw["ffn_w_down"], res["gu"], tm=FFN_TM, name="ffn_down_dx_swiglu")
    dgu = dgu.reshape(2 * N_DEV, s, -1)
    if pending is None:
        dh2 = matmul(dgu, lw["ffn_w_gu"], lead_a="k", lead_b="k", name="ffn_up_dx")
        dw_gu = matmul(dgu, res["h2"], ta=True, lead_a="batch", out_dtype=BF16, name="ffn_up_dw")
    else:
        dh2, (got["prev_w_branch"], got["prev_w_out"]) = matmul(
            dgu, lw["ffn_w_gu"], lead_a="k", lead_b="k",
            hosted=[("exchange", pending["w_branch"]), ("exchange", pending["w_out"])], name="ffn_up_dx_exchange")
        dw_gu, (got["prev_w_in"],) = matmul(dgu, res["h2"], ta=True, lead_a="batch", out_dtype=BF16,
                                            hosted=[("exchange", pending["w_in"])], name="ffn_up_dw_exchange")
    dw_gu = dw_gu.reshape(N_DEV, -1, d)
    dx1, dn_ffn_pre = rowwise_vjp(f_rms_pre, [res["x1"]], [lw["norm_ffn_pre"]], [dh2], [(0, F32, dx2)], [0],
                                  tr=TR, name="ffn_pre_norm_bwd")
    do, dn_mix_post = rowwise_vjp(f_rms_post, [x, res["o"]], [lw["norm_mix_post"]], [dx1], [(1, BF16, None)], [0],
                                  tr=TR, name="mix_post_norm_bwd")
    dmerged = matmul(do, lw["w_out"], tb=True, name="out_proj_dx")
    dw_out = matmul(res["merged"], do, ta=True, out_dtype=BF16, name="out_proj_dw")
    du3, d_gate_logits, *db_gate = gate_merge_bwd(res["u3"], proj, 5, lw["b_gate"], dmerged, tr=TR_WIDE,
                                                  name="gate_merge_bwd")
    dy3 = matmul(du3, lw["w_branch"], tb=True, lead_a="batch", lead_b="batch", name="branch_proj_dx")
    dw_branch = matmul(res["y3"], du3, ta=True, lead_a="batch", lead_b="batch", out_dtype=BF16, name="branch_proj_dw")
    ret_cols = tuple(i * heads for i in range(4))
    dq, dk, dv, dg, d_gn = retention_bwd(proj, ret_cols, lw["ret_gn_w"], cosf, sinf, res["ret_states"], (dy3, 0), heads,
                                         blk=BLK, name="retention_bwd")
    dsq, dsk, dsv = sb_attention_bwd(proj, (dy3, 1), res["sb_run"], res["sb_visited"], 4 * heads, 5 * heads, 6 * heads,
                                     heads, blk=SB_BLK, name="stickbreak_bwd")
    z_spec = (proj, d // 2, 7)
    dy_scan, dz, d_ssd_norm = rowwise_vjp(f_ssd_gate, [res["y_scan"], z_spec], [lw["ssd_norm_w"]], [(dy3, d // 2, 0, 2)],
                                          [(0, F32, None), (1, BF16, None)], [0], tr=TR, name="ssd_gate_norm_bwd")
    dxs, dbm, dcm, ddt, d_dtb, d_alog, d_dskip = ssd_scan_bwd(
        res["xc"], res["dt_raw"], 0, lw["ssd_dt_bias"], lw["ssd_a_log"], lw["ssd_d"], res["ssd_states"], dy_scan,
        blk=BLK, name="ssd_scan_bwd")
    dxc_pad = jnp.pad(jnp.concatenate([dxs, dbm, dcm], axis=1), ((0, CONV_PAD), (0, 0)))
    du, d_taps, d_conv_b = ssd_conv_bwd(res["u_pad"], lw["conv_taps"], lw["ssd_conv_b"], dxc_pad, chunk=CONV_CHUNK,
                                        name="ssd_conv_bwd")
    dproj = jnp.concatenate([dq, dk, dv, dg, dsq, dsk, dsv, dz, du, d_gate_logits], axis=1)
    dh_dt = matmul(ddt, lw["w_dt"], name="in_proj_dt_dx")
    dh, (got["ffn_w_down"], got["ffn_w_gu"]) = matmul(
        dproj, lw["w_cat"], add=dh_dt, hosted=[("exchange", dw_down), ("exchange", dw_gu)],
        name="in_proj_dx_exchange")
    dw_cat = matmul(dproj, res["h"], ta=True, tn=2048, out_dtype=BF16, name="in_proj_dw")
    dw_dt = matmul(ddt, res["h"], ta=True, out_dtype=BF16, name="in_proj_dt_dw")
    dx, dn_mix_pre = rowwise_vjp(f_rms_pre, [x], [lw["norm_mix_pre"]], [dh], [(0, F32, dx1)], [0], tr=TR,
                                 name="mix_pre_norm_bwd")
    mine = dict(
        w_in=_split_dw_in(dw_cat, dw_dt, d, n_dt),
        w_branch=jnp.transpose(dw_branch.reshape(3, d // 2, N_DEV, -1), (2, 0, 1, 3)).reshape(N_DEV, 3 * d // 2, -1),
        w_out=dw_out.reshape(N_DEV, d // N_DEV, d),
    )
    small = dict(
        norm_mix_pre=dn_mix_pre[0], norm_mix_post=dn_mix_post[0], norm_ffn_pre=dn_ffn_pre[0],
        norm_ffn_post=dn_ffn_post[0], b_gate=jnp.concatenate([b[0] for b in db_gate]), ret_gn_w=d_gn[0],
        ssd_conv_w=d_taps, ssd_conv_b=d_conv_b[0], ssd_dt_bias=d_dtb[0, :n_dt], ssd_a_log=d_alog[0, :n_dt],
        ssd_d=d_dskip[0, :n_dt], ssd_norm_w=d_ssd_norm[0],
    )
    return dx, got, mine, small


def _adam_rows(cols):
    return max(8, (1 << 17) // cols // 8 * 8)


def kernel(x, positions, norm_mix_pre, norm_mix_post, norm_ffn_pre, norm_ffn_post, w_in, b_gate, ret_gn_w, ssd_conv_w, ssd_conv_b, ssd_dt_bias, ssd_a_log, ssd_d, ssd_norm_w, w_branch_ret, w_branch_sb, w_branch_ssd, w_out, ffn_w_gate, ffn_w_up, ffn_w_down, loss_target, m_norm_mix_pre, m_norm_mix_post, m_norm_ffn_pre, m_norm_ffn_post, m_w_in, m_b_gate, m_ret_gn_w, m_ssd_conv_w, m_ssd_conv_b, m_ssd_dt_bias, m_ssd_a_log, m_ssd_d, m_ssd_norm_w, m_w_branch_ret, m_w_branch_sb, m_w_branch_ssd, m_w_out, m_ffn_w_gate, m_ffn_w_up, m_ffn_w_down, v_norm_mix_pre, v_norm_mix_post, v_norm_ffn_pre, v_norm_ffn_post, v_w_in, v_b_gate, v_ret_gn_w, v_ssd_conv_w, v_ssd_conv_b, v_ssd_dt_bias, v_ssd_a_log, v_ssd_d, v_ssd_norm_w, v_w_branch_ret, v_w_branch_sb, v_w_branch_ssd, v_w_out, v_ffn_w_gate, v_ffn_w_up, v_ffn_w_down):
    depth = w_in.shape[0]
    s, d = x.shape[1], x.shape[2]
    axes = ("x", "y", "c")
    me = _flat_index(lax.axis_index("x"), lax.axis_index("y"), lax.axis_index("c"))

    def tr_(a):
        return jnp.swapaxes(a, 1, 2)

    transposed = ("w_in", "ffn_w_gate", "ffn_w_up")
    wmv = dict(w_in=tuple(map(tr_, (w_in, m_w_in, v_w_in))), w_branch_ret=(w_branch_ret, m_w_branch_ret, v_w_branch_ret),
               w_branch_sb=(w_branch_sb, m_w_branch_sb, v_w_branch_sb),
               w_branch_ssd=(w_branch_ssd, m_w_branch_ssd, v_w_branch_ssd), w_out=(w_out, m_w_out, v_w_out),
               ffn_w_gate=tuple(map(tr_, (ffn_w_gate, m_ffn_w_gate, v_ffn_w_gate))),
               ffn_w_up=tuple(map(tr_, (ffn_w_up, m_ffn_w_up, v_ffn_w_up))),
               ffn_w_down=(ffn_w_down, m_ffn_w_down, v_ffn_w_down))
    members = dict(w_in=["w_in"], w_branch=["w_branch_ret", "w_branch_sb", "w_branch_ssd"], w_out=["w_out"],
                   ffn_w_gu=["ffn_w_gate", "ffn_w_up"], ffn_w_down=["ffn_w_down"])
    big_w = {g: (wmv[ns[0]][0] if len(ns) == 1 else jnp.concatenate([wmv[n][0] for n in ns], axis=1))
             for g, ns in members.items()}

    taps_all = all_gather(ssd_conv_w.reshape(-1, LANE), name="gather_conv_w")
    taps_all = jnp.transpose(taps_all.reshape(N_DEV, depth, SSD_K, -1), (1, 2, 0, 3)).reshape(depth, SSD_K, -1)

    cosf, sinf = rope_tables(positions.reshape(s), tr=TR)
    small_w = dict(norm_mix_pre=norm_mix_pre, norm_mix_post=norm_mix_post, norm_ffn_pre=norm_ffn_pre,
                   norm_ffn_post=norm_ffn_post, b_gate=b_gate, ret_gn_w=ret_gn_w, ssd_conv_b=ssd_conv_b,
                   ssd_dt_bias=ssd_dt_bias, ssd_a_log=ssd_a_log, ssd_d=ssd_d, ssd_norm_w=ssd_norm_w, taps=taps_all)

    def layer_weights(sw, gathered):
        lw = dict(gathered)
        for n in ("norm_mix_pre", "norm_mix_post", "norm_ffn_pre", "norm_ffn_post", "ret_gn_w", "ssd_conv_b", "ssd_norm_w"):
            lw[n] = _row(sw[n])
        for n in ("ssd_dt_bias", "ssd_a_log", "ssd_d"):
            lw[n] = _pad_lanes(sw[n])
        lw["b_gate"] = [_row(sw["b_gate"][i * d:(i + 1) * d]) for i in range(3)]
        lw["conv_taps"] = [sw["taps"][k:k + 1] for k in range(SSD_K)]
        return lw

    def layer_slice(t, l):
        return {n: a[l] for n, a in t.items()}

    def bf16_shards(l):
        return {n: big_w[n][l].astype(BF16) for n in BIG}

    def arrange(g):
        w_cat, w_dt = _assemble_w_in(g["w_in"], d)
        return dict(
            w_cat=w_cat, w_dt=w_dt,
            w_branch=jnp.transpose(g["w_branch"].reshape(N_DEV, 3, d // 2, -1), (1, 2, 0, 3)).reshape(3, d // 2, d),
            w_out=g["w_out"].reshape(d, d),
            ffn_w_gu=g["ffn_w_gu"].reshape(2 * N_DEV, -1, d),
            ffn_w_down=g["ffn_w_down"],
        )

    big_out = {}

    def adam(l, group, parts):
        rows, cols = big_w[group].shape[1:]
        parts = parts.reshape(N_DEV, rows, cols)
        off = 0
        for n in members[group]:
            big_out[n] = adamw_into(*wmv[n], parts, l, off, big_out.get(n), tr=_adam_rows(cols), name="adamw_" + n)
            off += wmv[n][0].shape[1]

    xs, saved = x.reshape(s, d), []
    gathered = {n: all_gather(a, name="gather_" + n) for n, a in bf16_shards(0).items()}
    for l in range(depth):
        lw = layer_weights(layer_slice(small_w, l), arrange(gathered))
        xs, res, gathered = _layer_fwd(xs, lw, cosf, sinf, bf16_shards(l + 1) if l + 1 < depth else None)
        saved.append((res, lw))
    loss_tile, dy = loss_head(xs, loss_target.reshape(s, d), tr=TR, name="loss_head")
    loss = lax.psum(loss_tile[0, 0], axes)

    dx, pending, small_layers = dy, None, [None] * depth
    for l in reversed(range(depth)):
        res, lw = saved[l]
        dx, got, pending, small_layers[l] = _layer_bwd(dx, res, lw, cosf, sinf, pending)
        for n in ("ffn_w_gu", "ffn_w_down"):
            adam(l, n, got[n])
        if l + 1 < depth:
            for n in ("w_in", "w_branch", "w_out"):
                adam(l + 1, n, got["prev_" + n])
    for n in ("w_in", "w_branch", "w_out"):
        adam(0, n, exchange(pending[n], name="exchange_" + n))
    small_g = {n: jnp.stack([small_layers[l][n] for l in range(depth)]) for n in SMALL}

    n_dt = ssd_dt_bias.shape[-1]
    small_in = dict(norm_mix_pre=(norm_mix_pre, m_norm_mix_pre, v_norm_mix_pre), norm_mix_post=(norm_mix_post, m_norm_mix_post, v_norm_mix_post),
                    norm_ffn_pre=(norm_ffn_pre, m_norm_ffn_pre, v_norm_ffn_pre), norm_ffn_post=(norm_ffn_post, m_norm_ffn_post, v_norm_ffn_post),
                    b_gate=(b_gate, m_b_gate, v_b_gate), ret_gn_w=(ret_gn_w, m_ret_gn_w, v_ret_gn_w),
                    ssd_conv_b=(ssd_conv_b, m_ssd_conv_b, v_ssd_conv_b), ssd_dt_bias=(ssd_dt_bias, m_ssd_dt_bias, v_ssd_dt_bias),
                    ssd_a_log=(ssd_a_log, m_ssd_a_log, v_ssd_a_log), ssd_d=(ssd_d, m_ssd_d, v_ssd_d),
                    ssd_norm_w=(ssd_norm_w, m_ssd_norm_w, v_ssd_norm_w))
    rep = [n for n in SMALL if n != "ssd_conv_w"]

    def pack(arrs):
        flat = jnp.concatenate([a.reshape(-1) for a in arrs])
        rows = -(-flat.shape[0] // (16 * LANE)) * 16
        return jnp.pad(flat, (0, rows * LANE - flat.shape[0])).reshape(rows, LANE)

    conv_g = small_g["ssd_conv_w"]
    g_pack = pack([small_g[n] for n in rep] + [conv_g])
    g_all = all_gather(g_pack, name="gather_small_grads")
    zeros_conv = jnp.zeros_like(conv_g)
    w_pack, m_pack, v_pack = (pack([small_in[n][i] for n in rep] + [zeros_conv]) for i in range(3))
    sm = adamw_sum(w_pack, m_pack, v_pack, g_all, tr=TR, name="adamw_small")

    def unpack(p):
        flat, out, off = p.reshape(-1), {}, 0
        for n in rep:
            shp = small_in[n][0].shape
            size = math.prod(shp)
            out[n] = flat[off:off + size].reshape(shp)
            off += size
        out["conv_sum"] = flat[off:off + conv_g.size].reshape(conv_g.shape)
        return out

    sm = [unpack(p) for p in sm]
    ch = ssd_conv_w.shape[-1]
    conv_mine = lax.dynamic_slice_in_dim(sm[0]["conv_sum"], me * ch, ch, axis=2)
    conv_out = adamw_sum(ssd_conv_w.reshape(-1, LANE), m_ssd_conv_w.reshape(-1, LANE), v_ssd_conv_w.reshape(-1, LANE),
                         conv_mine.reshape(1, -1, LANE), tr=TR, name="adamw_conv_w")
    for i in range(4):
        sm[i]["ssd_conv_w"] = conv_out[i].reshape(ssd_conv_w.shape)

    def big_named(i):
        return {n: (tr_(out[i]) if n in transposed else out[i]) for n, out in big_out.items()}

    order = ["norm_mix_pre", "norm_mix_post", "norm_ffn_pre", "norm_ffn_post", "w_in", "b_gate", "ret_gn_w", "ssd_conv_w",
             "ssd_conv_b", "ssd_dt_bias", "ssd_a_log", "ssd_d", "ssd_norm_w", "w_branch_ret", "w_branch_sb", "w_branch_ssd",
             "w_out", "ffn_w_gate", "ffn_w_up", "ffn_w_down"]
    outs = [loss, dx.reshape(x.shape)]
    for i in range(4):
        named = {**sm[i], **big_named(i)}
        outs += [named[n] for n in order]
    return tuple(outs)
```

```python
import functools
import math

import jax
import jax.numpy as jnp
import numpy as np
from jax import lax
from jax.experimental import pallas as pl
from jax.experimental.pallas import tpu as pltpu

F32 = jnp.float32
BF16 = jnp.bfloat16

N_DEV = 8
HEAD = 128
SSD_P = 64
SSD_G = 4
SSD_N = 128
SSD_K = 4
CHUNK = 64
NORM_EPS = 1e-6
ROPE_BASE = 10000.0
LANE = 128
VMEM_LIMIT = 56 * 1024 * 1024

ADAM_LR, ADAM_B1, ADAM_B2, ADAM_EPS, ADAM_WD, ADAM_STEP = 0.001, 0.9, 0.999, 1e-08, 0.01, 10


def _cparams(sem):
    return pltpu.CompilerParams(dimension_semantics=sem, vmem_limit_bytes=VMEM_LIMIT)


def _pick(n, pref):
    if n <= pref:
        return n
    t = pref
    while t >= LANE:
        if n % t == 0:
            return t
        t -= LANE
    return n


def matmul(a, b, *, ta=False, tb=False, lead_a=None, lead_b=None, out_dtype=F32, add=None, hosted=(),
           tm=1024, tn=1024, tk=2048, name="mm"):
    la, lb = lead_a is not None, lead_b is not None
    a2, b2 = a.shape[1:] if la else a.shape, b.shape[1:] if lb else b.shape
    (kd_a, m) = a2 if ta else a2[::-1]
    (kd_b, n) = b2[::-1] if tb else b2
    assert kd_a == kd_b, (a.shape, b.shape)
    nlead = a.shape[0] if la else (b.shape[0] if lb else 1)
    batch = "batch" in (lead_a, lead_b)
    kblocks = nlead if "k" in (lead_a, lead_b) else 1
    if la and lb:
        assert lead_a == lead_b and a.shape[0] == b.shape[0]
    tm, tn, tk = _pick(m, tm), _pick(n, tn), _pick(kd_a, tk)
    kt = kd_a // tk
    nk = kt * kblocks
    grid = (m // tm, (nlead if batch else 1), n // tn, nk)

    def lead_idx(g, k):
        return g if batch else k // kt

    def a_map(i, g, j, k):
        idx = (k % kt, i) if ta else (i, k % kt)
        return ((lead_idx(g, k),) + idx) if la else idx

    def b_map(i, g, j, k):
        idx = (j, k % kt) if tb else (k % kt, j)
        return ((lead_idx(g, k),) + idx) if lb else idx

    a_blk = (tk, tm) if ta else (tm, tk)
    b_blk = (tn, tk) if tb else (tk, tn)
    a_spec = pl.BlockSpec(((None,) + a_blk) if la else a_blk, a_map)
    b_spec = pl.BlockSpec(((None,) + b_blk) if lb else b_blk, b_map)
    if batch:
        o_spec = pl.BlockSpec((None, tm, tn), lambda i, g, j, k: (g, i, j))
        o_shape = jax.ShapeDtypeStruct((nlead, m, n), out_dtype)
    else:
        o_spec = pl.BlockSpec((tm, tn), lambda i, g, j, k: (i, j))
        o_shape = jax.ShapeDtypeStruct((m, n), out_dtype)
    dims = (((0 if ta else 1,), (1 if tb else 0,)), ((), ()))

    n_add, n_host = int(add is not None), len(hosted)
    n_acc = int(nk > 1)

    def body(a_ref, b_ref, *rest):
        c_ref = rest[0] if n_add else None
        src_refs = rest[n_add:n_add + n_host]
        o_ref = rest[n_add + n_host]
        dst_refs = rest[n_add + n_host + 1:n_add + 2 * n_host + 1]
        acc_ref = rest[n_add + 2 * n_host + 1] if n_acc else None
        sems = rest[n_add + 2 * n_host + 1 + n_acc:]
        k = pl.program_id(3)
        ids = [pl.program_id(ax) for ax in range(4)]
        comms = [_COMM[kind](src_refs[c], dst_refs[c], *sems[3 * c:3 * c + 3]) for c, (kind, _) in enumerate(hosted)]

        if hosted:
            @pl.when(functools.reduce(jnp.logical_and, [i == 0 for i in ids]))
            def _():
                for start, _ in comms:
                    start()

        def product():
            return lax.dot_general(a_ref[...].astype(BF16), b_ref[...].astype(BF16), dims, preferred_element_type=F32)

        if n_acc:
            @pl.when(k == 0)
            def _():
                acc_ref[...] = jnp.zeros_like(acc_ref) if c_ref is None else c_ref[...].astype(F32)

            acc_ref[...] += product()

            @pl.when(k == nk - 1)
            def _():
                o_ref[...] = acc_ref[...].astype(o_ref.dtype)
        else:
            o_ref[...] = (product() if c_ref is None else product() + c_ref[...].astype(F32)).astype(o_ref.dtype)

        if hosted:
            @pl.when(functools.reduce(jnp.logical_and, [i == g - 1 for i, g in zip(ids, grid)]))
            def _():
                for _, finish in comms:
                    finish()

    extra = ([] if add is None else [add]) + [arr for _, arr in hosted]
    extra_specs = ([] if add is None else [o_spec]) + [_HBM] * n_host
    out_shapes = [o_shape] + [_comm_out_shape(kind, arr) for kind, arr in hosted]
    scratch = [pltpu.VMEM((tm, tn), F32)] * n_acc + _COMM_SEMS * n_host
    sem = ("arbitrary",) * 4 if hosted else ("parallel", "parallel", "parallel", "arbitrary")
    res = pl.pallas_call(
        body, out_shape=out_shapes, grid=grid, in_specs=[a_spec, b_spec] + extra_specs,
        out_specs=[o_spec] + [_HBM] * n_host, scratch_shapes=scratch, name=name, compiler_params=_cparams(sem),
    )(a, b, *extra)
    return (res[0], res[1:]) if hosted else res[0]


def _row_spec(r, tr):
    if not isinstance(r, tuple):
        return r, pl.BlockSpec((tr, r.shape[-1]), lambda i: (i, 0))
    if len(r) == 3:
        arr, w, cb = r
        return arr, pl.BlockSpec((tr, w), lambda i: (i, cb))
    arr, w, cb, ld = r
    return arr, pl.BlockSpec((None, tr, w), lambda i: (ld, i, cb))


def _full_spec(c):
    nd = c.ndim
    return pl.BlockSpec(c.shape, lambda i: (0,) * nd)


def rowwise(fn, rows, consts, outs, *, tr, name):
    arrs, specs = zip(*[_row_spec(r, tr) for r in rows])
    n_rows = arrs[0].shape[-2]
    nr, nc = len(rows), len(consts)

    def body(*refs):
        vals = [r[...] for r in refs[:nr + nc]]
        res = fn(*vals)
        for o_ref, r in zip(refs[nr + nc:], res):
            o_ref[...] = r.astype(o_ref.dtype)

    return pl.pallas_call(
        body, grid=(n_rows // tr,),
        in_specs=list(specs) + [_full_spec(c) for c in consts],
        out_specs=[pl.BlockSpec((tr, w), lambda i: (i, 0)) for w, _ in outs],
        out_shape=[jax.ShapeDtypeStruct((n_rows, w), dt) for w, dt in outs],
        name=name, compiler_params=_cparams(("parallel",)),
    )(*arrs, *consts)


def rowwise_vjp(fn, rows, consts, cts, row_grads, const_grads, *, tr, name):
    arrs, specs = zip(*[_row_spec(r, tr) for r in rows])
    n_rows = arrs[0].shape[-2]
    nr, nc = len(rows), len(consts)
    ct_present = [c for c in cts if c is not None]
    ct_arrs, ct_specs = zip(*[_row_spec(c, tr) for c in ct_present])
    add_present = [g[2] for g in row_grads if g[2] is not None]
    add_arrs, add_specs = zip(*[_row_spec(c, tr) for c in add_present]) if add_present else ((), ())
    n_ct, n_add = len(ct_present), len(add_present)
    widths = [s.block_shape[-1] for s in specs]

    def body(*refs):
        ins = refs[:nr + nc]
        ct_refs = refs[nr + nc:nr + nc + n_ct]
        add_refs = refs[nr + nc + n_ct:nr + nc + n_ct + n_add]
        out_refs = refs[nr + nc + n_ct + n_add:]
        vals = [r[...] for r in ins]
        res, f_vjp = jax.vjp(fn, *vals)
        it = iter(ct_refs)
        ct_vals = tuple(next(it)[...].astype(r.dtype) if c is not None else jnp.zeros_like(r)
                        for c, r in zip(cts, res))
        grads = f_vjp(ct_vals)
        ita = iter(add_refs)
        for o_ref, (idx, _, add) in zip(out_refs, row_grads):
            g = grads[idx].astype(F32)
            if add is not None:
                g = g + next(ita)[...].astype(F32)
            o_ref[...] = g.astype(o_ref.dtype)
        first = pl.program_id(0) == 0
        for o_ref, idx in zip(out_refs[len(row_grads):], const_grads):
            g = grads[nr + idx].astype(F32)

            @pl.when(first)
            def _():
                o_ref[...] = g

            @pl.when(jnp.logical_not(first))
            def _():
                o_ref[...] += g

    out_specs = [pl.BlockSpec((tr, widths[idx]), lambda i: (i, 0)) for idx, _, _ in row_grads]
    out_shape = [jax.ShapeDtypeStruct((n_rows, widths[idx]), dt) for idx, dt, _ in row_grads]
    out_specs += [_full_spec(consts[idx]) for idx in const_grads]
    out_shape += [jax.ShapeDtypeStruct(consts[idx].shape, F32) for idx in const_grads]
    return pl.pallas_call(
        body, grid=(n_rows // tr,),
        in_specs=list(specs) + [_full_spec(c) for c in consts] + list(ct_specs) + list(add_specs),
        out_specs=out_specs, out_shape=out_shape,
        name=name, compiler_params=_cparams(("arbitrary",)),
    )(*arrs, *consts, *ct_arrs, *add_arrs)


def f_rms(x, w):
    xf = x.astype(F32)
    return xf * lax.rsqrt(jnp.mean(xf * xf, axis=-1, keepdims=True) + NORM_EPS) * w


def f_rms_pre(x, w):
    return (f_rms(x, w),)


def f_rms_post(x, o, w):
    return (x + f_rms(o, w),)


def f_ssd_gate(y, z, w):
    return (f_rms(y * jax.nn.silu(z), w),)


@jax.custom_vjp
def _swap_halves(x):
    return pltpu.roll(x, HEAD // 2, 1)


_swap_halves.defvjp(lambda x: (_swap_halves(x), None), lambda _, g: (_swap_halves(g),))


def rope_tables(positions, *, tr):
    s = positions.shape[0]
    half = HEAD // 2
    inv = ROPE_BASE ** (-2.0 * jnp.arange(half, dtype=F32) / HEAD)
    inv = jnp.concatenate([inv, inv]).reshape(1, HEAD)
    sign = jnp.concatenate([-jnp.ones((half,), F32), jnp.ones((half,), F32)]).reshape(1, HEAD)

    def fn(pos, inv, sign):
        ang = pos.astype(F32) * inv
        return jnp.cos(ang), jnp.sin(ang) * sign

    return rowwise(fn, [positions.reshape(s, 1)], [inv, sign], [(HEAD, F32), (HEAD, F32)], tr=tr, name="rope_tables")


def _ret_consts(n_heads, blk):
    lg = np.log1p(-np.exp2(-5.0 - np.arange(n_heads)))[:, None, None]
    i = np.arange(blk)
    dist = np.abs(i[:, None] - i[None, :])[None]
    allowed = ((i[None, :] // CHUNK) <= (i[:, None] // CHUNK))[None]
    dm = np.where(allowed, np.exp(lg * dist), 0.0)
    qd = np.broadcast_to(np.exp(lg * (i[None, :, None] + 1.0)), (n_heads, blk, HEAD))
    kd = np.broadcast_to(np.exp(lg * (blk - 1.0 - i[None, :, None])), (n_heads, blk, HEAD))
    cd = np.broadcast_to(np.exp(lg * blk), (n_heads, 1, HEAD))
    return [jnp.asarray(a, F32) for a in (dm, qd, kd, cd)]


def _ret_block(q, k, v, g, gnw, state, cosf, sinf, dm, qd, kd, cd):
    qr = q * cosf + _swap_halves(q) * sinf
    kr = (k * cosf + _swap_halves(k) * sinf) * (HEAD ** -0.5)
    vb = v.astype(BF16)
    scores = _dot(qr.astype(BF16), kr.astype(BF16), _NT) * dm
    o = _dot(scores.astype(BF16), vb) + _dot((qr * qd).astype(BF16), state.astype(BF16))
    new_state = state * cd + _dot((kr * kd).astype(BF16), vb, _TN)
    mu = jnp.mean(o, axis=-1, keepdims=True)
    var = jnp.mean(jnp.square(o - mu), axis=-1, keepdims=True)
    y = (o - mu) * lax.rsqrt(var + NORM_EPS) * gnw * jax.nn.silu(g)
    return y, new_state


def _ret_specs(n_heads, blk, nb, cols, reverse):
    jm = (lambda j: nb - 1 - j) if reverse else (lambda j: j)
    col = lambda c0: pl.BlockSpec((blk, HEAD), lambda h, j: (jm(j), c0 + h))
    tab = pl.BlockSpec((blk, HEAD), lambda h, j: (jm(j), 0))
    specs = [col(c) for c in cols]
    specs += [pl.BlockSpec((1, HEAD), lambda h, j: (0, h)), tab, tab]
    specs += [pl.BlockSpec((None, blk, blk), lambda h, j: (h, 0, 0)),
              pl.BlockSpec((None, blk, HEAD), lambda h, j: (h, 0, 0)),
              pl.BlockSpec((None, blk, HEAD), lambda h, j: (h, 0, 0)),
              pl.BlockSpec((None, 1, HEAD), lambda h, j: (h, 0, 0))]
    state = pl.BlockSpec((None, None, HEAD, HEAD), lambda h, j: (h, jm(j), 0, 0))
    out_col = pl.BlockSpec((blk, HEAD), lambda h, j: (jm(j), h))
    return specs, state, out_col


def retention_fwd(proj, cols, gn_w, cosf, sinf, n_heads, *, blk, name):
    s = proj.shape[0]
    nb = s // blk
    consts = _ret_consts(n_heads, blk)
    specs, state_spec, out_col = _ret_specs(n_heads, blk, nb, cols, False)

    def body(q_ref, k_ref, v_ref, g_ref, gn_ref, cos_ref, sin_ref, dm_ref, qd_ref, kd_ref, cd_ref,
             y_ref, st_ref, state):
        @pl.when(pl.program_id(1) == 0)
        def _():
            state[...] = jnp.zeros_like(state)

        st = state[...]
        st_ref[...] = st
        y, new_state = _ret_block(q_ref[...], k_ref[...], v_ref[...], g_ref[...], gn_ref[...], st,
                                  cos_ref[...], sin_ref[...], dm_ref[...], qd_ref[...], kd_ref[...], cd_ref[...])
        y_ref[...] = y.astype(y_ref.dtype)
        state[...] = new_state

    return pl.pallas_call(
        body, grid=(n_heads, nb), in_specs=specs, out_specs=[out_col, state_spec],
        out_shape=[jax.ShapeDtypeStruct((s, n_heads * HEAD), BF16),
                   jax.ShapeDtypeStruct((n_heads, nb, HEAD, HEAD), F32)],
        scratch_shapes=[pltpu.VMEM((HEAD, HEAD), F32)],
        name=name, compiler_params=_cparams(("parallel", "arbitrary")),
    )(proj, proj, proj, proj, gn_w, cosf, sinf, *consts)


def retention_bwd(proj, cols, gn_w, cosf, sinf, states, dy, n_heads, *, blk, name):
    s = proj.shape[0]
    nb = s // blk
    consts = _ret_consts(n_heads, blk)
    specs, state_spec, out_col = _ret_specs(n_heads, blk, nb, cols, True)

    def body(q_ref, k_ref, v_ref, g_ref, gn_ref, cos_ref, sin_ref, dm_ref, qd_ref, kd_ref, cd_ref,
             st_ref, dy_ref, dq_ref, dk_ref, dv_ref, dg_ref, dgn_ref, dstate):
        first = pl.program_id(1) == 0

        @pl.when(first)
        def _():
            dstate[...] = jnp.zeros_like(dstate)
            dgn_ref[...] = jnp.zeros_like(dgn_ref)

        tabs = (cos_ref[...], sin_ref[...], dm_ref[...], qd_ref[...], kd_ref[...], cd_ref[...])
        fn = lambda q, k, v, g, gnw, st: _ret_block(q, k, v, g, gnw, st, *tabs)
        _, f_vjp = jax.vjp(fn, q_ref[...], k_ref[...], v_ref[...], g_ref[...], gn_ref[...], st_ref[...])
        dq, dk, dv, dg, dgn, dst = f_vjp((dy_ref[...].astype(F32), dstate[...]))
        dq_ref[...] = dq.astype(dq_ref.dtype)
        dk_ref[...] = dk.astype(dk_ref.dtype)
        dv_ref[...] = dv.astype(dv_ref.dtype)
        dg_ref[...] = dg.astype(dg_ref.dtype)
        dgn_ref[...] += dgn
        dstate[...] = dst

    o_shape = jax.ShapeDtypeStruct((s, n_heads * HEAD), BF16)
    dy_spec = out_col
    if isinstance(dy, tuple):
        dy, lead = dy
        dy_spec = pl.BlockSpec((None, blk, HEAD), lambda h, j: (lead, nb - 1 - j, h))
    return pl.pallas_call(
        body, grid=(n_heads, nb), in_specs=specs + [state_spec, dy_spec],
        out_specs=[out_col, out_col, out_col, out_col, pl.BlockSpec((1, HEAD), lambda h, j: (0, h))],
        out_shape=[o_shape, o_shape, o_shape, o_shape, jax.ShapeDtypeStruct((1, n_heads * HEAD), F32)],
        scratch_shapes=[pltpu.VMEM((HEAD, HEAD), F32)],
        name=name, compiler_params=_cparams(("parallel", "arbitrary")),
    )(proj, proj, proj, proj, gn_w, cosf, sinf, *consts, states, dy)


CONV_PAD = 8


def _conv_pre(u_ext, taps, bias, n_out):
    n = u_ext.shape[0]
    views = [pltpu.roll(u_ext, n - (k + CONV_PAD - SSD_K + 1), 0)[:n_out] for k in range(SSD_K)]
    pre = bias
    for k in range(SSD_K):
        pre = pre + taps[k] * views[k]
    return pre, views


def ssd_conv_fwd(u_pad, taps, bias, *, chunk, name):
    s, c = u_pad.shape[0] - 2 * CONV_PAD, u_pad.shape[1]

    def body(u_ref, t0, t1, t2, t3, b_ref, o_ref):
        taps_v = [t[...] for t in (t0, t1, t2, t3)]
        bias_v = b_ref[...]

        @pl.loop(0, s // chunk)
        def _(ci):
            r0 = pl.multiple_of(ci * chunk, chunk)
            pre, _ = _conv_pre(u_ref[pl.ds(r0, chunk + CONV_PAD), :], taps_v, bias_v, chunk)
            o_ref[pl.ds(r0, chunk), :] = pre * jax.nn.sigmoid(pre)

    row = pl.BlockSpec((1, LANE), lambda i: (0, i))
    return pl.pallas_call(
        body, grid=(c // LANE,),
        in_specs=[pl.BlockSpec((s + 2 * CONV_PAD, LANE), lambda i: (0, i))] + [row] * 5,
        out_specs=pl.BlockSpec((s, LANE), lambda i: (0, i)),
        out_shape=jax.ShapeDtypeStruct((s, c), F32),
        name=name, compiler_params=_cparams(("parallel",)),
    )(u_pad, *taps, bias)


def ssd_conv_bwd(u_pad, taps, bias, dxc_pad, *, chunk, name):
    s, c = u_pad.shape[0] - 2 * CONV_PAD, u_pad.shape[1]
    ext = chunk + CONV_PAD

    def body(u_ref, t0, t1, t2, t3, b_ref, d_ref, du_ref, dw_ref, db_ref):
        taps_v = [t[...] for t in (t0, t1, t2, t3)]
        bias_v = b_ref[...]
        dw_ref[...] = jnp.zeros_like(dw_ref)
        db_ref[...] = jnp.zeros_like(db_ref)

        @pl.loop(0, s // chunk)
        def _(ci):
            r0 = pl.multiple_of(ci * chunk, chunk)
            pre, views = _conv_pre(u_ref[pl.ds(r0, ext + CONV_PAD), :], taps_v, bias_v, ext)
            sig = jax.nn.sigmoid(pre)
            dpre = d_ref[pl.ds(r0, ext), :] * (sig * (1.0 + pre * (1.0 - sig)))
            du = taps_v[SSD_K - 1] * dpre[:chunk]
            for k in range(SSD_K - 1):
                du = du + taps_v[k] * pltpu.roll(dpre, ext - (SSD_K - 1 - k), 0)[:chunk]
            du_ref[pl.ds(r0, chunk), :] = du.astype(du_ref.dtype)
            own = dpre[:chunk]
            for k in range(SSD_K):
                dw_ref[k:k + 1, :] += jnp.sum(own * views[k][:chunk], axis=0, keepdims=True)
            db_ref[...] += jnp.sum(own, axis=0, keepdims=True)

    row = pl.BlockSpec((1, LANE), lambda i: (0, i))
    return pl.pallas_call(
        body, grid=(c // LANE,),
        in_specs=[pl.BlockSpec((s + 2 * CONV_PAD, LANE), lambda i: (0, i))] + [row] * 5
        + [pl.BlockSpec((s + CONV_PAD, LANE), lambda i: (0, i))],
        out_specs=[pl.BlockSpec((s, LANE), lambda i: (0, i)), pl.BlockSpec((SSD_K, LANE), lambda i: (0, i)), row],
        out_shape=[jax.ShapeDtypeStruct((s, c), BF16), jax.ShapeDtypeStruct((SSD_K, c), F32),
                   jax.ShapeDtypeStruct((1, c), F32)],
        name=name, compiler_params=_cparams(("parallel",)),
    )(u_pad, *taps, bias, dxc_pad)


def _tri_dot(tri, x, passes=3):
    out = None
    rem = x
    for _ in range(passes):
        piece = rem.astype(BF16)
        rem = rem - piece.astype(F32)
        d = _dot(tri, piece)
        out = d if out is None else out + d
    return out


def _tri(n, upper):
    rr = lax.broadcasted_iota(jnp.int32, (n, n), 0)
    cc = lax.broadcasted_iota(jnp.int32, (n, n), 1)
    return ((rr <= cc) if upper else (rr >= cc)).astype(BF16)


@jax.custom_vjp
def _cumsum_rows(a):
    return _tri_dot(_tri(a.shape[0], False), a)


_cumsum_rows.defvjp(lambda a: (_cumsum_rows(a), None), lambda _, g: (_tri_dot(_tri(g.shape[0], True), g),))


def _softplus(x):
    return jnp.maximum(x, 0.0) + jnp.log(1.0 + jnp.exp(-jnp.abs(x)))


def _ssd_block(x, bm, cm, dtraw, dtb, alog, dsk, state_t, group):
    blk, width = x.shape
    e_heads = width // SSD_P
    dt = _softplus(dtraw + dtb)
    acum = _cumsum_rows(dt * (-jnp.exp(alog)))
    acum_t = acum.T
    lane_h = lax.broadcasted_iota(jnp.int32, (1, LANE), 1)
    sub_h = lax.broadcasted_iota(jnp.int32, (LANE, 1), 0)
    lane_e = lax.broadcasted_iota(jnp.int32, (1, width), 1) // SSD_P
    causal = lax.broadcasted_iota(jnp.int32, (blk, blk), 0) >= lax.broadcasted_iota(jnp.int32, (blk, blk), 1)
    last_row = lax.broadcasted_iota(jnp.int32, (blk, 1), 0) == blk - 1
    cb = _dot(cm.astype(BF16), bm.astype(BF16), _NT)
    y = jnp.zeros((blk, width), F32)
    dt_l = jnp.zeros((blk, width), F32)
    ac_l = jnp.zeros((blk, width), F32)
    d_l = jnp.zeros((1, width), F32)
    for e in range(e_heads):
        head = group * e_heads + e
        pick = lane_h == head
        col = jnp.sum(jnp.where(pick, acum, 0.0), axis=1, keepdims=True)
        dt_e = jnp.sum(jnp.where(pick, dt, 0.0), axis=1, keepdims=True)
        d_e = jnp.sum(jnp.where(pick, dsk, 0.0), axis=1, keepdims=True)
        row = jnp.sum(jnp.where(sub_h == head, acum_t, 0.0), axis=0, keepdims=True)
        mine = lane_e == e
        decay = jnp.exp(jnp.where(causal, col - row, -jnp.inf))
        y = y + _dot((cb * decay).astype(BF16), jnp.where(mine, x * dt_e, 0.0).astype(BF16))
        dt_l = dt_l + jnp.where(mine, dt_e, 0.0)
        ac_l = ac_l + jnp.where(mine, col, 0.0)
        d_l = d_l + jnp.where(mine, d_e, 0.0)
    ac_last = jnp.sum(jnp.where(last_row, ac_l, 0.0), axis=0, keepdims=True)
    y = y + jnp.exp(ac_l) * _dot(cm.astype(BF16), state_t.astype(BF16)) + x * d_l
    inject = (x * dt_l * jnp.exp(ac_last - ac_l)).astype(BF16)
    new_state = state_t * jnp.exp(ac_last) + _dot(bm.astype(BF16), inject, _TN)
    return y, new_state


def ssd_scan_fwd(xc, proj, dt_col, dtb, alog, dsk, *, blk, name):
    s = xc.shape[0]
    nb = s // blk
    e_w = 4 * SSD_P
    b_off = SSD_G * e_w // SSD_N
    c_off = b_off + SSD_G
    row = pl.BlockSpec((1, LANE), lambda j, g: (0, 0))

    def body(x_ref, b_ref, c_ref, dt_ref, dtb_ref, al_ref, d_ref, y_ref, st_ref, state):
        j, g = pl.program_id(0), pl.program_id(1)

        @pl.when(j == 0)
        def _():
            state[g] = jnp.zeros((SSD_N, e_w), F32)

        st = state[g]
        st_ref[...] = st
        y, new_state = _ssd_block(x_ref[...], b_ref[...], c_ref[...], dt_ref[...], dtb_ref[...], al_ref[...],
                                  d_ref[...], st, g)
        y_ref[...] = y
        state[g] = new_state

    return pl.pallas_call(
        body, grid=(nb, SSD_G),
        in_specs=[pl.BlockSpec((blk, e_w), lambda j, g: (j, g)),
                  pl.BlockSpec((blk, SSD_N), lambda j, g: (j, b_off + g)),
                  pl.BlockSpec((blk, SSD_N), lambda j, g: (j, c_off + g)),
                  pl.BlockSpec((blk, LANE), lambda j, g: (j, dt_col)), row, row, row],
        out_specs=[pl.BlockSpec((blk, e_w), lambda j, g: (j, g)),
                   pl.BlockSpec((None, None, SSD_N, e_w), lambda j, g: (j, g, 0, 0))],
        out_shape=[jax.ShapeDtypeStruct((s, SSD_G * e_w), F32),
                   jax.ShapeDtypeStruct((nb, SSD_G, SSD_N, e_w), F32)],
        scratch_shapes=[pltpu.VMEM((SSD_G, SSD_N, e_w), F32)],
        name=name, compiler_params=_cparams(("arbitrary", "arbitrary")),
    )(xc, xc, xc, proj, dtb, alog, dsk)


def ssd_scan_bwd(xc, proj, dt_col, dtb, alog, dsk, states, dy, *, blk, name):
    s = xc.shape[0]
    nb = s // blk
    e_w = 4 * SSD_P
    b_off = SSD_G * e_w // SSD_N
    c_off = b_off + SSD_G
    row = pl.BlockSpec((1, LANE), lambda j, g: (0, 0))
    jm = lambda j: nb - 1 - j

    def body(x_ref, b_ref, c_ref, dt_ref, dtb_ref, al_ref, d_ref, st_ref, dy_ref,
             dx_ref, db_ref, dc_ref, ddt_ref, ddtb_ref, dal_ref, dd_ref, dstate):
        j, g = pl.program_id(0), pl.program_id(1)

        @pl.when(j == 0)
        def _():
            dstate[g] = jnp.zeros((SSD_N, e_w), F32)

        @pl.when(jnp.logical_and(j == 0, g == 0))
        def _():
            ddtb_ref[...] = jnp.zeros_like(ddtb_ref)
            dal_ref[...] = jnp.zeros_like(dal_ref)
            dd_ref[...] = jnp.zeros_like(dd_ref)

        @pl.when(g == 0)
        def _():
            ddt_ref[...] = jnp.zeros_like(ddt_ref)

        fn = functools.partial(_ssd_block, group=g)
        _, f_vjp = jax.vjp(fn, x_ref[...], b_ref[...], c_ref[...], dt_ref[...], dtb_ref[...], al_ref[...],
                           d_ref[...], st_ref[...])
        dx, db, dc, ddt, ddtb, dal, dd, dst = f_vjp((dy_ref[...], dstate[g]))
        dx_ref[...] = dx
        db_ref[...] = db
        dc_ref[...] = dc
        ddt_ref[...] += ddt
        ddtb_ref[...] += ddtb
        dal_ref[...] += dal
        dd_ref[...] += dd
        dstate[g] = dst

    return pl.pallas_call(
        body, grid=(nb, SSD_G),
        in_specs=[pl.BlockSpec((blk, e_w), lambda j, g: (jm(j), g)),
                  pl.BlockSpec((blk, SSD_N), lambda j, g: (jm(j), b_off + g)),
                  pl.BlockSpec((blk, SSD_N), lambda j, g: (jm(j), c_off + g)),
                  pl.BlockSpec((blk, LANE), lambda j, g: (jm(j), dt_col)), row, row, row,
                  pl.BlockSpec((None, None, SSD_N, e_w), lambda j, g: (jm(j), g, 0, 0)),
                  pl.BlockSpec((blk, e_w), lambda j, g: (jm(j), g))],
        out_specs=[pl.BlockSpec((blk, e_w), lambda j, g: (jm(j), g)),
                   pl.BlockSpec((blk, SSD_N), lambda j, g: (jm(j), g)),
                   pl.BlockSpec((blk, SSD_N), lambda j, g: (jm(j), g)),
                   pl.BlockSpec((blk, LANE), lambda j, g: (jm(j), 0)), row, row, row],
        out_shape=[jax.ShapeDtypeStruct((s, SSD_G * e_w), F32),
                   jax.ShapeDtypeStruct((s, SSD_G * SSD_N), F32),
                   jax.ShapeDtypeStruct((s, SSD_G * SSD_N), F32),
                   jax.ShapeDtypeStruct((s, LANE), F32)] + [jax.ShapeDtypeStruct((1, LANE), F32)] * 3,
        scratch_shapes=[pltpu.VMEM((SSD_G, SSD_N, e_w), F32)],
        name=name, compiler_params=_cparams(("arbitrary", "arbitrary")),
    )(xc, xc, xc, proj, dtb, alog, dsk, states, dy)


SB_DEAD = -104.0

_NT = (((1,), (1,)), ((), ()))
_TN = (((0,), (0,)), ((), ()))


def _dot(a, b, dims=(((1,), (0,)), ((), ()))):
    return lax.dot_general(a, b, dims, preferred_element_type=F32)


def _split_dot(x, tri, passes):
    out = None
    rem = x
    for _ in range(passes):
        piece = rem.astype(BF16)
        rem = rem - piece.astype(F32)
        d = _dot(piece, tri)
        out = d if out is None else out + d
    return out


def _sb_scores(q, k_ref, j, blk, row, scale):
    kb = k_ref[pl.ds(pl.multiple_of(j * blk, blk), blk), :].astype(BF16)
    z = _dot(q, kb, _NT) * scale
    col = j * blk + lax.broadcasted_iota(jnp.int32, (blk, blk), 1)
    mask = col < row
    sp = jnp.maximum(z, 0.0) + jnp.log(1.0 + jnp.exp(-jnp.abs(z)))
    lk = jnp.where(mask, -sp, 0.0)
    return kb, mask, lk, z - sp


def sb_attention_fwd(proj, q_col, k_col, v_col, n_heads, *, blk, name):
    s = proj.shape[0]
    nq = s // blk
    scale = HEAD ** -0.5

    def body(q_ref, k_ref, v_ref, o_ref, r_ref, n_ref):
        i = pl.program_id(1)
        q = q_ref[...].astype(BF16)
        row = i * blk + lax.broadcasted_iota(jnp.int32, (blk, blk), 0)
        rr = lax.broadcasted_iota(jnp.int32, (blk, blk), 0)
        cc = lax.broadcasted_iota(jnp.int32, (blk, blk), 1)
        tri_after = (rr > cc).astype(BF16)

        def alive(carry):
            jj, _, run = carry
            return jnp.logical_and(jj <= i, jnp.max(run) > SB_DEAD)

        def step(carry):
            jj, acc, run = carry
            j = i - jj
            _, mask, lk, ls = _sb_scores(q, k_ref, j, blk, row, scale)
            later = _split_dot(lk, tri_after, 2) + run
            w = jnp.where(mask, jnp.exp(ls + later), 0.0)
            vb = v_ref[pl.ds(pl.multiple_of(j * blk, blk), blk), :].astype(BF16)
            return jj + 1, acc + _dot(w.astype(BF16), vb), run + jnp.sum(lk, axis=1, keepdims=True)

        n, acc, run = lax.while_loop(
            alive, step, (jnp.int32(0), jnp.zeros((blk, HEAD), F32), jnp.zeros((blk, 1), F32)))
        o_ref[...] = acc.astype(o_ref.dtype)
        r_ref[...] = jnp.broadcast_to(run, (blk, HEAD))
        n_ref[pl.program_id(0), i] = n

    blk_spec = lambda c0: pl.BlockSpec((blk, HEAD), lambda h, i: (i, c0 + h))
    full_spec = lambda c0: pl.BlockSpec((s, HEAD), lambda h, i: (0, c0 + h))
    out_spec = pl.BlockSpec((blk, HEAD), lambda h, i: (i, h))
    return pl.pallas_call(
        body, grid=(n_heads, nq),
        in_specs=[blk_spec(q_col), full_spec(k_col), full_spec(v_col)],
        out_specs=[out_spec, out_spec, pl.BlockSpec(memory_space=pltpu.SMEM)],
        out_shape=[jax.ShapeDtypeStruct((s, n_heads * HEAD), BF16),
                   jax.ShapeDtypeStruct((s, n_heads * HEAD), F32),
                   jax.ShapeDtypeStruct((n_heads, nq), jnp.int32)],
        name=name, compiler_params=_cparams(("arbitrary", "arbitrary")),
    )(proj, proj, proj)


def sb_attention_bwd(proj, d_out, run_tot, visited, q_col, k_col, v_col, n_heads, *, blk, name):
    s = proj.shape[0]
    nq = s // blk
    scale = HEAD ** -0.5

    def body(n_ref, q_ref, k_ref, v_ref, do_ref, r_ref, dq_ref, dk_ref, dv_ref, dk_acc, dv_acc):
        i = pl.program_id(1)
        first = i + 1 - jnp.clip(n_ref[pl.program_id(0), i], 1, i + 1)

        @pl.when(i == 0)
        def _():
            dk_acc[...] = jnp.zeros_like(dk_acc)
            dv_acc[...] = jnp.zeros_like(dv_acc)

        q = q_ref[...].astype(BF16)
        do = do_ref[...].astype(BF16)
        rtot = r_ref[:, :1]
        row = i * blk + lax.broadcasted_iota(jnp.int32, (blk, blk), 0)
        rr = lax.broadcasted_iota(jnp.int32, (blk, blk), 0)
        cc = lax.broadcasted_iota(jnp.int32, (blk, blk), 1)
        tri_upto = (rr <= cc).astype(BF16)
        tri_before = (rr < cc).astype(BF16)

        def step(j, carry):
            dq, pre, gpre = carry
            kb, mask, lk, ls = _sb_scores(q, k_ref, j, blk, row, scale)
            rows = pl.ds(pl.multiple_of(j * blk, blk), blk)
            vb = v_ref[rows, :].astype(BF16)
            later = rtot - (pre + _split_dot(lk, tri_upto, 2))
            w = jnp.where(mask, jnp.exp(ls + later), 0.0)
            g = w * _dot(do, vb, _NT)
            g_before = _split_dot(g, tri_before, 1) + gpre
            sig = jnp.exp(ls)
            dz = (jnp.where(mask, g * (1.0 - sig) - sig * g_before, 0.0) * scale).astype(BF16)
            dk_acc[rows, :] += _dot(dz, q, _TN)
            dv_acc[rows, :] += _dot(w.astype(BF16), do, _TN)
            return (dq + _dot(dz, kb), pre + jnp.sum(lk, axis=1, keepdims=True),
                    gpre + jnp.sum(g, axis=1, keepdims=True))

        zero = jnp.zeros((blk, 1), F32)
        dq, _, _ = lax.fori_loop(first, i + 1, step, (jnp.zeros((blk, HEAD), F32), zero, zero))
        dq_ref[...] = dq.astype(dq_ref.dtype)

        @pl.when(i == nq - 1)
        def _():
            dk_ref[...] = dk_acc[...].astype(dk_ref.dtype)
            dv_ref[...] = dv_acc[...].astype(dv_ref.dtype)

    blk_spec = lambda c0: pl.BlockSpec((blk, HEAD), lambda h, i: (i, c0 + h))
    full_spec = lambda c0: pl.BlockSpec((s, HEAD), lambda h, i: (0, c0 + h))
    o_shape = jax.ShapeDtypeStruct((s, n_heads * HEAD), BF16)
    do_spec = blk_spec(0)
    if isinstance(d_out, tuple):
        d_out, lead = d_out
        do_spec = pl.BlockSpec((None, blk, HEAD), lambda h, i: (lead, i, h))
    return pl.pallas_call(
        body, grid=(n_heads, nq),
        in_specs=[pl.BlockSpec(memory_space=pltpu.SMEM), blk_spec(q_col), full_spec(k_col), full_spec(v_col),
                  do_spec, blk_spec(0)],
        out_specs=[blk_spec(0), full_spec(0), full_spec(0)],
        out_shape=[o_shape, o_shape, o_shape],
        scratch_shapes=[pltpu.VMEM((s, HEAD), F32), pltpu.VMEM((s, HEAD), F32)],
        name=name, compiler_params=_cparams(("parallel", "arbitrary")),
    )(visited, proj, proj, proj, d_out, run_tot)


def _hosting(hosted):
    arrs = [arr for _, arr in hosted]
    shapes = [_comm_out_shape(kind, arr) for kind, arr in hosted]

    def ops(src_refs, dst_refs, sems):
        return [_COMM[kind](src_refs[c], dst_refs[c], *sems[3 * c:3 * c + 3]) for c, (kind, _) in enumerate(hosted)]

    return arrs, [_HBM] * len(hosted), shapes, _COMM_SEMS * len(hosted), ops


def ffn_up_swiglu(h, w_gu, *, tm, hosted=(), name):
    s, d = h.shape
    tm = _pick(s, tm)
    nb, _, hb, _ = w_gu.shape
    n_host = len(hosted)
    h_arrs, h_specs, h_shapes, h_sems, h_ops = _hosting(hosted)
    grid = (s // tm, nb)

    def body(h_ref, w_ref, *rest):
        src_refs, (gu_ref, act_ref) = rest[:n_host], rest[n_host:n_host + 2]
        dst_refs, sems = rest[n_host + 2:2 * n_host + 2], rest[2 * n_host + 2:]
        comms = h_ops(src_refs, dst_refs, sems)
        i, g = pl.program_id(0), pl.program_id(1)

        if hosted:
            @pl.when(jnp.logical_and(i == 0, g == 0))
            def _():
                for start, _ in comms:
                    start()

        a = h_ref[...]
        gate = _dot(a, w_ref[0], _NT)
        up = _dot(a, w_ref[1], _NT)
        gu_ref[0] = gate
        gu_ref[1] = up
        act_ref[...] = (gate * jax.nn.sigmoid(gate) * up).astype(act_ref.dtype)

        if hosted:
            @pl.when(jnp.logical_and(i == grid[0] - 1, g == grid[1] - 1))
            def _():
                for _, finish in comms:
                    finish()

    res = pl.pallas_call(
        body, grid=grid,
        in_specs=[pl.BlockSpec((tm, d), lambda i, g: (i, 0)),
                  pl.BlockSpec((None, 2, hb, d), lambda i, g: (g, 0, 0, 0))] + h_specs,
        out_specs=[pl.BlockSpec((None, 2, tm, hb), lambda i, g: (g, 0, i, 0)),
                   pl.BlockSpec((None, tm, hb), lambda i, g: (g, i, 0))] + h_specs,
        out_shape=[jax.ShapeDtypeStruct((nb, 2, s, hb), F32), jax.ShapeDtypeStruct((nb, s, hb), BF16)] + h_shapes,
        scratch_shapes=h_sems, name=name,
        compiler_params=_cparams(("arbitrary", "arbitrary") if hosted else ("parallel", "parallel")),
    )(h, w_gu, *h_arrs)
    return res[0], res[1], res[2:]


def branch_merge(y3, w3, proj, gate_col, biases, *, tm, tn, name):
    nbr, s, k = y3.shape
    n = w3.shape[2]
    tm, tn = _pick(s, tm), _pick(n, tn)
    per = n // tn

    def body(y_ref, w_ref, g0, g1, g2, b0, b1, b2, m_ref, u_ref):
        merged = None
        for i, (g_ref, b_ref) in enumerate(((g0, b0), (g1, b1), (g2, b2))):
            u = _dot(y_ref[i], w_ref[i])
            u_ref[i] = u
            part = jax.nn.sigmoid(g_ref[...] + b_ref[...]) * u
            merged = part if merged is None else merged + part
        m_ref[...] = merged.astype(m_ref.dtype)

    gate_spec = lambda i: pl.BlockSpec((tm, tn), lambda r, j: (r, (gate_col + i) * per + j))
    bias_spec = pl.BlockSpec((1, tn), lambda r, j: (0, j))
    return pl.pallas_call(
        body, grid=(s // tm, per),
        in_specs=[pl.BlockSpec((nbr, tm, k), lambda r, j: (0, r, 0)), pl.BlockSpec((nbr, k, tn), lambda r, j: (0, 0, j)),
                  gate_spec(0), gate_spec(1), gate_spec(2), bias_spec, bias_spec, bias_spec],
        out_specs=[pl.BlockSpec((tm, tn), lambda r, j: (r, j)), pl.BlockSpec((nbr, tm, tn), lambda r, j: (0, r, j))],
        out_shape=[jax.ShapeDtypeStruct((s, n), BF16), jax.ShapeDtypeStruct((nbr, s, n), F32)],
        name=name, compiler_params=_cparams(("parallel", "parallel")),
    )(y3, w3, proj, proj, proj, *biases)


def gate_merge_bwd(u3, proj, gate_col, biases, dmerged, *, tr, name):
    nbr, s, n = u3.shape

    def body(u_ref, g0, g1, g2, b0, b1, b2, dm_ref, du_ref, dg_ref, db0, db1, db2):
        first = pl.program_id(0) == 0
        dm = dm_ref[...]
        for i, (g_ref, b_ref, db_ref) in enumerate(((g0, b0, db0), (g1, b1, db1), (g2, b2, db2))):
            sig = jax.nn.sigmoid(g_ref[...] + b_ref[...])
            du_ref[i] = (dm * sig).astype(du_ref.dtype)
            dlogit = dm * u_ref[i] * (sig * (1.0 - sig))
            dg_ref[:, i * n:(i + 1) * n] = dlogit.astype(dg_ref.dtype)
            part = jnp.sum(dlogit, axis=0, keepdims=True)

            @pl.when(first)
            def _():
                db_ref[...] = part

            @pl.when(jnp.logical_not(first))
            def _():
                db_ref[...] += part

    gate_spec = lambda i: pl.BlockSpec((tr, n), lambda r: (r, gate_col + i))
    bias_spec = pl.BlockSpec((1, n), lambda r: (0, 0))
    u_spec = pl.BlockSpec((nbr, tr, n), lambda r: (0, r, 0))
    return pl.pallas_call(
        body, grid=(s // tr,),
        in_specs=[u_spec, gate_spec(0), gate_spec(1), gate_spec(2), bias_spec, bias_spec, bias_spec,
                  pl.BlockSpec((tr, n), lambda r: (r, 0))],
        out_specs=[u_spec, pl.BlockSpec((tr, nbr * n), lambda r: (r, 0)), bias_spec, bias_spec, bias_spec],
        out_shape=[jax.ShapeDtypeStruct(u3.shape, BF16), jax.ShapeDtypeStruct((s, nbr * n), BF16)]
        + [jax.ShapeDtypeStruct((1, n), F32)] * 3,
        name=name, compiler_params=_cparams(("arbitrary",)),
    )(u3, proj, proj, proj, *biases, dmerged)


def ffn_down_dx_swiglu(df, w_down, gu, *, tm, name):
    s, d = df.shape
    tm = _pick(s, tm)
    nb, hb, _ = w_down.shape

    def body(df_ref, w_ref, gu_ref, dgu_ref):
        dact = _dot(df_ref[...], w_ref[...], _NT)
        gate, up = gu_ref[0], gu_ref[1]
        sig = jax.nn.sigmoid(gate)
        dgu_ref[0] = (dact * up * (sig * (1.0 + gate * (1.0 - sig)))).astype(dgu_ref.dtype)
        dgu_ref[1] = (dact * gate * sig).astype(dgu_ref.dtype)

    blk = pl.BlockSpec((None, 2, tm, hb), lambda i, g: (g, 0, i, 0))
    return pl.pallas_call(
        body, grid=(s // tm, nb),
        in_specs=[pl.BlockSpec((tm, d), lambda i, g: (i, 0)), pl.BlockSpec((None, hb, d), lambda i, g: (g, 0, 0)), blk],
        out_specs=blk, out_shape=jax.ShapeDtypeStruct(gu.shape, BF16),
        name=name, compiler_params=_cparams(("parallel", "parallel")),
    )(df, w_down, gu)


def loss_head(y, target, *, tr, name):
    s, d = y.shape

    def body(y_ref, t_ref, l_ref, dy_ref):
        err = y_ref[...] - t_ref[...]
        dy_ref[...] = err * (1.0 / d)
        part = 0.5 * jnp.sum(jnp.mean(err * err, axis=-1, keepdims=True), axis=0, keepdims=True)

        @pl.when(pl.program_id(0) == 0)
        def _():
            l_ref[...] = jnp.zeros_like(l_ref)

        l_ref[...] += jnp.broadcast_to(part, l_ref.shape)

    row = pl.BlockSpec((tr, d), lambda i: (i, 0))
    return pl.pallas_call(
        body, grid=(s // tr,), in_specs=[row, row],
        out_specs=[pl.BlockSpec((8, LANE), lambda i: (0, 0)), row],
        out_shape=[jax.ShapeDtypeStruct((8, LANE), F32), jax.ShapeDtypeStruct((s, d), F32)],
        name=name, compiler_params=_cparams(("arbitrary",)),
    )(y, target)


def _adamw_math(w, g, m, v):
    m = ADAM_B1 * m + (1.0 - ADAM_B1) * g
    v = ADAM_B2 * v + (1.0 - ADAM_B2) * jnp.square(g)
    m_hat = m / (1.0 - ADAM_B1 ** ADAM_STEP)
    v_hat = v / (1.0 - ADAM_B2 ** ADAM_STEP)
    delta = -ADAM_LR * (m_hat / (jnp.sqrt(v_hat) + ADAM_EPS) + ADAM_WD * w)
    return delta, m, v


def adamw_sum(w, m, v, parts, *, tr, name):
    n, r, c = parts.shape
    tr = _pick_rows(r, tr)

    def body(w_ref, m_ref, v_ref, p_ref, g_ref, d_ref, nm_ref, nv_ref):
        g = p_ref[0].astype(F32)
        for i in range(1, n):
            g = g + p_ref[i].astype(F32)
        delta, nm, nv = _adamw_math(w_ref[...], g, m_ref[...], v_ref[...])
        g_ref[...] = g
        d_ref[...] = delta
        nm_ref[...] = nm
        nv_ref[...] = nv

    row = pl.BlockSpec((tr, c), lambda i: (i, 0))
    shape = jax.ShapeDtypeStruct((r, c), F32)
    return pl.pallas_call(
        body, grid=(r // tr,),
        in_specs=[row, row, row, pl.BlockSpec((n, tr, c), lambda i: (0, i, 0))],
        out_specs=[row] * 4, out_shape=[shape] * 4,
        name=name, compiler_params=_cparams(("parallel",)),
    )(w, m, v, parts)


def adamw_into(w_all, m_all, v_all, parts, layer, row_off, prev, *, tr, name):
    depth, r, c = w_all.shape
    n = parts.shape[0]
    if r % 16 == 0:
        tr = _pick_rows(r, tr)
        assert row_off % tr == 0
        off = row_off // tr
        grid = (r // tr,)
        lay = pl.BlockSpec((None, tr, c), lambda i: (layer, i, 0))
        p_spec = pl.BlockSpec((n, tr, c), lambda i: (0, off + i, 0))
    else:
        assert row_off == 0 and parts.shape[1] == r
        grid = (c // LANE,)
        lay = pl.BlockSpec((None, r, LANE), lambda j: (layer, 0, j))
        p_spec = pl.BlockSpec((n, r, LANE), lambda j: (0, 0, j))
    if prev is None:
        prev = [lax.empty(w_all.shape, F32) for _ in range(4)]

    def body(w_ref, m_ref, v_ref, p_ref, _g, _d, _m, _v, g_ref, d_ref, nm_ref, nv_ref):
        g = p_ref[0].astype(F32)
        for i in range(1, n):
            g = g + p_ref[i].astype(F32)
        delta, nm, nv = _adamw_math(w_ref[...], g, m_ref[...], v_ref[...])
        g_ref[...] = g
        d_ref[...] = delta
        nm_ref[...] = nm
        nv_ref[...] = nv

    untouched = pl.BlockSpec(memory_space=pl.ANY)
    return pl.pallas_call(
        body, grid=grid,
        in_specs=[lay, lay, lay, p_spec] + [untouched] * 4,
        out_specs=[lay] * 4, out_shape=[jax.ShapeDtypeStruct(w_all.shape, F32)] * 4,
        input_output_aliases={4: 0, 5: 1, 6: 2, 7: 3},
        name=name, compiler_params=_cparams(("parallel",)),
    )(w_all, m_all, v_all, parts, *prev)


def _pick_rows(r, pref):
    t = min(pref, r)
    while r % t or (t % 16 and t != r):
        t -= 1
    return t


_HBM = pl.BlockSpec(memory_space=pltpu.HBM)
_MESH = pl.DeviceIdType.MESH


def _flat_index(px, py, pc):
    return 4 * px + 2 * py + pc


def _gather_ops(x_ref, out_ref, send_sems, recv_sems, local_sem):
    x, y, c = lax.axis_index("x"), lax.axis_index("y"), lax.axis_index("c")
    me, sibling = (x, y, c), (x, y, 1 - c)
    chips = [(1 - x, y), (x, 1 - y), (1 - x, 1 - y)]

    def slot(p):
        return out_ref.at[_flat_index(*p)]

    def copy(k, block, to, src=None):
        return pltpu.make_async_remote_copy(
            src_ref=slot(block) if src is None else src, dst_ref=slot(block),
            send_sem=send_sems.at[k], recv_sem=recv_sems.at[k], device_id=to, device_id_type=_MESH)

    mine = pltpu.make_async_copy(x_ref, slot(me), local_sem)
    first = [copy(0, me, sibling, src=x_ref)]
    first += [copy(1 + j, me, (*chip, c), src=x_ref) for j, chip in enumerate(chips)]
    passed = [copy(4 + j, (*chip, c), sibling) for j, chip in enumerate(chips)]

    def start():
        mine.start()
        for cp in first:
            cp.start()

    def finish():
        for j, chip in enumerate(chips):
            copy(1 + j, (*chip, c), me).wait_recv()
            passed[j].start()
        copy(0, sibling, me).wait_recv()
        for j, chip in enumerate(chips):
            copy(4 + j, (*chip, 1 - c), me).wait_recv()
        for cp in first + passed:
            cp.wait_send()
        mine.wait()

    return start, finish


def _exchange_ops(p_ref, out_ref, send_sems, recv_sems, local_sem):
    x, y, c = lax.axis_index("x"), lax.axis_index("y"), lax.axis_index("c")
    me = _flat_index(x, y, c)
    peers = [(1 - x if k & 4 else x, 1 - y if k & 2 else y, 1 - c if k & 1 else c) for k in range(1, N_DEV)]

    def copy(k, peer, dst_slot):
        return pltpu.make_async_remote_copy(
            src_ref=p_ref.at[_flat_index(*peer)], dst_ref=out_ref.at[dst_slot],
            send_sem=send_sems.at[k], recv_sem=recv_sems.at[k], device_id=peer, device_id_type=_MESH)

    mine = pltpu.make_async_copy(p_ref.at[me], out_ref.at[me], local_sem)
    sends = [copy(k, peer, me) for k, peer in enumerate(peers)]

    def start():
        mine.start()
        for cp in sends:
            cp.start()

    def finish():
        for k, peer in enumerate(peers):
            copy(k, peer, _flat_index(*peer)).wait_recv()
        for cp in sends:
            cp.wait_send()
        mine.wait()

    return start, finish


_COMM = {"gather": _gather_ops, "exchange": _exchange_ops}
_COMM_SEMS = [pltpu.SemaphoreType.DMA((N_DEV - 1,)), pltpu.SemaphoreType.DMA((N_DEV - 1,)), pltpu.SemaphoreType.DMA]


def _comm_out_shape(kind, arr):
    return jax.ShapeDtypeStruct(((N_DEV,) + arr.shape) if kind == "gather" else arr.shape, arr.dtype)


def _comm_call(kind, arr, name):
    def body(src_ref, dst_ref, send_sems, recv_sems, local_sem):
        start, finish = _COMM[kind](src_ref, dst_ref, send_sems, recv_sems, local_sem)
        start()
        finish()

    return pl.pallas_call(body, out_shape=_comm_out_shape(kind, arr), in_specs=[_HBM], out_specs=_HBM,
                          scratch_shapes=_COMM_SEMS, name=name)(arr)


def all_gather(x, *, name):
    return _comm_call("gather", x, name)


def exchange(parts, *, name):
    return _comm_call("exchange", parts, name)


TR = 256
TR_WIDE = 128
BLK = 256
SB_BLK = 256
FFN_TM = 1024
CONV_CHUNK = 512
DT_PAD = LANE

BIG = ("w_in", "w_branch", "w_out", "ffn_w_gu", "ffn_w_down")
SMALL = ("norm_mix_pre", "norm_mix_post", "norm_ffn_pre", "norm_ffn_post", "b_gate", "ret_gn_w", "ssd_conv_w",
         "ssd_conv_b", "ssd_dt_bias", "ssd_a_log", "ssd_d", "ssd_norm_w")


def _row(v):
    return v.reshape(1, -1)


def _pad_lanes(v):
    return jnp.pad(v.reshape(1, -1), ((0, 0), (0, LANE - v.shape[-1])))


def _rows_of(segments, lo, hi):
    out, start = [], 0
    for seg in segments:
        a, b = max(lo, start), min(hi, start + seg.shape[0])
        if a < b:
            out.append(seg[a - start:b - start])
        start += seg.shape[0]
    return out


def _assemble_w_in(g, d):
    full = g.reshape(-1, d)
    n_main, n_dt = 5 * d, full.shape[0] - 8 * d
    main, dt, gates = full[:n_main], full[n_main:n_main + n_dt], full[n_main + n_dt:]
    return jnp.concatenate([main, gates], axis=0), jnp.pad(dt, ((0, DT_PAD - n_dt), (0, 0)))


def _split_dw_in(dw, dw_dt, d, n_dt):
    n_main = 5 * d
    segments = [dw[:n_main], dw_dt[:n_dt], dw[n_main:]]
    rows = (8 * d + n_dt) // N_DEV
    return jnp.stack([jnp.concatenate(_rows_of(segments, j * rows, (j + 1) * rows), axis=0) for j in range(N_DEV)])


def _layer_fwd(x, lw, cosf, sinf, next_shards):
    s, d = x.shape
    heads = d // 2 // HEAD
    nxt = {}
    (h,) = rowwise(f_rms_pre, [x], [lw["norm_mix_pre"]], [(d, BF16)], tr=TR, name="mix_pre_norm")
    if next_shards is None:
        proj = matmul(h, lw["w_cat"], tb=True, tn=2048, name="in_proj")
    else:
        proj, (nxt["w_in"],) = matmul(h, lw["w_cat"], tb=True, tn=2048, hosted=[("gather", next_shards["w_in"])],
                                      name="in_proj_gather")
    dt_raw = matmul(h, lw["w_dt"], tb=True, name="in_proj_dt")
    ret_cols = tuple(i * heads for i in range(4))
    y_ret, ret_states = retention_fwd(proj, ret_cols, lw["ret_gn_w"], cosf, sinf, heads, blk=BLK, name="retention_fwd")
    y_sb, sb_run, sb_visited = sb_attention_fwd(proj, 4 * heads, 5 * heads, 6 * heads, heads, blk=SB_BLK,
                                                name="stickbreak_fwd")
    u_pad = jnp.pad(proj[:, 4 * d:5 * d], ((CONV_PAD, CONV_PAD), (0, 0)))
    xc = ssd_conv_fwd(u_pad, lw["conv_taps"], lw["ssd_conv_b"], chunk=CONV_CHUNK, name="ssd_conv_fwd")
    y_scan, ssd_states = ssd_scan_fwd(xc, dt_raw, 0, lw["ssd_dt_bias"], lw["ssd_a_log"], lw["ssd_d"], blk=BLK,
                                      name="ssd_scan_fwd")
    z_spec = (proj, d // 2, 7)
    (y_ssd,) = rowwise(f_ssd_gate, [y_scan, z_spec], [lw["ssd_norm_w"]], [(d // 2, BF16)], tr=TR, name="ssd_gate_norm")
    y3 = jnp.stack([y_ret, y_sb, y_ssd])
    merged, u3 = branch_merge(y3, lw["w_branch"], proj, 5, lw["b_gate"], tm=512, tn=1024, name="branch_merge")
    o = matmul(merged, lw["w_out"], name="out_proj")
    (x1,) = rowwise(f_rms_post, [x, o], [lw["norm_mix_post"]], [(d, F32)], tr=TR, name="mix_post_norm")
    (h2,) = rowwise(f_rms_pre, [x1], [lw["norm_ffn_pre"]], [(d, BF16)], tr=TR, name="ffn_pre_norm")
    w_gu = lw["ffn_w_gu"].reshape(N_DEV, 2, -1, d)
    if next_shards is None:
        gu, act, _ = ffn_up_swiglu(h2, w_gu, tm=FFN_TM, name="ffn_up_swiglu")
    else:
        rest = [n for n in BIG if n != "w_in"]
        gu, act, got = ffn_up_swiglu(h2, w_gu, tm=FFN_TM, hosted=[("gather", next_shards[n]) for n in rest],
                                     name="ffn_up_swiglu_gather")
        nxt.update(zip(rest, got))
    f = matmul(act, lw["ffn_w_down"], lead_a="k", lead_b="k", name="ffn_down")
    (x2,) = rowwise(f_rms_post, [x1, f], [lw["norm_ffn_post"]], [(d, F32)], tr=TR, name="ffn_post_norm")
    res = dict(x=x, h=h, proj=proj, dt_raw=dt_raw, ret_states=ret_states, sb_run=sb_run, sb_visited=sb_visited, xc=xc,
               u_pad=u_pad,
               y_scan=y_scan, ssd_states=ssd_states, y3=y3, u3=u3, merged=merged, o=o, x1=x1, h2=h2, gu=gu, act=act, f=f)
    return x2, res, (nxt if next_shards is not None else None)


def _layer_bwd(dx2, res, lw, cosf, sinf, pending):
    x, proj = res["x"], res["proj"]
    s, d = x.shape
    heads = d // 2 // HEAD
    n_dt = d // 2 // SSD_P
    got = {}
    df, dn_ffn_post = rowwise_vjp(f_rms_post, [res["x1"], res["f"]], [lw["norm_ffn_post"]], [dx2],
                                  [(1, BF16, None)], [0], tr=TR, name="ffn_post_norm_bwd")
    dw_down = matmul(res["act"], df, ta=True, lead_a="batch", out_dtype=BF16, name="ffn_down_dw")
    dgu = ffn_down_dx_swiglu(df, lw["ffn_w_down"], res["gu"], tm=FFN_TM, name="ffn_down_dx_swiglu")
    dgu = dgu.reshape(2 * N_DEV, s, -1)
    if pending is None:
        dh2 = matmul(dgu, lw["ffn_w_gu"], lead_a="k", lead_b="k", name="ffn_up_dx")
        dw_gu = matmul(dgu, res["h2"], ta=True, lead_a="batch", out_dtype=BF16, name="ffn_up_dw")
    else:
        dh2, (got["prev_w_branch"], got["prev_w_out"]) = matmul(
            dgu, lw["ffn_w_gu"], lead_a="k", lead_b="k",
            hosted=[("exchange", pending["w_branch"]), ("exchange", pending["w_out"])], name="ffn_up_dx_exchange")
        dw_gu, (got["prev_w_in"],) = matmul(dgu, res["h2"], ta=True, lead_a="batch", out_dtype=BF16,
                                            hosted=[("exchange", pending["w_in"])], name="ffn_up_dw_exchange")
    dw_gu = dw_gu.reshape(N_DEV, -1, d)
    dx1, dn_ffn_pre = rowwise_vjp(f_rms_pre, [res["x1"]], [lw["norm_ffn_pre"]], [dh2], [(0, F32, dx2)], [0],
                                  tr=TR, name="ffn_pre_norm_bwd")
    do, dn_mix_post = rowwise_vjp(f_rms_post, [x, res["o"]], [lw["norm_mix_post"]], [dx1], [(1, BF16, None)], [0],
                                  tr=TR, name="mix_post_norm_bwd")
    dmerged = matmul(do, lw["w_out"], tb=True, name="out_proj_dx")
    dw_out = matmul(res["merged"], do, ta=True, out_dtype=BF16, name="out_proj_dw")
    du3, d_gate_logits, *db_gate = gate_merge_bwd(res["u3"], proj, 5, lw["b_gate"], dmerged, tr=TR_WIDE,
                                                  name="gate_merge_bwd")
    dy3 = matmul(du3, lw["w_branch"], tb=True, lead_a="batch", lead_b="batch", name="branch_proj_dx")
    dw_branch = matmul(res["y3"], du3, ta=True, lead_a="batch", lead_b="batch", out_dtype=BF16, name="branch_proj_dw")
    ret_cols = tuple(i * heads for i in range(4))
    dq, dk, dv, dg, d_gn = retention_bwd(proj, ret_cols, lw["ret_gn_w"], cosf, sinf, res["ret_states"], (dy3, 0), heads,
                                         blk=BLK, name="retention_bwd")
    dsq, dsk, dsv = sb_attention_bwd(proj, (dy3, 1), res["sb_run"], res["sb_visited"], 4 * heads, 5 * heads, 6 * heads,
                                     heads, blk=SB_BLK, name="stickbreak_bwd")
    z_spec = (proj, d // 2, 7)
    dy_scan, dz, d_ssd_norm = rowwise_vjp(f_ssd_gate, [res["y_scan"], z_spec], [lw["ssd_norm_w"]], [(dy3, d // 2, 0, 2)],
                                          [(0, F32, None), (1, BF16, None)], [0], tr=TR, name="ssd_gate_norm_bwd")
    dxs, dbm, dcm, ddt, d_dtb, d_alog, d_dskip = ssd_scan_bwd(
        res["xc"], res["dt_raw"], 0, lw["ssd_dt_bias"], lw["ssd_a_log"], lw["ssd_d"], res["ssd_states"], dy_scan,
        blk=BLK, name="ssd_scan_bwd")
    dxc_pad = jnp.pad(jnp.concatenate([dxs, dbm, dcm], axis=1), ((0, CONV_PAD), (0, 0)))
    du, d_taps, d_conv_b = ssd_conv_bwd(res["u_pad"], lw["conv_taps"], lw["ssd_conv_b"], dxc_pad, chunk=CONV_CHUNK,
                                        name="ssd_conv_bwd")
    dproj = jnp.concatenate([dq, dk, dv, dg, dsq, dsk, dsv, dz, du, d_gate_logits], axis=1)
    dh_dt = matmul(ddt, lw["w_dt"], name="in_proj_dt_dx")
    dh, (got["ffn_w_down"], got["ffn_w_gu"]) = matmul(
        dproj, lw["w_cat"], add=dh_dt, hosted=[("exchange", dw_down), ("exchange", dw_gu)],
        name="in_proj_dx_exchange")
    dw_cat = matmul(dproj, res["h"], ta=True, tn=2048, out_dtype=BF16, name="in_proj_dw")
    dw_dt = matmul(ddt, res["h"], ta=True, out_dtype=BF16, name="in_proj_dt_dw")
    dx, dn_mix_pre = rowwise_vjp(f_rms_pre, [x], [lw["norm_mix_pre"]], [dh], [(0, F32, dx1)], [0], tr=TR,
                                 name="mix_pre_norm_bwd")
    mine = dict(
        w_in=_split_dw_in(dw_cat, dw_dt, d, n_dt),
        w_branch=jnp.transpose(dw_branch.reshape(3, d // 2, N_DEV, -1), (2, 0, 1, 3)).reshape(N_DEV, 3 * d // 2, -1),
        w_out=dw_out.reshape(N_DEV, d // N_DEV, d),
    )
    small = dict(
        norm_mix_pre=dn_mix_pre[0], norm_mix_post=dn_mix_post[0], norm_ffn_pre=dn_ffn_pre[0],
        norm_ffn_post=dn_ffn_post[0], b_gate=jnp.concatenate([b[0] for b in db_gate]), ret_gn_w=d_gn[0],
        ssd_conv_w=d_taps, ssd_conv_b=d_conv_b[0], ssd_dt_bias=d_dtb[0, :n_dt], ssd_a_log=d_alog[0, :n_dt],
        ssd_d=d_dskip[0, :n_dt], ssd_norm_w=d_ssd_norm[0],
    )
    return dx, got, mine, small


def _adam_rows(cols):
    return max(8, (1 << 17) // cols // 8 * 8)


def kernel(x, positions, norm_mix_pre, norm_mix_post, norm_ffn_pre, norm_ffn_post, w_in, b_gate, ret_gn_w, ssd_conv_w, ssd_conv_b, ssd_dt_bias, ssd_a_log, ssd_d, ssd_norm_w, w_branch_ret, w_branch_sb, w_branch_ssd, w_out, ffn_w_gate, ffn_w_up, ffn_w_down, loss_target, m_norm_mix_pre, m_norm_mix_post, m_norm_ffn_pre, m_norm_ffn_post, m_w_in, m_b_gate, m_ret_gn_w, m_ssd_conv_w, m_ssd_conv_b, m_ssd_dt_bias, m_ssd_a_log, m_ssd_d, m_ssd_norm_w, m_w_branch_ret, m_w_branch_sb, m_w_branch_ssd, m_w_out, m_ffn_w_gate, m_ffn_w_up, m_ffn_w_down, v_norm_mix_pre, v_norm_mix_post, v_norm_ffn_pre, v_norm_ffn_post, v_w_in, v_b_gate, v_ret_gn_w, v_ssd_conv_w, v_ssd_conv_b, v_ssd_dt_bias, v_ssd_a_log, v_ssd_d, v_ssd_norm_w, v_w_branch_ret, v_w_branch_sb, v_w_branch_ssd, v_w_out, v_ffn_w_gate, v_ffn_w_up, v_ffn_w_down):
    depth = w_in.shape[0]
    s, d = x.shape[1], x.shape[2]
    axes = ("x", "y", "c")
    me = _flat_index(lax.axis_index("x"), lax.axis_index("y"), lax.axis_index("c"))

    def tr_(a):
        return jnp.swapaxes(a, 1, 2)

    transposed = ("w_in", "ffn_w_gate", "ffn_w_up")
    wmv = dict(w_in=tuple(map(tr_, (w_in, m_w_in, v_w_in))), w_branch_ret=(w_branch_ret, m_w_branch_ret, v_w_branch_ret),
               w_branch_sb=(w_branch_sb, m_w_branch_sb, v_w_branch_sb),
               w_branch_ssd=(w_branch_ssd, m_w_branch_ssd, v_w_branch_ssd), w_out=(w_out, m_w_out, v_w_out),
               ffn_w_gate=tuple(map(tr_, (ffn_w_gate, m_ffn_w_gate, v_ffn_w_gate))),
               ffn_w_up=tuple(map(tr_, (ffn_w_up, m_ffn_w_up, v_ffn_w_up))),
               ffn_w_down=(ffn_w_down, m_ffn_w_down, v_ffn_w_down))
    members = dict(w_in=["w_in"], w_branch=["w_branch_ret", "w_branch_sb", "w_branch_ssd"], w_out=["w_out"],
                   ffn_w_gu=["ffn_w_gate", "ffn_w_up"], ffn_w_down=["ffn_w_down"])
    big_w = {g: (wmv[ns[0]][0] if len(ns) == 1 else jnp.concatenate([wmv[n][0] for n in ns], axis=1))
             for g, ns in members.items()}

    taps_all = all_gather(ssd_conv_w.reshape(-1, LANE), name="gather_conv_w")
    taps_all = jnp.transpose(taps_all.reshape(N_DEV, depth, SSD_K, -1), (1, 2, 0, 3)).reshape(depth, SSD_K, -1)

    cosf, sinf = rope_tables(positions.reshape(s), tr=TR)
    small_w = dict(norm_mix_pre=norm_mix_pre, norm_mix_post=norm_mix_post, norm_ffn_pre=norm_ffn_pre,
                   norm_ffn_post=norm_ffn_post, b_gate=b_gate, ret_gn_w=ret_gn_w, ssd_conv_b=ssd_conv_b,
                   ssd_dt_bias=ssd_dt_bias, ssd_a_log=ssd_a_log, ssd_d=ssd_d, ssd_norm_w=ssd_norm_w, taps=taps_all)

    def layer_weights(sw, gathered):
        lw = dict(gathered)
        for n in ("norm_mix_pre", "norm_mix_post", "norm_ffn_pre", "norm_ffn_post", "ret_gn_w", "ssd_conv_b", "ssd_norm_w"):
            lw[n] = _row(sw[n])
        for n in ("ssd_dt_bias", "ssd_a_log", "ssd_d"):
            lw[n] = _pad_lanes(sw[n])
        lw["b_gate"] = [_row(sw["b_gate"][i * d:(i + 1) * d]) for i in range(3)]
        lw["conv_taps"] = [sw["taps"][k:k + 1] for k in range(SSD_K)]
        return lw

    def layer_slice(t, l):
        return {n: a[l] for n, a in t.items()}

    def bf16_shards(l):
        return {n: big_w[n][l].astype(BF16) for n in BIG}

    def arrange(g):
        w_cat, w_dt = _assemble_w_in(g["w_in"], d)
        return dict(
            w_cat=w_cat, w_dt=w_dt,
            w_branch=jnp.transpose(g["w_branch"].reshape(N_DEV, 3, d // 2, -1), (1, 2, 0, 3)).reshape(3, d // 2, d),
            w_out=g["w_out"].reshape(d, d),
            ffn_w_gu=g["ffn_w_gu"].reshape(2 * N_DEV, -1, d),
            ffn_w_down=g["ffn_w_down"],
        )

    big_out = {}

    def adam(l, group, parts):
        rows, cols = big_w[group].shape[1:]
        parts = parts.reshape(N_DEV, rows, cols)
        off = 0
        for n in members[group]:
            big_out[n] = adamw_into(*wmv[n], parts, l, off, big_out.get(n), tr=_adam_rows(cols), name="adamw_" + n)
            off += wmv[n][0].shape[1]

    xs, saved = x.reshape(s, d), []
    gathered = {n: all_gather(a, name="gather_" + n) for n, a in bf16_shards(0).items()}
    for l in range(depth):
        lw = layer_weights(layer_slice(small_w, l), arrange(gathered))
        xs, res, gathered = _layer_fwd(xs, lw, cosf, sinf, bf16_shards(l + 1) if l + 1 < depth else None)
        saved.append((res, lw))
    loss_tile, dy = loss_head(xs, loss_target.reshape(s, d), tr=TR, name="loss_head")
    loss = lax.psum(loss_tile[0, 0], axes)

    dx, pending, small_layers = dy, None, [None] * depth
    for l in reversed(range(depth)):
        res, lw = saved[l]
        dx, got, pending, small_layers[l] = _layer_bwd(dx, res, lw, cosf, sinf, pending)
        for n in ("ffn_w_gu", "ffn_w_down"):
            adam(l, n, got[n])
        if l + 1 < depth:
            for n in ("w_in", "w_branch", "w_out"):
                adam(l + 1, n, got["prev_" + n])
    for n in ("w_in", "w_branch", "w_out"):
        adam(0, n, exchange(pending[n], name="exchange_" + n))
    small_g = {n: jnp.stack([small_layers[l][n] for l in range(depth)]) for n in SMALL}

    n_dt = ssd_dt_bias.shape[-1]
    small_in = dict(norm_mix_pre=(norm_mix_pre, m_norm_mix_pre, v_norm_mix_pre), norm_mix_post=(norm_mix_post, m_norm_mix_post, v_norm_mix_post),
                    norm_ffn_pre=(norm_ffn_pre, m_norm_ffn_pre, v_norm_ffn_pre), norm_ffn_post=(norm_ffn_post, m_norm_ffn_post, v_norm_ffn_post),
                    b_gate=(b_gate, m_b_gate, v_b_gate), ret_gn_w=(ret_gn_w, m_ret_gn_w, v_ret_gn_w),
                    ssd_conv_b=(ssd_conv_b, m_ssd_conv_b, v_ssd_conv_b), ssd_dt_bias=(ssd_dt_bias, m_ssd_dt_bias, v_ssd_dt_bias),
                    ssd_a_log=(ssd_a_log, m_ssd_a_log, v_ssd_a_log), ssd_d=(ssd_d, m_ssd_d, v_ssd_d),
                    ssd_norm_w=(ssd_norm_w, m_ssd_norm_w, v_ssd_norm_w))
    rep = [n for n in SMALL if n != "ssd_conv_w"]

    def pack(arrs):
        flat = jnp.concatenate([a.reshape(-1) for a in arrs])
        rows = -(-flat.shape[0] // (16 * LANE)) * 16
        return jnp.pad(flat, (0, rows * LANE - flat.shape[0])).reshape(rows, LANE)

    conv_g = small_g["ssd_conv_w"]
    g_pack = pack([small_g[n] for n in rep] + [conv_g])
    g_all = all_gather(g_pack, name="gather_small_grads")
    zeros_conv = jnp.zeros_like(conv_g)
    w_pack, m_pack, v_pack = (pack([small_in[n][i] for n in rep] + [zeros_conv]) for i in range(3))
    sm = adamw_sum(w_pack, m_pack, v_pack, g_all, tr=TR, name="adamw_small")

    def unpack(p):
        flat, out, off = p.reshape(-1), {}, 0
        for n in rep:
            shp = small_in[n][0].shape
            size = math.prod(shp)
            out[n] = flat[off:off + size].reshape(shp)
            off += size
        out["conv_sum"] = flat[off:off + conv_g.size].reshape(conv_g.shape)
        return out

    sm = [unpack(p) for p in sm]
    ch = ssd_conv_w.shape[-1]
    conv_mine = lax.dynamic_slice_in_dim(sm[0]["conv_sum"], me * ch, ch, axis=2)
    conv_out = adamw_sum(ssd_conv_w.reshape(-1, LANE), m_ssd_conv_w.reshape(-1, LANE), v_ssd_conv_w.reshape(-1, LANE),
                         conv_mine.reshape(1, -1, LANE), tr=TR, name="adamw_conv_w")
    for i in range(4):
        sm[i]["ssd_conv_w"] = conv_out[i].reshape(ssd_conv_w.shape)

    def big_named(i):
        return {n: (tr_(out[i]) if n in transposed else out[i]) for n, out in big_out.items()}

    order = ["norm_mix_pre", "norm_mix_post", "norm_ffn_pre", "norm_ffn_post", "w_in", "b_gate", "ret_gn_w", "ssd_conv_w",
             "ssd_conv_b", "ssd_dt_bias", "ssd_a_log", "ssd_d", "ssd_norm_w", "w_branch_ret", "w_branch_sb", "w_branch_ssd",
             "w_out", "ffn_w_gate", "ffn_w_up", "ffn_w_down"]
    outs = [loss, dx.reshape(x.shape)]
    for i in range(4):
        named = {**sm[i], **big_named(i)}
        outs += [named[n] for n in order]
    return tuple(outs)
```

```python
import functools
import math

import jax
import jax.numpy as jnp
import numpy as np
from jax import lax
from jax.experimental import pallas as pl
from jax.experimental.pallas import tpu as pltpu

F32 = jnp.float32
BF16 = jnp.bfloat16

N_DEV = 8
HEAD = 128
SSD_P = 64
SSD_G = 4
SSD_N = 128
SSD_K = 4
CHUNK = 64
NORM_EPS = 1e-6
ROPE_BASE = 10000.0
LANE = 128
VMEM_LIMIT = 56 * 1024 * 1024

ADAM_LR, ADAM_B1, ADAM_B2, ADAM_EPS, ADAM_WD, ADAM_STEP = 0.001, 0.9, 0.999, 1e-08, 0.01, 10


def _cparams(sem):
    return pltpu.CompilerParams(dimension_semantics=sem, vmem_limit_bytes=VMEM_LIMIT)


def _pick(n, pref):
    if n <= pref:
        return n
    t = pref
    while t >= LANE:
        if n % t == 0:
            return t
        t -= LANE
    return n


def matmul(a, b, *, ta=False, tb=False, lead_a=None, lead_b=None, out_dtype=F32, add=None, hosted=(),
           tm=1024, tn=1024, tk=2048, name="mm"):
    la, lb = lead_a is not None, lead_b is not None
    a2, b2 = a.shape[1:] if la else a.shape, b.shape[1:] if lb else b.shape
    (kd_a, m) = a2 if ta else a2[::-1]
    (kd_b, n) = b2[::-1] if tb else b2
    assert kd_a == kd_b, (a.shape, b.shape)
    nlead = a.shape[0] if la else (b.shape[0] if lb else 1)
    batch = "batch" in (lead_a, lead_b)
    kblocks = nlead if "k" in (lead_a, lead_b) else 1
    if la and lb:
        assert lead_a == lead_b and a.shape[0] == b.shape[0]
    tm, tn, tk = _pick(m, tm), _pick(n, tn), _pick(kd_a, tk)
    kt = kd_a // tk
    nk = kt * kblocks
    grid = (m // tm, (nlead if batch else 1), n // tn, nk)

    def lead_idx(g, k):
        return g if batch else k // kt

    def a_map(i, g, j, k):
        idx = (k % kt, i) if ta else (i, k % kt)
        return ((lead_idx(g, k),) + idx) if la else idx

    def b_map(i, g, j, k):
        idx = (j, k % kt) if tb else (k % kt, j)
        return ((lead_idx(g, k),) + idx) if lb else idx

    a_blk = (tk, tm) if ta else (tm, tk)
    b_blk = (tn, tk) if tb else (tk, tn)
    a_spec = pl.BlockSpec(((None,) + a_blk) if la else a_blk, a_map)
    b_spec = pl.BlockSpec(((None,) + b_blk) if lb else b_blk, b_map)
    if batch:
        o_spec = pl.BlockSpec((None, tm, tn), lambda i, g, j, k: (g, i, j))
        o_shape = jax.ShapeDtypeStruct((nlead, m, n), out_dtype)
    else:
        o_spec = pl.BlockSpec((tm, tn), lambda i, g, j, k: (i, j))
        o_shape = jax.ShapeDtypeStruct((m, n), out_dtype)
    dims = (((0 if ta else 1,), (1 if tb else 0,)), ((), ()))

    n_add, n_host = int(add is not None), len(hosted)
    n_acc = int(nk > 1)

    def body(a_ref, b_ref, *rest):
        c_ref = rest[0] if n_add else None
        src_refs = rest[n_add:n_add + n_host]
        o_ref = rest[n_add + n_host]
        dst_refs = rest[n_add + n_host + 1:n_add + 2 * n_host + 1]
        acc_ref = rest[n_add + 2 * n_host + 1] if n_acc else None
        sems = rest[n_add + 2 * n_host + 1 + n_acc:]
        k = pl.program_id(3)
        ids = [pl.program_id(ax) for ax in range(4)]
        comms = [_COMM[kind](src_refs[c], dst_refs[c], *sems[3 * c:3 * c + 3]) for c, (kind, _) in enumerate(hosted)]

        if hosted:
            @pl.when(functools.reduce(jnp.logical_and, [i == 0 for i in ids]))
            def _():
                for start, _ in comms:
                    start()

        def product():
            return lax.dot_general(a_ref[...].astype(BF16), b_ref[...].astype(BF16), dims, preferred_element_type=F32)

        if n_acc:
            @pl.when(k == 0)
            def _():
                acc_ref[...] = jnp.zeros_like(acc_ref) if c_ref is None else c_ref[...].astype(F32)

            acc_ref[...] += product()

            @pl.when(k == nk - 1)
            def _():
                o_ref[...] = acc_ref[...].astype(o_ref.dtype)
        else:
            o_ref[...] = (product() if c_ref is None else product() + c_ref[...].astype(F32)).astype(o_ref.dtype)

        if hosted:
            @pl.when(functools.reduce(jnp.logical_and, [i == g - 1 for i, g in zip(ids, grid)]))
            def _():
                for _, finish in comms:
                    finish()

    extra = ([] if add is None else [add]) + [arr for _, arr in hosted]
    extra_specs = ([] if add is None else [o_spec]) + [_HBM] * n_host
    out_shapes = [o_shape] + [_comm_out_shape(kind, arr) for kind, arr in hosted]
    scratch = [pltpu.VMEM((tm, tn), F32)] * n_acc + _COMM_SEMS * n_host
    sem = ("arbitrary",) * 4 if hosted else ("parallel", "parallel", "parallel", "arbitrary")
    res = pl.pallas_call(
        body, out_shape=out_shapes, grid=grid, in_specs=[a_spec, b_spec] + extra_specs,
        out_specs=[o_spec] + [_HBM] * n_host, scratch_shapes=scratch, name=name, compiler_params=_cparams(sem),
    )(a, b, *extra)
    return (res[0], res[1:]) if hosted else res[0]


def _row_spec(r, tr):
    if not isinstance(r, tuple):
        return r, pl.BlockSpec((tr, r.shape[-1]), lambda i: (i, 0))
    if len(r) == 3:
        arr, w, cb = r
        return arr, pl.BlockSpec((tr, w), lambda i: (i, cb))
    arr, w, cb, ld = r
    return arr, pl.BlockSpec((None, tr, w), lambda i: (ld, i, cb))


def _full_spec(c):
    nd = c.ndim
    return pl.BlockSpec(c.shape, lambda i: (0,) * nd)


def rowwise(fn, rows, consts, outs, *, tr, name):
    arrs, specs = zip(*[_row_spec(r, tr) for r in rows])
    n_rows = arrs[0].shape[-2]
    nr, nc = len(rows), len(consts)

    def body(*refs):
        vals = [r[...] for r in refs[:nr + nc]]
        res = fn(*vals)
        for o_ref, r in zip(refs[nr + nc:], res):
            o_ref[...] = r.astype(o_ref.dtype)

    return pl.pallas_call(
        body, grid=(n_rows // tr,),
        in_specs=list(specs) + [_full_spec(c) for c in consts],
        out_specs=[pl.BlockSpec((tr, w), lambda i: (i, 0)) for w, _ in outs],
        out_shape=[jax.ShapeDtypeStruct((n_rows, w), dt) for w, dt in outs],
        name=name, compiler_params=_cparams(("parallel",)),
    )(*arrs, *consts)


def rowwise_vjp(fn, rows, consts, cts, row_grads, const_grads, *, tr, name):
    arrs, specs = zip(*[_row_spec(r, tr) for r in rows])
    n_rows = arrs[0].shape[-2]
    nr, nc = len(rows), len(consts)
    ct_present = [c for c in cts if c is not None]
    ct_arrs, ct_specs = zip(*[_row_spec(c, tr) for c in ct_present])
    add_present = [g[2] for g in row_grads if g[2] is not None]
    add_arrs, add_specs = zip(*[_row_spec(c, tr) for c in add_present]) if add_present else ((), ())
    n_ct, n_add = len(ct_present), len(add_present)
    widths = [s.block_shape[-1] for s in specs]

    def body(*refs):
        ins = refs[:nr + nc]
        ct_refs = refs[nr + nc:nr + nc + n_ct]
        add_refs = refs[nr + nc + n_ct:nr + nc + n_ct + n_add]
        out_refs = refs[nr + nc + n_ct + n_add:]
        vals = [r[...] for r in ins]
        res, f_vjp = jax.vjp(fn, *vals)
        it = iter(ct_refs)
        ct_vals = tuple(next(it)[...].astype(r.dtype) if c is not None else jnp.zeros_like(r)
                        for c, r in zip(cts, res))
        grads = f_vjp(ct_vals)
        ita = iter(add_refs)
        for o_ref, (idx, _, add) in zip(out_refs, row_grads):
            g = grads[idx].astype(F32)
            if add is not None:
                g = g + next(ita)[...].astype(F32)
            o_ref[...] = g.astype(o_ref.dtype)
        first = pl.program_id(0) == 0
        for o_ref, idx in zip(out_refs[len(row_grads):], const_grads):
            g = grads[nr + idx].astype(F32)

            @pl.when(first)
            def _():
                o_ref[...] = g

            @pl.when(jnp.logical_not(first))
            def _():
                o_ref[...] += g

    out_specs = [pl.BlockSpec((tr, widths[idx]), lambda i: (i, 0)) for idx, _, _ in row_grads]
    out_shape = [jax.ShapeDtypeStruct((n_rows, widths[idx]), dt) for idx, dt, _ in row_grads]
    out_specs += [_full_spec(consts[idx]) for idx in const_grads]
    out_shape += [jax.ShapeDtypeStruct(consts[idx].shape, F32) for idx in const_grads]
    return pl.pallas_call(
        body, grid=(n_rows // tr,),
        in_specs=list(specs) + [_full_spec(c) for c in consts] + list(ct_specs) + list(add_specs),
        out_specs=out_specs, out_shape=out_shape,
        name=name, compiler_params=_cparams(("arbitrary",)),
    )(*arrs, *consts, *ct_arrs, *add_arrs)


def f_rms(x, w):
    xf = x.astype(F32)
    return xf * lax.rsqrt(jnp.mean(xf * xf, axis=-1, keepdims=True) + NORM_EPS) * w


def f_rms_pre(x, w):
    return (f_rms(x, w),)


def f_rms_post(x, o, w):
    return (x + f_rms(o, w),)


def f_ssd_gate(y, z, w):
    return (f_rms(y * jax.nn.silu(z), w),)


@jax.custom_vjp
def _swap_halves(x):
    return pltpu.roll(x, HEAD // 2, 1)


_swap_halves.defvjp(lambda x: (_swap_halves(x), None), lambda _, g: (_swap_halves(g),))


def rope_tables(positions, *, tr):
    s = positions.shape[0]
    half = HEAD // 2
    inv = ROPE_BASE ** (-2.0 * jnp.arange(half, dtype=F32) / HEAD)
    inv = jnp.concatenate([inv, inv]).reshape(1, HEAD)
    sign = jnp.concatenate([-jnp.ones((half,), F32), jnp.ones((half,), F32)]).reshape(1, HEAD)

    def fn(pos, inv, sign):
        ang = pos.astype(F32) * inv
        return jnp.cos(ang), jnp.sin(ang) * sign

    return rowwise(fn, [positions.reshape(s, 1)], [inv, sign], [(HEAD, F32), (HEAD, F32)], tr=tr, name="rope_tables")


def _ret_consts(n_heads, blk):
    lg = np.log1p(-np.exp2(-5.0 - np.arange(n_heads)))[:, None, None]
    i = np.arange(blk)
    dist = np.abs(i[:, None] - i[None, :])[None]
    allowed = ((i[None, :] // CHUNK) <= (i[:, None] // CHUNK))[None]
    dm = np.where(allowed, np.exp(lg * dist), 0.0)
    qd = np.broadcast_to(np.exp(lg * (i[None, :, None] + 1.0)), (n_heads, blk, HEAD))
    kd = np.broadcast_to(np.exp(lg * (blk - 1.0 - i[None, :, None])), (n_heads, blk, HEAD))
    cd = np.broadcast_to(np.exp(lg * blk), (n_heads, 1, HEAD))
    return [jnp.asarray(a, F32) for a in (dm, qd, kd, cd)]


def _ret_block(q, k, v, g, gnw, state, cosf, sinf, dm, qd, kd, cd):
    qr = q * cosf + _swap_halves(q) * sinf
    kr = (k * cosf + _swap_halves(k) * sinf) * (HEAD ** -0.5)
    vb = v.astype(BF16)
    scores = _dot(qr.astype(BF16), kr.astype(BF16), _NT) * dm
    o = _dot(scores.astype(BF16), vb) + _dot((qr * qd).astype(BF16), state.astype(BF16))
    new_state = state * cd + _dot((kr * kd).astype(BF16), vb, _TN)
    mu = jnp.mean(o, axis=-1, keepdims=True)
    var = jnp.mean(jnp.square(o - mu), axis=-1, keepdims=True)
    y = (o - mu) * lax.rsqrt(var + NORM_EPS) * gnw * jax.nn.silu(g)
    return y, new_state


def _ret_specs(n_heads, blk, nb, cols, reverse):
    jm = (lambda j: nb - 1 - j) if reverse else (lambda j: j)
    col = lambda c0: pl.BlockSpec((blk, HEAD), lambda h, j: (jm(j), c0 + h))
    tab = pl.BlockSpec((blk, HEAD), lambda h, j: (jm(j), 0))
    specs = [col(c) for c in cols]
    specs += [pl.BlockSpec((1, HEAD), lambda h, j: (0, h)), tab, tab]
    specs += [pl.BlockSpec((None, blk, blk), lambda h, j: (h, 0, 0)),
              pl.BlockSpec((None, blk, HEAD), lambda h, j: (h, 0, 0)),
              pl.BlockSpec((None, blk, HEAD), lambda h, j: (h, 0, 0)),
              pl.BlockSpec((None, 1, HEAD), lambda h, j: (h, 0, 0))]
    state = pl.BlockSpec((None, None, HEAD, HEAD), lambda h, j: (h, jm(j), 0, 0))
    out_col = pl.BlockSpec((blk, HEAD), lambda h, j: (jm(j), h))
    return specs, state, out_col


def retention_fwd(proj, cols, gn_w, cosf, sinf, n_heads, *, blk, name):
    s = proj.shape[0]
    nb = s // blk
    consts = _ret_consts(n_heads, blk)
    specs, state_spec, out_col = _ret_specs(n_heads, blk, nb, cols, False)

    def body(q_ref, k_ref, v_ref, g_ref, gn_ref, cos_ref, sin_ref, dm_ref, qd_ref, kd_ref, cd_ref,
             y_ref, st_ref, state):
        @pl.when(pl.program_id(1) == 0)
        def _():
            state[...] = jnp.zeros_like(state)

        st = state[...]
        st_ref[...] = st
        y, new_state = _ret_block(q_ref[...], k_ref[...], v_ref[...], g_ref[...], gn_ref[...], st,
                                  cos_ref[...], sin_ref[...], dm_ref[...], qd_ref[...], kd_ref[...], cd_ref[...])
        y_ref[...] = y.astype(y_ref.dtype)
        state[...] = new_state

    return pl.pallas_call(
        body, grid=(n_heads, nb), in_specs=specs, out_specs=[out_col, state_spec],
        out_shape=[jax.ShapeDtypeStruct((s, n_heads * HEAD), BF16),
                   jax.ShapeDtypeStruct((n_heads, nb, HEAD, HEAD), F32)],
        scratch_shapes=[pltpu.VMEM((HEAD, HEAD), F32)],
        name=name, compiler_params=_cparams(("parallel", "arbitrary")),
    )(proj, proj, proj, proj, gn_w, cosf, sinf, *consts)


def retention_bwd(proj, cols, gn_w, cosf, sinf, states, dy, n_heads, *, blk, name):
    s = proj.shape[0]
    nb = s // blk
    consts = _ret_consts(n_heads, blk)
    specs, state_spec, out_col = _ret_specs(n_heads, blk, nb, cols, True)

    def body(q_ref, k_ref, v_ref, g_ref, gn_ref, cos_ref, sin_ref, dm_ref, qd_ref, kd_ref, cd_ref,
             st_ref, dy_ref, dq_ref, dk_ref, dv_ref, dg_ref, dgn_ref, dstate):
        first = pl.program_id(1) == 0

        @pl.when(first)
        def _():
            dstate[...] = jnp.zeros_like(dstate)
            dgn_ref[...] = jnp.zeros_like(dgn_ref)

        tabs = (cos_ref[...], sin_ref[...], dm_ref[...], qd_ref[...], kd_ref[...], cd_ref[...])
        fn = lambda q, k, v, g, gnw, st: _ret_block(q, k, v, g, gnw, st, *tabs)
        _, f_vjp = jax.vjp(fn, q_ref[...], k_ref[...], v_ref[...], g_ref[...], gn_ref[...], st_ref[...])
        dq, dk, dv, dg, dgn, dst = f_vjp((dy_ref[...].astype(F32), dstate[...]))
        dq_ref[...] = dq.astype(dq_ref.dtype)
        dk_ref[...] = dk.astype(dk_ref.dtype)
        dv_ref[...] = dv.astype(dv_ref.dtype)
        dg_ref[...] = dg.astype(dg_ref.dtype)
        dgn_ref[...] += dgn
        dstate[...] = dst

    o_shape = jax.ShapeDtypeStruct((s, n_heads * HEAD), BF16)
    dy_spec = out_col
    if isinstance(dy, tuple):
        dy, lead = dy
        dy_spec = pl.BlockSpec((None, blk, HEAD), lambda h, j: (lead, nb - 1 - j, h))
    return pl.pallas_call(
        body, grid=(n_heads, nb), in_specs=specs + [state_spec, dy_spec],
        out_specs=[out_col, out_col, out_col, out_col, pl.BlockSpec((1, HEAD), lambda h, j: (0, h))],
        out_shape=[o_shape, o_shape, o_shape, o_shape, jax.ShapeDtypeStruct((1, n_heads * HEAD), F32)],
        scratch_shapes=[pltpu.VMEM((HEAD, HEAD), F32)],
        name=name, compiler_params=_cparams(("parallel", "arbitrary")),
    )(proj, proj, proj, proj, gn_w, cosf, sinf, *consts, states, dy)


CONV_PAD = 8


def _conv_pre(u_ext, taps, bias, n_out):
    n = u_ext.shape[0]
    views = [pltpu.roll(u_ext, n - (k + CONV_PAD - SSD_K + 1), 0)[:n_out] for k in range(SSD_K)]
    pre = bias
    for k in range(SSD_K):
        pre = pre + taps[k] * views[k]
    return pre, views


def ssd_conv_fwd(u_pad, taps, bias, *, chunk, name):
    s, c = u_pad.shape[0] - 2 * CONV_PAD, u_pad.shape[1]

    def body(u_ref, t0, t1, t2, t3, b_ref, o_ref):
        taps_v = [t[...] for t in (t0, t1, t2, t3)]
        bias_v = b_ref[...]

        @pl.loop(0, s // chunk)
        def _(ci):
            r0 = pl.multiple_of(ci * chunk, chunk)
            pre, _ = _conv_pre(u_ref[pl.ds(r0, chunk + CONV_PAD), :], taps_v, bias_v, chunk)
            o_ref[pl.ds(r0, chunk), :] = pre * jax.nn.sigmoid(pre)

    row = pl.BlockSpec((1, LANE), lambda i: (0, i))
    return pl.pallas_call(
        body, grid=(c // LANE,),
        in_specs=[pl.BlockSpec((s + 2 * CONV_PAD, LANE), lambda i: (0, i))] + [row] * 5,
        out_specs=pl.BlockSpec((s, LANE), lambda i: (0, i)),
        out_shape=jax.ShapeDtypeStruct((s, c), F32),
        name=name, compiler_params=_cparams(("parallel",)),
    )(u_pad, *taps, bias)


def ssd_conv_bwd(u_pad, taps, bias, dxc_pad, *, chunk, name):
    s, c = u_pad.shape[0] - 2 * CONV_PAD, u_pad.shape[1]
    ext = chunk + CONV_PAD

    def body(u_ref, t0, t1, t2, t3, b_ref, d_ref, du_ref, dw_ref, db_ref):
        taps_v = [t[...] for t in (t0, t1, t2, t3)]
        bias_v = b_ref[...]
        dw_ref[...] = jnp.zeros_like(dw_ref)
        db_ref[...] = jnp.zeros_like(db_ref)

        @pl.loop(0, s // chunk)
        def _(ci):
            r0 = pl.multiple_of(ci * chunk, chunk)
            pre, views = _conv_pre(u_ref[pl.ds(r0, ext + CONV_PAD), :], taps_v, bias_v, ext)
            sig = jax.nn.sigmoid(pre)
            dpre = d_ref[pl.ds(r0, ext), :] * (sig * (1.0 + pre * (1.0 - sig)))
            du = taps_v[SSD_K - 1] * dpre[:chunk]
            for k in range(SSD_K - 1):
                du = du + taps_v[k] * pltpu.roll(dpre, ext - (SSD_K - 1 - k), 0)[:chunk]
            du_ref[pl.ds(r0, chunk), :] = du.astype(du_ref.dtype)
            own = dpre[:chunk]
            for k in range(SSD_K):
                dw_ref[k:k + 1, :] += jnp.sum(own * views[k][:chunk], axis=0, keepdims=True)
            db_ref[...] += jnp.sum(own, axis=0, keepdims=True)

    row = pl.BlockSpec((1, LANE), lambda i: (0, i))
    return pl.pallas_call(
        body, grid=(c // LANE,),
        in_specs=[pl.BlockSpec((s + 2 * CONV_PAD, LANE), lambda i: (0, i))] + [row] * 5
        + [pl.BlockSpec((s + CONV_PAD, LANE), lambda i: (0, i))],
        out_specs=[pl.BlockSpec((s, LANE), lambda i: (0, i)), pl.BlockSpec((SSD_K, LANE), lambda i: (0, i)), row],
        out_shape=[jax.ShapeDtypeStruct((s, c), BF16), jax.ShapeDtypeStruct((SSD_K, c), F32),
                   jax.ShapeDtypeStruct((1, c), F32)],
        name=name, compiler_params=_cparams(("parallel",)),
    )(u_pad, *taps, bias, dxc_pad)


def _tri_dot(tri, x, passes=3):
    out = None
    rem = x
    for _ in range(passes):
        piece = rem.astype(BF16)
        rem = rem - piece.astype(F32)
        d = _dot(tri, piece)
        out = d if out is None else out + d
    return out


def _tri(n, upper):
    rr = lax.broadcasted_iota(jnp.int32, (n, n), 0)
    cc = lax.broadcasted_iota(jnp.int32, (n, n), 1)
    return ((rr <= cc) if upper else (rr >= cc)).astype(BF16)


@jax.custom_vjp
def _cumsum_rows(a):
    return _tri_dot(_tri(a.shape[0], False), a)


_cumsum_rows.defvjp(lambda a: (_cumsum_rows(a), None), lambda _, g: (_tri_dot(_tri(g.shape[0], True), g),))


def _softplus(x):
    return jnp.maximum(x, 0.0) + jnp.log(1.0 + jnp.exp(-jnp.abs(x)))


def _ssd_block(x, bm, cm, dtraw, dtb, alog, dsk, state_t, group):
    blk, width = x.shape
    e_heads = width // SSD_P
    dt = _softplus(dtraw + dtb)
    acum = _cumsum_rows(dt * (-jnp.exp(alog)))
    acum_t = acum.T
    lane_h = lax.broadcasted_iota(jnp.int32, (1, LANE), 1)
    sub_h = lax.broadcasted_iota(jnp.int32, (LANE, 1), 0)
    lane_e = lax.broadcasted_iota(jnp.int32, (1, width), 1) // SSD_P
    causal = lax.broadcasted_iota(jnp.int32, (blk, blk), 0) >= lax.broadcasted_iota(jnp.int32, (blk, blk), 1)
    last_row = lax.broadcasted_iota(jnp.int32, (blk, 1), 0) == blk - 1
    cb = _dot(cm.astype(BF16), bm.astype(BF16), _NT)
    y = jnp.zeros((blk, width), F32)
    dt_l = jnp.zeros((blk, width), F32)
    ac_l = jnp.zeros((blk, width), F32)
    d_l = jnp.zeros((1, width), F32)
    for e in range(e_heads):
        head = group * e_heads + e
        pick = lane_h == head
        col = jnp.sum(jnp.where(pick, acum, 0.0), axis=1, keepdims=True)
        dt_e = jnp.sum(jnp.where(pick, dt, 0.0), axis=1, keepdims=True)
        d_e = jnp.sum(jnp.where(pick, dsk, 0.0), axis=1, keepdims=True)
        row = jnp.sum(jnp.where(sub_h == head, acum_t, 0.0), axis=0, keepdims=True)
        mine = lane_e == e
        decay = jnp.exp(jnp.where(causal, col - row, -jnp.inf))
        y = y + _dot((cb * decay).astype(BF16), jnp.where(mine, x * dt_e, 0.0).astype(BF16))
        dt_l = dt_l + jnp.where(mine, dt_e, 0.0)
        ac_l = ac_l + jnp.where(mine, col, 0.0)
        d_l = d_l + jnp.where(mine, d_e, 0.0)
    ac_last = jnp.sum(jnp.where(last_row, ac_l, 0.0), axis=0, keepdims=True)
    y = y + jnp.exp(ac_l) * _dot(cm.astype(BF16), state_t.astype(BF16)) + x * d_l
    inject = (x * dt_l * jnp.exp(ac_last - ac_l)).astype(BF16)
    new_state = state_t * jnp.exp(ac_last) + _dot(bm.astype(BF16), inject, _TN)
    return y, new_state


def ssd_scan_fwd(xc, proj, dt_col, dtb, alog, dsk, *, blk, name):
    s = xc.shape[0]
    nb = s // blk
    e_w = 4 * SSD_P
    b_off = SSD_G * e_w // SSD_N
    c_off = b_off + SSD_G
    row = pl.BlockSpec((1, LANE), lambda j, g: (0, 0))

    def body(x_ref, b_ref, c_ref, dt_ref, dtb_ref, al_ref, d_ref, y_ref, st_ref, state):
        j, g = pl.program_id(0), pl.program_id(1)

        @pl.when(j == 0)
        def _():
            state[g] = jnp.zeros((SSD_N, e_w), F32)

        st = state[g]
        st_ref[...] = st
        y, new_state = _ssd_block(x_ref[...], b_ref[...], c_ref[...], dt_ref[...], dtb_ref[...], al_ref[...],
                                  d_ref[...], st, g)
        y_ref[...] = y
        state[g] = new_state

    return pl.pallas_call(
        body, grid=(nb, SSD_G),
        in_specs=[pl.BlockSpec((blk, e_w), lambda j, g: (j, g)),
                  pl.BlockSpec((blk, SSD_N), lambda j, g: (j, b_off + g)),
                  pl.BlockSpec((blk, SSD_N), lambda j, g: (j, c_off + g)),
                  pl.BlockSpec((blk, LANE), lambda j, g: (j, dt_col)), row, row, row],
        out_specs=[pl.BlockSpec((blk, e_w), lambda j, g: (j, g)),
                   pl.BlockSpec((None, None, SSD_N, e_w), lambda j, g: (j, g, 0, 0))],
        out_shape=[jax.ShapeDtypeStruct((s, SSD_G * e_w), F32),
                   jax.ShapeDtypeStruct((nb, SSD_G, SSD_N, e_w), F32)],
        scratch_shapes=[pltpu.VMEM((SSD_G, SSD_N, e_w), F32)],
        name=name, compiler_params=_cparams(("arbitrary", "arbitrary")),
    )(xc, xc, xc, proj, dtb, alog, dsk)


def ssd_scan_bwd(xc, proj, dt_col, dtb, alog, dsk, states, dy, *, blk, name):
    s = xc.shape[0]
    nb = s // blk
    e_w = 4 * SSD_P
    b_off = SSD_G * e_w // SSD_N
    c_off = b_off + SSD_G
    row = pl.BlockSpec((1, LANE), lambda j, g: (0, 0))
    jm = lambda j: nb - 1 - j

    def body(x_ref, b_ref, c_ref, dt_ref, dtb_ref, al_ref, d_ref, st_ref, dy_ref,
             dx_ref, db_ref, dc_ref, ddt_ref, ddtb_ref, dal_ref, dd_ref, dstate):
        j, g = pl.program_id(0), pl.program_id(1)

        @pl.when(j == 0)
        def _():
            dstate[g] = jnp.zeros((SSD_N, e_w), F32)

        @pl.when(jnp.logical_and(j == 0, g == 0))
        def _():
            ddtb_ref[...] = jnp.zeros_like(ddtb_ref)
            dal_ref[...] = jnp.zeros_like(dal_ref)
            dd_ref[...] = jnp.zeros_like(dd_ref)

        @pl.when(g == 0)
        def _():
            ddt_ref[...] = jnp.zeros_like(ddt_ref)

        fn = functools.partial(_ssd_block, group=g)
        _, f_vjp = jax.vjp(fn, x_ref[...], b_ref[...], c_ref[...], dt_ref[...], dtb_ref[...], al_ref[...],
                           d_ref[...], st_ref[...])
        dx, db, dc, ddt, ddtb, dal, dd, dst = f_vjp((dy_ref[...], dstate[g]))
        dx_ref[...] = dx
        db_ref[...] = db
        dc_ref[...] = dc
        ddt_ref[...] += ddt
        ddtb_ref[...] += ddtb
        dal_ref[...] += dal
        dd_ref[...] += dd
        dstate[g] = dst

    return pl.pallas_call(
        body, grid=(nb, SSD_G),
        in_specs=[pl.BlockSpec((blk, e_w), lambda j, g: (jm(j), g)),
                  pl.BlockSpec((blk, SSD_N), lambda j, g: (jm(j), b_off + g)),
                  pl.BlockSpec((blk, SSD_N), lambda j, g: (jm(j), c_off + g)),
                  pl.BlockSpec((blk, LANE), lambda j, g: (jm(j), dt_col)), row, row, row,
                  pl.BlockSpec((None, None, SSD_N, e_w), lambda j, g: (jm(j), g, 0, 0)),
                  pl.BlockSpec((blk, e_w), lambda j, g: (jm(j), g))],
        out_specs=[pl.BlockSpec((blk, e_w), lambda j, g: (jm(j), g)),
                   pl.BlockSpec((blk, SSD_N), lambda j, g: (jm(j), g)),
                   pl.BlockSpec((blk, SSD_N), lambda j, g: (jm(j), g)),
                   pl.BlockSpec((blk, LANE), lambda j, g: (jm(j), 0)), row, row, row],
        out_shape=[jax.ShapeDtypeStruct((s, SSD_G * e_w), F32),
                   jax.ShapeDtypeStruct((s, SSD_G * SSD_N), F32),
                   jax.ShapeDtypeStruct((s, SSD_G * SSD_N), F32),
                   jax.ShapeDtypeStruct((s, LANE), F32)] + [jax.ShapeDtypeStruct((1, LANE), F32)] * 3,
        scratch_shapes=[pltpu.VMEM((SSD_G, SSD_N, e_w), F32)],
        name=name, compiler_params=_cparams(("arbitrary", "arbitrary")),
    )(xc, xc, xc, proj, dtb, alog, dsk, states, dy)


SB_DEAD = -104.0

_NT = (((1,), (1,)), ((), ()))
_TN = (((0,), (0,)), ((), ()))


def _dot(a, b, dims=(((1,), (0,)), ((), ()))):
    return lax.dot_general(a, b, dims, preferred_element_type=F32)


def _split_dot(x, tri, passes):
    out = None
    rem = x
    for _ in range(passes):
        piece = rem.astype(BF16)
        rem = rem - piece.astype(F32)
        d = _dot(piece, tri)
        out = d if out is None else out + d
    return out


def _sb_scores(q, k_ref, j, blk, row, scale):
    kb = k_ref[pl.ds(pl.multiple_of(j * blk, blk), blk), :].astype(BF16)
    z = _dot(q, kb, _NT) * scale
    col = j * blk + lax.broadcasted_iota(jnp.int32, (blk, blk), 1)
    mask = col < row
    sp = jnp.maximum(z, 0.0) + jnp.log(1.0 + jnp.exp(-jnp.abs(z)))
    lk = jnp.where(mask, -sp, 0.0)
    return kb, mask, lk, z - sp


def sb_attention_fwd(proj, q_col, k_col, v_col, n_heads, *, blk, name):
    s = proj.shape[0]
    nq = s // blk
    scale = HEAD ** -0.5

    def body(q_ref, k_ref, v_ref, o_ref, r_ref, n_ref):
        i = pl.program_id(1)
        q = q_ref[...].astype(BF16)
        row = i * blk + lax.broadcasted_iota(jnp.int32, (blk, blk), 0)
        rr = lax.broadcasted_iota(jnp.int32, (blk, blk), 0)
        cc = lax.broadcasted_iota(jnp.int32, (blk, blk), 1)
        tri_after = (rr > cc).astype(BF16)

        def alive(carry):
            jj, _, run = carry
            return jnp.logical_and(jj <= i, jnp.max(run) > SB_DEAD)

        def step(carry):
            jj, acc, run = carry
            j = i - jj
            _, mask, lk, ls = _sb_scores(q, k_ref, j, blk, row, scale)
            later = _split_dot(lk, tri_after, 2) + run
            w = jnp.where(mask, jnp.exp(ls + later), 0.0)
            vb = v_ref[pl.ds(pl.multiple_of(j * blk, blk), blk), :].astype(BF16)
            return jj + 1, acc + _dot(w.astype(BF16), vb), run + jnp.sum(lk, axis=1, keepdims=True)

        n, acc, run = lax.while_loop(
            alive, step, (jnp.int32(0), jnp.zeros((blk, HEAD), F32), jnp.zeros((blk, 1), F32)))
        o_ref[...] = acc.astype(o_ref.dtype)
        r_ref[...] = jnp.broadcast_to(run, (blk, HEAD))
        n_ref[pl.program_id(0), i] = n

    blk_spec = lambda c0: pl.BlockSpec((blk, HEAD), lambda h, i: (i, c0 + h))
    full_spec = lambda c0: pl.BlockSpec((s, HEAD), lambda h, i: (0, c0 + h))
    out_spec = pl.BlockSpec((blk, HEAD), lambda h, i: (i, h))
    return pl.pallas_call(
        body, grid=(n_heads, nq),
        in_specs=[blk_spec(q_col), full_spec(k_col), full_spec(v_col)],
        out_specs=[out_spec, out_spec, pl.BlockSpec(memory_space=pltpu.SMEM)],
        out_shape=[jax.ShapeDtypeStruct((s, n_heads * HEAD), BF16),
                   jax.ShapeDtypeStruct((s, n_heads * HEAD), F32),
                   jax.ShapeDtypeStruct((n_heads, nq), jnp.int32)],
        name=name, compiler_params=_cparams(("arbitrary", "arbitrary")),
    )(proj, proj, proj)


def sb_attention_bwd(proj, d_out, run_tot, visited, q_col, k_col, v_col, n_heads, *, blk, name):
    s = proj.shape[0]
    nq = s // blk
    scale = HEAD ** -0.5

    def body(n_ref, q_ref, k_ref, v_ref, do_ref, r_ref, dq_ref, dk_ref, dv_ref, dk_acc, dv_acc):
        i = pl.program_id(1)
        first = i + 1 - jnp.clip(n_ref[pl.program_id(0), i], 1, i + 1)

        @pl.when(i == 0)
        def _():
            dk_acc[...] = jnp.zeros_like(dk_acc)
            dv_acc[...] = jnp.zeros_like(dv_acc)

        q = q_ref[...].astype(BF16)
        do = do_ref[...].astype(BF16)
        rtot = r_ref[:, :1]
        row = i * blk + lax.broadcasted_iota(jnp.int32, (blk, blk), 0)
        rr = lax.broadcasted_iota(jnp.int32, (blk, blk), 0)
        cc = lax.broadcasted_iota(jnp.int32, (blk, blk), 1)
        tri_upto = (rr <= cc).astype(BF16)
        tri_before = (rr < cc).astype(BF16)

        def step(j, carry):
            dq, pre, gpre = carry
            kb, mask, lk, ls = _sb_scores(q, k_ref, j, blk, row, scale)
            rows = pl.ds(pl.multiple_of(j * blk, blk), blk)
            vb = v_ref[rows, :].astype(BF16)
            later = rtot - (pre + _split_dot(lk, tri_upto, 2))
            w = jnp.where(mask, jnp.exp(ls + later), 0.0)
            g = w * _dot(do, vb, _NT)
            g_before = _split_dot(g, tri_before, 1) + gpre
            sig = jnp.exp(ls)
            dz = (jnp.where(mask, g * (1.0 - sig) - sig * g_before, 0.0) * scale).astype(BF16)
            dk_acc[rows, :] += _dot(dz, q, _TN)
            dv_acc[rows, :] += _dot(w.astype(BF16), do, _TN)
            return (dq + _dot(dz, kb), pre + jnp.sum(lk, axis=1, keepdims=True),
                    gpre + jnp.sum(g, axis=1, keepdims=True))

        zero = jnp.zeros((blk, 1), F32)
        dq, _, _ = lax.fori_loop(first, i + 1, step, (jnp.zeros((blk, HEAD), F32), zero, zero))
        dq_ref[...] = dq.astype(dq_ref.dtype)

        @pl.when(i == nq - 1)
        def _():
            dk_ref[...] = dk_acc[...].astype(dk_ref.dtype)
            dv_ref[...] = dv_acc[...].astype(dv_ref.dtype)

    blk_spec = lambda c0: pl.BlockSpec((blk, HEAD), lambda h, i: (i, c0 + h))
    full_spec = lambda c0: pl.BlockSpec((s, HEAD), lambda h, i: (0, c0 + h))
    o_shape = jax.ShapeDtypeStruct((s, n_heads * HEAD), BF16)
    do_spec = blk_spec(0)
    if isinstance(d_out, tuple):
        d_out, lead = d_out
        do_spec = pl.BlockSpec((None, blk, HEAD), lambda h, i: (lead, i, h))
    return pl.pallas_call(
        body, grid=(n_heads, nq),
        in_specs=[pl.BlockSpec(memory_space=pltpu.SMEM), blk_spec(q_col), full_spec(k_col), full_spec(v_col),
                  do_spec, blk_spec(0)],
        out_specs=[blk_spec(0), full_spec(0), full_spec(0)],
        out_shape=[o_shape, o_shape, o_shape],
        scratch_shapes=[pltpu.VMEM((s, HEAD), F32), pltpu.VMEM((s, HEAD), F32)],
        name=name, compiler_params=_cparams(("parallel", "arbitrary")),
    )(visited, proj, proj, proj, d_out, run_tot)


def _hosting(hosted):
    arrs = [arr for _, arr in hosted]
    shapes = [_comm_out_shape(kind, arr) for kind, arr in hosted]

    def ops(src_refs, dst_refs, sems):
        return [_COMM[kind](src_refs[c], dst_refs[c], *sems[3 * c:3 * c + 3]) for c, (kind, _) in enumerate(hosted)]

    return arrs, [_HBM] * len(hosted), shapes, _COMM_SEMS * len(hosted), ops


def ffn_up_swiglu(h, w_gu, *, tm, hosted=(), name):
    s, d = h.shape
    tm = _pick(s, tm)
    nb, _, hb, _ = w_gu.shape
    n_host = len(hosted)
    h_arrs, h_specs, h_shapes, h_sems, h_ops = _hosting(hosted)
    grid = (s // tm, nb)

    def body(h_ref, w_ref, *rest):
        src_refs, (gu_ref, act_ref) = rest[:n_host], rest[n_host:n_host + 2]
        dst_refs, sems = rest[n_host + 2:2 * n_host + 2], rest[2 * n_host + 2:]
        comms = h_ops(src_refs, dst_refs, sems)
        i, g = pl.program_id(0), pl.program_id(1)

        if hosted:
            @pl.when(jnp.logical_and(i == 0, g == 0))
            def _():
                for start, _ in comms:
                    start()

        a = h_ref[...]
        gate = _dot(a, w_ref[0], _NT)
        up = _dot(a, w_ref[1], _NT)
        gu_ref[0] = gate
        gu_ref[1] = up
        act_ref[...] = (gate * jax.nn.sigmoid(gate) * up).astype(act_ref.dtype)

        if hosted:
            @pl.when(jnp.logical_and(i == grid[0] - 1, g == grid[1] - 1))
            def _():
                for _, finish in comms:
                    finish()

    res = pl.pallas_call(
        body, grid=grid,
        in_specs=[pl.BlockSpec((tm, d), lambda i, g: (i, 0)),
                  pl.BlockSpec((None, 2, hb, d), lambda i, g: (g, 0, 0, 0))] + h_specs,
        out_specs=[pl.BlockSpec((None, 2, tm, hb), lambda i, g: (g, 0, i, 0)),
                   pl.BlockSpec((None, tm, hb), lambda i, g: (g, i, 0))] + h_specs,
        out_shape=[jax.ShapeDtypeStruct((nb, 2, s, hb), F32), jax.ShapeDtypeStruct((nb, s, hb), BF16)] + h_shapes,
        scratch_shapes=h_sems, name=name,
        compiler_params=_cparams(("arbitrary", "arbitrary") if hosted else ("parallel", "parallel")),
    )(h, w_gu, *h_arrs)
    return res[0], res[1], res[2:]


def branch_merge(y3, w3, proj, gate_col, biases, *, tm, tn, name):
    nbr, s, k = y3.shape
    n = w3.shape[2]
    tm, tn = _pick(s, tm), _pick(n, tn)
    per = n // tn

    def body(y_ref, w_ref, g0, g1, g2, b0, b1, b2, m_ref, u_ref):
        merged = None
        for i, (g_ref, b_ref) in enumerate(((g0, b0), (g1, b1), (g2, b2))):
            u = _dot(y_ref[i], w_ref[i])
            u_ref[i] = u
            part = jax.nn.sigmoid(g_ref[...] + b_ref[...]) * u
            merged = part if merged is None else merged + part
        m_ref[...] = merged.astype(m_ref.dtype)

    gate_spec = lambda i: pl.BlockSpec((tm, tn), lambda r, j: (r, (gate_col + i) * per + j))
    bias_spec = pl.BlockSpec((1, tn), lambda r, j: (0, j))
    return pl.pallas_call(
        body, grid=(s // tm, per),
        in_specs=[pl.BlockSpec((nbr, tm, k), lambda r, j: (0, r, 0)), pl.BlockSpec((nbr, k, tn), lambda r, j: (0, 0, j)),
                  gate_spec(0), gate_spec(1), gate_spec(2), bias_spec, bias_spec, bias_spec],
        out_specs=[pl.BlockSpec((tm, tn), lambda r, j: (r, j)), pl.BlockSpec((nbr, tm, tn), lambda r, j: (0, r, j))],
        out_shape=[jax.ShapeDtypeStruct((s, n), BF16), jax.ShapeDtypeStruct((nbr, s, n), F32)],
        name=name, compiler_params=_cparams(("parallel", "parallel")),
    )(y3, w3, proj, proj, proj, *biases)


def gate_merge_bwd(u3, proj, gate_col, biases, dmerged, *, tr, name):
    nbr, s, n = u3.shape

    def body(u_ref, g0, g1, g2, b0, b1, b2, dm_ref, du_ref, dg_ref, db0, db1, db2):
        first = pl.program_id(0) == 0
        dm = dm_ref[...]
        for i, (g_ref, b_ref, db_ref) in enumerate(((g0, b0, db0), (g1, b1, db1), (g2, b2, db2))):
            sig = jax.nn.sigmoid(g_ref[...] + b_ref[...])
            du_ref[i] = (dm * sig).astype(du_ref.dtype)
            dlogit = dm * u_ref[i] * (sig * (1.0 - sig))
            dg_ref[:, i * n:(i + 1) * n] = dlogit.astype(dg_ref.dtype)
            part = jnp.sum(dlogit, axis=0, keepdims=True)

            @pl.when(first)
            def _():
                db_ref[...] = part

            @pl.when(jnp.logical_not(first))
            def _():
                db_ref[...] += part

    gate_spec = lambda i: pl.BlockSpec((tr, n), lambda r: (r, gate_col + i))
    bias_spec = pl.BlockSpec((1, n), lambda r: (0, 0))
    u_spec = pl.BlockSpec((nbr, tr, n), lambda r: (0, r, 0))
    return pl.pallas_call(
        body, grid=(s // tr,),
        in_specs=[u_spec, gate_spec(0), gate_spec(1), gate_spec(2), bias_spec, bias_spec, bias_spec,
                  pl.BlockSpec((tr, n), lambda r: (r, 0))],
        out_specs=[u_spec, pl.BlockSpec((tr, nbr * n), lambda r: (r, 0)), bias_spec, bias_spec, bias_spec],
        out_shape=[jax.ShapeDtypeStruct(u3.shape, BF16), jax.ShapeDtypeStruct((s, nbr * n), BF16)]
        + [jax.ShapeDtypeStruct((1, n), F32)] * 3,
        name=name, compiler_params=_cparams(("arbitrary",)),
    )(u3, proj, proj, proj, *biases, dmerged)


def ffn_down_dx_swiglu(df, w_down, gu, *, tm, name):
    s, d = df.shape
    tm = _pick(s, tm)
    nb, hb, _ = w_down.shape

    def body(df_ref, w_ref, gu_ref, dgu_ref):
        dact = _dot(df_ref[...], w_ref[...], _NT)
        gate, up = gu_ref[0], gu_ref[1]
        sig = jax.nn.sigmoid(gate)
        dgu_ref[0] = (dact * up * (sig * (1.0 + gate * (1.0 - sig)))).astype(dgu_ref.dtype)
        dgu_ref[1] = (dact * gate * sig).astype(dgu_ref.dtype)

    blk = pl.BlockSpec((None, 2, tm, hb), lambda i, g: (g, 0, i, 0))
    return pl.pallas_call(
        body, grid=(s // tm, nb),
        in_specs=[pl.BlockSpec((tm, d), lambda i, g: (i, 0)), pl.BlockSpec((None, hb, d), lambda i, g: (g, 0, 0)), blk],
        out_specs=blk, out_shape=jax.ShapeDtypeStruct(gu.shape, BF16),
        name=name, compiler_params=_cparams(("parallel", "parallel")),
    )(df, w_down, gu)


def loss_head(y, target, *, tr, name):
    s, d = y.shape

    def body(y_ref, t_ref, l_ref, dy_ref):
        err = y_ref[...] - t_ref[...]
        dy_ref[...] = err * (1.0 / d)
        part = 0.5 * jnp.sum(jnp.mean(err * err, axis=-1, keepdims=True), axis=0, keepdims=True)

        @pl.when(pl.program_id(0) == 0)
        def _():
            l_ref[...] = jnp.zeros_like(l_ref)

        l_ref[...] += jnp.broadcast_to(part, l_ref.shape)

    row = pl.BlockSpec((tr, d), lambda i: (i, 0))
    return pl.pallas_call(
        body, grid=(s // tr,), in_specs=[row, row],
        out_specs=[pl.BlockSpec((8, LANE), lambda i: (0, 0)), row],
        out_shape=[jax.ShapeDtypeStruct((8, LANE), F32), jax.ShapeDtypeStruct((s, d), F32)],
        name=name, compiler_params=_cparams(("arbitrary",)),
    )(y, target)


def _adamw_math(w, g, m, v):
    m = ADAM_B1 * m + (1.0 - ADAM_B1) * g
    v = ADAM_B2 * v + (1.0 - ADAM_B2) * jnp.square(g)
    m_hat = m / (1.0 - ADAM_B1 ** ADAM_STEP)
    v_hat = v / (1.0 - ADAM_B2 ** ADAM_STEP)
    delta = -ADAM_LR * (m_hat / (jnp.sqrt(v_hat) + ADAM_EPS) + ADAM_WD * w)
    return delta, m, v


def adamw_sum(w, m, v, parts, *, tr, name):
    n, r, c = parts.shape
    tr = _pick_rows(r, tr)

    def body(w_ref, m_ref, v_ref, p_ref, g_ref, d_ref, nm_ref, nv_ref):
        g = p_ref[0].astype(F32)
        for i in range(1, n):
            g = g + p_ref[i].astype(F32)
        delta, nm, nv = _adamw_math(w_ref[...], g, m_ref[...], v_ref[...])
        g_ref[...] = g
        d_ref[...] = delta
        nm_ref[...] = nm
        nv_ref[...] = nv

    row = pl.BlockSpec((tr, c), lambda i: (i, 0))
    shape = jax.ShapeDtypeStruct((r, c), F32)
    return pl.pallas_call(
        body, grid=(r // tr,),
        in_specs=[row, row, row, pl.BlockSpec((n, tr, c), lambda i: (0, i, 0))],
        out_specs=[row] * 4, out_shape=[shape] * 4,
        name=name, compiler_params=_cparams(("parallel",)),
    )(w, m, v, parts)


def adamw_into(w_all, m_all, v_all, parts, layer, row_off, prev, *, tr, name):
    depth, r, c = w_all.shape
    n = parts.shape[0]
    if r % 16 == 0:
        tr = _pick_rows(r, tr)
        assert row_off % tr == 0
        off = row_off // tr
        grid = (r // tr,)
        lay = pl.BlockSpec((None, tr, c), lambda i: (layer, i, 0))
        p_spec = pl.BlockSpec((n, tr, c), lambda i: (0, off + i, 0))
    else:
        assert row_off == 0 and parts.shape[1] == r
        grid = (c // LANE,)
        lay = pl.BlockSpec((None, r, LANE), lambda j: (layer, 0, j))
        p_spec = pl.BlockSpec((n, r, LANE), lambda j: (0, 0, j))
    if prev is None:
        prev = [lax.empty(w_all.shape, F32) for _ in range(4)]

    def body(w_ref, m_ref, v_ref, p_ref, _g, _d, _m, _v, g_ref, d_ref, nm_ref, nv_ref):
        g = p_ref[0].astype(F32)
        for i in range(1, n):
            g = g + p_ref[i].astype(F32)
        delta, nm, nv = _adamw_math(w_ref[...], g, m_ref[...], v_ref[...])
        g_ref[...] = g
        d_ref[...] = delta
        nm_ref[...] = nm
        nv_ref[...] = nv

    untouched = pl.BlockSpec(memory_space=pl.ANY)
    return pl.pallas_call(
        body, grid=grid,
        in_specs=[lay, lay, lay, p_spec] + [untouched] * 4,
        out_specs=[lay] * 4, out_shape=[jax.ShapeDtypeStruct(w_all.shape, F32)] * 4,
        input_output_aliases={4: 0, 5: 1, 6: 2, 7: 3},
        name=name, compiler_params=_cparams(("parallel",)),
    )(w_all, m_all, v_all, parts, *prev)


def _pick_rows(r, pref):
    t = min(pref, r)
    while r % t or (t % 16 and t != r):
        t -= 1
    return t


_HBM = pl.BlockSpec(memory_space=pltpu.HBM)
_MESH = pl.DeviceIdType.MESH


def _flat_index(px, py, pc):
    return 4 * px + 2 * py + pc


def _gather_ops(x_ref, out_ref, send_sems, recv_sems, local_sem):
    x, y, c = lax.axis_index("x"), lax.axis_index("y"), lax.axis_index("c")
    me, sibling = (x, y, c), (x, y, 1 - c)
    chips = [(1 - x, y), (x, 1 - y), (1 - x, 1 - y)]

    def slot(p):
        return out_ref.at[_flat_index(*p)]

    def copy(k, block, to, src=None):
        return pltpu.make_async_remote_copy(
            src_ref=slot(block) if src is None else src, dst_ref=slot(block),
            send_sem=send_sems.at[k], recv_sem=recv_sems.at[k], device_id=to, device_id_type=_MESH)

    mine = pltpu.make_async_copy(x_ref, slot(me), local_sem)
    first = [copy(0, me, sibling, src=x_ref)]
    first += [copy(1 + j, me, (*chip, c), src=x_ref) for j, chip in enumerate(chips)]
    passed = [copy(4 + j, (*chip, c), sibling) for j, chip in enumerate(chips)]

    def start():
        mine.start()
        for cp in first:
            cp.start()

    def finish():
        for j, chip in enumerate(chips):
            copy(1 + j, (*chip, c), me).wait_recv()
            passed[j].start()
        copy(0, sibling, me).wait_recv()
        for j, chip in enumerate(chips):
            copy(4 + j, (*chip, 1 - c), me).wait_recv()
        for cp in first + passed:
            cp.wait_send()
        mine.wait()

    return start, finish


def _exchange_ops(p_ref, out_ref, send_sems, recv_sems, local_sem):
    x, y, c = lax.axis_index("x"), lax.axis_index("y"), lax.axis_index("c")
    me = _flat_index(x, y, c)
    peers = [(1 - x if k & 4 else x, 1 - y if k & 2 else y, 1 - c if k & 1 else c) for k in range(1, N_DEV)]

    def copy(k, peer, dst_slot):
        return pltpu.make_async_remote_copy(
            src_ref=p_ref.at[_flat_index(*peer)], dst_ref=out_ref.at[dst_slot],
            send_sem=send_sems.at[k], recv_sem=recv_sems.at[k], device_id=peer, device_id_type=_MESH)

    mine = pltpu.make_async_copy(p_ref.at[me], out_ref.at[me], local_sem)
    sends = [copy(k, peer, me) for k, peer in enumerate(peers)]

    def start():
        mine.start()
        for cp in sends:
            cp.start()

    def finish():
        for k, peer in enumerate(peers):
            copy(k, peer, _flat_index(*peer)).wait_recv()
        for cp in sends:
            cp.wait_send()
        mine.wait()

    return start, finish


_COMM = {"gather": _gather_ops, "exchange": _exchange_ops}
_COMM_SEMS = [pltpu.SemaphoreType.DMA((N_DEV - 1,)), pltpu.SemaphoreType.DMA((N_DEV - 1,)), pltpu.SemaphoreType.DMA]


def _comm_out_shape(kind, arr):
    return jax.ShapeDtypeStruct(((N_DEV,) + arr.shape) if kind == "gather" else arr.shape, arr.dtype)


def _comm_call(kind, arr, name):
    def body(src_ref, dst_ref, send_sems, recv_sems, local_sem):
        start, finish = _COMM[kind](src_ref, dst_ref, send_sems, recv_sems, local_sem)
        start()
        finish()

    return pl.pallas_call(body, out_shape=_comm_out_shape(kind, arr), in_specs=[_HBM], out_specs=_HBM,
                          scratch_shapes=_COMM_SEMS, name=name)(arr)


def all_gather(x, *, name):
    return _comm_call("gather", x, name)


def exchange(parts, *, name):
    return _comm_call("exchange", parts, name)


TR = 256
TR_WIDE = 128
BLK = 256
SB_BLK = 256
FFN_TM = 1024
CONV_CHUNK = 512
DT_PAD = LANE

BIG = ("w_in", "w_branch", "w_out", "ffn_w_gu", "ffn_w_down")
SMALL = ("norm_mix_pre", "norm_mix_post", "norm_ffn_pre", "norm_ffn_post", "b_gate", "ret_gn_w", "ssd_conv_w",
         "ssd_conv_b", "ssd_dt_bias", "ssd_a_log", "ssd_d", "ssd_norm_w")


def _row(v):
    return v.reshape(1, -1)


def _pad_lanes(v):
    return jnp.pad(v.reshape(1, -1), ((0, 0), (0, LANE - v.shape[-1])))


def _rows_of(segments, lo, hi):
    out, start = [], 0
    for seg in segments:
        a, b = max(lo, start), min(hi, start + seg.shape[0])
        if a < b:
            out.append(seg[a - start:b - start])
        start += seg.shape[0]
    return out


def _assemble_w_in(g, d):
    full = g.reshape(-1, d)
    n_main, n_dt = 5 * d, full.shape[0] - 8 * d
    main, dt, gates = full[:n_main], full[n_main:n_main + n_dt], full[n_main + n_dt:]
    return jnp.concatenate([main, gates], axis=0), jnp.pad(dt, ((0, DT_PAD - n_dt), (0, 0)))


def _split_dw_in(dw, dw_dt, d, n_dt):
    n_main = 5 * d
    segments = [dw[:n_main], dw_dt[:n_dt], dw[n_main:]]
    rows = (8 * d + n_dt) // N_DEV
    return jnp.stack([jnp.concatenate(_rows_of(segments, j * rows, (j + 1) * rows), axis=0) for j in range(N_DEV)])


def _layer_fwd(x, lw, cosf, sinf, next_shards):
    s, d = x.shape
    heads = d // 2 // HEAD
    nxt = {}
    (h,) = rowwise(f_rms_pre, [x], [lw["norm_mix_pre"]], [(d, BF16)], tr=TR, name="mix_pre_norm")
    if next_shards is None:
        proj = matmul(h, lw["w_cat"], tb=True, tn=2048, name="in_proj")
    else:
        proj, (nxt["w_in"],) = matmul(h, lw["w_cat"], tb=True, tn=2048, hosted=[("gather", next_shards["w_in"])],
                                      name="in_proj_gather")
    dt_raw = matmul(h, lw["w_dt"], tb=True, name="in_proj_dt")
    ret_cols = tuple(i * heads for i in range(4))
    y_ret, ret_states = retention_fwd(proj, ret_cols, lw["ret_gn_w"], cosf, sinf, heads, blk=BLK, name="retention_fwd")
    y_sb, sb_run, sb_visited = sb_attention_fwd(proj, 4 * heads, 5 * heads, 6 * heads, heads, blk=SB_BLK,
                                                name="stickbreak_fwd")
    u_pad = jnp.pad(proj[:, 4 * d:5 * d], ((CONV_PAD, CONV_PAD), (0, 0)))
    xc = ssd_conv_fwd(u_pad, lw["conv_taps"], lw["ssd_conv_b"], chunk=CONV_CHUNK, name="ssd_conv_fwd")
    y_scan, ssd_states = ssd_scan_fwd(xc, dt_raw, 0, lw["ssd_dt_bias"], lw["ssd_a_log"], lw["ssd_d"], blk=BLK // 2,
                                      name="ssd_scan_fwd")
    z_spec = (proj, d // 2, 7)
    (y_ssd,) = rowwise(f_ssd_gate, [y_scan, z_spec], [lw["ssd_norm_w"]], [(d // 2, BF16)], tr=TR, name="ssd_gate_norm")
    y3 = jnp.stack([y_ret, y_sb, y_ssd])
    merged, u3 = branch_merge(y3, lw["w_branch"], proj, 5, lw["b_gate"], tm=512, tn=1024, name="branch_merge")
    o = matmul(merged, lw["w_out"], name="out_proj")
    (x1,) = rowwise(f_rms_post, [x, o], [lw["norm_mix_post"]], [(d, F32)], tr=TR, name="mix_post_norm")
    (h2,) = rowwise(f_rms_pre, [x1], [lw["norm_ffn_pre"]], [(d, BF16)], tr=TR, name="ffn_pre_norm")
    w_gu = lw["ffn_w_gu"].reshape(N_DEV, 2, -1, d)
    if next_shards is None:
        gu, act, _ = ffn_up_swiglu(h2, w_gu, tm=FFN_TM, name="ffn_up_swiglu")
    else:
        rest = [n for n in BIG if n != "w_in"]
        gu, act, got = ffn_up_swiglu(h2, w_gu, tm=FFN_TM, hosted=[("gather", next_shards[n]) for n in rest],
                                     name="ffn_up_swiglu_gather")
        nxt.update(zip(rest, got))
    f = matmul(act, lw["ffn_w_down"], lead_a="k", lead_b="k", name="ffn_down")
    (x2,) = rowwise(f_rms_post, [x1, f], [lw["norm_ffn_post"]], [(d, F32)], tr=TR, name="ffn_post_norm")
    res = dict(x=x, h=h, proj=proj, dt_raw=dt_raw, ret_states=ret_states, sb_run=sb_run, sb_visited=sb_visited, xc=xc,
               u_pad=u_pad,
               y_scan=y_scan, ssd_states=ssd_states, y3=y3, u3=u3, merged=merged, o=o, x1=x1, h2=h2, gu=gu, act=act, f=f)
    return x2, res, (nxt if next_shards is not None else None)


def _layer_bwd(dx2, res, lw, cosf, sinf, pending):
    x, proj = res["x"], res["proj"]
    s, d = x.shape
    heads = d // 2 // HEAD
    n_dt = d // 2 // SSD_P
    got = {}
    df, dn_ffn_post = rowwise_vjp(f_rms_post, [res["x1"], res["f"]], [lw["norm_ffn_post"]], [dx2],
                                  [(1, BF16, None)], [0], tr=TR, name="ffn_post_norm_bwd")
    dw_down = matmul(res["act"], df, ta=True, lead_a="batch", out_dtype=BF16, name="ffn_down_dw")
    dgu = ffn_down_dx_swiglu(df, lw["ffn_w_down"], res["gu"], tm=FFN_TM, name="ffn_down_dx_swiglu")
    dgu = dgu.reshape(2 * N_DEV, s, -1)
    if pending is None:
        dh2 = matmul(dgu, lw["ffn_w_gu"], lead_a="k", lead_b="k", name="ffn_up_dx")
        dw_gu = matmul(dgu, res["h2"], ta=True, lead_a="batch", out_dtype=BF16, name="ffn_up_dw")
    else:
        dh2, (got["prev_w_branch"], got["prev_w_out"]) = matmul(
            dgu, lw["ffn_w_gu"], lead_a="k", lead_b="k",
            hosted=[("exchange", pending["w_branch"]), ("exchange", pending["w_out"])], name="ffn_up_dx_exchange")
        dw_gu, (got["prev_w_in"],) = matmul(dgu, res["h2"], ta=True, lead_a="batch", out_dtype=BF16,
                                            hosted=[("exchange", pending["w_in"])], name="ffn_up_dw_exchange")
    dw_gu = dw_gu.reshape(N_DEV, -1, d)
    dx1, dn_ffn_pre = rowwise_vjp(f_rms_pre, [res["x1"]], [lw["norm_ffn_pre"]], [dh2], [(0, F32, dx2)], [0],
                                  tr=TR, name="ffn_pre_norm_bwd")
    do, dn_mix_post = rowwise_vjp(f_rms_post, [x, res["o"]], [lw["norm_mix_post"]], [dx1], [(1, BF16, None)], [0],
                                  tr=TR, name="mix_post_norm_bwd")
    dmerged = matmul(do, lw["w_out"], tb=True, name="out_proj_dx")
    dw_out = matmul(res["merged"], do, ta=True, out_dtype=BF16, name="out_proj_dw")
    du3, d_gate_logits, *db_gate = gate_merge_bwd(res["u3"], proj, 5, lw["b_gate"], dmerged, tr=TR_WIDE,
                                                  name="gate_merge_bwd")
    dy3 = matmul(du3, lw["w_branch"], tb=True, lead_a="batch", lead_b="batch", name="branch_proj_dx")
    dw_branch = matmul(res["y3"], du3, ta=True, lead_a="batch", lead_b="batch", out_dtype=BF16, name="branch_proj_dw")
    ret_cols = tuple(i * heads for i in range(4))
    dq, dk, dv, dg, d_gn = retention_bwd(proj, ret_cols, lw["ret_gn_w"], cosf, sinf, res["ret_states"], (dy3, 0), heads,
                                         blk=BLK, name="retention_bwd")
    dsq, dsk, dsv = sb_attention_bwd(proj, (dy3, 1), res["sb_run"], res["sb_visited"], 4 * heads, 5 * heads, 6 * heads,
                                     heads, blk=SB_BLK, name="stickbreak_bwd")
    z_spec = (proj, d // 2, 7)
    dy_scan, dz, d_ssd_norm = rowwise_vjp(f_ssd_gate, [res["y_scan"], z_spec], [lw["ssd_norm_w"]], [(dy3, d // 2, 0, 2)],
                                          [(0, F32, None), (1, BF16, None)], [0], tr=TR, name="ssd_gate_norm_bwd")
    dxs, dbm, dcm, ddt, d_dtb, d_alog, d_dskip = ssd_scan_bwd(
        res["xc"], res["dt_raw"], 0, lw["ssd_dt_bias"], lw["ssd_a_log"], lw["ssd_d"], res["ssd_states"], dy_scan,
        blk=BLK // 2, name="ssd_scan_bwd")
    dxc_pad = jnp.pad(jnp.concatenate([dxs, dbm, dcm], axis=1), ((0, CONV_PAD), (0, 0)))
    du, d_taps, d_conv_b = ssd_conv_bwd(res["u_pad"], lw["conv_taps"], lw["ssd_conv_b"], dxc_pad, chunk=CONV_CHUNK,
                                        name="ssd_conv_bwd")
    dproj = jnp.concatenate([dq, dk, dv, dg, dsq, dsk, dsv, dz, du, d_gate_logits], axis=1)
    dh_dt = matmul(ddt, lw["w_dt"], name="in_proj_dt_dx")
    dh, (got["ffn_w_down"], got["ffn_w_gu"]) = matmul(
        dproj, lw["w_cat"], add=dh_dt, hosted=[("exchange", dw_down), ("exchange", dw_gu)],
        name="in_proj_dx_exchange")
    dw_cat = matmul(dproj, res["h"], ta=True, tn=2048, out_dtype=BF16, name="in_proj_dw")
    dw_dt = matmul(ddt, res["h"], ta=True, out_dtype=BF16, name="in_proj_dt_dw")
    dx, dn_mix_pre = rowwise_vjp(f_rms_pre, [x], [lw["norm_mix_pre"]], [dh], [(0, F32, dx1)], [0], tr=TR,
                                 name="mix_pre_norm_bwd")
    mine = dict(
        w_in=_split_dw_in(dw_cat, dw_dt, d, n_dt),
        w_branch=jnp.transpose(dw_branch.reshape(3, d // 2, N_DEV, -1), (2, 0, 1, 3)).reshape(N_DEV, 3 * d // 2, -1),
        w_out=dw_out.reshape(N_DEV, d // N_DEV, d),
    )
    small = dict(
        norm_mix_pre=dn_mix_pre[0], norm_mix_post=dn_mix_post[0], norm_ffn_pre=dn_ffn_pre[0],
        norm_ffn_post=dn_ffn_post[0], b_gate=jnp.concatenate([b[0] for b in db_gate]), ret_gn_w=d_gn[0],
        ssd_conv_w=d_taps, ssd_conv_b=d_conv_b[0], ssd_dt_bias=d_dtb[0, :n_dt], ssd_a_log=d_alog[0, :n_dt],
        ssd_d=d_dskip[0, :n_dt], ssd_norm_w=d_ssd_norm[0],
    )
    return dx, got, mine, small


def _adam_rows(cols):
    return max(8, (1 << 17) // cols // 8 * 8)


def kernel(x, positions, norm_mix_pre, norm_mix_post, norm_ffn_pre, norm_ffn_post, w_in, b_gate, ret_gn_w, ssd_conv_w, ssd_conv_b, ssd_dt_bias, ssd_a_log, ssd_d, ssd_norm_w, w_branch_ret, w_branch_sb, w_branch_ssd, w_out, ffn_w_gate, ffn_w_up, ffn_w_down, loss_target, m_norm_mix_pre, m_norm_mix_post, m_norm_ffn_pre, m_norm_ffn_post, m_w_in, m_b_gate, m_ret_gn_w, m_ssd_conv_w, m_ssd_conv_b, m_ssd_dt_bias, m_ssd_a_log, m_ssd_d, m_ssd_norm_w, m_w_branch_ret, m_w_branch_sb, m_w_branch_ssd, m_w_out, m_ffn_w_gate, m_ffn_w_up, m_ffn_w_down, v_norm_mix_pre, v_norm_mix_post, v_norm_ffn_pre, v_norm_ffn_post, v_w_in, v_b_gate, v_ret_gn_w, v_ssd_conv_w, v_ssd_conv_b, v_ssd_dt_bias, v_ssd_a_log, v_ssd_d, v_ssd_norm_w, v_w_branch_ret, v_w_branch_sb, v_w_branch_ssd, v_w_out, v_ffn_w_gate, v_ffn_w_up, v_ffn_w_down):
    depth = w_in.shape[0]
    s, d = x.shape[1], x.shape[2]
    axes = ("x", "y", "c")
    me = _flat_index(lax.axis_index("x"), lax.axis_index("y"), lax.axis_index("c"))

    def tr_(a):
        return jnp.swapaxes(a, 1, 2)

    transposed = ("w_in", "ffn_w_gate", "ffn_w_up")
    wmv = dict(w_in=tuple(map(tr_, (w_in, m_w_in, v_w_in))), w_branch_ret=(w_branch_ret, m_w_branch_ret, v_w_branch_ret),
               w_branch_sb=(w_branch_sb, m_w_branch_sb, v_w_branch_sb),
               w_branch_ssd=(w_branch_ssd, m_w_branch_ssd, v_w_branch_ssd), w_out=(w_out, m_w_out, v_w_out),
               ffn_w_gate=tuple(map(tr_, (ffn_w_gate, m_ffn_w_gate, v_ffn_w_gate))),
               ffn_w_up=tuple(map(tr_, (ffn_w_up, m_ffn_w_up, v_ffn_w_up))),
               ffn_w_down=(ffn_w_down, m_ffn_w_down, v_ffn_w_down))
    members = dict(w_in=["w_in"], w_branch=["w_branch_ret", "w_branch_sb", "w_branch_ssd"], w_out=["w_out"],
                   ffn_w_gu=["ffn_w_gate", "ffn_w_up"], ffn_w_down=["ffn_w_down"])
    big_w = {g: (wmv[ns[0]][0] if len(ns) == 1 else jnp.concatenate([wmv[n][0] for n in ns], axis=1))
             for g, ns in members.items()}

    taps_all = all_gather(ssd_conv_w.reshape(-1, LANE), name="gather_conv_w")
    taps_all = jnp.transpose(taps_all.reshape(N_DEV, depth, SSD_K, -1), (1, 2, 0, 3)).reshape(depth, SSD_K, -1)

    cosf, sinf = rope_tables(positions.reshape(s), tr=TR)
    small_w = dict(norm_mix_pre=norm_mix_pre, norm_mix_post=norm_mix_post, norm_ffn_pre=norm_ffn_pre,
                   norm_ffn_post=norm_ffn_post, b_gate=b_gate, ret_gn_w=ret_gn_w, ssd_conv_b=ssd_conv_b,
                   ssd_dt_bias=ssd_dt_bias, ssd_a_log=ssd_a_log, ssd_d=ssd_d, ssd_norm_w=ssd_norm_w, taps=taps_all)

    def layer_weights(sw, gathered):
        lw = dict(gathered)
        for n in ("norm_mix_pre", "norm_mix_post", "norm_ffn_pre", "norm_ffn_post", "ret_gn_w", "ssd_conv_b", "ssd_norm_w"):
            lw[n] = _row(sw[n])
        for n in ("ssd_dt_bias", "ssd_a_log", "ssd_d"):
            lw[n] = _pad_lanes(sw[n])
        lw["b_gate"] = [_row(sw["b_gate"][i * d:(i + 1) * d]) for i in range(3)]
        lw["conv_taps"] = [sw["taps"][k:k + 1] for k in range(SSD_K)]
        return lw

    def layer_slice(t, l):
        return {n: a[l] for n, a in t.items()}

    def bf16_shards(l):
        return {n: big_w[n][l].astype(BF16) for n in BIG}

    def arrange(g):
        w_cat, w_dt = _assemble_w_in(g["w_in"], d)
        return dict(
            w_cat=w_cat, w_dt=w_dt,
            w_branch=jnp.transpose(g["w_branch"].reshape(N_DEV, 3, d // 2, -1), (1, 2, 0, 3)).reshape(3, d // 2, d),
            w_out=g["w_out"].reshape(d, d),
            ffn_w_gu=g["ffn_w_gu"].reshape(2 * N_DEV, -1, d),
            ffn_w_down=g["ffn_w_down"],
        )

    big_out = {}

    def adam(l, group, parts):
        rows, cols = big_w[group].shape[1:]
        parts = parts.reshape(N_DEV, rows, cols)
        off = 0
        for n in members[group]:
            big_out[n] = adamw_into(*wmv[n], parts, l, off, big_out.get(n), tr=_adam_rows(cols), name="adamw_" + n)
            off += wmv[n][0].shape[1]

    xs, saved = x.reshape(s, d), []
    gathered = {n: all_gather(a, name="gather_" + n) for n, a in bf16_shards(0).items()}
    for l in range(depth):
        lw = layer_weights(layer_slice(small_w, l), arrange(gathered))
        xs, res, gathered = _layer_fwd(xs, lw, cosf, sinf, bf16_shards(l + 1) if l + 1 < depth else None)
        saved.append((res, lw))
    loss_tile, dy = loss_head(xs, loss_target.reshape(s, d), tr=TR, name="loss_head")
    loss = lax.psum(loss_tile[0, 0], axes)

    dx, pending, small_layers = dy, None, [None] * depth
    for l in reversed(range(depth)):
        res, lw = saved[l]
        dx, got, pending, small_layers[l] = _layer_bwd(dx, res, lw, cosf, sinf, pending)
        for n in ("ffn_w_gu", "ffn_w_down"):
            adam(l, n, got[n])
        if l + 1 < depth:
            for n in ("w_in", "w_branch", "w_out"):
                adam(l + 1, n, got["prev_" + n])
    for n in ("w_in", "w_branch", "w_out"):
        adam(0, n, exchange(pending[n], name="exchange_" + n))
    small_g = {n: jnp.stack([small_layers[l][n] for l in range(depth)]) for n in SMALL}

    n_dt = ssd_dt_bias.shape[-1]
    small_in = dict(norm_mix_pre=(norm_mix_pre, m_norm_mix_pre, v_norm_mix_pre), norm_mix_post=(norm_mix_post, m_norm_mix_post, v_norm_mix_post),
                    norm_ffn_pre=(norm_ffn_pre, m_norm_ffn_pre, v_norm_ffn_pre), norm_ffn_post=(norm_ffn_post, m_norm_ffn_post, v_norm_ffn_post),
                    b_gate=(b_gate, m_b_gate, v_b_gate), ret_gn_w=(ret_gn_w, m_ret_gn_w, v_ret_gn_w),
                    ssd_conv_b=(ssd_conv_b, m_ssd_conv_b, v_ssd_conv_b), ssd_dt_bias=(ssd_dt_bias, m_ssd_dt_bias, v_ssd_dt_bias),
                    ssd_a_log=(ssd_a_log, m_ssd_a_log, v_ssd_a_log), ssd_d=(ssd_d, m_ssd_d, v_ssd_d),
                    ssd_norm_w=(ssd_norm_w, m_ssd_norm_w, v_ssd_norm_w))
    rep = [n for n in SMALL if n != "ssd_conv_w"]

    def pack(arrs):
        flat = jnp.concatenate([a.reshape(-1) for a in arrs])
        rows = -(-flat.shape[0] // (16 * LANE)) * 16
        return jnp.pad(flat, (0, rows * LANE - flat.shape[0])).reshape(rows, LANE)

    conv_g = small_g["ssd_conv_w"]
    g_pack = pack([small_g[n] for n in rep] + [conv_g])
    g_all = all_gather(g_pack, name="gather_small_grads")
    zeros_conv = jnp.zeros_like(conv_g)
    w_pack, m_pack, v_pack = (pack([small_in[n][i] for n in rep] + [zeros_conv]) for i in range(3))
    sm = adamw_sum(w_pack, m_pack, v_pack, g_all, tr=TR, name="adamw_small")

    def unpack(p):
        flat, out, off = p.reshape(-1), {}, 0
        for n in rep:
            shp = small_in[n][0].shape
            size = math.prod(shp)
            out[n] = flat[off:off + size].reshape(shp)
            off += size
        out["conv_sum"] = flat[off:off + conv_g.size].reshape(conv_g.shape)
        return out

    sm = [unpack(p) for p in sm]
    ch = ssd_conv_w.shape[-1]
    conv_mine = lax.dynamic_slice_in_dim(sm[0]["conv_sum"], me * ch, ch, axis=2)
    conv_out = adamw_sum(ssd_conv_w.reshape(-1, LANE), m_ssd_conv_w.reshape(-1, LANE), v_ssd_conv_w.reshape(-1, LANE),
                         conv_mine.reshape(1, -1, LANE), tr=TR, name="adamw_conv_w")
    for i in range(4):
        sm[i]["ssd_conv_w"] = conv_out[i].reshape(ssd_conv_w.shape)

    def big_named(i):
        return {n: (tr_(out[i]) if n in transposed else out[i]) for n, out in big_out.items()}

    order = ["norm_mix_pre", "norm_mix_post", "norm_ffn_pre", "norm_ffn_post", "w_in", "b_gate", "ret_gn_w", "ssd_conv_w",
             "ssd_conv_b", "ssd_dt_bias", "ssd_a_log", "ssd_d", "ssd_norm_w", "w_branch_ret", "w_branch_sb", "w_branch_ssd",
             "w_out", "ffn_w_gate", "ffn_w_up", "ffn_w_down"]
    outs = [loss, dx.reshape(x.shape)]
    for i in range(4):
        named = {**sm[i], **big_named(i)}
        outs += [named[n] for n in order]
    return tuple(outs)
```

```python
import functools
import math

import jax
import jax.numpy as jnp
import numpy as np
from jax import lax
from jax.experimental import pallas as pl
from jax.experimental.pallas import tpu as pltpu

F32 = jnp.float32
BF16 = jnp.bfloat16

N_DEV = 8
HEAD = 128
SSD_P = 64
SSD_G = 4
SSD_N = 128
SSD_K = 4
CHUNK = 64
NORM_EPS = 1e-6
ROPE_BASE = 10000.0
LANE = 128
VMEM_LIMIT = 56 * 1024 * 1024

ADAM_LR, ADAM_B1, ADAM_B2, ADAM_EPS, ADAM_WD, ADAM_STEP = 0.001, 0.9, 0.999, 1e-08, 0.01, 10


def _cparams(sem):
    return pltpu.CompilerParams(dimension_semantics=sem, vmem_limit_bytes=VMEM_LIMIT)


def _pick(n, pref):
    if n <= pref:
        return n
    t = pref
    while t >= LANE:
        if n % t == 0:
            return t
        t -= LANE
    return n


def matmul(a, b, *, ta=False, tb=False, lead_a=None, lead_b=None, out_dtype=F32, add=None, hosted=(),
           tm=1024, tn=1024, tk=2048, name="mm"):
    la, lb = lead_a is not None, lead_b is not None
    a2, b2 = a.shape[1:] if la else a.shape, b.shape[1:] if lb else b.shape
    (kd_a, m) = a2 if ta else a2[::-1]
    (kd_b, n) = b2[::-1] if tb else b2
    assert kd_a == kd_b, (a.shape, b.shape)
    nlead = a.shape[0] if la else (b.shape[0] if lb else 1)
    batch = "batch" in (lead_a, lead_b)
    kblocks = nlead if "k" in (lead_a, lead_b) else 1
    if la and lb:
        assert lead_a == lead_b and a.shape[0] == b.shape[0]
    tm, tn, tk = _pick(m, tm), _pick(n, tn), _pick(kd_a, tk)
    kt = kd_a // tk
    nk = kt * kblocks
    grid = (m // tm, (nlead if batch else 1), n // tn, nk)

    def lead_idx(g, k):
        return g if batch else k // kt

    def a_map(i, g, j, k):
        idx = (k % kt, i) if ta else (i, k % kt)
        return ((lead_idx(g, k),) + idx) if la else idx

    def b_map(i, g, j, k):
        idx = (j, k % kt) if tb else (k % kt, j)
        return ((lead_idx(g, k),) + idx) if lb else idx

    a_blk = (tk, tm) if ta else (tm, tk)
    b_blk = (tn, tk) if tb else (tk, tn)
    a_spec = pl.BlockSpec(((None,) + a_blk) if la else a_blk, a_map)
    b_spec = pl.BlockSpec(((None,) + b_blk) if lb else b_blk, b_map)
    if batch:
        o_spec = pl.BlockSpec((None, tm, tn), lambda i, g, j, k: (g, i, j))
        o_shape = jax.ShapeDtypeStruct((nlead, m, n), out_dtype)
    else:
        o_spec = pl.BlockSpec((tm, tn), lambda i, g, j, k: (i, j))
        o_shape = jax.ShapeDtypeStruct((m, n), out_dtype)
    dims = (((0 if ta else 1,), (1 if tb else 0,)), ((), ()))

    n_add, n_host = int(add is not None), len(hosted)
    n_acc = int(nk > 1)

    def body(a_ref, b_ref, *rest):
        c_ref = rest[0] if n_add else None
        src_refs = rest[n_add:n_add + n_host]
        o_ref = rest[n_add + n_host]
        dst_refs = rest[n_add + n_host + 1:n_add + 2 * n_host + 1]
        acc_ref = rest[n_add + 2 * n_host + 1] if n_acc else None
        sems = rest[n_add + 2 * n_host + 1 + n_acc:]
        k = pl.program_id(3)
        ids = [pl.program_id(ax) for ax in range(4)]
        comms = [_COMM[kind](src_refs[c], dst_refs[c], *sems[3 * c:3 * c + 3]) for c, (kind, _) in enumerate(hosted)]

        if hosted:
            @pl.when(functools.reduce(jnp.logical_and, [i == 0 for i in ids]))
            def _():
                for start, _ in comms:
                    start()

        def product():
            return lax.dot_general(a_ref[...].astype(BF16), b_ref[...].astype(BF16), dims, preferred_element_type=F32)

        if n_acc:
            @pl.when(k == 0)
            def _():
                acc_ref[...] = jnp.zeros_like(acc_ref) if c_ref is None else c_ref[...].astype(F32)

            acc_ref[...] += product()

            @pl.when(k == nk - 1)
            def _():
                o_ref[...] = acc_ref[...].astype(o_ref.dtype)
        else:
            o_ref[...] = (product() if c_ref is None else product() + c_ref[...].astype(F32)).astype(o_ref.dtype)

        if hosted:
            @pl.when(functools.reduce(jnp.logical_and, [i == g - 1 for i, g in zip(ids, grid)]))
            def _():
                for _, finish in comms:
                    finish()

    extra = ([] if add is None else [add]) + [arr for _, arr in hosted]
    extra_specs = ([] if add is None else [o_spec]) + [_HBM] * n_host
    out_shapes = [o_shape] + [_comm_out_shape(kind, arr) for kind, arr in hosted]
    scratch = [pltpu.VMEM((tm, tn), F32)] * n_acc + _COMM_SEMS * n_host
    sem = ("arbitrary",) * 4 if hosted else ("parallel", "parallel", "parallel", "arbitrary")
    res = pl.pallas_call(
        body, out_shape=out_shapes, grid=grid, in_specs=[a_spec, b_spec] + extra_specs,
        out_specs=[o_spec] + [_HBM] * n_host, scratch_shapes=scratch, name=name, compiler_params=_cparams(sem),
    )(a, b, *extra)
    return (res[0], res[1:]) if hosted else res[0]


def _row_spec(r, tr):
    if not isinstance(r, tuple):
        return r, pl.BlockSpec((tr, r.shape[-1]), lambda i: (i, 0))
    if len(r) == 3:
        arr, w, cb = r
        return arr, pl.BlockSpec((tr, w), lambda i: (i, cb))
    arr, w, cb, ld = r
    return arr, pl.BlockSpec((None, tr, w), lambda i: (ld, i, cb))


def _full_spec(c):
    nd = c.ndim
    return pl.BlockSpec(c.shape, lambda i: (0,) * nd)


def rowwise(fn, rows, consts, outs, *, tr, name):
    arrs, specs = zip(*[_row_spec(r, tr) for r in rows])
    n_rows = arrs[0].shape[-2]
    nr, nc = len(rows), len(consts)

    def body(*refs):
        vals = [r[...] for r in refs[:nr + nc]]
        res = fn(*vals)
        for o_ref, r in zip(refs[nr + nc:], res):
            o_ref[...] = r.astype(o_ref.dtype)

    return pl.pallas_call(
        body, grid=(n_rows // tr,),
        in_specs=list(specs) + [_full_spec(c) for c in consts],
        out_specs=[pl.BlockSpec((tr, w), lambda i: (i, 0)) for w, _ in outs],
        out_shape=[jax.ShapeDtypeStruct((n_rows, w), dt) for w, dt in outs],
        name=name, compiler_params=_cparams(("parallel",)),
    )(*arrs, *consts)


def rowwise_vjp(fn, rows, consts, cts, row_grads, const_grads, *, tr, name):
    arrs, specs = zip(*[_row_spec(r, tr) for r in rows])
    n_rows = arrs[0].shape[-2]
    nr, nc = len(rows), len(consts)
    ct_present = [c for c in cts if c is not None]
    ct_arrs, ct_specs = zip(*[_row_spec(c, tr) for c in ct_present])
    add_present = [g[2] for g in row_grads if g[2] is not None]
    add_arrs, add_specs = zip(*[_row_spec(c, tr) for c in add_present]) if add_present else ((), ())
    n_ct, n_add = len(ct_present), len(add_present)
    widths = [s.block_shape[-1] for s in specs]

    def body(*refs):
        ins = refs[:nr + nc]
        ct_refs = refs[nr + nc:nr + nc + n_ct]
        add_refs = refs[nr + nc + n_ct:nr + nc + n_ct + n_add]
        out_refs = refs[nr + nc + n_ct + n_add:]
        vals = [r[...] for r in ins]
        res, f_vjp = jax.vjp(fn, *vals)
        it = iter(ct_refs)
        ct_vals = tuple(next(it)[...].astype(r.dtype) if c is not None else jnp.zeros_like(r)
                        for c, r in zip(cts, res))
        grads = f_vjp(ct_vals)
        ita = iter(add_refs)
        for o_ref, (idx, _, add) in zip(out_refs, row_grads):
            g = grads[idx].astype(F32)
            if add is not None:
                g = g + next(ita)[...].astype(F32)
            o_ref[...] = g.astype(o_ref.dtype)
        first = pl.program_id(0) == 0
        for o_ref, idx in zip(out_refs[len(row_grads):], const_grads):
            g = grads[nr + idx].astype(F32)

            @pl.when(first)
            def _():
                o_ref[...] = g

            @pl.when(jnp.logical_not(first))
            def _():
                o_ref[...] += g

    out_specs = [pl.BlockSpec((tr, widths[idx]), lambda i: (i, 0)) for idx, _, _ in row_grads]
    out_shape = [jax.ShapeDtypeStruct((n_rows, widths[idx]), dt) for idx, dt, _ in row_grads]
    out_specs += [_full_spec(consts[idx]) for idx in const_grads]
    out_shape += [jax.ShapeDtypeStruct(consts[idx].shape, F32) for idx in const_grads]
    return pl.pallas_call(
        body, grid=(n_rows // tr,),
        in_specs=list(specs) + [_full_spec(c) for c in consts] + list(ct_specs) + list(add_specs),
        out_specs=out_specs, out_shape=out_shape,
        name=name, compiler_params=_cparams(("arbitrary",)),
    )(*arrs, *consts, *ct_arrs, *add_arrs)


def f_rms(x, w):
    xf = x.astype(F32)
    return xf * lax.rsqrt(jnp.mean(xf * xf, axis=-1, keepdims=True) + NORM_EPS) * w


def f_rms_pre(x, w):
    return (f_rms(x, w),)


def f_rms_post(x, o, w):
    return (x + f_rms(o, w),)


def f_ssd_gate(y, z, w):
    return (f_rms(y * jax.nn.silu(z), w),)


@jax.custom_vjp
def _swap_halves(x):
    return pltpu.roll(x, HEAD // 2, 1)


_swap_halves.defvjp(lambda x: (_swap_halves(x), None), lambda _, g: (_swap_halves(g),))


def rope_tables(positions, *, tr):
    s = positions.shape[0]
    half = HEAD // 2
    inv = ROPE_BASE ** (-2.0 * jnp.arange(half, dtype=F32) / HEAD)
    inv = jnp.concatenate([inv, inv]).reshape(1, HEAD)
    sign = jnp.concatenate([-jnp.ones((half,), F32), jnp.ones((half,), F32)]).reshape(1, HEAD)

    def fn(pos, inv, sign):
        ang = pos.astype(F32) * inv
        return jnp.cos(ang), jnp.sin(ang) * sign

    return rowwise(fn, [positions.reshape(s, 1)], [inv, sign], [(HEAD, F32), (HEAD, F32)], tr=tr, name="rope_tables")


def _ret_consts(n_heads, blk):
    lg = np.log1p(-np.exp2(-5.0 - np.arange(n_heads)))[:, None, None]
    i = np.arange(blk)
    dist = np.abs(i[:, None] - i[None, :])[None]
    allowed = ((i[None, :] // CHUNK) <= (i[:, None] // CHUNK))[None]
    dm = np.where(allowed, np.exp(lg * dist), 0.0)
    qd = np.broadcast_to(np.exp(lg * (i[None, :, None] + 1.0)), (n_heads, blk, HEAD))
    kd = np.broadcast_to(np.exp(lg * (blk - 1.0 - i[None, :, None])), (n_heads, blk, HEAD))
    cd = np.broadcast_to(np.exp(lg * blk), (n_heads, 1, HEAD))
    return [jnp.asarray(a, F32) for a in (dm, qd, kd, cd)]


def _ret_block(q, k, v, g, gnw, state, cosf, sinf, dm, qd, kd, cd):
    qr = q * cosf + _swap_halves(q) * sinf
    kr = (k * cosf + _swap_halves(k) * sinf) * (HEAD ** -0.5)
    vb = v.astype(BF16)
    scores = _dot(qr.astype(BF16), kr.astype(BF16), _NT) * dm
    o = _dot(scores.astype(BF16), vb) + _dot((qr * qd).astype(BF16), state.astype(BF16))
    new_state = state * cd + _dot((kr * kd).astype(BF16), vb, _TN)
    mu = jnp.mean(o, axis=-1, keepdims=True)
    var = jnp.mean(jnp.square(o - mu), axis=-1, keepdims=True)
    y = (o - mu) * lax.rsqrt(var + NORM_EPS) * gnw * jax.nn.silu(g)
    return y, new_state


def _ret_specs(n_heads, blk, nb, cols, reverse):
    jm = (lambda j: nb - 1 - j) if reverse else (lambda j: j)
    col = lambda c0: pl.BlockSpec((blk, HEAD), lambda h, j: (jm(j), c0 + h))
    tab = pl.BlockSpec((blk, HEAD), lambda h, j: (jm(j), 0))
    specs = [col(c) for c in cols]
    specs += [pl.BlockSpec((1, HEAD), lambda h, j: (0, h)), tab, tab]
    specs += [pl.BlockSpec((None, blk, blk), lambda h, j: (h, 0, 0)),
              pl.BlockSpec((None, blk, HEAD), lambda h, j: (h, 0, 0)),
              pl.BlockSpec((None, blk, HEAD), lambda h, j: (h, 0, 0)),
              pl.BlockSpec((None, 1, HEAD), lambda h, j: (h, 0, 0))]
    state = pl.BlockSpec((None, None, HEAD, HEAD), lambda h, j: (h, jm(j), 0, 0))
    out_col = pl.BlockSpec((blk, HEAD), lambda h, j: (jm(j), h))
    return specs, state, out_col


def retention_fwd(proj, cols, gn_w, cosf, sinf, n_heads, *, blk, name):
    s = proj.shape[0]
    nb = s // blk
    consts = _ret_consts(n_heads, blk)
    specs, state_spec, out_col = _ret_specs(n_heads, blk, nb, cols, False)

    def body(q_ref, k_ref, v_ref, g_ref, gn_ref, cos_ref, sin_ref, dm_ref, qd_ref, kd_ref, cd_ref,
             y_ref, st_ref, state):
        @pl.when(pl.program_id(1) == 0)
        def _():
            state[...] = jnp.zeros_like(state)

        st = state[...]
        st_ref[...] = st
        y, new_state = _ret_block(q_ref[...], k_ref[...], v_ref[...], g_ref[...], gn_ref[...], st,
                                  cos_ref[...], sin_ref[...], dm_ref[...], qd_ref[...], kd_ref[...], cd_ref[...])
        y_ref[...] = y.astype(y_ref.dtype)
        state[...] = new_state

    return pl.pallas_call(
        body, grid=(n_heads, nb), in_specs=specs, out_specs=[out_col, state_spec],
        out_shape=[jax.ShapeDtypeStruct((s, n_heads * HEAD), BF16),
                   jax.ShapeDtypeStruct((n_heads, nb, HEAD, HEAD), F32)],
        scratch_shapes=[pltpu.VMEM((HEAD, HEAD), F32)],
        name=name, compiler_params=_cparams(("parallel", "arbitrary")),
    )(proj, proj, proj, proj, gn_w, cosf, sinf, *consts)


def retention_bwd(proj, cols, gn_w, cosf, sinf, states, dy, n_heads, *, blk, name):
    s = proj.shape[0]
    nb = s // blk
    consts = _ret_consts(n_heads, blk)
    specs, state_spec, out_col = _ret_specs(n_heads, blk, nb, cols, True)

    def body(q_ref, k_ref, v_ref, g_ref, gn_ref, cos_ref, sin_ref, dm_ref, qd_ref, kd_ref, cd_ref,
             st_ref, dy_ref, dq_ref, dk_ref, dv_ref, dg_ref, dgn_ref, dstate):
        first = pl.program_id(1) == 0

        @pl.when(first)
        def _():
            dstate[...] = jnp.zeros_like(dstate)
            dgn_ref[...] = jnp.zeros_like(dgn_ref)

        tabs = (cos_ref[...], sin_ref[...], dm_ref[...], qd_ref[...], kd_ref[...], cd_ref[...])
        fn = lambda q, k, v, g, gnw, st: _ret_block(q, k, v, g, gnw, st, *tabs)
        _, f_vjp = jax.vjp(fn, q_ref[...], k_ref[...], v_ref[...], g_ref[...], gn_ref[...], st_ref[...])
        dq, dk, dv, dg, dgn, dst = f_vjp((dy_ref[...].astype(F32), dstate[...]))
        dq_ref[...] = dq.astype(dq_ref.dtype)
        dk_ref[...] = dk.astype(dk_ref.dtype)
        dv_ref[...] = dv.astype(dv_ref.dtype)
        dg_ref[...] = dg.astype(dg_ref.dtype)
        dgn_ref[...] += dgn
        dstate[...] = dst

    o_shape = jax.ShapeDtypeStruct((s, n_heads * HEAD), BF16)
    dy_spec = out_col
    if isinstance(dy, tuple):
        dy, lead = dy
        dy_spec = pl.BlockSpec((None, blk, HEAD), lambda h, j: (lead, nb - 1 - j, h))
    return pl.pallas_call(
        body, grid=(n_heads, nb), in_specs=specs + [state_spec, dy_spec],
        out_specs=[out_col, out_col, out_col, out_col, pl.BlockSpec((1, HEAD), lambda h, j: (0, h))],
        out_shape=[o_shape, o_shape, o_shape, o_shape, jax.ShapeDtypeStruct((1, n_heads * HEAD), F32)],
        scratch_shapes=[pltpu.VMEM((HEAD, HEAD), F32)],
        name=name, compiler_params=_cparams(("parallel", "arbitrary")),
    )(proj, proj, proj, proj, gn_w, cosf, sinf, *consts, states, dy)


CONV_PAD = 8


def _conv_pre(u_ext, taps, bias, n_out):
    n = u_ext.shape[0]
    views = [pltpu.roll(u_ext, n - (k + CONV_PAD - SSD_K + 1), 0)[:n_out] for k in range(SSD_K)]
    pre = bias
    for k in range(SSD_K):
        pre = pre + taps[k] * views[k]
    return pre, views


def ssd_conv_fwd(u_pad, taps, bias, *, chunk, name):
    s, c = u_pad.shape[0] - 2 * CONV_PAD, u_pad.shape[1]

    def body(u_ref, t0, t1, t2, t3, b_ref, o_ref):
        taps_v = [t[...] for t in (t0, t1, t2, t3)]
        bias_v = b_ref[...]

        @pl.loop(0, s // chunk)
        def _(ci):
            r0 = pl.multiple_of(ci * chunk, chunk)
            pre, _ = _conv_pre(u_ref[pl.ds(r0, chunk + CONV_PAD), :], taps_v, bias_v, chunk)
            o_ref[pl.ds(r0, chunk), :] = pre * jax.nn.sigmoid(pre)

    row = pl.BlockSpec((1, LANE), lambda i: (0, i))
    return pl.pallas_call(
        body, grid=(c // LANE,),
        in_specs=[pl.BlockSpec((s + 2 * CONV_PAD, LANE), lambda i: (0, i))] + [row] * 5,
        out_specs=pl.BlockSpec((s, LANE), lambda i: (0, i)),
        out_shape=jax.ShapeDtypeStruct((s, c), F32),
        name=name, compiler_params=_cparams(("parallel",)),
    )(u_pad, *taps, bias)


def ssd_conv_bwd(u_pad, taps, bias, dxc_pad, *, chunk, name):
    s, c = u_pad.shape[0] - 2 * CONV_PAD, u_pad.shape[1]
    ext = chunk + CONV_PAD

    def body(u_ref, t0, t1, t2, t3, b_ref, d_ref, du_ref, dw_ref, db_ref):
        taps_v = [t[...] for t in (t0, t1, t2, t3)]
        bias_v = b_ref[...]
        dw_ref[...] = jnp.zeros_like(dw_ref)
        db_ref[...] = jnp.zeros_like(db_ref)

        @pl.loop(0, s // chunk)
        def _(ci):
            r0 = pl.multiple_of(ci * chunk, chunk)
            pre, views = _conv_pre(u_ref[pl.ds(r0, ext + CONV_PAD), :], taps_v, bias_v, ext)
            sig = jax.nn.sigmoid(pre)
            dpre = d_ref[pl.ds(r0, ext), :] * (sig * (1.0 + pre * (1.0 - sig)))
            du = taps_v[SSD_K - 1] * dpre[:chunk]
            for k in range(SSD_K - 1):
                du = du + taps_v[k] * pltpu.roll(dpre, ext - (SSD_K - 1 - k), 0)[:chunk]
            du_ref[pl.ds(r0, chunk), :] = du.astype(du_ref.dtype)
            own = dpre[:chunk]
            for k in range(SSD_K):
                dw_ref[k:k + 1, :] += jnp.sum(own * views[k][:chunk], axis=0, keepdims=True)
            db_ref[...] += jnp.sum(own, axis=0, keepdims=True)

    row = pl.BlockSpec((1, LANE), lambda i: (0, i))
    return pl.pallas_call(
        body, grid=(c // LANE,),
        in_specs=[pl.BlockSpec((s + 2 * CONV_PAD, LANE), lambda i: (0, i))] + [row] * 5
        + [pl.BlockSpec((s + CONV_PAD, LANE), lambda i: (0, i))],
        out_specs=[pl.BlockSpec((s, LANE), lambda i: (0, i)), pl.BlockSpec((SSD_K, LANE), lambda i: (0, i)), row],
        out_shape=[jax.ShapeDtypeStruct((s, c), BF16), jax.ShapeDtypeStruct((SSD_K, c), F32),
                   jax.ShapeDtypeStruct((1, c), F32)],
        name=name, compiler_params=_cparams(("parallel",)),
    )(u_pad, *taps, bias, dxc_pad)


def _tri_dot(tri, x, passes=3):
    out = None
    rem = x
    for _ in range(passes):
        piece = rem.astype(BF16)
        rem = rem - piece.astype(F32)
        d = _dot(tri, piece)
        out = d if out is None else out + d
    return out


def _tri(n, upper):
    rr = lax.broadcasted_iota(jnp.int32, (n, n), 0)
    cc = lax.broadcasted_iota(jnp.int32, (n, n), 1)
    return ((rr <= cc) if upper else (rr >= cc)).astype(BF16)


@jax.custom_vjp
def _cumsum_rows(a):
    return _tri_dot(_tri(a.shape[0], False), a)


_cumsum_rows.defvjp(lambda a: (_cumsum_rows(a), None), lambda _, g: (_tri_dot(_tri(g.shape[0], True), g),))


def _softplus(x):
    return jnp.maximum(x, 0.0) + jnp.log(1.0 + jnp.exp(-jnp.abs(x)))


def _ssd_block(x, bm, cm, dtraw, dtb, alog, dsk, state_t, group):
    blk, width = x.shape
    e_heads = width // SSD_P
    dt = _softplus(dtraw + dtb)
    acum = _cumsum_rows(dt * (-jnp.exp(alog)))
    acum_t = acum.T
    lane_h = lax.broadcasted_iota(jnp.int32, (1, LANE), 1)
    sub_h = lax.broadcasted_iota(jnp.int32, (LANE, 1), 0)
    lane_e = lax.broadcasted_iota(jnp.int32, (1, width), 1) // SSD_P
    causal = lax.broadcasted_iota(jnp.int32, (blk, blk), 0) >= lax.broadcasted_iota(jnp.int32, (blk, blk), 1)
    last_row = lax.broadcasted_iota(jnp.int32, (blk, 1), 0) == blk - 1
    cb = _dot(cm.astype(BF16), bm.astype(BF16), _NT)
    y = jnp.zeros((blk, width), F32)
    dt_l = jnp.zeros((blk, width), F32)
    ac_l = jnp.zeros((blk, width), F32)
    d_l = jnp.zeros((1, width), F32)
    for e in range(e_heads):
        head = group * e_heads + e
        pick = lane_h == head
        col = jnp.sum(jnp.where(pick, acum, 0.0), axis=1, keepdims=True)
        dt_e = jnp.sum(jnp.where(pick, dt, 0.0), axis=1, keepdims=True)
        d_e = jnp.sum(jnp.where(pick, dsk, 0.0), axis=1, keepdims=True)
        row = jnp.sum(jnp.where(sub_h == head, acum_t, 0.0), axis=0, keepdims=True)
        mine = lane_e == e
        decay = jnp.exp(jnp.where(causal, col - row, -jnp.inf))
        y = y + _dot((cb * decay).astype(BF16), jnp.where(mine, x * dt_e, 0.0).astype(BF16))
        dt_l = dt_l + jnp.where(mine, dt_e, 0.0)
        ac_l = ac_l + jnp.where(mine, col, 0.0)
        d_l = d_l + jnp.where(mine, d_e, 0.0)
    ac_last = jnp.sum(jnp.where(last_row, ac_l, 0.0), axis=0, keepdims=True)
    y = y + jnp.exp(ac_l) * _dot(cm.astype(BF16), state_t.astype(BF16)) + x * d_l
    inject = (x * dt_l * jnp.exp(ac_last - ac_l)).astype(BF16)
    new_state = state_t * jnp.exp(ac_last) + _dot(bm.astype(BF16), inject, _TN)
    return y, new_state


def ssd_scan_fwd(xc, proj, dt_col, dtb, alog, dsk, *, blk, name):
    s = xc.shape[0]
    nb = s // blk
    e_w = 4 * SSD_P
    b_off = SSD_G * e_w // SSD_N
    c_off = b_off + SSD_G
    row = pl.BlockSpec((1, LANE), lambda j, g: (0, 0))

    def body(x_ref, b_ref, c_ref, dt_ref, dtb_ref, al_ref, d_ref, y_ref, st_ref, state):
        j, g = pl.program_id(0), pl.program_id(1)

        @pl.when(j == 0)
        def _():
            state[g] = jnp.zeros((SSD_N, e_w), F32)

        st = state[g]
        st_ref[...] = st
        y, new_state = _ssd_block(x_ref[...], b_ref[...], c_ref[...], dt_ref[...], dtb_ref[...], al_ref[...],
                                  d_ref[...], st, g)
        y_ref[...] = y
        state[g] = new_state

    return pl.pallas_call(
        body, grid=(nb, SSD_G),
        in_specs=[pl.BlockSpec((blk, e_w), lambda j, g: (j, g)),
                  pl.BlockSpec((blk, SSD_N), lambda j, g: (j, b_off + g)),
                  pl.BlockSpec((blk, SSD_N), lambda j, g: (j, c_off + g)),
                  pl.BlockSpec((blk, LANE), lambda j, g: (j, dt_col)), row, row, row],
        out_specs=[pl.BlockSpec((blk, e_w), lambda j, g: (j, g)),
                   pl.BlockSpec((None, None, SSD_N, e_w), lambda j, g: (j, g, 0, 0))],
        out_shape=[jax.ShapeDtypeStruct((s, SSD_G * e_w), F32),
                   jax.ShapeDtypeStruct((nb, SSD_G, SSD_N, e_w), F32)],
        scratch_shapes=[pltpu.VMEM((SSD_G, SSD_N, e_w), F32)],
        name=name, compiler_params=_cparams(("arbitrary", "arbitrary")),
    )(xc, xc, xc, proj, dtb, alog, dsk)


def ssd_scan_bwd(xc, proj, dt_col, dtb, alog, dsk, states, dy, *, blk, name):
    s = xc.shape[0]
    nb = s // blk
    e_w = 4 * SSD_P
    b_off = SSD_G * e_w // SSD_N
    c_off = b_off + SSD_G
    row = pl.BlockSpec((1, LANE), lambda j, g: (0, 0))
    jm = lambda j: nb - 1 - j

    def body(x_ref, b_ref, c_ref, dt_ref, dtb_ref, al_ref, d_ref, st_ref, dy_ref,
             dx_ref, db_ref, dc_ref, ddt_ref, ddtb_ref, dal_ref, dd_ref, dstate):
        j, g = pl.program_id(0), pl.program_id(1)

        @pl.when(j == 0)
        def _():
            dstate[g] = jnp.zeros((SSD_N, e_w), F32)

        @pl.when(jnp.logical_and(j == 0, g == 0))
        def _():
            ddtb_ref[...] = jnp.zeros_like(ddtb_ref)
            dal_ref[...] = jnp.zeros_like(dal_ref)
            dd_ref[...] = jnp.zeros_like(dd_ref)

        @pl.when(g == 0)
        def _():
            ddt_ref[...] = jnp.zeros_like(ddt_ref)

        fn = functools.partial(_ssd_block, group=g)
        _, f_vjp = jax.vjp(fn, x_ref[...], b_ref[...], c_ref[...], dt_ref[...], dtb_ref[...], al_ref[...],
                           d_ref[...], st_ref[...])
        dx, db, dc, ddt, ddtb, dal, dd, dst = f_vjp((dy_ref[...], dstate[g]))
        dx_ref[...] = dx
        db_ref[...] = db
        dc_ref[...] = dc
        ddt_ref[...] += ddt
        ddtb_ref[...] += ddtb
        dal_ref[...] += dal
        dd_ref[...] += dd
        dstate[g] = dst

    return pl.pallas_call(
        body, grid=(nb, SSD_G),
        in_specs=[pl.BlockSpec((blk, e_w), lambda j, g: (jm(j), g)),
                  pl.BlockSpec((blk, SSD_N), lambda j, g: (jm(j), b_off + g)),
                  pl.BlockSpec((blk, SSD_N), lambda j, g: (jm(j), c_off + g)),
                  pl.BlockSpec((blk, LANE), lambda j, g: (jm(j), dt_col)), row, row, row,
                  pl.BlockSpec((None, None, SSD_N, e_w), lambda j, g: (jm(j), g, 0, 0)),
                  pl.BlockSpec((blk, e_w), lambda j, g: (jm(j), g))],
        out_specs=[pl.BlockSpec((blk, e_w), lambda j, g: (jm(j), g)),
                   pl.BlockSpec((blk, SSD_N), lambda j, g: (jm(j), g)),
                   pl.BlockSpec((blk, SSD_N), lambda j, g: (jm(j), g)),
                   pl.BlockSpec((blk, LANE), lambda j, g: (jm(j), 0)), row, row, row],
        out_shape=[jax.ShapeDtypeStruct((s, SSD_G * e_w), F32),
                   jax.ShapeDtypeStruct((s, SSD_G * SSD_N), F32),
                   jax.ShapeDtypeStruct((s, SSD_G * SSD_N), F32),
                   jax.ShapeDtypeStruct((s, LANE), F32)] + [jax.ShapeDtypeStruct((1, LANE), F32)] * 3,
        scratch_shapes=[pltpu.VMEM((SSD_G, SSD_N, e_w), F32)],
        name=name, compiler_params=_cparams(("arbitrary", "arbitrary")),
    )(xc, xc, xc, proj, dtb, alog, dsk, states, dy)


SB_DEAD = -104.0

_NT = (((1,), (1,)), ((), ()))
_TN = (((0,), (0,)), ((), ()))


def _dot(a, b, dims=(((1,), (0,)), ((), ()))):
    return lax.dot_general(a, b, dims, preferred_element_type=F32)


def _split_dot(x, tri, passes):
    out = None
    rem = x
    for _ in range(passes):
        piece = rem.astype(BF16)
        rem = rem - piece.astype(F32)
        d = _dot(piece, tri)
        out = d if out is None else out + d
    return out


def _sb_scores(q, k_ref, j, blk, row, scale):
    kb = k_ref[pl.ds(pl.multiple_of(j * blk, blk), blk), :].astype(BF16)
    z = _dot(q, kb, _NT) * scale
    col = j * blk + lax.broadcasted_iota(jnp.int32, (blk, blk), 1)
    mask = col < row
    sp = jnp.maximum(z, 0.0) + jnp.log(1.0 + jnp.exp(-jnp.abs(z)))
    lk = jnp.where(mask, -sp, 0.0)
    return kb, mask, lk, z - sp


def sb_attention_fwd(proj, q_col, k_col, v_col, n_heads, *, blk, name):
    s = proj.shape[0]
    nq = s // blk
    scale = HEAD ** -0.5

    def body(q_ref, k_ref, v_ref, o_ref, r_ref, n_ref):
        i = pl.program_id(1)
        q = q_ref[...].astype(BF16)
        row = i * blk + lax.broadcasted_iota(jnp.int32, (blk, blk), 0)
        rr = lax.broadcasted_iota(jnp.int32, (blk, blk), 0)
        cc = lax.broadcasted_iota(jnp.int32, (blk, blk), 1)
        tri_after = (rr > cc).astype(BF16)

        def alive(carry):
            jj, _, run = carry
            return jnp.logical_and(jj <= i, jnp.max(run) > SB_DEAD)

        def step(carry):
            jj, acc, run = carry
            j = i - jj
            _, mask, lk, ls = _sb_scores(q, k_ref, j, blk, row, scale)
            later = _split_dot(lk, tri_after, 2) + run
            w = jnp.where(mask, jnp.exp(ls + later), 0.0)
            vb = v_ref[pl.ds(pl.multiple_of(j * blk, blk), blk), :].astype(BF16)
            return jj + 1, acc + _dot(w.astype(BF16), vb), run + jnp.sum(lk, axis=1, keepdims=True)

        n, acc, run = lax.while_loop(
            alive, step, (jnp.int32(0), jnp.zeros((blk, HEAD), F32), jnp.zeros((blk, 1), F32)))
        o_ref[...] = acc.astype(o_ref.dtype)
        r_ref[...] = jnp.broadcast_to(run, (blk, HEAD))
        n_ref[pl.program_id(0), i] = n

    blk_spec = lambda c0: pl.BlockSpec((blk, HEAD), lambda h, i: (i, c0 + h))
    full_spec = lambda c0: pl.BlockSpec((s, HEAD), lambda h, i: (0, c0 + h))
    out_spec = pl.BlockSpec((blk, HEAD), lambda h, i: (i, h))
    return pl.pallas_call(
        body, grid=(n_heads, nq),
        in_specs=[blk_spec(q_col), full_spec(k_col), full_spec(v_col)],
        out_specs=[out_spec, out_spec, pl.BlockSpec(memory_space=pltpu.SMEM)],
        out_shape=[jax.ShapeDtypeStruct((s, n_heads * HEAD), BF16),
                   jax.ShapeDtypeStruct((s, n_heads * HEAD), F32),
                   jax.ShapeDtypeStruct((n_heads, nq), jnp.int32)],
        name=name, compiler_params=_cparams(("arbitrary", "arbitrary")),
    )(proj, proj, proj)


def sb_attention_bwd(proj, d_out, run_tot, visited, q_col, k_col, v_col, n_heads, *, blk, name):
    s = proj.shape[0]
    nq = s // blk
    scale = HEAD ** -0.5

    def body(n_ref, q_ref, k_ref, v_ref, do_ref, r_ref, dq_ref, dk_ref, dv_ref, dk_acc, dv_acc):
        i = pl.program_id(1)
        first = i + 1 - jnp.clip(n_ref[pl.program_id(0), i], 1, i + 1)

        @pl.when(i == 0)
        def _():
            dk_acc[...] = jnp.zeros_like(dk_acc)
            dv_acc[...] = jnp.zeros_like(dv_acc)

        q = q_ref[...].astype(BF16)
        do = do_ref[...].astype(BF16)
        rtot = r_ref[:, :1]
        row = i * blk + lax.broadcasted_iota(jnp.int32, (blk, blk), 0)
        rr = lax.broadcasted_iota(jnp.int32, (blk, blk), 0)
        cc = lax.broadcasted_iota(jnp.int32, (blk, blk), 1)
        tri_upto = (rr <= cc).astype(BF16)
        tri_before = (rr < cc).astype(BF16)

        def step(j, carry):
            dq, pre, gpre = carry
            kb, mask, lk, ls = _sb_scores(q, k_ref, j, blk, row, scale)
            rows = pl.ds(pl.multiple_of(j * blk, blk), blk)
            vb = v_ref[rows, :].astype(BF16)
            later = rtot - (pre + _split_dot(lk, tri_upto, 2))
            w = jnp.where(mask, jnp.exp(ls + later), 0.0)
            g = w * _dot(do, vb, _NT)
            g_before = _split_dot(g, tri_before, 1) + gpre
            sig = jnp.exp(ls)
            dz = (jnp.where(mask, g * (1.0 - sig) - sig * g_before, 0.0) * scale).astype(BF16)
            dk_acc[rows, :] += _dot(dz, q, _TN)
            dv_acc[rows, :] += _dot(w.astype(BF16), do, _TN)
            return (dq + _dot(dz, kb), pre + jnp.sum(lk, axis=1, keepdims=True),
                    gpre + jnp.sum(g, axis=1, keepdims=True))

        zero = jnp.zeros((blk, 1), F32)
        dq, _, _ = lax.fori_loop(first, i + 1, step, (jnp.zeros((blk, HEAD), F32), zero, zero))
        dq_ref[...] = dq.astype(dq_ref.dtype)

        @pl.when(i == nq - 1)
        def _():
            dk_ref[...] = dk_acc[...].astype(dk_ref.dtype)
            dv_ref[...] = dv_acc[...].astype(dv_ref.dtype)

    blk_spec = lambda c0: pl.BlockSpec((blk, HEAD), lambda h, i: (i, c0 + h))
    full_spec = lambda c0: pl.BlockSpec((s, HEAD), lambda h, i: (0, c0 + h))
    o_shape = jax.ShapeDtypeStruct((s, n_heads * HEAD), BF16)
    do_spec = blk_spec(0)
    if isinstance(d_out, tuple):
        d_out, lead = d_out
        do_spec = pl.BlockSpec((None, blk, HEAD), lambda h, i: (lead, i, h))
    return pl.pallas_call(
        body, grid=(n_heads, nq),
        in_specs=[pl.BlockSpec(memory_space=pltpu.SMEM), blk_spec(q_col), full_spec(k_col), full_spec(v_col),
                  do_spec, blk_spec(0)],
        out_specs=[blk_spec(0), full_spec(0), full_spec(0)],
        out_shape=[o_shape, o_shape, o_shape],
        scratch_shapes=[pltpu.VMEM((s, HEAD), F32), pltpu.VMEM((s, HEAD), F32)],
        name=name, compiler_params=_cparams(("parallel", "arbitrary")),
    )(visited, proj, proj, proj, d_out, run_tot)


def _hosting(hosted):
    arrs = [arr for _, arr in hosted]
    shapes = [_comm_out_shape(kind, arr) for kind, arr in hosted]

    def ops(src_refs, dst_refs, sems):
        return [_COMM[kind](src_refs[c], dst_refs[c], *sems[3 * c:3 * c + 3]) for c, (kind, _) in enumerate(hosted)]

    return arrs, [_HBM] * len(hosted), shapes, _COMM_SEMS * len(hosted), ops


def ffn_up_swiglu(h, w_gu, *, tm, hosted=(), name):
    s, d = h.shape
    tm = _pick(s, tm)
    nb, _, hb, _ = w_gu.shape
    n_host = len(hosted)
    h_arrs, h_specs, h_shapes, h_sems, h_ops = _hosting(hosted)
    grid = (s // tm, nb)

    def body(h_ref, w_ref, *rest):
        src_refs, (gu_ref, act_ref) = rest[:n_host], rest[n_host:n_host + 2]
        dst_refs, sems = rest[n_host + 2:2 * n_host + 2], rest[2 * n_host + 2:]
        comms = h_ops(src_refs, dst_refs, sems)
        i, g = pl.program_id(0), pl.program_id(1)

        if hosted:
            @pl.when(jnp.logical_and(i == 0, g == 0))
            def _():
                for start, _ in comms:
                    start()

        a = h_ref[...]
        gate = _dot(a, w_ref[0], _NT)
        up = _dot(a, w_ref[1], _NT)
        gu_ref[0] = gate
        gu_ref[1] = up
        act_ref[...] = (gate * jax.nn.sigmoid(gate) * up).astype(act_ref.dtype)

        if hosted:
            @pl.when(jnp.logical_and(i == grid[0] - 1, g == grid[1] - 1))
            def _():
                for _, finish in comms:
                    finish()

    res = pl.pallas_call(
        body, grid=grid,
        in_specs=[pl.BlockSpec((tm, d), lambda i, g: (i, 0)),
                  pl.BlockSpec((None, 2, hb, d), lambda i, g: (g, 0, 0, 0))] + h_specs,
        out_specs=[pl.BlockSpec((None, 2, tm, hb), lambda i, g: (g, 0, i, 0)),
                   pl.BlockSpec((None, tm, hb), lambda i, g: (g, i, 0))] + h_specs,
        out_shape=[jax.ShapeDtypeStruct((nb, 2, s, hb), F32), jax.ShapeDtypeStruct((nb, s, hb), BF16)] + h_shapes,
        scratch_shapes=h_sems, name=name,
        compiler_params=_cparams(("arbitrary", "arbitrary") if hosted else ("parallel", "parallel")),
    )(h, w_gu, *h_arrs)
    return res[0], res[1], res[2:]


def branch_merge(y3, w3, proj, gate_col, biases, *, tm, tn, name):
    nbr, s, k = y3.shape
    n = w3.shape[2]
    tm, tn = _pick(s, tm), _pick(n, tn)
    per = n // tn

    def body(y_ref, w_ref, g0, g1, g2, b0, b1, b2, m_ref, u_ref):
        merged = None
        for i, (g_ref, b_ref) in enumerate(((g0, b0), (g1, b1), (g2, b2))):
            u = _dot(y_ref[i], w_ref[i])
            u_ref[i] = u
            part = jax.nn.sigmoid(g_ref[...] + b_ref[...]) * u
            merged = part if merged is None else merged + part
        m_ref[...] = merged.astype(m_ref.dtype)

    gate_spec = lambda i: pl.BlockSpec((tm, tn), lambda r, j: (r, (gate_col + i) * per + j))
    bias_spec = pl.BlockSpec((1, tn), lambda r, j: (0, j))
    return pl.pallas_call(
        body, grid=(s // tm, per),
        in_specs=[pl.BlockSpec((nbr, tm, k), lambda r, j: (0, r, 0)), pl.BlockSpec((nbr, k, tn), lambda r, j: (0, 0, j)),
                  gate_spec(0), gate_spec(1), gate_spec(2), bias_spec, bias_spec, bias_spec],
        out_specs=[pl.BlockSpec((tm, tn), lambda r, j: (r, j)), pl.BlockSpec((nbr, tm, tn), lambda r, j: (0, r, j))],
        out_shape=[jax.ShapeDtypeStruct((s, n), BF16), jax.ShapeDtypeStruct((nbr, s, n), F32)],
        name=name, compiler_params=_cparams(("parallel", "parallel")),
    )(y3, w3, proj, proj, proj, *biases)


def gate_merge_bwd(u3, proj, gate_col, biases, dmerged, *, tr, name):
    nbr, s, n = u3.shape

    def body(u_ref, g0, g1, g2, b0, b1, b2, dm_ref, du_ref, dg_ref, db0, db1, db2):
        first = pl.program_id(0) == 0
        dm = dm_ref[...]
        for i, (g_ref, b_ref, db_ref) in enumerate(((g0, b0, db0), (g1, b1, db1), (g2, b2, db2))):
            sig = jax.nn.sigmoid(g_ref[...] + b_ref[...])
            du_ref[i] = (dm * sig).astype(du_ref.dtype)
            dlogit = dm * u_ref[i] * (sig * (1.0 - sig))
            dg_ref[:, i * n:(i + 1) * n] = dlogit.astype(dg_ref.dtype)
            part = jnp.sum(dlogit, axis=0, keepdims=True)

            @pl.when(first)
            def _():
                db_ref[...] = part

            @pl.when(jnp.logical_not(first))
            def _():
                db_ref[...] += part

    gate_spec = lambda i: pl.BlockSpec((tr, n), lambda r: (r, gate_col + i))
    bias_spec = pl.BlockSpec((1, n), lambda r: (0, 0))
    u_spec = pl.BlockSpec((nbr, tr, n), lambda r: (0, r, 0))
    return pl.pallas_call(
        body, grid=(s // tr,),
        in_specs=[u_spec, gate_spec(0), gate_spec(1), gate_spec(2), bias_spec, bias_spec, bias_spec,
                  pl.BlockSpec((tr, n), lambda r: (r, 0))],
        out_specs=[u_spec, pl.BlockSpec((tr, nbr * n), lambda r: (r, 0)), bias_spec, bias_spec, bias_spec],
        out_shape=[jax.ShapeDtypeStruct(u3.shape, BF16), jax.ShapeDtypeStruct((s, nbr * n), BF16)]
        + [jax.ShapeDtypeStruct((1, n), F32)] * 3,
        name=name, compiler_params=_cparams(("arbitrary",)),
    )(u3, proj, proj, proj, *biases, dmerged)


def ffn_down_dx_swiglu(df, w_down, gu, *, tm, name):
    s, d = df.shape
    tm = _pick(s, tm)
    nb, hb, _ = w_down.shape

    def body(df_ref, w_ref, gu_ref, dgu_ref):
        dact = _dot(df_ref[...], w_ref[...], _NT)
        gate, up = gu_ref[0], gu_ref[1]
        sig = jax.nn.sigmoid(gate)
        dgu_ref[0] = (dact * up * (sig * (1.0 + gate * (1.0 - sig)))).astype(dgu_ref.dtype)
        dgu_ref[1] = (dact * gate * sig).astype(dgu_ref.dtype)

    blk = pl.BlockSpec((None, 2, tm, hb), lambda i, g: (g, 0, i, 0))
    return pl.pallas_call(
        body, grid=(s // tm, nb),
        in_specs=[pl.BlockSpec((tm, d), lambda i, g: (i, 0)), pl.BlockSpec((None, hb, d), lambda i, g: (g, 0, 0)), blk],
        out_specs=blk, out_shape=jax.ShapeDtypeStruct(gu.shape, BF16),
        name=name, compiler_params=_cparams(("parallel", "parallel")),
    )(df, w_down, gu)


def loss_head(y, target, *, tr, name):
    s, d = y.shape

    def body(y_ref, t_ref, l_ref, dy_ref):
        err = y_ref[...] - t_ref[...]
        dy_ref[...] = err * (1.0 / d)
        part = 0.5 * jnp.sum(jnp.mean(err * err, axis=-1, keepdims=True), axis=0, keepdims=True)

        @pl.when(pl.program_id(0) == 0)
        def _():
            l_ref[...] = jnp.zeros_like(l_ref)

        l_ref[...] += jnp.broadcast_to(part, l_ref.shape)

    row = pl.BlockSpec((tr, d), lambda i: (i, 0))
    return pl.pallas_call(
        body, grid=(s // tr,), in_specs=[row, row],
        out_specs=[pl.BlockSpec((8, LANE), lambda i: (0, 0)), row],
        out_shape=[jax.ShapeDtypeStruct((8, LANE), F32), jax.ShapeDtypeStruct((s, d), F32)],
        name=name, compiler_params=_cparams(("arbitrary",)),
    )(y, target)


def _adamw_math(w, g, m, v):
    m = ADAM_B1 * m + (1.0 - ADAM_B1) * g
    v = ADAM_B2 * v + (1.0 - ADAM_B2) * jnp.square(g)
    m_hat = m / (1.0 - ADAM_B1 ** ADAM_STEP)
    v_hat = v / (1.0 - ADAM_B2 ** ADAM_STEP)
    delta = -ADAM_LR * (m_hat / (jnp.sqrt(v_hat) + ADAM_EPS) + ADAM_WD * w)
    return delta, m, v


def adamw_sum(w, m, v, parts, *, tr, name):
    n, r, c = parts.shape
    tr = _pick_rows(r, tr)

    def body(w_ref, m_ref, v_ref, p_ref, g_ref, d_ref, nm_ref, nv_ref):
        g = p_ref[0].astype(F32)
        for i in range(1, n):
            g = g + p_ref[i].astype(F32)
        delta, nm, nv = _adamw_math(w_ref[...], g, m_ref[...], v_ref[...])
        g_ref[...] = g
        d_ref[...] = delta
        nm_ref[...] = nm
        nv_ref[...] = nv

    row = pl.BlockSpec((tr, c), lambda i: (i, 0))
    shape = jax.ShapeDtypeStruct((r, c), F32)
    return pl.pallas_call(
        body, grid=(r // tr,),
        in_specs=[row, row, row, pl.BlockSpec((n, tr, c), lambda i: (0, i, 0))],
        out_specs=[row] * 4, out_shape=[shape] * 4,
        name=name, compiler_params=_cparams(("parallel",)),
    )(w, m, v, parts)


def adamw_into(w_all, m_all, v_all, parts, layer, row_off, prev, *, tr, name):
    depth, r, c = w_all.shape
    n = parts.shape[0]
    if r % 16 == 0:
        tr = _pick_rows(r, tr)
        assert row_off % tr == 0
        off = row_off // tr
        grid = (r // tr,)
        lay = pl.BlockSpec((None, tr, c), lambda i: (layer, i, 0))
        p_spec = pl.BlockSpec((n, tr, c), lambda i: (0, off + i, 0))
    else:
        assert row_off == 0 and parts.shape[1] == r
        grid = (c // LANE,)
        lay = pl.BlockSpec((None, r, LANE), lambda j: (layer, 0, j))
        p_spec = pl.BlockSpec((n, r, LANE), lambda j: (0, 0, j))
    if prev is None:
        prev = [lax.empty(w_all.shape, F32) for _ in range(4)]

    def body(w_ref, m_ref, v_ref, p_ref, _g, _d, _m, _v, g_ref, d_ref, nm_ref, nv_ref):
        g = p_ref[0].astype(F32)
        for i in range(1, n):
            g = g + p_ref[i].astype(F32)
        delta, nm, nv = _adamw_math(w_ref[...], g, m_ref[...], v_ref[...])
        g_ref[...] = g
        d_ref[...] = delta
        nm_ref[...] = nm
        nv_ref[...] = nv

    untouched = pl.BlockSpec(memory_space=pl.ANY)
    return pl.pallas_call(
        body, grid=grid,
        in_specs=[lay, lay, lay, p_spec] + [untouched] * 4,
        out_specs=[lay] * 4, out_shape=[jax.ShapeDtypeStruct(w_all.shape, F32)] * 4,
        input_output_aliases={4: 0, 5: 1, 6: 2, 7: 3},
        name=name, compiler_params=_cparams(("parallel",)),
    )(w_all, m_all, v_all, parts, *prev)


def _pick_rows(r, pref):
    t = min(pref, r)
    while r % t or (t % 16 and t != r):
        t -= 1
    return t


_HBM = pl.BlockSpec(memory_space=pltpu.HBM)
_MESH = pl.DeviceIdType.MESH


def _flat_index(px, py, pc):
    return 4 * px + 2 * py + pc


def _gather_ops(x_ref, out_ref, send_sems, recv_sems, local_sem):
    x, y, c = lax.axis_index("x"), lax.axis_index("y"), lax.axis_index("c")
    me, sibling = (x, y, c), (x, y, 1 - c)
    chips = [(1 - x, y), (x, 1 - y), (1 - x, 1 - y)]

    def slot(p):
        return out_ref.at[_flat_index(*p)]

    def copy(k, block, to, src=None):
        return pltpu.make_async_remote_copy(
            src_ref=slot(block) if src is None else src, dst_ref=slot(block),
            send_sem=send_sems.at[k], recv_sem=recv_sems.at[k], device_id=to, device_id_type=_MESH)

    mine = pltpu.make_async_copy(x_ref, slot(me), local_sem)
    first = [copy(0, me, sibling, src=x_ref)]
    first += [copy(1 + j, me, (*chip, c), src=x_ref) for j, chip in enumerate(chips)]
    passed = [copy(4 + j, (*chip, c), sibling) for j, chip in enumerate(chips)]

    def start():
        mine.start()
        for cp in first:
            cp.start()

    def finish():
        for j, chip in enumerate(chips):
            copy(1 + j, (*chip, c), me).wait_recv()
            passed[j].start()
        copy(0, sibling, me).wait_recv()
        for j, chip in enumerate(chips):
            copy(4 + j, (*chip, 1 - c), me).wait_recv()
        for cp in first + passed:
            cp.wait_send()
        mine.wait()

    return start, finish


def _exchange_ops(p_ref, out_ref, send_sems, recv_sems, local_sem):
    x, y, c = lax.axis_index("x"), lax.axis_index("y"), lax.axis_index("c")
    me = _flat_index(x, y, c)
    peers = [(1 - x if k & 4 else x, 1 - y if k & 2 else y, 1 - c if k & 1 else c) for k in range(1, N_DEV)]

    def copy(k, peer, dst_slot):
        return pltpu.make_async_remote_copy(
            src_ref=p_ref.at[_flat_index(*peer)], dst_ref=out_ref.at[dst_slot],
            send_sem=send_sems.at[k], recv_sem=recv_sems.at[k], device_id=peer, device_id_type=_MESH)

    mine = pltpu.make_async_copy(p_ref.at[me], out_ref.at[me], local_sem)
    sends = [copy(k, peer, me) for k, peer in enumerate(peers)]

    def start():
        mine.start()
        for cp in sends:
            cp.start()

    def finish():
        for k, peer in enumerate(peers):
            copy(k, peer, _flat_index(*peer)).wait_recv()
        for cp in sends:
            cp.wait_send()
        mine.wait()

    return start, finish


_COMM = {"gather": _gather_ops, "exchange": _exchange_ops}
_COMM_SEMS = [pltpu.SemaphoreType.DMA((N_DEV - 1,)), pltpu.SemaphoreType.DMA((N_DEV - 1,)), pltpu.SemaphoreType.DMA]


def _comm_out_shape(kind, arr):
    return jax.ShapeDtypeStruct(((N_DEV,) + arr.shape) if kind == "gather" else arr.shape, arr.dtype)


def _comm_call(kind, arr, name):
    def body(src_ref, dst_ref, send_sems, recv_sems, local_sem):
        start, finish = _COMM[kind](src_ref, dst_ref, send_sems, recv_sems, local_sem)
        start()
        finish()

    return pl.pallas_call(body, out_shape=_comm_out_shape(kind, arr), in_specs=[_HBM], out_specs=_HBM,
                          scratch_shapes=_COMM_SEMS, name=name)(arr)


def all_gather(x, *, name):
    return _comm_call("gather", x, name)


def exchange(parts, *, name):
    return _comm_call("exchange", parts, name)


TR = 256
TR_WIDE = 128
BLK = 256
SB_BLK = 256
FFN_TM = 1024
CONV_CHUNK = 512
DT_PAD = LANE

BIG = ("w_in", "w_branch", "w_out", "ffn_w_gu", "ffn_w_down")
SMALL = ("norm_mix_pre", "norm_mix_post", "norm_ffn_pre", "norm_ffn_post", "b_gate", "ret_gn_w", "ssd_conv_w",
         "ssd_conv_b", "ssd_dt_bias", "ssd_a_log", "ssd_d", "ssd_norm_w")


def _row(v):
    return v.reshape(1, -1)


def _pad_lanes(v):
    return jnp.pad(v.reshape(1, -1), ((0, 0), (0, LANE - v.shape[-1])))


def _rows_of(segments, lo, hi):
    out, start = [], 0
    for seg in segments:
        a, b = max(lo, start), min(hi, start + seg.shape[0])
        if a < b:
            out.append(seg[a - start:b - start])
        start += seg.shape[0]
    return out


def _assemble_w_in(g, d):
    full = g.reshape(-1, d)
    n_main, n_dt = 5 * d, full.shape[0] - 8 * d
    main, dt, gates = full[:n_main], full[n_main:n_main + n_dt], full[n_main + n_dt:]
    return jnp.concatenate([main, gates], axis=0), jnp.pad(dt, ((0, DT_PAD - n_dt), (0, 0)))


def _split_dw_in(dw, dw_dt, d, n_dt):
    n_main = 5 * d
    segments = [dw[:n_main], dw_dt[:n_dt], dw[n_main:]]
    rows = (8 * d + n_dt) // N_DEV
    return jnp.stack([jnp.concatenate(_rows_of(segments, j * rows, (j + 1) * rows), axis=0) for j in range(N_DEV)])


def _layer_fwd(x, lw, cosf, sinf, next_shards):
    s, d = x.shape
    heads = d // 2 // HEAD
    nxt = {}
    (h,) = rowwise(f_rms_pre, [x], [lw["norm_mix_pre"]], [(d, BF16)], tr=TR, name="mix_pre_norm")
    if next_shards is None:
        proj = matmul(h, lw["w_cat"], tb=True, tn=2048, name="in_proj")
    else:
        proj, (nxt["w_in"],) = matmul(h, lw["w_cat"], tb=True, tn=2048, hosted=[("gather", next_shards["w_in"])],
                                      name="in_proj_gather")
    dt_raw = matmul(h, lw["w_dt"], tb=True, name="in_proj_dt")
    ret_cols = tuple(i * heads for i in range(4))
    y_ret, ret_states = retention_fwd(proj, ret_cols, lw["ret_gn_w"], cosf, sinf, heads, blk=BLK, name="retention_fwd")
    y_sb, sb_run, sb_visited = sb_attention_fwd(proj, 4 * heads, 5 * heads, 6 * heads, heads, blk=SB_BLK,
                                                name="stickbreak_fwd")
    u_pad = jnp.pad(proj[:, 4 * d:5 * d], ((CONV_PAD, CONV_PAD), (0, 0)))
    xc = ssd_conv_fwd(u_pad, lw["conv_taps"], lw["ssd_conv_b"], chunk=CONV_CHUNK, name="ssd_conv_fwd")
    y_scan, ssd_states = ssd_scan_fwd(xc, dt_raw, 0, lw["ssd_dt_bias"], lw["ssd_a_log"], lw["ssd_d"], blk=BLK,
                                      name="ssd_scan_fwd")
    z_spec = (proj, d // 2, 7)
    (y_ssd,) = rowwise(f_ssd_gate, [y_scan, z_spec], [lw["ssd_norm_w"]], [(d // 2, BF16)], tr=TR, name="ssd_gate_norm")
    y3 = jnp.stack([y_ret, y_sb, y_ssd])
    merged, u3 = branch_merge(y3, lw["w_branch"], proj, 5, lw["b_gate"], tm=512, tn=1024, name="branch_merge")
    o = matmul(merged, lw["w_out"], name="out_proj")
    (x1,) = rowwise(f_rms_post, [x, o], [lw["norm_mix_post"]], [(d, F32)], tr=TR, name="mix_post_norm")
    (h2,) = rowwise(f_rms_pre, [x1], [lw["norm_ffn_pre"]], [(d, BF16)], tr=TR, name="ffn_pre_norm")
    w_gu = lw["ffn_w_gu"].reshape(N_DEV, 2, -1, d)
    if next_shards is None:
        gu, act, _ = ffn_up_swiglu(h2, w_gu, tm=FFN_TM, name="ffn_up_swiglu")
    else:
        rest = [n for n in BIG if n != "w_in"]
        gu, act, got = ffn_up_swiglu(h2, w_gu, tm=FFN_TM, hosted=[("gather", next_shards[n]) for n in rest],
                                     name="ffn_up_swiglu_gather")
        nxt.update(zip(rest, got))
    f = matmul(act, lw["ffn_w_down"], lead_a="k", lead_b="k", name="ffn_down")
    (x2,) = rowwise(f_rms_post, [x1, f], [lw["norm_ffn_post"]], [(d, F32)], tr=TR, name="ffn_post_norm")
    res = dict(x=x, h=h, proj=proj, dt_raw=dt_raw, ret_states=ret_states, sb_run=sb_run, sb_visited=sb_visited, xc=xc,
               u_pad=u_pad,
               y_scan=y_scan, ssd_states=ssd_states, y3=y3, u3=u3, merged=merged, o=o, x1=x1, h2=h2, gu=gu, act=act, f=f)
    return x2, res, (nxt if next_shards is not None else None)


def _layer_bwd(dx2, res, lw, cosf, sinf, pending, last=False):
    x, proj = res["x"], res["proj"]
    s, d = x.shape
    heads = d // 2 // HEAD
    n_dt = d // 2 // SSD_P
    got = {}
    df, dn_ffn_post = rowwise_vjp(f_rms_post, [res["x1"], res["f"]], [lw["norm_ffn_post"]], [dx2],
                                  [(1, BF16, None)], [0], tr=TR, name="ffn_post_norm_bwd")
    dw_down = matmul(res["act"], df, ta=True, lead_a="batch", out_dtype=BF16, name="ffn_down_dw")
    dgu = ffn_down_dx_swiglu(df, lw["ffn_w_down"], res["gu"], tm=FFN_TM, name="ffn_down_dx_swiglu")
    dgu = dgu.reshape(2 * N_DEV, s, -1)
    if pending is None:
        dh2 = matmul(dgu, lw["ffn_w_gu"], lead_a="k", lead_b="k", name="ffn_up_dx")
        dw_gu = matmul(dgu, res["h2"], ta=True, lead_a="batch", out_dtype=BF16, name="ffn_up_dw")
    else:
        dh2, (got["prev_w_branch"], got["prev_w_out"]) = matmul(
            dgu, lw["ffn_w_gu"], lead_a="k", lead_b="k",
            hosted=[("exchange", pending["w_branch"]), ("exchange", pending["w_out"])], name="ffn_up_dx_exchange")
        dw_gu, (got["prev_w_in"],) = matmul(dgu, res["h2"], ta=True, lead_a="batch", out_dtype=BF16,
                                            hosted=[("exchange", pending["w_in"])], name="ffn_up_dw_exchange")
    dw_gu = dw_gu.reshape(N_DEV, -1, d)
    dx1, dn_ffn_pre = rowwise_vjp(f_rms_pre, [res["x1"]], [lw["norm_ffn_pre"]], [dh2], [(0, F32, dx2)], [0],
                                  tr=TR, name="ffn_pre_norm_bwd")
    do, dn_mix_post = rowwise_vjp(f_rms_post, [x, res["o"]], [lw["norm_mix_post"]], [dx1], [(1, BF16, None)], [0],
                                  tr=TR, name="mix_post_norm_bwd")
    dmerged = matmul(do, lw["w_out"], tb=True, name="out_proj_dx")
    dw_out = matmul(res["merged"], do, ta=True, out_dtype=BF16, name="out_proj_dw")
    du3, d_gate_logits, *db_gate = gate_merge_bwd(res["u3"], proj, 5, lw["b_gate"], dmerged, tr=TR_WIDE,
                                                  name="gate_merge_bwd")
    dy3 = matmul(du3, lw["w_branch"], tb=True, lead_a="batch", lead_b="batch", name="branch_proj_dx")
    dw_branch = matmul(res["y3"], du3, ta=True, lead_a="batch", lead_b="batch", out_dtype=BF16, name="branch_proj_dw")
    ret_cols = tuple(i * heads for i in range(4))
    dq, dk, dv, dg, d_gn = retention_bwd(proj, ret_cols, lw["ret_gn_w"], cosf, sinf, res["ret_states"], (dy3, 0), heads,
                                         blk=BLK, name="retention_bwd")
    dsq, dsk, dsv = sb_attention_bwd(proj, (dy3, 1), res["sb_run"], res["sb_visited"], 4 * heads, 5 * heads, 6 * heads,
                                     heads, blk=SB_BLK, name="stickbreak_bwd")
    z_spec = (proj, d // 2, 7)
    dy_scan, dz, d_ssd_norm = rowwise_vjp(f_ssd_gate, [res["y_scan"], z_spec], [lw["ssd_norm_w"]], [(dy3, d // 2, 0, 2)],
                                          [(0, F32, None), (1, BF16, None)], [0], tr=TR, name="ssd_gate_norm_bwd")
    dxs, dbm, dcm, ddt, d_dtb, d_alog, d_dskip = ssd_scan_bwd(
        res["xc"], res["dt_raw"], 0, lw["ssd_dt_bias"], lw["ssd_a_log"], lw["ssd_d"], res["ssd_states"], dy_scan,
        blk=BLK, name="ssd_scan_bwd")
    dxc_pad = jnp.pad(jnp.concatenate([dxs, dbm, dcm], axis=1), ((0, CONV_PAD), (0, 0)))
    du, d_taps, d_conv_b = ssd_conv_bwd(res["u_pad"], lw["conv_taps"], lw["ssd_conv_b"], dxc_pad, chunk=CONV_CHUNK,
                                        name="ssd_conv_bwd")
    dproj = jnp.concatenate([dq, dk, dv, dg, dsq, dsk, dsv, dz, du, d_gate_logits], axis=1)
    dh_dt = matmul(ddt, lw["w_dt"], name="in_proj_dt_dx")
    dh, (got["ffn_w_down"], got["ffn_w_gu"]) = matmul(
        dproj, lw["w_cat"], add=dh_dt, hosted=[("exchange", dw_down), ("exchange", dw_gu)],
        name="in_proj_dx_exchange")
    mine = dict(
        w_branch=jnp.transpose(dw_branch.reshape(3, d // 2, N_DEV, -1), (2, 0, 1, 3)).reshape(N_DEV, 3 * d // 2, -1),
        w_out=dw_out.reshape(N_DEV, d // N_DEV, d),
    )
    if last:
        dw_cat, (got["w_branch"], got["w_out"]) = matmul(
            dproj, res["h"], ta=True, tn=2048, out_dtype=BF16,
            hosted=[("exchange", mine.pop("w_branch")), ("exchange", mine.pop("w_out"))], name="in_proj_dw_exchange")
    else:
        dw_cat = matmul(dproj, res["h"], ta=True, tn=2048, out_dtype=BF16, name="in_proj_dw")
    dw_dt = matmul(ddt, res["h"], ta=True, out_dtype=BF16, name="in_proj_dt_dw")
    dx, dn_mix_pre = rowwise_vjp(f_rms_pre, [x], [lw["norm_mix_pre"]], [dh], [(0, F32, dx1)], [0], tr=TR,
                                 name="mix_pre_norm_bwd")
    mine["w_in"] = _split_dw_in(dw_cat, dw_dt, d, n_dt)
    small = dict(
        norm_mix_pre=dn_mix_pre[0], norm_mix_post=dn_mix_post[0], norm_ffn_pre=dn_ffn_pre[0],
        norm_ffn_post=dn_ffn_post[0], b_gate=jnp.concatenate([b[0] for b in db_gate]), ret_gn_w=d_gn[0],
        ssd_conv_w=d_taps, ssd_conv_b=d_conv_b[0], ssd_dt_bias=d_dtb[0, :n_dt], ssd_a_log=d_alog[0, :n_dt],
        ssd_d=d_dskip[0, :n_dt], ssd_norm_w=d_ssd_norm[0],
    )
    return dx, got, mine, small


def _adam_rows(cols):
    return max(8, (1 << 17) // cols // 8 * 8)


def kernel(x, positions, norm_mix_pre, norm_mix_post, norm_ffn_pre, norm_ffn_post, w_in, b_gate, ret_gn_w, ssd_conv_w, ssd_conv_b, ssd_dt_bias, ssd_a_log, ssd_d, ssd_norm_w, w_branch_ret, w_branch_sb, w_branch_ssd, w_out, ffn_w_gate, ffn_w_up, ffn_w_down, loss_target, m_norm_mix_pre, m_norm_mix_post, m_norm_ffn_pre, m_norm_ffn_post, m_w_in, m_b_gate, m_ret_gn_w, m_ssd_conv_w, m_ssd_conv_b, m_ssd_dt_bias, m_ssd_a_log, m_ssd_d, m_ssd_norm_w, m_w_branch_ret, m_w_branch_sb, m_w_branch_ssd, m_w_out, m_ffn_w_gate, m_ffn_w_up, m_ffn_w_down, v_norm_mix_pre, v_norm_mix_post, v_norm_ffn_pre, v_norm_ffn_post, v_w_in, v_b_gate, v_ret_gn_w, v_ssd_conv_w, v_ssd_conv_b, v_ssd_dt_bias, v_ssd_a_log, v_ssd_d, v_ssd_norm_w, v_w_branch_ret, v_w_branch_sb, v_w_branch_ssd, v_w_out, v_ffn_w_gate, v_ffn_w_up, v_ffn_w_down):
    depth = w_in.shape[0]
    s, d = x.shape[1], x.shape[2]
    axes = ("x", "y", "c")
    me = _flat_index(lax.axis_index("x"), lax.axis_index("y"), lax.axis_index("c"))

    def tr_(a):
        return jnp.swapaxes(a, 1, 2)

    transposed = ("w_in", "ffn_w_gate", "ffn_w_up")
    wmv = dict(w_in=tuple(map(tr_, (w_in, m_w_in, v_w_in))), w_branch_ret=(w_branch_ret, m_w_branch_ret, v_w_branch_ret),
               w_branch_sb=(w_branch_sb, m_w_branch_sb, v_w_branch_sb),
               w_branch_ssd=(w_branch_ssd, m_w_branch_ssd, v_w_branch_ssd), w_out=(w_out, m_w_out, v_w_out),
               ffn_w_gate=tuple(map(tr_, (ffn_w_gate, m_ffn_w_gate, v_ffn_w_gate))),
               ffn_w_up=tuple(map(tr_, (ffn_w_up, m_ffn_w_up, v_ffn_w_up))),
               ffn_w_down=(ffn_w_down, m_ffn_w_down, v_ffn_w_down))
    members = dict(w_in=["w_in"], w_branch=["w_branch_ret", "w_branch_sb", "w_branch_ssd"], w_out=["w_out"],
                   ffn_w_gu=["ffn_w_gate", "ffn_w_up"], ffn_w_down=["ffn_w_down"])
    big_w = {g: (wmv[ns[0]][0] if len(ns) == 1 else jnp.concatenate([wmv[n][0] for n in ns], axis=1))
             for g, ns in members.items()}

    taps_all = all_gather(ssd_conv_w.reshape(-1, LANE), name="gather_conv_w")
    taps_all = jnp.transpose(taps_all.reshape(N_DEV, depth, SSD_K, -1), (1, 2, 0, 3)).reshape(depth, SSD_K, -1)

    cosf, sinf = rope_tables(positions.reshape(s), tr=TR)
    small_w = dict(norm_mix_pre=norm_mix_pre, norm_mix_post=norm_mix_post, norm_ffn_pre=norm_ffn_pre,
                   norm_ffn_post=norm_ffn_post, b_gate=b_gate, ret_gn_w=ret_gn_w, ssd_conv_b=ssd_conv_b,
                   ssd_dt_bias=ssd_dt_bias, ssd_a_log=ssd_a_log, ssd_d=ssd_d, ssd_norm_w=ssd_norm_w, taps=taps_all)

    def layer_weights(sw, gathered):
        lw = dict(gathered)
        for n in ("norm_mix_pre", "norm_mix_post", "norm_ffn_pre", "norm_ffn_post", "ret_gn_w", "ssd_conv_b", "ssd_norm_w"):
            lw[n] = _row(sw[n])
        for n in ("ssd_dt_bias", "ssd_a_log", "ssd_d"):
            lw[n] = _pad_lanes(sw[n])
        lw["b_gate"] = [_row(sw["b_gate"][i * d:(i + 1) * d]) for i in range(3)]
        lw["conv_taps"] = [sw["taps"][k:k + 1] for k in range(SSD_K)]
        return lw

    def layer_slice(t, l):
        return {n: a[l] for n, a in t.items()}

    def bf16_shards(l):
        return {n: big_w[n][l].astype(BF16) for n in BIG}

    def arrange(g):
        w_cat, w_dt = _assemble_w_in(g["w_in"], d)
        return dict(
            w_cat=w_cat, w_dt=w_dt,
            w_branch=jnp.transpose(g["w_branch"].reshape(N_DEV, 3, d // 2, -1), (1, 2, 0, 3)).reshape(3, d // 2, d),
            w_out=g["w_out"].reshape(d, d),
            ffn_w_gu=g["ffn_w_gu"].reshape(2 * N_DEV, -1, d),
            ffn_w_down=g["ffn_w_down"],
        )

    big_out = {}

    def adam(l, group, parts):
        rows, cols = big_w[group].shape[1:]
        parts = parts.reshape(N_DEV, rows, cols)
        off = 0
        for n in members[group]:
            big_out[n] = adamw_into(*wmv[n], parts, l, off, big_out.get(n), tr=_adam_rows(cols), name="adamw_" + n)
            off += wmv[n][0].shape[1]

    xs, saved = x.reshape(s, d), []
    gathered = {n: all_gather(a, name="gather_" + n) for n, a in bf16_shards(0).items()}
    for l in range(depth):
        lw = layer_weights(layer_slice(small_w, l), arrange(gathered))
        xs, res, gathered = _layer_fwd(xs, lw, cosf, sinf, bf16_shards(l + 1) if l + 1 < depth else None)
        saved.append((res, lw))
    loss_tile, dy = loss_head(xs, loss_target.reshape(s, d), tr=TR, name="loss_head")
    loss = lax.psum(loss_tile[0, 0], axes)

    dx, pending, small_layers = dy, None, [None] * depth
    for l in reversed(range(depth)):
        res, lw = saved[l]
        dx, got, pending, small_layers[l] = _layer_bwd(dx, res, lw, cosf, sinf, pending, last=(l == 0))
        for n in ("ffn_w_gu", "ffn_w_down"):
            adam(l, n, got[n])
        if l + 1 < depth:
            for n in ("w_in", "w_branch", "w_out"):
                adam(l + 1, n, got["prev_" + n])
    for n in ("w_branch", "w_out"):
        adam(0, n, got[n])
    adam(0, "w_in", exchange(pending["w_in"], name="exchange_w_in"))
    small_g = {n: jnp.stack([small_layers[l][n] for l in range(depth)]) for n in SMALL}

    n_dt = ssd_dt_bias.shape[-1]
    small_in = dict(norm_mix_pre=(norm_mix_pre, m_norm_mix_pre, v_norm_mix_pre), norm_mix_post=(norm_mix_post, m_norm_mix_post, v_norm_mix_post),
                    norm_ffn_pre=(norm_ffn_pre, m_norm_ffn_pre, v_norm_ffn_pre), norm_ffn_post=(norm_ffn_post, m_norm_ffn_post, v_norm_ffn_post),
                    b_gate=(b_gate, m_b_gate, v_b_gate), ret_gn_w=(ret_gn_w, m_ret_gn_w, v_ret_gn_w),
                    ssd_conv_b=(ssd_conv_b, m_ssd_conv_b, v_ssd_conv_b), ssd_dt_bias=(ssd_dt_bias, m_ssd_dt_bias, v_ssd_dt_bias),
                    ssd_a_log=(ssd_a_log, m_ssd_a_log, v_ssd_a_log), ssd_d=(ssd_d, m_ssd_d, v_ssd_d),
                    ssd_norm_w=(ssd_norm_w, m_ssd_norm_w, v_ssd_norm_w))
    rep = [n for n in SMALL if n != "ssd_conv_w"]

    def pack(arrs):
        flat = jnp.concatenate([a.reshape(-1) for a in arrs])
        rows = -(-flat.shape[0] // (16 * LANE)) * 16
        return jnp.pad(flat, (0, rows * LANE - flat.shape[0])).reshape(rows, LANE)

    conv_g = small_g["ssd_conv_w"]
    g_pack = pack([small_g[n] for n in rep] + [conv_g])
    g_all = all_gather(g_pack, name="gather_small_grads")
    zeros_conv = jnp.zeros_like(conv_g)
    w_pack, m_pack, v_pack = (pack([small_in[n][i] for n in rep] + [zeros_conv]) for i in range(3))
    sm = adamw_sum(w_pack, m_pack, v_pack, g_all, tr=TR, name="adamw_small")

    def unpack(p):
        flat, out, off = p.reshape(-1), {}, 0
        for n in rep:
            shp = small_in[n][0].shape
            size = math.prod(shp)
            out[n] = flat[off:off + size].reshape(shp)
            off += size
        out["conv_sum"] = flat[off:off + conv_g.size].reshape(conv_g.shape)
        return out

    sm = [unpack(p) for p in sm]
    ch = ssd_conv_w.shape[-1]
    conv_mine = lax.dynamic_slice_in_dim(sm[0]["conv_sum"], me * ch, ch, axis=2)
    conv_out = adamw_sum(ssd_conv_w.reshape(-1, LANE), m_ssd_conv_w.reshape(-1, LANE), v_ssd_conv_w.reshape(-1, LANE),
                         conv_mine.reshape(1, -1, LANE), tr=TR, name="adamw_conv_w")
    for i in range(4):
        sm[i]["ssd_conv_w"] = conv_out[i].reshape(ssd_conv_w.shape)

    def big_named(i):
        return {n: (tr_(out[i]) if n in transposed else out[i]) for n, out in big_out.items()}

    order = ["norm_mix_pre", "norm_mix_post", "norm_ffn_pre", "norm_ffn_post", "w_in", "b_gate", "ret_gn_w", "ssd_conv_w",
             "ssd_conv_b", "ssd_dt_bias", "ssd_a_log", "ssd_d", "ssd_norm_w", "w_branch_ret", "w_branch_sb", "w_branch_ssd",
             "w_out", "ffn_w_gate", "ffn_w_up", "ffn_w_down"]
    outs = [loss, dx.reshape(x.shape)]
    for i in range(4):
        named = {**sm[i], **big_named(i)}
        outs += [named[n] for n in order]
    return tuple(outs)
```
